```python
import jax, jax.numpy as jnp
from jax import lax
import numpy as np

D_MODEL = 1024
BATCH = 8
SEQ = 2048
DEPTH = 4

CTX_LEN = 256
GRID_W = 64
MLA_H = 6
Q_RANK = 256
KV_RANK = 128
NOPE_D = 64
ROPE_D = 32
V_D = 64
QK_D = NOPE_D + ROPE_D
ROPE_BASE = 10000.0
ML_H = 4
ML_D = 96
ML_W = ML_H * ML_D
ML_CONV = 3
ML_CHUNK = 64
NA_H = 4
NA_D = 64
NA_W = NA_H * NA_D
WIN_R = 8
WIN_C = 16
MLA_W = MLA_H * V_D
D_MIX = MLA_W + ML_W + NA_W
SPLIT_SIZES = (Q_RANK, KV_RANK, ROPE_D, ML_W, ML_W, 4 * ML_H, NA_W, NA_W, NA_W)
N_IN = Q_RANK + KV_RANK + ROPE_D + 2 * ML_W + 4 * ML_H + 3 * NA_W
N_EXPERTS = 16
N_GROUPS = 4
EXPERTS_PER_GROUP = N_EXPERTS // N_GROUPS
TOP_K = 2
D_FF_EXPERT = 256
Q_BLOCK = 128
EPS = 1e-6

kernel_name = "hybrid_mla_mlstm_natten_moe_dit"


def rmsnorm(x, g):
    xf = x.astype(jnp.float32)
    y = xf * lax.rsqrt(jnp.mean(xf * xf, axis=-1, keepdims=True) + EPS)
    return (y * g.astype(jnp.float32)).astype(x.dtype)


def head_layernorm(x, g):
    xf = x.astype(jnp.float32)
    mu = jnp.mean(xf, axis=-1, keepdims=True)
    var = jnp.mean(jnp.square(xf - mu), axis=-1, keepdims=True)
    y = (xf - mu) * lax.rsqrt(var + EPS)
    return (y * g.astype(jnp.float32).reshape(x.shape[-2:])).astype(x.dtype)


def heads_first(a):
    return jnp.transpose(a, (0, 2, 1, 3))


def merge_heads(o):
    B, H, L, d = o.shape
    return jnp.transpose(o, (0, 2, 1, 3)).reshape(B, L, H * d)


def split_cols(p):
    cuts = [int(v) for v in np.cumsum(SPLIT_SIZES)[:-1]]
    return jnp.split(p, cuts, axis=-1)


def axial_rope_tables(L, dtype):
    t = jnp.arange(L)
    row = (t // GRID_W).astype(jnp.float32)
    col = (t % GRID_W).astype(jnp.float32)
    quarter = ROPE_D // 4
    inv = ROPE_BASE ** (-jnp.arange(quarter, dtype=jnp.float32) / quarter)
    ar = row[:, None] * inv
    ac = col[:, None] * inv
    ang = jnp.concatenate([ar, ar, ac, ac], axis=-1)
    return jnp.cos(ang).astype(dtype), jnp.sin(ang).astype(dtype)


def apply_rope(a, cos, sin):
    a4 = a.reshape(a.shape[:-1] + (2, 2, ROPE_D // 4))
    rot = jnp.stack([-a4[..., 1, :], a4[..., 0, :]], axis=-2).reshape(a.shape)
    return a * cos[:, None, :] + rot * sin[:, None, :]


def with_rope(a, rope):
    if rope is None:
        return a
    cos, sin = rope
    return jnp.concatenate([a[..., :NOPE_D], apply_rope(a[..., NOPE_D:], cos, sin)], axis=-1)


def attend_blocks(q, k, v):
    B, H, Lq, d = q.shape
    scale = d ** -0.5
    nb = Lq // Q_BLOCK
    qb = jnp.moveaxis(q.reshape(B, H, nb, Q_BLOCK, d), 2, 0)

    def one(qi):
        s = jnp.einsum('bhqd,bhkd->bhqk', qi, k).astype(jnp.float32) * scale
        p = jax.nn.softmax(s, axis=-1)
        return jnp.einsum('bhqk,bhkd->bhqd', p.astype(v.dtype), v)

    o = lax.map(one, qb)
    return jnp.moveaxis(o, 0, 2).reshape(B, H, Lq, v.shape[-1])


def mla_q(q_c, qnorm_g, w_uq, q_g, rope):
    B, L, _ = q_c.shape
    q = (rmsnorm(q_c, qnorm_g) @ w_uq).reshape(B, L, MLA_H, QK_D)
    return heads_first(with_rope(rmsnorm(q, q_g), rope))


def mla_kv(c_kv, k_r, kvnorm_g, w_ukv, k_g, rope):
    B, L, _ = c_kv.shape
    kv = (rmsnorm(c_kv, kvnorm_g) @ w_ukv).reshape(B, L, MLA_H, NOPE_D + V_D)
    k = jnp.concatenate([kv[..., :NOPE_D], jnp.broadcast_to(k_r[:, :, None, :], (B, L, MLA_H, ROPE_D))], axis=-1)
    k = with_rope(rmsnorm(k, k_g), rope)
    return heads_first(k), heads_first(kv[..., NOPE_D:])


def ml_conv(u, w, b):
    out = lax.conv_general_dilated(u, w[:, None, :], window_strides=(1,),
                                   padding=[(ML_CONV // 2, ML_CONV // 2)],
                                   dimension_numbers=('NWC', 'WIO', 'NWC'),
                                   feature_group_count=u.shape[-1])
    return out + b


def ml_qkv(u, conv_w, conv_b, w_q, w_k, w_v):
    B, L, _ = u.shape
    uc = jax.nn.silu(ml_conv(u, conv_w, conv_b))
    uch = uc.reshape(B, L, ML_H, ML_D)
    uh = u.reshape(B, L, ML_H, ML_D)
    q = jnp.einsum('blhd,hde->bhle', uch, w_q)
    k = jnp.einsum('blhd,hde->bhle', uch, w_k) * (ML_D ** -0.5)
    v = jnp.einsum('blhd,hde->bhle', uh, w_v)
    return uc, q, k, v


def ml_gates(g, gate_b):
    B, L, _ = g.shape
    gg = (g + gate_b).astype(jnp.float32).reshape(B, L, 4, ML_H)
    gg = jnp.transpose(gg, (2, 0, 3, 1))
    return gg[0], gg[1], gg[2], gg[3]


def mlstm_chunkwise(q, k, v, i_pre, f_pre, state):
    B, H, L, d = q.shape
    nc = L // ML_CHUNK
    f32 = jnp.float32

    def chunks(a):
        a = a.reshape((B, H, nc, ML_CHUNK) + a.shape[3:])
        return jnp.moveaxis(a, 2, 0)

    xs = (chunks(q.astype(f32)), chunks(k.astype(f32)), chunks(v.astype(f32)),
          chunks(i_pre), chunks(jax.nn.log_sigmoid(f_pre)))
    tril = jnp.tril(jnp.ones((ML_CHUNK, ML_CHUNK), dtype=bool))

    def step(carry, inp):
        C, n, m = carry
        qc, kc, vc, ic, lf = inp
        b = jnp.cumsum(lf, axis=-1)
        dlog = jnp.where(tril, b[..., :, None] - b[..., None, :] + ic[..., None, :], -jnp.inf)
        inter = b + m[..., None]
        m_t = jnp.maximum(inter, jnp.max(dlog, axis=-1))
        dw = jnp.exp(dlog - m_t[..., None])
        iw = jnp.exp(inter - m_t)
        s = jnp.einsum('bhtd,bhsd->bhts', qc, kc) * dw
        num = iw[..., None] * jnp.einsum('bhed,bhtd->bhte', C, qc) + jnp.einsum('bhts,bhse->bhte', s, vc)
        nq = iw * jnp.einsum('bhd,bhtd->bht', n, qc) + jnp.sum(s, axis=-1)
        h = num / jnp.maximum(jnp.abs(nq), jnp.exp(-m_t))[..., None]
        b_end = b[..., -1]
        wlog = b_end[..., None] - b + ic
        m_new = jnp.maximum(b_end + m, jnp.max(wlog, axis=-1))
        a = jnp.exp(b_end + m - m_new)
        ws = jnp.exp(wlog - m_new[..., None])
        C_new = a[..., None, None] * C + jnp.einsum('bhs,bhse,bhsd->bhed', ws, vc, kc)
        n_new = a[..., None] * n + jnp.einsum('bhs,bhsd->bhd', ws, kc)
        return (C_new, n_new, m_new), h

    state, hs = lax.scan(step, state, xs)
    h = jnp.moveaxis(hs, 0, 2).reshape(B, H, L, d)
    return h.astype(q.dtype), state


def ml_out(h, uc, z, norm_g, skip):
    B, H, L, d = h.shape
    hn = head_layernorm(jnp.transpose(h, (0, 2, 1, 3)), norm_g).reshape(B, L, ML_W)
    return (hn + skip * uc) * jax.nn.silu(z)


def mlstm_mixer(u_x, z_x, g_x, u_y, z_y, g_y, conv_w, conv_b, w_q, w_k, w_v, gate_b, norm_g, skip, need_ctx):
    ucx, qx, kx, vx = ml_qkv(u_x, conv_w, conv_b, w_q, w_k, w_v)
    ucy, qy, ky, vy = ml_qkv(u_y, conv_w, conv_b, w_q, w_k, w_v)
    ifx, ffx, ibx, fbx = ml_gates(g_x, gate_b)
    ify, ffy, iby, fby = ml_gates(g_y, gate_b)
    B = u_y.shape[0]
    zero = (jnp.zeros((B, ML_H, ML_D, ML_D), jnp.float32), jnp.zeros((B, ML_H, ML_D), jnp.float32),
            jnp.zeros((B, ML_H), jnp.float32))
    rev = lambda a: jnp.flip(a, axis=2)
    hyf, st_f = mlstm_chunkwise(qy, ky, vy, ify, ffy, zero)
    hyb, st_b = mlstm_chunkwise(rev(qy), rev(ky), rev(vy), rev(iby), rev(fby), zero)
    hxf, _ = mlstm_chunkwise(qx, kx, vx, ifx, ffx, st_f)
    hxb, _ = mlstm_chunkwise(rev(qx), rev(kx), rev(vx), rev(ibx), rev(fbx), st_b)
    out_x = ml_out(hxf + rev(hxb), ucx, z_x, norm_g, skip)
    out_y = ml_out(hyf + rev(hyb), ucy, z_y, norm_g, skip) if need_ctx else None
    return out_x, out_y


def na_proj(a, g):
    B, L, _ = a.shape
    a = a.reshape(B, L, NA_H, NA_D)
    if g is not None:
        a = rmsnorm(a, g)
    return heads_first(a)


def na_latent(q, k, v, kc, vc, rpb):
    B, H, L, d = q.shape
    rows = L // GRID_W
    kh = min(WIN_R, rows)
    scale = d ** -0.5
    qg = q.reshape(B, H, rows, GRID_W, d)
    kg = k.reshape(B, H, rows, GRID_W, d)
    vg = v.reshape(B, H, rows, GRID_W, d)
    r = jnp.arange(rows)
    rs = jnp.clip(r - kh // 2, 0, rows - kh)
    ridx = rs[:, None] + jnp.arange(kh)[None, :]
    kb = kg[:, :, ridx]
    vb = vg[:, :, ridx]
    cq = jnp.arange(GRID_W)
    cs = jnp.clip(cq - WIN_C // 2, 0, GRID_W - WIN_C)
    col_ok = (cq[None, :] >= cs[:, None]) & (cq[None, :] < cs[:, None] + WIN_C)
    dr = ridx - r[:, None] + (WIN_R - 1)
    dc = jnp.clip(cq[None, :] - cq[:, None], -(WIN_C - 1), WIN_C - 1) + (WIN_C - 1)
    bias = rpb[:, dr[:, None, :, None], dc[None, :, None, :]].astype(jnp.float32)
    s_loc = jnp.einsum('bhrqd,bhrjkd->bhrqjk', qg, kb).astype(jnp.float32) * scale + bias[None]
    s_loc = jnp.where(col_ok[:, None, :], s_loc, -jnp.inf)
    s_ctx = jnp.einsum('bhrqd,bhcd->bhrqc', qg, kc).astype(jnp.float32) * scale
    nloc = kh * GRID_W
    s = jnp.concatenate([s_loc.reshape(B, H, rows, GRID_W, nloc), s_ctx], axis=-1)
    p = jax.nn.softmax(s, axis=-1)
    p_loc = p[..., :nloc].reshape(B, H, rows, GRID_W, kh, GRID_W).astype(v.dtype)
    p_ctx = p[..., nloc:].astype(v.dtype)
    o = jnp.einsum('bhrqjk,bhrjkd->bhrqd', p_loc, vb) + jnp.einsum('bhrqc,bhcd->bhrqd', p_ctx, vc)
    return o.reshape(B, H, L, d)


def moe(h, router_w, router_b, w1, w3, w2):
    B, L, D = h.shape
    t = h.reshape(-1, D)
    aff = jax.nn.sigmoid((t @ router_w).astype(jnp.float32))
    sel = aff + router_b.astype(jnp.float32)
    grp_score = jnp.sum(lax.top_k(sel.reshape(-1, N_GROUPS, EXPERTS_PER_GROUP), TOP_K)[0], axis=-1)
    best = jnp.argmax(grp_score, axis=-1)
    in_group = (jnp.arange(N_EXPERTS) // EXPERTS_PER_GROUP)[None, :] == best[:, None]
    _, idx = lax.top_k(jnp.where(in_group, sel, -jnp.inf), TOP_K)
    wsel = jnp.take_along_axis(aff, idx, axis=-1)
    wsel = wsel / jnp.sum(wsel, axis=-1, keepdims=True)
    gates = jnp.sum(jax.nn.one_hot(idx, N_EXPERTS, dtype=jnp.float32) * wsel[..., None], axis=1)
    a = jnp.einsum('nd,edf->nef', t, w1)
    b = jnp.einsum('nd,edf->nef', t, w3)
    hid = jax.nn.silu(a) * b * gates[..., None].astype(t.dtype)
    return jnp.einsum('nef,efd->nd', hid, w2).reshape(B, L, D)


def layer(x, y, mod_x, mod_y, rope, need_ctx,
          norm1_g, norm2_g, w_in, w_out,
          mla_qnorm_g, mla_w_uq, mla_kvnorm_g, mla_w_ukv, mla_q_g, mla_k_g,
          ml_conv_w, ml_conv_b, ml_w_q, ml_w_k, ml_w_v, ml_gate_b, ml_norm_g, ml_skip,
          na_q_g, na_k_g, na_rpb,
          router_w, router_b, moe_w1, moe_w3, moe_w2):
    sh1x, sc1x, g1x, sh2x, sc2x, g2x = [m[:, None, :] for m in jnp.split(mod_x, 6, axis=-1)]
    sh1y, sc1y, g1y, sh2y, sc2y, g2y = jnp.split(mod_y, 6, axis=-1)
    xn = rmsnorm(x, norm1_g) * (1 + sc1x) + sh1x
    yn = rmsnorm(y, norm1_g) * (1 + sc1y) + sh1y
    px = split_cols(xn @ w_in)
    py = split_cols(yn @ w_in)

    qx = mla_q(px[0], mla_qnorm_g, mla_w_uq, mla_q_g, rope)
    kx, vx = mla_kv(px[1], px[2], mla_kvnorm_g, mla_w_ukv, mla_k_g, rope)
    ky, vy = mla_kv(py[1], py[2], mla_kvnorm_g, mla_w_ukv, mla_k_g, None)
    mla_x = merge_heads(attend_blocks(qx, jnp.concatenate([kx, ky], axis=2), jnp.concatenate([vx, vy], axis=2)))

    ml_x, ml_y = mlstm_mixer(px[3], px[4], px[5], py[3], py[4], py[5],
                             ml_conv_w, ml_conv_b, ml_w_q, ml_w_k, ml_w_v, ml_gate_b, ml_norm_g, ml_skip, need_ctx)

    nkx, nvx = na_proj(px[7], na_k_g), na_proj(px[8], None)
    nky, nvy = na_proj(py[7], na_k_g), na_proj(py[8], None)
    na_x = merge_heads(na_latent(na_proj(px[6], na_q_g), nkx, nvx, nky, nvy, na_rpb))

    x = x + g1x * (jnp.concatenate([mla_x, ml_x, na_x], axis=-1) @ w_out)
    x = x + g2x * moe(rmsnorm(x, norm2_g) * (1 + sc2x) + sh2x, router_w, router_b, moe_w1, moe_w3, moe_w2)
    if not need_ctx:
        return x, None

    qy = mla_q(py[0], mla_qnorm_g, mla_w_uq, mla_q_g, None)
    mla_y = merge_heads(attend_blocks(qy, ky, vy))
    na_y = merge_heads(attend_blocks(na_proj(py[6], na_q_g), nky, nvy))
    y = y + g1y * (jnp.concatenate([mla_y, ml_y, na_y], axis=-1) @ w_out)
    y = y + g2y * moe(rmsnorm(y, norm2_g) * (1 + sc2y) + sh2y, router_w, router_b, moe_w1, moe_w3, moe_w2)
    return x, y


def setup_inputs(seed: int = 0) -> dict:
    key = jax.random.key(seed)
    ks = iter(list(jax.random.split(key, 48)))

    def nrm(shape, scale):
        return jax.random.normal(next(ks), shape, jnp.float32) * scale

    def gain(shape):
        return 1.0 + nrm(shape, 0.02)

    D = D_MODEL
    fb = jnp.linspace(3.0, 6.0, ML_H, dtype=jnp.float32)
    zh = jnp.zeros((ML_H,), jnp.float32)
    gate_base = jnp.concatenate([zh, fb, zh, fb])
    return {
        "x": nrm((BATCH, SEQ, D), 1.0),
        "c": nrm((BATCH, D), 1.0),
        "ctx": nrm((BATCH, CTX_LEN, D), 1.0),
        "c_ctx": nrm((D,), 1.0),
        "w_mod": nrm((DEPTH, D, 6 * D), 0.5 * D ** -0.5),
        "b_mod": nrm((DEPTH, 6 * D), 0.02),
        "norm1_g": gain((DEPTH, D)),
        "norm2_g": gain((DEPTH, D)),
        "w_in": nrm((DEPTH, D, N_IN), D ** -0.5),
        "w_out": nrm((DEPTH, D_MIX, D), D_MIX ** -0.5),
        "mla_qnorm_g": gain((DEPTH, Q_RANK)),
        "mla_w_uq": nrm((DEPTH, Q_RANK, MLA_H * QK_D), Q_RANK ** -0.5),
        "mla_kvnorm_g": gain((DEPTH, KV_RANK)),
        "mla_w_ukv": nrm((DEPTH, KV_RANK, MLA_H * (NOPE_D + V_D)), KV_RANK ** -0.5),
        "mla_q_g": gain((DEPTH, QK_D)),
        "mla_k_g": gain((DEPTH, QK_D)),
        "ml_conv_w": nrm((DEPTH, ML_CONV, ML_W), ML_CONV ** -0.5),
        "ml_conv_b": nrm((DEPTH, ML_W), 0.02),
        "ml_w_q": nrm((DEPTH, ML_H, ML_D, ML_D), ML_D ** -0.5),
        "ml_w_k": nrm((DEPTH, ML_H, ML_D, ML_D), ML_D ** -0.5),
        "ml_w_v": nrm((DEPTH, ML_H, ML_D, ML_D), ML_D ** -0.5),
        "ml_gate_b": gate_base[None, :] + nrm((DEPTH, 4 * ML_H), 0.1),
        "ml_norm_g": gain((DEPTH, ML_W)),
        "ml_skip": gain((DEPTH, ML_W)),
        "na_q_g": gain((DEPTH, NA_D)),
        "na_k_g": gain((DEPTH, NA_D)),
        "na_rpb": nrm((DEPTH, NA_H, 2 * WIN_R - 1, 2 * WIN_C - 1), 0.02),
        "router_w": nrm((D, N_EXPERTS), D ** -0.5),
        "router_b": nrm((N_EXPERTS,), 0.01),
        "moe_w1": nrm((DEPTH, N_EXPERTS, D, D_FF_EXPERT), D ** -0.5),
        "moe_w3": nrm((DEPTH, N_EXPERTS, D, D_FF_EXPERT), D ** -0.5),
        "moe_w2": nrm((DEPTH, N_EXPERTS, D_FF_EXPERT, D), D_FF_EXPERT ** -0.5),
    }


def reference(x, c, ctx, c_ctx, w_mod, b_mod, norm1_g, norm2_g, w_in, w_out,
              mla_qnorm_g, mla_w_uq, mla_kvnorm_g, mla_w_ukv, mla_q_g, mla_k_g,
              ml_conv_w, ml_conv_b, ml_w_q, ml_w_k, ml_w_v, ml_gate_b, ml_norm_g, ml_skip,
              na_q_g, na_k_g, na_rpb, router_w, router_b, moe_w1, moe_w3, moe_w2):
    rope = axial_rope_tables(x.shape[1], x.dtype)
    sc = jax.nn.silu(c)
    scc = jax.nn.silu(c_ctx)
    y = ctx
    for l in range(DEPTH):
        mod_x = sc @ w_mod[l] + b_mod[l]
        mod_y = scc @ w_mod[l] + b_mod[l]
        x, y = layer(x, y, mod_x, mod_y, rope, l < DEPTH - 1,
                     norm1_g[l], norm2_g[l], w_in[l], w_out[l],
                     mla_qnorm_g[l], mla_w_uq[l], mla_kvnorm_g[l], mla_w_ukv[l], mla_q_g[l], mla_k_g[l],
                     ml_conv_w[l], ml_conv_b[l], ml_w_q[l], ml_w_k[l], ml_w_v[l], ml_gate_b[l], ml_norm_g[l], ml_skip[l],
                     na_q_g[l], na_k_g[l], na_rpb[l],
                     router_w, router_b, moe_w1[l], moe_w3[l], moe_w2[l])
    return x
```

```python
import functools

import numpy as np
import jax
import jax.numpy as jnp
from jax import lax
from jax.experimental import pallas as pl
from jax.experimental.pallas import tpu as pltpu

F32 = jnp.float32
BF16 = jnp.bfloat16

D = 1024
SEQ = 2048
CTX = 256
S = SEQ + CTX
DEPTH = 4
GRID_W = 64
ROWS = SEQ // GRID_W
EPS = 1e-6

MLA_H = 6
Q_RANK = 256
KV_RANK = 128
NOPE_D = 64
ROPE_D = 32
V_D = 64
QK_D = NOPE_D + ROPE_D
ROPE_BASE = 10000.0

ML_H = 4
ML_D = 96
ML_W = ML_H * ML_D
HP = 128
ML_WP = ML_H * HP
ML_CHUNK = 128
N_CHUNK = S // ML_CHUNK
N_CTX_CHUNK = CTX // ML_CHUNK

NA_H = 4
NA_D = 64
NA_W = NA_H * NA_D
WIN_R = 8
WIN_C = 16
NA_QROWS = 8
NA_KROWS = 16
NA_QB = NA_QROWS * GRID_W
NA_KB = NA_KROWS * GRID_W
NA_NBLK = ROWS // NA_QROWS

N_EXPERTS = 16
N_GROUPS = 4
EPG = N_EXPERTS // N_GROUPS
D_FF = 256

TT = 256
NT = S // TT
MOE_T = 1152
NEG = -1e30

C_QC = 0
C_CKV = 256
C_KR = 384
C_U = 512
C_Z = C_U + ML_WP
C_G = C_Z + ML_WP
C_NA = C_G + 128
NP_IN = C_NA + 3 * NA_W

VMEM_LIMIT = 56 * 1024 * 1024


def _cparams(sem):
    return pltpu.CompilerParams(dimension_semantics=sem, vmem_limit_bytes=VMEM_LIMIT)


def _sigmoid(x):
    return 1.0 / (1.0 + jnp.exp(-x))


def _silu(x):
    return x * _sigmoid(x)


def _dot(a, b):
    return jnp.dot(a, b, preferred_element_type=F32)


def _dot_nt(a, b):
    return lax.dot_general(a, b, (((1,), (1,)), ((), ())), preferred_element_type=F32)


def _dot_tn(a, b):
    return lax.dot_general(a, b, (((0,), (0,)), ((), ())), preferred_element_type=F32)


def _dot_hi(a, b):
    return jnp.dot(a, b, preferred_element_type=F32, precision=lax.Precision.HIGHEST)


def _mod_rows(mod_ref, t):
    m = mod_ref[0]
    return [m[:, i * D:(i + 1) * D] for i in range(6)]


def _mod_kernel(c_ref, w_ref, b_ref, o_ref):
    sc = _silu(c_ref[...])
    o_ref[0] = _dot_hi(sc, w_ref[0]) + b_ref[0]


def _modulation(cc, w_mod, b_mod):
    nc = 6
    return pl.pallas_call(
        _mod_kernel,
        grid=(DEPTH, nc),
        in_specs=[pl.BlockSpec((16, D), lambda l, j: (0, 0)),
                  pl.BlockSpec((1, D, D), lambda l, j: (l, 0, j)),
                  pl.BlockSpec((1, 1, D), lambda l, j: (l, 0, j))],
        out_specs=pl.BlockSpec((1, 16, D), lambda l, j: (l, 0, j)),
        out_shape=jax.ShapeDtypeStruct((DEPTH, 16, 6 * D), F32),
        compiler_params=_cparams(("parallel", "parallel")),
        name="modulation",
    )(cc, w_mod, b_mod.reshape(DEPTH, 1, 6 * D))


def _in_proj_kernel(z_ref, mod_ref, g_ref, w_ref, pmla_ref, pu_ref, pg_ref, pna_ref):
    sh1, sc1 = _mod_rows(mod_ref, None)[:2]
    x = z_ref[0]
    xn = x * lax.rsqrt(jnp.mean(x * x, axis=-1, keepdims=True) + EPS) * g_ref[...]
    xn = xn * (1.0 + sc1) + sh1
    p = _dot(xn.astype(BF16), w_ref[...])
    pmla_ref[0] = p[:, :C_U].astype(BF16)
    pu_ref[0] = p[:, C_U:C_G].astype(BF16)
    pg_ref[0] = p[:, C_G:C_NA]
    pna_ref[0] = p[:, C_NA:].astype(BF16)


def _mod_spec():
    return pl.BlockSpec((1, 1, 6 * D), lambda b, t: (2 * b + t // (NT - 1), 0, 0))


def _in_proj(z, mods, g, w):
    B = z.shape[0]
    tok = lambda w_: pl.BlockSpec((1, TT, w_), lambda b, t: (b, t, 0))
    return pl.pallas_call(
        _in_proj_kernel,
        grid=(B, NT),
        in_specs=[tok(D), _mod_spec(),
                  pl.BlockSpec((1, D), lambda b, t: (0, 0)),
                  pl.BlockSpec((D, NP_IN), lambda b, t: (0, 0))],
        out_specs=[tok(C_U), tok(2 * ML_WP), tok(128), tok(3 * NA_W)],
        out_shape=[jax.ShapeDtypeStruct((B, S, C_U), BF16),
                   jax.ShapeDtypeStruct((B, S, 2 * ML_WP), BF16),
                   jax.ShapeDtypeStruct((B, S, 128), F32),
                   jax.ShapeDtypeStruct((B, S, 3 * NA_W), BF16)],
        compiler_params=_cparams(("parallel", "parallel")),
        name="in_proj",
    )(z, mods, g, w)


def _mla_prep_kernel(p_ref, cos_ref, sin_ref, qng_ref, wuq_ref, kvng_ref, wuk_ref, wuv_ref,
                     qg_ref, kg_ref, q_out, k_out, v_out):
    p = p_ref[0].astype(F32)
    qc = p[:, C_QC:C_CKV]
    ckv = p[:, C_CKV:C_KR]
    kr = p[:, C_KR:C_U]
    qcn = (qc * lax.rsqrt(jnp.mean(qc * qc, axis=-1, keepdims=True) + EPS) * qng_ref[...]).astype(BF16)
    ckvn = (ckv * lax.rsqrt(jnp.mean(ckv * ckv, axis=-1, keepdims=True) + EPS) * kvng_ref[...]).astype(BF16)
    cos = cos_ref[...]
    sin = sin_ref[...]
    lane = lax.broadcasted_iota(jnp.int32, (TT, HP), 1)
    first_half = (lane & (ROPE_D // 4)) == 0

    def norm_rope(x, g):
        r = lax.rsqrt(jnp.sum(x * x, axis=-1, keepdims=True) * (1.0 / QK_D) + EPS)
        xn = x * r * g
        rot = jnp.where(first_half, -pltpu.roll(xn, HP - ROPE_D // 4, 1), pltpu.roll(xn, ROPE_D // 4, 1))
        return xn * cos + rot * sin

    scale = QK_D ** -0.5
    for h in range(MLA_H):
        qh = _dot(qcn, wuq_ref[h])
        q_out[0, h] = (norm_rope(qh, qg_ref[...]) * scale).astype(BF16)
        kh = _dot(ckvn, wuk_ref[h]) + kr
        k_out[0, h] = norm_rope(kh, kg_ref[...]).astype(BF16)
        v_out[0, h] = _dot(ckvn, wuv_ref[h]).astype(BF16)


def _mla_prep(pmla, cos, sin, qng, wuq, kvng, wuk, wuv, qg, kg):
    B = pmla.shape[0]
    full = lambda a: pl.BlockSpec(a.shape, lambda b, t, _n=a.ndim: (0,) * _n)
    hd = lambda w_: pl.BlockSpec((1, MLA_H, TT, w_), lambda b, t: (b, 0, t, 0))
    return pl.pallas_call(
        _mla_prep_kernel,
        grid=(B, NT),
        in_specs=[pl.BlockSpec((1, TT, C_U), lambda b, t: (b, t, 0)),
                  pl.BlockSpec((TT, HP), lambda b, t: (t, 0)),
                  pl.BlockSpec((TT, HP), lambda b, t: (t, 0)),
                  full(qng), full(wuq), full(kvng), full(wuk), full(wuv), full(qg), full(kg)],
        out_specs=[hd(HP), hd(HP), hd(V_D)],
        out_shape=[jax.ShapeDtypeStruct((B, MLA_H, S, HP), BF16),
                   jax.ShapeDtypeStruct((B, MLA_H, S, HP), BF16),
                   jax.ShapeDtypeStruct((B, MLA_H, S, V_D), BF16)],
        compiler_params=_cparams(("parallel", "parallel")),
        name="mla_prep",
    )(pmla, cos, sin, qng, wuq, kvng, wuk, wuv, qg, kg)


def _mla_attn_kernel(q_ref, k_ref, v_ref, o_ref):
    t = pl.program_id(1)

    def attend(k_lo, k_n):
        outs = []
        for h in range(MLA_H):
            s = _dot_nt(q_ref[0, h], k_ref[0, h, k_lo:k_lo + k_n, :])
            m = jnp.max(s, axis=-1, keepdims=True)
            p = jnp.exp(s - m)
            l = jnp.sum(p, axis=-1, keepdims=True)
            outs.append(_dot(p.astype(BF16), v_ref[0, h, k_lo:k_lo + k_n, :]) / l)
        o_ref[0] = jnp.concatenate(outs, axis=-1).astype(BF16)

    @pl.when(t < NT - 1)
    def _():
        attend(0, S)

    @pl.when(t == NT - 1)
    def _():
        attend(SEQ, CTX)


def _mla_attn(q, k, v):
    B = q.shape[0]
    return pl.pallas_call(
        _mla_attn_kernel,
        grid=(B, NT),
        in_specs=[pl.BlockSpec((1, MLA_H, TT, HP), lambda b, t: (b, 0, t, 0)),
                  pl.BlockSpec((1, MLA_H, S, HP), lambda b, t: (b, 0, 0, 0)),
                  pl.BlockSpec((1, MLA_H, S, V_D), lambda b, t: (b, 0, 0, 0))],
        out_specs=pl.BlockSpec((1, TT, MLA_H * V_D), lambda b, t: (b, t, 0)),
        out_shape=jax.ShapeDtypeStruct((B, S, MLA_H * V_D), BF16),
        compiler_params=_cparams(("parallel", "arbitrary")),
        name="mla_attn",
    )(q, k, v)


def _log_sigmoid(x):
    return jnp.minimum(x, 0.0) - jnp.log(1.0 + jnp.exp(-jnp.abs(x)))


def _mlstm_kernel(pu_ref, pg_ref, cw_ref, cb_ref, wq_ref, wk_ref, wv_ref, gb_ref, ng_ref, sk_ref,
                  tril_ref, o_ref, uc_s, q_s, k_s, v_s, h_s, ct_s, n_s, m_s):
    CA = 256
    row = lax.broadcasted_iota(jnp.int32, (CA, ML_WP), 0)

    def conv_body(i, carry):
        r0 = pl.multiple_of(i * CA, CA)
        x = pu_ref[0, pl.ds(r0, CA), 0:ML_WP].astype(F32)
        pr = pl.multiple_of(jnp.maximum(r0 - 16, 0), 16)
        nx = pl.multiple_of(jnp.minimum(r0 + CA, S - 16), 16)
        prev = pu_ref[0, pl.ds(pr, 16), 0:ML_WP].astype(F32)[15:16, :]
        nxt = pu_ref[0, pl.ds(nx, 16), 0:ML_WP].astype(F32)[0:1, :]
        seq_start = jnp.logical_or(r0 == 0, r0 == SEQ)
        seq_end = jnp.logical_or(r0 + CA == SEQ, r0 + CA == S)
        prev = jnp.where(seq_start, 0.0, prev)
        nxt = jnp.where(seq_end, 0.0, nxt)
        up = jnp.where(row == 0, prev, pltpu.roll(x, 1, 0))
        dn = jnp.where(row == CA - 1, nxt, pltpu.roll(x, CA - 1, 0))
        uc = _silu(cw_ref[0:1, :] * up + cw_ref[1:2, :] * x + cw_ref[2:3, :] * dn + cb_ref[...])
        ucb = uc.astype(BF16)
        uc_s[pl.ds(r0, CA), :] = ucb
        xb = x.astype(BF16)
        for h in range(ML_H):
            sl = slice(h * HP, (h + 1) * HP)
            q_s[pl.ds(r0, CA), sl] = _dot(ucb[:, sl], wq_ref[h]).astype(BF16)
            k_s[pl.ds(r0, CA), sl] = _dot(ucb[:, sl], wk_ref[h]).astype(BF16)
            v_s[pl.ds(r0, CA), sl] = _dot(xb[:, sl], wv_ref[h]).astype(BF16)
        h_s[pl.ds(r0, CA), :] = jnp.zeros((CA, ML_WP), F32)
        return carry

    lax.fori_loop(0, S // CA, conv_body, 0)

    ct_s[...] = jnp.zeros_like(ct_s)
    n_s[...] = jnp.zeros_like(n_s)
    m_s[...] = jnp.zeros_like(m_s)
    tril = tril_ref[...]
    ti = lax.broadcasted_iota(jnp.int32, (ML_CHUNK, ML_CHUNK), 0)
    si = lax.broadcasted_iota(jnp.int32, (ML_CHUNK, ML_CHUNK), 1)
    masks = (si <= ti, si >= ti)
    tris = (tril, tril.T)

    def scan_body(j, carry):
        chunk = (jnp.where(j < N_CTX_CHUNK, j + N_CHUNK - N_CTX_CHUNK, j - N_CTX_CHUNK), N_CHUNK - 1 - j)
        for d in range(2):
            r0 = pl.multiple_of(chunk[d] * ML_CHUNK, ML_CHUNK)
            g = pg_ref[0, pl.ds(r0, ML_CHUNK), :] + gb_ref[...]
            cum = _dot_hi(tris[d], _log_sigmoid(g))
            g_t = g.T
            cum_t = cum.T
            end = ML_CHUNK - 1 if d == 0 else 0
            for h in range(ML_H):
                ci = d * 2 * ML_H + h
                cf = ci + ML_H
                c = d * ML_H + h
                sl = slice(h * HP, (h + 1) * HP)
                qc = q_s[pl.ds(r0, ML_CHUNK), sl]
                kc = k_s[pl.ds(r0, ML_CHUNK), sl]
                vc = v_s[pl.ds(r0, ML_CHUNK), sl]
                b_col = cum[:, cf:cf + 1]
                b_row = cum_t[cf:cf + 1, :]
                i_col = g[:, ci:ci + 1]
                i_row = g_t[ci:ci + 1, :]
                m = m_s[c][:, 0:1]
                n = n_s[c]
                ct = ct_s[c]
                dlog = jnp.where(masks[d], b_col - b_row + i_row, NEG)
                inter = b_col + m
                m_t = jnp.maximum(inter, jnp.max(dlog, axis=-1, keepdims=True))
                dw = jnp.exp(dlog - m_t)
                iw = jnp.exp(inter - m_t)
                s = _dot_nt(qc, kc) * dw
                num = iw * _dot(qc, ct.astype(BF16)) + _dot(s.astype(BF16), vc)
                nq = iw * jnp.sum(qc.astype(F32) * n, axis=-1, keepdims=True) + jnp.sum(s, axis=-1, keepdims=True)
                hv = num / jnp.maximum(jnp.abs(nq), jnp.exp(-m_t))
                h_s[pl.ds(r0, ML_CHUNK), sl] += hv
                b_end = b_row[:, end:end + 1]
                m_new = jnp.maximum(b_end + m, jnp.max(b_end - b_row + i_row, axis=-1, keepdims=True))
                a = jnp.exp(b_end + m - m_new)
                ws = jnp.exp(b_end - b_col + i_col - m_new)
                kf = kc.astype(F32)
                ct_s[c] = a * ct + _dot_tn(kc, (ws * vc.astype(F32)).astype(BF16))
                n_s[c] = a * n + jnp.sum(ws * kf, axis=0, keepdims=True)
                m_s[c] = jnp.broadcast_to(m_new, (1, HP))
        return carry

    lax.fori_loop(0, N_CHUNK, scan_body, 0)

    live = (lax.broadcasted_iota(jnp.int32, (CA, HP), 1) < ML_D).astype(F32)

    def out_body(i, carry):
        r0 = pl.multiple_of(i * CA, CA)
        for h in range(ML_H):
            sl = slice(h * HP, (h + 1) * HP)
            hh = h_s[pl.ds(r0, CA), sl]
            mu = jnp.sum(hh, axis=-1, keepdims=True) * (1.0 / ML_D)
            dv = (hh - mu) * live
            var = jnp.sum(dv * dv, axis=-1, keepdims=True) * (1.0 / ML_D)
            hn = dv * lax.rsqrt(var + EPS) * ng_ref[:, sl]
            uc = uc_s[pl.ds(r0, CA), sl].astype(F32)
            zz = pu_ref[0, pl.ds(r0, CA), ML_WP + h * HP:ML_WP + (h + 1) * HP].astype(F32)
            o_ref[0, pl.ds(r0, CA), sl] = ((hn + sk_ref[:, sl] * uc) * _silu(zz)).astype(BF16)
        return carry

    lax.fori_loop(0, S // CA, out_body, 0)


def _mlstm(pu, pg, cw, cb, wq, wk, wv, gb, ng, sk, tril):
    B = pu.shape[0]
    full = lambda a: pl.BlockSpec(a.shape, lambda b, _n=a.ndim: (0,) * _n)
    n_chain = 2 * ML_H
    return pl.pallas_call(
        _mlstm_kernel,
        grid=(B,),
        in_specs=[pl.BlockSpec((1, S, 2 * ML_WP), lambda b: (b, 0, 0)),
                  pl.BlockSpec((1, S, 128), lambda b: (b, 0, 0)),
                  full(cw), full(cb), full(wq), full(wk), full(wv), full(gb), full(ng), full(sk), full(tril)],
        out_specs=pl.BlockSpec((1, S, ML_WP), lambda b: (b, 0, 0)),
        out_shape=jax.ShapeDtypeStruct((B, S, ML_WP), BF16),
        scratch_shapes=[pltpu.VMEM((S, ML_WP), BF16), pltpu.VMEM((S, ML_WP), BF16),
                        pltpu.VMEM((S, ML_WP), BF16), pltpu.VMEM((S, ML_WP), BF16),
                        pltpu.VMEM((S, ML_WP), F32),
                        pltpu.VMEM((n_chain, HP, HP), F32),
                        pltpu.VMEM((n_chain, 1, HP), F32),
                        pltpu.VMEM((n_chain, 1, HP), F32)],
        compiler_params=_cparams(("parallel",)),
        name="mlstm",
    )(pu, pg, cw, cb, wq, wk, wv, gb, ng, sk, tril)


def _na_kernel(p_ref, bias_ref, qg_ref, kg_ref, seg_ref, o_ref, kn_s):
    j = pl.program_id(1)
    seg = seg_ref[...]

    def headnorm(x, g):
        ss = _dot((x * x).astype(BF16), seg)
        return x * lax.rsqrt(ss * (1.0 / NA_D) + EPS) * g

    @pl.when(j == 0)
    def _():
        def body(i, carry):
            r0 = pl.multiple_of(i * TT, TT)
            kk = p_ref[0, pl.ds(r0, TT), NA_W:2 * NA_W].astype(F32)
            kn_s[pl.ds(r0, TT), :] = headnorm(kk, kg_ref[...]).astype(BF16)
            return carry
        lax.fori_loop(0, NT, body, 0)

    scale = NA_D ** -0.5
    kctx = kn_s[SEQ:S, :]
    vctx = p_ref[0, SEQ:S, 2 * NA_W:3 * NA_W]

    @pl.when(j < NA_NBLK)
    def _():
        q0 = pl.multiple_of(j * NA_QB, NA_QB)
        k0 = pl.multiple_of(jnp.clip(j * NA_QROWS - WIN_R // 2, 0, ROWS - NA_KROWS) * GRID_W, 256)
        q = headnorm(p_ref[0, pl.ds(q0, NA_QB), 0:NA_W].astype(F32), qg_ref[...]) * scale
        kl = kn_s[pl.ds(k0, NA_KB), :]
        vl = p_ref[0, pl.ds(k0, NA_KB), 2 * NA_W:3 * NA_W]
        head = lax.broadcasted_iota(jnp.int32, (NA_QB, NA_W), 1) // NA_D
        acc = jnp.zeros((NA_QB, NA_W), F32)
        for h in range(NA_H):
            qm = jnp.where(head == h, q, 0.0).astype(BF16)
            s1 = _dot_nt(qm, kl) + bias_ref[0, h].astype(F32)
            s2 = _dot_nt(qm, kctx)
            m = jnp.maximum(jnp.max(s1, axis=-1, keepdims=True), jnp.max(s2, axis=-1, keepdims=True))
            p1 = jnp.exp(s1 - m)
            p2 = jnp.exp(s2 - m)
            l = jnp.sum(p1, axis=-1, keepdims=True) + jnp.sum(p2, axis=-1, keepdims=True)
            o = (_dot(p1.astype(BF16), vl) + _dot(p2.astype(BF16), vctx)) / l
            acc = jnp.where(head == h, o, acc)
        o_ref[0, pl.ds(q0, NA_QB), :] = acc.astype(BF16)

    @pl.when(j == NA_NBLK)
    def _():
        q = headnorm(p_ref[0, SEQ:S, 0:NA_W].astype(F32), qg_ref[...]) * scale
        head = lax.broadcasted_iota(jnp.int32, (CTX, NA_W), 1) // NA_D
        acc = jnp.zeros((CTX, NA_W), F32)
        for h in range(NA_H):
            qm = jnp.where(head == h, q, 0.0).astype(BF16)
            s2 = _dot_nt(qm, kctx)
            m = jnp.max(s2, axis=-1, keepdims=True)
            p2 = jnp.exp(s2 - m)
            l = jnp.sum(p2, axis=-1, keepdims=True)
            o = _dot(p2.astype(BF16), vctx) / l
            acc = jnp.where(head == h, o, acc)
        o_ref[0, SEQ:S, :] = acc.astype(BF16)


def _na_attn(pna, bias, qg, kg, seg):
    B = pna.shape[0]
    return pl.pallas_call(
        _na_kernel,
        grid=(B, NA_NBLK + 1),
        in_specs=[pl.BlockSpec((1, S, 3 * NA_W), lambda b, j: (b, 0, 0)),
                  pl.BlockSpec((1, NA_H, NA_QB, NA_KB), lambda b, j: (jnp.minimum(j, NA_NBLK - 1), 0, 0, 0)),
                  pl.BlockSpec((1, NA_W), lambda b, j: (0, 0)),
                  pl.BlockSpec((1, NA_W), lambda b, j: (0, 0)),
                  pl.BlockSpec((NA_W, NA_W), lambda b, j: (0, 0))],
        out_specs=pl.BlockSpec((1, S, NA_W), lambda b, j: (b, 0, 0)),
        out_shape=jax.ShapeDtypeStruct((B, S, NA_W), BF16),
        scratch_shapes=[pltpu.VMEM((S, NA_W), BF16)],
        compiler_params=_cparams(("parallel", "arbitrary")),
        name="na_attn",
    )(pna, bias, qg, kg, seg)


def _out_proj_kernel(z_ref, mod_ref, a_ref, m_ref, n_ref, wa_ref, wm_ref, wn_ref, g2_ref, rw_ref, rb_ref,
                     z1_ref, h2_ref, gate_ref):
    mods = _mod_rows(mod_ref, None)
    g1, sh2, sc2 = mods[2], mods[3], mods[4]
    mix = _dot(a_ref[0], wa_ref[...]) + _dot(m_ref[0], wm_ref[...]) + _dot(n_ref[0], wn_ref[...])
    x = z_ref[0] + g1 * mix
    z1_ref[0] = x
    hn = x * lax.rsqrt(jnp.mean(x * x, axis=-1, keepdims=True) + EPS) * g2_ref[...]
    hn = hn * (1.0 + sc2) + sh2
    h2_ref[0] = hn.astype(BF16)

    lane = lax.broadcasted_iota(jnp.int32, (TT, 128), 1)
    live = lane < N_EXPERTS
    aff = _sigmoid(_dot_hi(hn, rw_ref[...]))
    sel = aff + rb_ref[...]

    def cyc(x, k, width):
        fwd = pltpu.roll(x, 128 - k, 1)
        back = pltpu.roll(x, width - k, 1)
        return jnp.where((lane % width) + k < width, fwd, back)

    def rank(x, width, step):
        r = jnp.zeros((TT, 128), F32)
        for k in range(1, width // step):
            y = cyc(x, k * step, width)
            wrapped = (lane % width) + k * step >= width
            beats = jnp.logical_or(y > x, jnp.logical_and(y == x, wrapped))
            r = r + beats.astype(F32)
        return r

    top2 = rank(sel, EPG, 1) < 2.0
    part = jnp.where(top2, sel, 0.0)
    gscore = part
    for k in range(1, EPG):
        gscore = gscore + cyc(part, k, EPG)
    best = rank(gscore, N_EXPERTS, EPG) < 1.0
    chosen = jnp.logical_and(jnp.logical_and(top2, best), live)
    w = jnp.where(chosen, aff, 0.0)
    gate_ref[0] = w / jnp.sum(w, axis=-1, keepdims=True)


def _out_proj(z, mods, mla_o, ml_o, na_o, wa, wm, wn, g2, rw, rb):
    B = z.shape[0]
    tok = lambda w_: pl.BlockSpec((1, TT, w_), lambda b, t: (b, t, 0))
    full = lambda a: pl.BlockSpec(a.shape, lambda b, t, _n=a.ndim: (0,) * _n)
    return pl.pallas_call(
        _out_proj_kernel,
        grid=(B, NT),
        in_specs=[tok(D), _mod_spec(), tok(MLA_H * V_D), tok(ML_WP), tok(NA_W),
                  full(wa), full(wm), full(wn), full(g2), full(rw), full(rb)],
        out_specs=[tok(D), tok(D), tok(128)],
        out_shape=[jax.ShapeDtypeStruct((B, S, D), F32),
                   jax.ShapeDtypeStruct((B, S, D), BF16),
                   jax.ShapeDtypeStruct((B, S, 128), F32)],
        compiler_params=_cparams(("parallel", "parallel")),
        name="out_proj",
    )(z, mods, mla_o, ml_o, na_o, wa, wm, wn, g2, rw, rb)


def _moe_kernel(z1_ref, h2_ref, gate_ref, modx_ref, mody_ref, w1_ref, w3_ref, w2_ref, o_ref, acc_s):
    e = pl.program_id(2)
    half = pl.program_id(1)

    @pl.when(e == 0)
    def _():
        acc_s[...] = jnp.zeros_like(acc_s)

    h = h2_ref[0]
    a = _dot(h, w1_ref[0])
    b = _dot(h, w3_ref[0])
    lane = lax.broadcasted_iota(jnp.int32, (MOE_T, 128), 1)
    ge = jnp.sum(jnp.where(lane == e, gate_ref[0], 0.0), axis=-1, keepdims=True)
    hid = _silu(a) * b * ge
    acc_s[...] += _dot(hid.astype(BF16), w2_ref[0])

    @pl.when(e == N_EXPERTS - 1)
    def _():
        rowi = lax.broadcasted_iota(jnp.int32, (MOE_T, 1), 0) + half * MOE_T
        g2 = jnp.where(rowi < SEQ, modx_ref[0][:, 5 * D:], mody_ref[0][:, 5 * D:])
        o_ref[0] = z1_ref[0] + g2 * acc_s[...]


def _moe(z1, h2, gates, mods, w1, w3, w2):
    B = z1.shape[0]
    nh = S // MOE_T
    tok = lambda w_: pl.BlockSpec((1, MOE_T, w_), lambda b, t, e: (b, t, 0))
    return pl.pallas_call(
        _moe_kernel,
        grid=(B, nh, N_EXPERTS),
        in_specs=[tok(D), tok(D), tok(128),
                  pl.BlockSpec((1, 1, 6 * D), lambda b, t, e: (2 * b, 0, 0)),
                  pl.BlockSpec((1, 1, 6 * D), lambda b, t, e: (2 * b + 1, 0, 0)),
                  pl.BlockSpec((1, D, D_FF), lambda b, t, e: (e, 0, 0)),
                  pl.BlockSpec((1, D, D_FF), lambda b, t, e: (e, 0, 0)),
                  pl.BlockSpec((1, D_FF, D), lambda b, t, e: (e, 0, 0))],
        out_specs=tok(D),
        out_shape=jax.ShapeDtypeStruct((B, S, D), F32),
        scratch_shapes=[pltpu.VMEM((MOE_T, D), F32)],
        compiler_params=_cparams(("parallel", "parallel", "arbitrary")),
        name="moe",
    )(z1, h2, gates, mods, mods, w1, w3, w2)


def _in_proj_layout(w):
    cuts = np.cumsum([Q_RANK, KV_RANK, ROPE_D, ML_W, ML_W, 4 * ML_H])
    qc, ckv, kr, u, zz, g, na = jnp.split(w, [int(v) for v in cuts], axis=-1)
    zeros = lambda n: jnp.zeros((w.shape[0], n), w.dtype)
    out = jnp.concatenate([qc, ckv, zeros(NOPE_D), kr, zeros(HP - QK_D),
                           _pad_heads(u, ML_H, ML_D, HP), _pad_heads(zz, ML_H, ML_D, HP),
                           g, zeros(128 - 4 * ML_H), na], axis=-1)
    assert out.shape[-1] == NP_IN
    return out


def _pad_heads(v, nh, d, dp):
    lead = v.shape[:-1]
    v = v.reshape(lead + (nh, d))
    v = jnp.pad(v, [(0, 0)] * len(lead) + [(0, 0), (0, dp - d)])
    return v.reshape(lead + (nh * dp,))


def _rope_tables():
    t = np.arange(SEQ)
    row = (t // GRID_W).astype(np.float32)
    col = (t % GRID_W).astype(np.float32)
    quarter = ROPE_D // 4
    inv = jnp.asarray(ROPE_BASE, F32) ** (-jnp.arange(quarter, dtype=F32) / quarter)
    ar = jnp.asarray(row)[:, None] * inv
    ac = jnp.asarray(col)[:, None] * inv
    ang = jnp.concatenate([ar, ar, ac, ac], axis=-1)
    cos = jnp.ones((S, HP), F32).at[:SEQ, NOPE_D:QK_D].set(jnp.cos(ang))
    sin = jnp.zeros((S, HP), F32).at[:SEQ, NOPE_D:QK_D].set(jnp.sin(ang))
    return cos, sin


def _na_bias_index():
    idx = np.full((NA_NBLK, NA_QROWS, NA_KROWS), 2 * WIN_R - 1, np.int32)
    for blk in range(NA_NBLK):
        k0 = int(np.clip(blk * NA_QROWS - WIN_R // 2, 0, ROWS - NA_KROWS))
        for i in range(NA_QROWS):
            qr = blk * NA_QROWS + i
            rs = int(np.clip(qr - WIN_R // 2, 0, ROWS - WIN_R))
            for jj in range(NA_KROWS):
                kr = k0 + jj
                if rs <= kr < rs + WIN_R:
                    idx[blk, i, jj] = kr - qr + WIN_R - 1
    return idx


def _na_bias(rpb):
    cq = np.arange(GRID_W)
    cs = np.clip(cq - WIN_C // 2, 0, GRID_W - WIN_C)
    col_ok = (cq[None, :] >= cs[:, None]) & (cq[None, :] < cs[:, None] + WIN_C)
    dc = np.clip(cq[None, :] - cq[:, None], -(WIN_C - 1), WIN_C - 1) + (WIN_C - 1)
    onehot = jnp.asarray(np.eye(2 * WIN_C - 1, dtype=np.float32)[dc])
    tiles = jnp.einsum('hrc,qkc->hrqk', rpb, onehot, precision=lax.Precision.HIGHEST)
    tiles = jnp.where(jnp.asarray(col_ok), tiles, NEG)
    tiles = jnp.concatenate([tiles, jnp.full((NA_H, 1, GRID_W, GRID_W), NEG, F32)], axis=1)
    b = tiles[:, jnp.asarray(_na_bias_index())]
    b = jnp.transpose(b, (1, 0, 2, 4, 3, 5)).reshape(NA_NBLK, NA_H, NA_QB, NA_KB)
    return b.astype(BF16)


def kernel(x, c, ctx, c_ctx, w_mod, b_mod, norm1_g, norm2_g, w_in, w_out, mla_qnorm_g, mla_w_uq, mla_kvnorm_g, mla_w_ukv, mla_q_g, mla_k_g, ml_conv_w, ml_conv_b, ml_w_q, ml_w_k, ml_w_v, ml_gate_b, ml_norm_g, ml_skip, na_q_g, na_k_g, na_rpb, router_w, router_b, moe_w1, moe_w3, moe_w2):
    B = x.shape[0]
    z = jnp.concatenate([x, ctx], axis=1)
    cc = jnp.zeros((16, D), F32).at[:B].set(c).at[B].set(c_ctx)
    mod_all = _modulation(cc, w_mod, b_mod)
    cos, sin = _rope_tables()
    seg = jnp.asarray(np.kron(np.eye(NA_H), np.ones((NA_D, NA_D))), BF16)
    tril = jnp.asarray(np.tril(np.ones((ML_CHUNK, ML_CHUNK))), F32)
    rw = jnp.pad(router_w, ((0, 0), (0, 128 - N_EXPERTS)))
    rb = jnp.pad(router_b, (0, 128 - N_EXPERTS), constant_values=NEG).reshape(1, 128)

    def pad_lanes(v, n):
        return jnp.pad(v, [(0, 0)] * (v.ndim - 1) + [(0, n - v.shape[-1])])

    for l in range(DEPTH):
        mx = mod_all[l, :B]
        my = jnp.broadcast_to(mod_all[l, B], (B, 6 * D))
        mods = jnp.stack([mx, my], axis=1).reshape(2 * B, 1, 6 * D)

        w_in_p = _in_proj_layout(w_in[l]).astype(BF16)
        pmla, pu, pg, pna = _in_proj(z, mods, norm1_g[l].reshape(1, D), w_in_p)

        wuq = pad_lanes(jnp.transpose(mla_w_uq[l].reshape(Q_RANK, MLA_H, QK_D), (1, 0, 2)), HP).astype(BF16)
        wukv = jnp.transpose(mla_w_ukv[l].reshape(KV_RANK, MLA_H, NOPE_D + V_D), (1, 0, 2))
        wuk = pad_lanes(wukv[..., :NOPE_D], HP).astype(BF16)
        wuv = wukv[..., NOPE_D:].astype(BF16)
        q, k, v = _mla_prep(pmla, cos, sin, mla_qnorm_g[l].reshape(1, Q_RANK), wuq,
                            mla_kvnorm_g[l].reshape(1, KV_RANK), wuk, wuv,
                            pad_lanes(mla_q_g[l].reshape(1, QK_D), HP), pad_lanes(mla_k_g[l].reshape(1, QK_D), HP))
        mla_o = _mla_attn(q, k, v)

        padh = lambda a: _pad_heads(a, ML_H, ML_D, HP)
        padw = lambda w_: jnp.pad(w_, ((0, 0), (0, HP - ML_D), (0, HP - ML_D))).astype(BF16)
        cw = jnp.pad(padh(ml_conv_w[l]), ((0, 8 - 3), (0, 0)))
        ml_o = _mlstm(pu, pg, cw, padh(ml_conv_b[l]).reshape(1, ML_WP),
                      padw(ml_w_q[l]), padw(ml_w_k[l] * (ML_D ** -0.5)), padw(ml_w_v[l]),
                      pad_lanes(ml_gate_b[l].reshape(1, 4 * ML_H), 128),
                      padh(ml_norm_g[l]).reshape(1, ML_WP), padh(ml_skip[l]).reshape(1, ML_WP), tril)

        na_o = _na_attn(pna, _na_bias(na_rpb[l]), jnp.tile(na_q_g[l], NA_H).reshape(1, NA_W),
                        jnp.tile(na_k_g[l], NA_H).reshape(1, NA_W), seg)

        wo = w_out[l]
        wa = wo[:MLA_H * V_D].astype(BF16)
        wm = jnp.pad(wo[MLA_H * V_D:MLA_H * V_D + ML_W].reshape(ML_H, ML_D, D),
                     ((0, 0), (0, HP - ML_D), (0, 0))).reshape(ML_WP, D).astype(BF16)
        wn = wo[MLA_H * V_D + ML_W:].astype(BF16)
        z1, h2, gates = _out_proj(z, mods, mla_o, ml_o, na_o, wa, wm, wn, norm2_g[l].reshape(1, D), rw, rb)
        z = _moe(z1, h2, gates, mods, moe_w1[l].astype(BF16), moe_w3[l].astype(BF16), moe_w2[l].astype(BF16))

    return z[:, :SEQ]
```

```python
import functools

import numpy as np
import jax
import jax.numpy as jnp
from jax import lax
from jax.experimental import pallas as pl
from jax.experimental.pallas import tpu as pltpu

F32 = jnp.float32
BF16 = jnp.bfloat16

D = 1024
SEQ = 2048
CTX = 256
S = SEQ + CTX
DEPTH = 4
GRID_W = 64
ROWS = SEQ // GRID_W
EPS = 1e-6

MLA_H = 6
Q_RANK = 256
KV_RANK = 128
NOPE_D = 64
ROPE_D = 32
V_D = 64
QK_D = NOPE_D + ROPE_D
ROPE_BASE = 10000.0

ML_H = 4
ML_D = 96
ML_W = ML_H * ML_D
HP = 128
ML_WP = ML_H * HP
ML_CHUNK = 128
N_CHUNK = S // ML_CHUNK
N_CTX_CHUNK = CTX // ML_CHUNK

NA_H = 4
NA_D = 64
NA_W = NA_H * NA_D
WIN_R = 8
WIN_C = 16
NA_QROWS = 8
NA_KROWS = 16
NA_QB = NA_QROWS * GRID_W
NA_KB = NA_KROWS * GRID_W
NA_NBLK = ROWS // NA_QROWS

N_EXPERTS = 16
N_GROUPS = 4
EPG = N_EXPERTS // N_GROUPS
D_FF = 256

TT = 256
NT = S // TT
MOE_T = 1152
NEG = -1e30

C_QC = 0
C_CKV = 256
C_KR = 384
C_U = 512
C_Z = C_U + ML_WP
C_G = C_Z + ML_WP
C_NA = C_G + 128
NP_IN = C_NA + 3 * NA_W

VMEM_LIMIT = 56 * 1024 * 1024


def _cparams(sem):
    return pltpu.CompilerParams(dimension_semantics=sem, vmem_limit_bytes=VMEM_LIMIT)


def _sigmoid(x):
    return 1.0 / (1.0 + jnp.exp(-x))


def _silu(x):
    return x * _sigmoid(x)


def _dot(a, b):
    return jnp.dot(a, b, preferred_element_type=F32)


def _dot_nt(a, b):
    return lax.dot_general(a, b, (((1,), (1,)), ((), ())), preferred_element_type=F32)


def _dot_tn(a, b):
    return lax.dot_general(a, b, (((0,), (0,)), ((), ())), preferred_element_type=F32)


def _dot_hi(a, b):
    return jnp.dot(a, b, preferred_element_type=F32, precision=lax.Precision.HIGHEST)


def _split_bf16(x, n):
    parts = []
    for _ in range(n):
        p = x.astype(BF16)
        parts.append(p)
        x = x - p.astype(F32)
    return parts


def _mod_rows(mod_ref, t):
    m = mod_ref[0]
    return [m[:, i * D:(i + 1) * D] for i in range(6)]


def _mod_kernel(c_ref, w_ref, b_ref, o_ref):
    sc = _silu(c_ref[...])
    o_ref[0] = _dot_hi(sc, w_ref[0]) + b_ref[0]


def _modulation(cc, w_mod, b_mod):
    nc = 6
    return pl.pallas_call(
        _mod_kernel,
        grid=(DEPTH, nc),
        in_specs=[pl.BlockSpec((16, D), lambda l, j: (0, 0)),
                  pl.BlockSpec((1, D, D), lambda l, j: (l, 0, j)),
                  pl.BlockSpec((1, 1, D), lambda l, j: (l, 0, j))],
        out_specs=pl.BlockSpec((1, 16, D), lambda l, j: (l, 0, j)),
        out_shape=jax.ShapeDtypeStruct((DEPTH, 16, 6 * D), F32),
        compiler_params=_cparams(("parallel", "parallel")),
        name="modulation",
    )(cc, w_mod, b_mod.reshape(DEPTH, 1, 6 * D))


def _in_proj_kernel(z_ref, mod_ref, g_ref, w_ref, pmla_ref, pu_ref, pg_ref, pna_ref):
    sh1, sc1 = _mod_rows(mod_ref, None)[:2]
    x = z_ref[0]
    xn = x * lax.rsqrt(jnp.mean(x * x, axis=-1, keepdims=True) + EPS) * g_ref[...]
    xn = xn * (1.0 + sc1) + sh1
    p = _dot(xn.astype(BF16), w_ref[...])
    pmla_ref[0] = p[:, :C_U].astype(BF16)
    pu_ref[0] = p[:, C_U:C_G].astype(BF16)
    pg_ref[0] = p[:, C_G:C_NA]
    pna_ref[0] = p[:, C_NA:].astype(BF16)


def _mod_spec():
    return pl.BlockSpec((1, 1, 6 * D), lambda b, t: (2 * b + t // (NT - 1), 0, 0))


def _in_proj(z, mods, g, w):
    B = z.shape[0]
    tok = lambda w_: pl.BlockSpec((1, TT, w_), lambda b, t: (b, t, 0))
    return pl.pallas_call(
        _in_proj_kernel,
        grid=(B, NT),
        in_specs=[tok(D), _mod_spec(),
                  pl.BlockSpec((1, D), lambda b, t: (0, 0)),
                  pl.BlockSpec((D, NP_IN), lambda b, t: (0, 0))],
        out_specs=[tok(C_U), tok(2 * ML_WP), tok(128), tok(3 * NA_W)],
        out_shape=[jax.ShapeDtypeStruct((B, S, C_U), BF16),
                   jax.ShapeDtypeStruct((B, S, 2 * ML_WP), BF16),
                   jax.ShapeDtypeStruct((B, S, 128), F32),
                   jax.ShapeDtypeStruct((B, S, 3 * NA_W), BF16)],
        compiler_params=_cparams(("parallel", "parallel")),
        name="in_proj",
    )(z, mods, g, w)


def _mla_prep_kernel(p_ref, cos_ref, sin_ref, qng_ref, wuq_ref, kvng_ref, wuk_ref, wuv_ref,
                     qg_ref, kg_ref, q_out, k_out, v_out):
    p = p_ref[0].astype(F32)
    qc = p[:, C_QC:C_CKV]
    ckv = p[:, C_CKV:C_KR]
    kr = p[:, C_KR:C_U]
    qcn = (qc * lax.rsqrt(jnp.mean(qc * qc, axis=-1, keepdims=True) + EPS) * qng_ref[...]).astype(BF16)
    ckvn = (ckv * lax.rsqrt(jnp.mean(ckv * ckv, axis=-1, keepdims=True) + EPS) * kvng_ref[...]).astype(BF16)
    cos = cos_ref[...]
    sin = sin_ref[...]
    lane = lax.broadcasted_iota(jnp.int32, (TT, HP), 1)
    first_half = (lane & (ROPE_D // 4)) == 0

    def norm_rope(x, g):
        r = lax.rsqrt(jnp.sum(x * x, axis=-1, keepdims=True) * (1.0 / QK_D) + EPS)
        xn = x * r * g
        rot = jnp.where(first_half, -pltpu.roll(xn, HP - ROPE_D // 4, 1), pltpu.roll(xn, ROPE_D // 4, 1))
        return xn * cos + rot * sin

    scale = float(QK_D ** -0.5 * np.log2(np.e))
    for h in range(MLA_H):
        qh = _dot(qcn, wuq_ref[h])
        q_out[0, h] = (norm_rope(qh, qg_ref[...]) * scale).astype(BF16)
        kh = _dot(ckvn, wuk_ref[h]) + kr
        k_out[0, h] = norm_rope(kh, kg_ref[...]).astype(BF16)
        v_out[0, h] = jnp.where(lane < V_D, _dot(ckvn, wuv_ref[h]), 1.0).astype(BF16)


def _mla_prep(pmla, cos, sin, qng, wuq, kvng, wuk, wuv, qg, kg):
    B = pmla.shape[0]
    full = lambda a: pl.BlockSpec(a.shape, lambda b, t, _n=a.ndim: (0,) * _n)
    hd = lambda w_: pl.BlockSpec((1, MLA_H, TT, w_), lambda b, t: (b, 0, t, 0))
    return pl.pallas_call(
        _mla_prep_kernel,
        grid=(B, NT),
        in_specs=[pl.BlockSpec((1, TT, C_U), lambda b, t: (b, t, 0)),
                  pl.BlockSpec((TT, HP), lambda b, t: (t, 0)),
                  pl.BlockSpec((TT, HP), lambda b, t: (t, 0)),
                  full(qng), full(wuq), full(kvng), full(wuk), full(wuv), full(qg), full(kg)],
        out_specs=[hd(HP), hd(HP), hd(HP)],
        out_shape=[jax.ShapeDtypeStruct((B, MLA_H, S, HP), BF16),
                   jax.ShapeDtypeStruct((B, MLA_H, S, HP), BF16),
                   jax.ShapeDtypeStruct((B, MLA_H, S, HP), BF16)],
        compiler_params=_cparams(("parallel", "parallel")),
        name="mla_prep",
    )(pmla, cos, sin, qng, wuq, kvng, wuk, wuv, qg, kg)


def _mla_attn_kernel(q_ref, k_ref, v_ref, o_ref):
    t = pl.program_id(1)

    def attend(k_lo, k_n):
        outs = []
        for h in range(MLA_H):
            s = _dot_nt(q_ref[0, h], k_ref[0, h, k_lo:k_lo + k_n, :])
            m = jnp.max(s, axis=-1, keepdims=True)
            p = jnp.exp2(s - m)
            pv = _dot(p.astype(BF16), v_ref[0, h, k_lo:k_lo + k_n, :])
            outs.append(pv[:, :V_D] / pv[:, V_D:V_D + 1])
        o_ref[0] = jnp.concatenate(outs, axis=-1).astype(BF16)

    @pl.when(t < NT - 1)
    def _():
        attend(0, S)

    @pl.when(t == NT - 1)
    def _():
        attend(SEQ, CTX)


def _mla_attn(q, k, v):
    B = q.shape[0]
    return pl.pallas_call(
        _mla_attn_kernel,
        grid=(B, NT),
        in_specs=[pl.BlockSpec((1, MLA_H, TT, HP), lambda b, t: (b, 0, t, 0)),
                  pl.BlockSpec((1, MLA_H, S, HP), lambda b, t: (b, 0, 0, 0)),
                  pl.BlockSpec((1, MLA_H, S, HP), lambda b, t: (b, 0, 0, 0))],
        out_specs=pl.BlockSpec((1, TT, MLA_H * V_D), lambda b, t: (b, t, 0)),
        out_shape=jax.ShapeDtypeStruct((B, S, MLA_H * V_D), BF16),
        compiler_params=_cparams(("parallel", "arbitrary")),
        name="mla_attn",
    )(q, k, v)


def _log_sigmoid(x):
    return jnp.minimum(x, 0.0) - jnp.log(1.0 + jnp.exp(-jnp.abs(x)))


def _mlstm_kernel(pu_ref, pg_ref, cw_ref, cb_ref, wq_ref, wk_ref, wv_ref, gb_ref, ng_ref, sk_ref,
                  tril_ref, o_ref, uc_s, q_s, kt_s, v_s, h_s, c_s, m_s, pm_s, b_s, rt_s):
    CA = 2 * ML_CHUNK
    row = lax.broadcasted_iota(jnp.int32, (CA, ML_WP), 0)

    def conv_body(i, carry):
        r0 = pl.multiple_of(i * CA, CA)
        x = pu_ref[0, pl.ds(r0, CA), 0:ML_WP].astype(F32)
        pr = pl.multiple_of(jnp.maximum(r0 - 16, 0), 16)
        nx = pl.multiple_of(jnp.minimum(r0 + CA, S - 16), 16)
        prev = pu_ref[0, pl.ds(pr, 16), 0:ML_WP].astype(F32)[15:16, :]
        nxt = pu_ref[0, pl.ds(nx, 16), 0:ML_WP].astype(F32)[0:1, :]
        seq_start = jnp.logical_or(r0 == 0, r0 == SEQ)
        seq_end = jnp.logical_or(r0 + CA == SEQ, r0 + CA == S)
        prev = jnp.where(seq_start, 0.0, prev)
        nxt = jnp.where(seq_end, 0.0, nxt)
        up = jnp.where(row == 0, prev, pltpu.roll(x, 1, 0))
        dn = jnp.where(row == CA - 1, nxt, pltpu.roll(x, CA - 1, 0))
        uc = _silu(cw_ref[0:1, :] * up + cw_ref[1:2, :] * x + cw_ref[2:3, :] * dn + cb_ref[...])
        ucb = uc.astype(BF16)
        uc_s[pl.ds(r0, CA), :] = ucb
        xb = x.astype(BF16)
        for h in range(ML_H):
            sl = slice(h * HP, (h + 1) * HP)
            q_s[pl.ds(r0, CA), sl] = _dot(ucb[:, sl], wq_ref[h]).astype(BF16)
            kt = _dot_nt(wk_ref[h], ucb[:, sl])
            kt_s[2 * i, sl, :] = kt[:, :ML_CHUNK].astype(BF16)
            kt_s[2 * i + 1, sl, :] = kt[:, ML_CHUNK:].astype(BF16)
            v_s[pl.ds(r0, CA), 2 * h * HP:(2 * h + 1) * HP] = _dot(xb[:, sl], wv_ref[h]).astype(BF16)
            v_s[pl.ds(r0, CA), (2 * h + 1) * HP:(2 * h + 2) * HP] = jnp.ones((CA, HP), BF16)

        for half in range(2):
            rows = pl.ds(r0 + half * ML_CHUNK, ML_CHUNK)
            g = pg_ref[0, rows, :] + gb_ref[...]
            parts = _split_bf16(_log_sigmoid(g), 3)
            cum_f = sum(_dot(tril_ref[0], part) for part in parts)
            cum_b = sum(_dot(tril_ref[1], part) for part in parts)
            bsh = pltpu.roll(jnp.where(bwd_lane, cum_b, cum_f), 128 - n_chain, 1)
            r = g - bsh
            pf = r
            pb = r
            k = 1
            while k < ML_CHUNK:
                pf = jnp.maximum(pf, jnp.where(ti >= k, pltpu.roll(pf, k, 0), NEG))
                pb = jnp.maximum(pb, jnp.where(ti < ML_CHUNK - k, pltpu.roll(pb, ML_CHUNK - k, 0), NEG))
                k *= 2
            pm_s[2 * i + half] = jnp.where(bwd_lane, pb, pf)
            b_s[2 * i + half] = bsh
            rt_s[2 * i + half] = r.T[0:n_chain, :]
        return carry

    n_chain = 2 * ML_H
    ti = lax.broadcasted_iota(jnp.int32, (ML_CHUNK, ML_CHUNK), 0)
    si = lax.broadcasted_iota(jnp.int32, (ML_CHUNK, ML_CHUNK), 1)
    bwd_lane = (si % n_chain) >= ML_H
    lax.fori_loop(0, S // CA, conv_body, 0)

    c_s[...] = jnp.zeros_like(c_s)
    m_s[...] = jnp.zeros_like(m_s)
    masks = (si <= ti, si >= ti)

    def scan_body(j, carry):
        chunk = (jnp.where(j < N_CTX_CHUNK, j + N_CHUNK - N_CTX_CHUNK, j - N_CTX_CHUNK), N_CHUNK - 1 - j)
        chains = []
        for d in range(2):
            r0 = pl.multiple_of(chunk[d] * ML_CHUNK, ML_CHUNK)
            p_col = pm_s[chunk[d]]
            bsh = b_s[chunk[d]]
            r_t = rt_s[chunk[d]]
            end = ML_CHUNK - 1 if d == 0 else 0
            for h in range(ML_H):
                c = d * ML_H + h
                sl = slice(h * HP, (h + 1) * HP)
                qc = q_s[pl.ds(r0, ML_CHUNK), sl]
                kt = kt_s[chunk[d], sl, :]
                vx = v_s[pl.ds(r0, ML_CHUNK), 2 * h * HP:(2 * h + 2) * HP]
                r_row = r_t[c:c + 1, :]
                m = m_s[c]
                st = c_s[c]
                big_m = jnp.maximum(m, jnp.broadcast_to(p_col[:, c:c + 1], (ML_CHUNK, HP)))
                b_b = jnp.broadcast_to(bsh[:, c:c + 1], (ML_CHUNK, HP))
                m_end = big_m[end:end + 1, :]
                ktw = (kt.astype(F32) * jnp.exp(r_row - m_end)).astype(BF16)
                chains.append(dict(d=d, r0=r0, sl=sl, c=c, vx=vx, r_row=r_row, m=m, st=st, big_m=big_m,
                                   b_b=b_b, m_end=m_end, end=end,
                                   qk=_dot(qc, kt), inter=_dot(qc, st.astype(BF16)), upd=_dot(ktw, vx)))
        for ch in chains:
            dw = jnp.exp(jnp.where(masks[ch["d"]], ch["r_row"] - ch["big_m"], NEG))
            ch["intra"] = _dot((ch["qk"] * dw).astype(BF16), ch["vx"])
        for ch in chains:
            m, big_m, inter, intra, end = ch["m"], ch["big_m"], ch["inter"], ch["intra"], ch["end"]
            iw = jnp.exp(m - big_m)
            num = iw * inter[:, :HP] + intra[:, :HP]
            nq = iw * inter[:, HP:] + intra[:, HP:]
            hv = num / jnp.maximum(jnp.abs(nq), jnp.exp(-(ch["b_b"] + big_m)))
            a = jnp.exp(m - ch["m_end"])
            ch["out"] = (hv, jnp.concatenate([a, a], axis=1) * ch["st"] + ch["upd"],
                         ch["b_b"][end:end + 1, :] + ch["m_end"])
        for ch in chains:
            hv, st_new, m_new = ch["out"]
            h_s[ch["d"], pl.ds(ch["r0"], ML_CHUNK), ch["sl"]] = hv
            c_s[ch["c"]] = st_new
            m_s[ch["c"]] = m_new
        return carry

    lax.fori_loop(0, N_CHUNK, scan_body, 0)

    live = (lax.broadcasted_iota(jnp.int32, (CA, HP), 1) < ML_D).astype(F32)

    def out_body(i, carry):
        r0 = pl.multiple_of(i * CA, CA)
        for h in range(ML_H):
            sl = slice(h * HP, (h + 1) * HP)
            hh = h_s[0, pl.ds(r0, CA), sl] + h_s[1, pl.ds(r0, CA), sl]
            mu = jnp.sum(hh, axis=-1, keepdims=True) * (1.0 / ML_D)
            dv = (hh - mu) * live
            var = jnp.sum(dv * dv, axis=-1, keepdims=True) * (1.0 / ML_D)
            hn = dv * lax.rsqrt(var + EPS) * ng_ref[:, sl]
            uc = uc_s[pl.ds(r0, CA), sl].astype(F32)
            zz = pu_ref[0, pl.ds(r0, CA), ML_WP + h * HP:ML_WP + (h + 1) * HP].astype(F32)
            o_ref[0, pl.ds(r0, CA), sl] = ((hn + sk_ref[:, sl] * uc) * _silu(zz)).astype(BF16)
        return carry

    lax.fori_loop(0, S // CA, out_body, 0)


def _mlstm(pu, pg, cw, cb, wq, wk, wv, gb, ng, sk, tril):
    B = pu.shape[0]
    full = lambda a: pl.BlockSpec(a.shape, lambda b, _n=a.ndim: (0,) * _n)
    n_chain = 2 * ML_H
    return pl.pallas_call(
        _mlstm_kernel,
        grid=(B,),
        in_specs=[pl.BlockSpec((1, S, 2 * ML_WP), lambda b: (b, 0, 0)),
                  pl.BlockSpec((1, S, 128), lambda b: (b, 0, 0)),
                  full(cw), full(cb), full(wq), full(wk), full(wv), full(gb), full(ng), full(sk), full(tril)],
        out_specs=pl.BlockSpec((1, S, ML_WP), lambda b: (b, 0, 0)),
        out_shape=jax.ShapeDtypeStruct((B, S, ML_WP), BF16),
        scratch_shapes=[pltpu.VMEM((S, ML_WP), BF16), pltpu.VMEM((S, ML_WP), BF16),
                        pltpu.VMEM((N_CHUNK, ML_WP, ML_CHUNK), BF16), pltpu.VMEM((S, 2 * ML_WP), BF16),
                        pltpu.VMEM((2, S, ML_WP), F32),
                        pltpu.VMEM((n_chain, HP, 2 * HP), F32),
                        pltpu.VMEM((n_chain, 1, HP), F32),
                        pltpu.VMEM((N_CHUNK, ML_CHUNK, 128), F32),
                        pltpu.VMEM((N_CHUNK, ML_CHUNK, 128), F32),
                        pltpu.VMEM((N_CHUNK, n_chain, ML_CHUNK), F32)],
        compiler_params=_cparams(("parallel",)),
        name="mlstm",
    )(pu, pg, cw, cb, wq, wk, wv, gb, ng, sk, tril)


def _na_kernel(idx_ref, p_ref, pt_ref, qg_ref, kg_ref, seg_ref, o_ref, kn_s, bias_s):
    j = pl.program_id(1)
    seg = seg_ref[...]

    def headnorm(x, g):
        ss = _dot((x * x).astype(BF16), seg)
        return x * lax.rsqrt(ss * (1.0 / NA_D) + EPS) * g

    @pl.when(j == 0)
    def _():
        def body(i, carry):
            r0 = pl.multiple_of(i * TT, TT)
            kk = p_ref[0, pl.ds(r0, TT), NA_W:2 * NA_W].astype(F32)
            kn_s[pl.ds(r0, TT), :] = headnorm(kk, kg_ref[...]).astype(BF16)
            return carry
        lax.fori_loop(0, NT, body, 0)

    scale = NA_D ** -0.5
    kctx = kn_s[SEQ:S, :]
    vctx = p_ref[0, SEQ:S, 2 * NA_W:3 * NA_W]

    @pl.when(j < NA_NBLK)
    def _():
        q0 = pl.multiple_of(j * NA_QB, NA_QB)
        k0 = pl.multiple_of(jnp.clip(j * NA_QROWS - WIN_R // 2, 0, ROWS - NA_KROWS) * GRID_W, 256)
        q = headnorm(p_ref[0, pl.ds(q0, NA_QB), 0:NA_W].astype(F32), qg_ref[...]) * scale
        kl = kn_s[pl.ds(k0, NA_KB), :]
        vl = p_ref[0, pl.ds(k0, NA_KB), 2 * NA_W:3 * NA_W]
        head = lax.broadcasted_iota(jnp.int32, (NA_QB, NA_W), 1) // NA_D
        acc = jnp.zeros((NA_QB, NA_W), F32)
        for h in range(NA_H):
            qm = jnp.where(head == h, q, 0.0).astype(BF16)
            for i in range(NA_QROWS):
                for p in range(NA_KROWS // 2):
                    code = idx_ref[(j * NA_QROWS + i) * (NA_KROWS // 2) + p]
                    bias_s[i * GRID_W:(i + 1) * GRID_W, p * 2 * GRID_W:(p + 1) * 2 * GRID_W] = pt_ref[h, code]
            s1 = _dot_nt(qm, kl) + bias_s[...].astype(F32)
            s2 = _dot_nt(qm, kctx)
            m = jnp.maximum(jnp.max(s1, axis=-1, keepdims=True), jnp.max(s2, axis=-1, keepdims=True))
            p1 = jnp.exp(s1 - m)
            p2 = jnp.exp(s2 - m)
            l = jnp.sum(p1, axis=-1, keepdims=True) + jnp.sum(p2, axis=-1, keepdims=True)
            o = (_dot(p1.astype(BF16), vl) + _dot(p2.astype(BF16), vctx)) / l
            acc = jnp.where(head == h, o, acc)
        o_ref[0, pl.ds(q0, NA_QB), :] = acc.astype(BF16)

    @pl.when(j == NA_NBLK)
    def _():
        q = headnorm(p_ref[0, SEQ:S, 0:NA_W].astype(F32), qg_ref[...]) * scale
        head = lax.broadcasted_iota(jnp.int32, (CTX, NA_W), 1) // NA_D
        acc = jnp.zeros((CTX, NA_W), F32)
        for h in range(NA_H):
            qm = jnp.where(head == h, q, 0.0).astype(BF16)
            s2 = _dot_nt(qm, kctx)
            m = jnp.max(s2, axis=-1, keepdims=True)
            p2 = jnp.exp(s2 - m)
            l = jnp.sum(p2, axis=-1, keepdims=True)
            o = _dot(p2.astype(BF16), vctx) / l
            acc = jnp.where(head == h, o, acc)
        o_ref[0, SEQ:S, :] = acc.astype(BF16)


def _na_attn(pair_idx, pna, pair_tiles, qg, kg, seg):
    B = pna.shape[0]
    return pl.pallas_call(
        _na_kernel,
        grid_spec=pltpu.PrefetchScalarGridSpec(
            num_scalar_prefetch=1,
            grid=(B, NA_NBLK + 1),
            in_specs=[pl.BlockSpec((1, S, 3 * NA_W), lambda b, j, idx: (b, 0, 0)),
                      pl.BlockSpec(pair_tiles.shape, lambda b, j, idx: (0, 0, 0, 0)),
                      pl.BlockSpec((1, NA_W), lambda b, j, idx: (0, 0)),
                      pl.BlockSpec((1, NA_W), lambda b, j, idx: (0, 0)),
                      pl.BlockSpec((NA_W, NA_W), lambda b, j, idx: (0, 0))],
            out_specs=pl.BlockSpec((1, S, NA_W), lambda b, j, idx: (b, 0, 0)),
            scratch_shapes=[pltpu.VMEM((S, NA_W), BF16), pltpu.VMEM((NA_QB, NA_KB), BF16)]),
        out_shape=jax.ShapeDtypeStruct((B, S, NA_W), BF16),
        compiler_params=_cparams(("parallel", "arbitrary")),
        name="na_attn",
    )(pair_idx, pna, pair_tiles, qg, kg, seg)


def _out_proj_kernel(z_ref, mod_ref, a_ref, m_ref, n_ref, wa_ref, wm_ref, wn_ref, g2_ref, rw_ref, rb_ref,
                     z1_ref, h2_ref, gate_ref):
    mods = _mod_rows(mod_ref, None)
    g1, sh2, sc2 = mods[2], mods[3], mods[4]
    mix = _dot(a_ref[0], wa_ref[...]) + _dot(m_ref[0], wm_ref[...]) + _dot(n_ref[0], wn_ref[...])
    x = z_ref[0] + g1 * mix
    z1_ref[0] = x
    hn = x * lax.rsqrt(jnp.mean(x * x, axis=-1, keepdims=True) + EPS) * g2_ref[...]
    hn = hn * (1.0 + sc2) + sh2
    h2_ref[0] = hn.astype(BF16)

    lane = lax.broadcasted_iota(jnp.int32, (TT, 128), 1)
    live = lane < N_EXPERTS
    h_hi, h_lo = _split_bf16(hn, 2)
    aff = _sigmoid(_dot(h_hi, rw_ref[0]) + (_dot(h_hi, rw_ref[1]) + _dot(h_lo, rw_ref[0])))
    sel = aff + rb_ref[...]

    def cyc(x, k, width):
        fwd = pltpu.roll(x, 128 - k, 1)
        back = pltpu.roll(x, width - k, 1)
        return jnp.where((lane % width) + k < width, fwd, back)

    def rank(x, width, step):
        r = jnp.zeros((TT, 128), F32)
        for k in range(1, width // step):
            y = cyc(x, k * step, width)
            wrapped = (lane % width) + k * step >= width
            beats = jnp.logical_or(y > x, jnp.logical_and(y == x, wrapped))
            r = r + beats.astype(F32)
        return r

    top2 = rank(sel, EPG, 1) < 2.0
    part = jnp.where(top2, sel, 0.0)
    gscore = part
    for k in range(1, EPG):
        gscore = gscore + cyc(part, k, EPG)
    best = rank(gscore, N_EXPERTS, EPG) < 1.0
    chosen = jnp.logical_and(jnp.logical_and(top2, best), live)
    w = jnp.where(chosen, aff, 0.0)
    gate_ref[0] = w / jnp.sum(w, axis=-1, keepdims=True)


def _out_proj(z, mods, mla_o, ml_o, na_o, wa, wm, wn, g2, rw, rb):
    B = z.shape[0]
    tok = lambda w_: pl.BlockSpec((1, TT, w_), lambda b, t: (b, t, 0))
    full = lambda a: pl.BlockSpec(a.shape, lambda b, t, _n=a.ndim: (0,) * _n)
    return pl.pallas_call(
        _out_proj_kernel,
        grid=(B, NT),
        in_specs=[tok(D), _mod_spec(), tok(MLA_H * V_D), tok(ML_WP), tok(NA_W),
                  full(wa), full(wm), full(wn), full(g2), full(rw), full(rb)],
        out_specs=[tok(D), tok(D), tok(128)],
        out_shape=[jax.ShapeDtypeStruct((B, S, D), F32),
                   jax.ShapeDtypeStruct((B, S, D), BF16),
                   jax.ShapeDtypeStruct((B, S, 128), F32)],
        compiler_params=_cparams(("parallel", "parallel")),
        name="out_proj",
    )(z, mods, mla_o, ml_o, na_o, wa, wm, wn, g2, rw, rb)


def _moe_kernel(z1_ref, h2_ref, gate_ref, modx_ref, mody_ref, w1_ref, w3_ref, w2_ref, o_ref, acc_s):
    e = pl.program_id(2)
    half = pl.program_id(1)

    @pl.when(e == 0)
    def _():
        acc_s[...] = jnp.zeros_like(acc_s)

    h = h2_ref[0]
    a = _dot(h, w1_ref[0, 0].astype(BF16))
    b = _dot(h, w3_ref[0, 0].astype(BF16))
    lane = lax.broadcasted_iota(jnp.int32, (MOE_T, 128), 1)
    ge = jnp.sum(jnp.where(lane == e, gate_ref[0], 0.0), axis=-1, keepdims=True)
    hid = _silu(a) * b * ge
    acc_s[...] += _dot(hid.astype(BF16), w2_ref[0, 0].astype(BF16))

    @pl.when(e == N_EXPERTS - 1)
    def _():
        rowi = lax.broadcasted_iota(jnp.int32, (MOE_T, 1), 0) + half * MOE_T
        g2 = jnp.where(rowi < SEQ, modx_ref[0][:, 5 * D:], mody_ref[0][:, 5 * D:])
        o_ref[0] = z1_ref[0] + g2 * acc_s[...]


def _moe(z1, h2, gates, mods, w1, w3, w2, l):
    B = z1.shape[0]
    nh = S // MOE_T
    tok = lambda w_: pl.BlockSpec((1, MOE_T, w_), lambda b, t, e: (b, t, 0))
    return pl.pallas_call(
        _moe_kernel,
        grid=(B, nh, N_EXPERTS),
        in_specs=[tok(D), tok(D), tok(128),
                  pl.BlockSpec((1, 1, 6 * D), lambda b, t, e: (2 * b, 0, 0)),
                  pl.BlockSpec((1, 1, 6 * D), lambda b, t, e: (2 * b + 1, 0, 0)),
                  pl.BlockSpec((1, 1, D, D_FF), lambda b, t, e: (l, e, 0, 0)),
                  pl.BlockSpec((1, 1, D, D_FF), lambda b, t, e: (l, e, 0, 0)),
                  pl.BlockSpec((1, 1, D_FF, D), lambda b, t, e: (l, e, 0, 0))],
        out_specs=tok(D),
        out_shape=jax.ShapeDtypeStruct((B, S, D), F32),
        scratch_shapes=[pltpu.VMEM((MOE_T, D), F32)],
        compiler_params=_cparams(("parallel", "parallel", "arbitrary")),
        name="moe",
    )(z1, h2, gates, mods, mods, w1, w3, w2)


def _in_proj_layout(w):
    cuts = np.cumsum([Q_RANK, KV_RANK, ROPE_D, ML_W, ML_W, 4 * ML_H])
    qc, ckv, kr, u, zz, g, na = jnp.split(w, [int(v) for v in cuts], axis=-1)
    zeros = lambda n: jnp.zeros((w.shape[0], n), w.dtype)
    out = jnp.concatenate([qc, ckv, zeros(NOPE_D), kr, zeros(HP - QK_D),
                           _pad_heads(u, ML_H, ML_D, HP), _pad_heads(zz, ML_H, ML_D, HP),
                           _gate_order(g), zeros(128 - 4 * ML_H), na], axis=-1)
    assert out.shape[-1] == NP_IN
    return out


def _gate_order(g):
    i_f, f_f, i_b, f_b = jnp.split(g, 4, axis=-1)
    return jnp.concatenate([i_f, i_b, f_f, f_b], axis=-1)


def _pad_heads(v, nh, d, dp):
    lead = v.shape[:-1]
    v = v.reshape(lead + (nh, d))
    v = jnp.pad(v, [(0, 0)] * len(lead) + [(0, 0), (0, dp - d)])
    return v.reshape(lead + (nh * dp,))


def _rope_tables():
    t = np.arange(SEQ)
    row = (t // GRID_W).astype(np.float32)
    col = (t % GRID_W).astype(np.float32)
    quarter = ROPE_D // 4
    inv = jnp.asarray(ROPE_BASE, F32) ** (-jnp.arange(quarter, dtype=F32) / quarter)
    ar = jnp.asarray(row)[:, None] * inv
    ac = jnp.asarray(col)[:, None] * inv
    ang = jnp.concatenate([ar, ar, ac, ac], axis=-1)
    cos = jnp.ones((S, HP), F32).at[:SEQ, NOPE_D:QK_D].set(jnp.cos(ang))
    sin = jnp.zeros((S, HP), F32).at[:SEQ, NOPE_D:QK_D].set(jnp.sin(ang))
    return cos, sin


NA_NDR = 2 * WIN_R - 1
NA_NPAIR = 3 * NA_NDR


def _na_pair_index():
    idx = np.zeros((NA_NBLK, NA_QROWS, NA_KROWS // 2), np.int32)
    for blk in range(NA_NBLK):
        k0 = int(np.clip(blk * NA_QROWS - WIN_R // 2, 0, ROWS - NA_KROWS))
        for i in range(NA_QROWS):
            qr = blk * NA_QROWS + i
            rs = int(np.clip(qr - WIN_R // 2, 0, ROWS - WIN_R))
            for p in range(NA_KROWS // 2):
                kr = k0 + 2 * p
                dr = kr - qr + WIN_R - 1
                left = rs <= kr < rs + WIN_R
                right = rs <= kr + 1 < rs + WIN_R
                if left and right:
                    idx[blk, i, p] = 1 + dr
                elif left:
                    idx[blk, i, p] = NA_NDR + dr
                elif right:
                    idx[blk, i, p] = 2 * NA_NDR + dr + 1
    return idx.reshape(-1)


def _na_pair_tiles(rpb):
    cq = np.arange(GRID_W)
    cs = np.clip(cq - WIN_C // 2, 0, GRID_W - WIN_C)
    col_ok = (cq[None, :] >= cs[:, None]) & (cq[None, :] < cs[:, None] + WIN_C)
    dc = np.clip(cq[None, :] - cq[:, None], -(WIN_C - 1), WIN_C - 1) + (WIN_C - 1)
    onehot = jnp.asarray(np.eye(2 * WIN_C - 1, dtype=np.float32)[dc])
    tiles = jnp.einsum('hrc,qkc->hrqk', rpb, onehot, precision=lax.Precision.HIGHEST)
    tiles = jnp.where(jnp.asarray(col_ok), tiles, NEG)
    masked = jnp.full_like(tiles, NEG)
    both = jnp.concatenate([tiles[:, :-1], tiles[:, 1:]], axis=-1)
    left = jnp.concatenate([tiles, masked], axis=-1)
    right = jnp.concatenate([masked, tiles], axis=-1)
    none = jnp.concatenate([masked[:, :1], masked[:, :1]], axis=-1)
    out = jnp.concatenate([none, both, left, right], axis=1)
    assert out.shape[1] == NA_NPAIR
    return out.astype(BF16)


def kernel(x, c, ctx, c_ctx, w_mod, b_mod, norm1_g, norm2_g, w_in, w_out, mla_qnorm_g, mla_w_uq, mla_kvnorm_g, mla_w_ukv, mla_q_g, mla_k_g, ml_conv_w, ml_conv_b, ml_w_q, ml_w_k, ml_w_v, ml_gate_b, ml_norm_g, ml_skip, na_q_g, na_k_g, na_rpb, router_w, router_b, moe_w1, moe_w3, moe_w2):
    B = x.shape[0]
    z = jnp.concatenate([x, ctx], axis=1)
    cc = jnp.zeros((16, D), F32).at[:B].set(c).at[B].set(c_ctx)
    mod_all = _modulation(cc, w_mod, b_mod)
    cos, sin = _rope_tables()
    seg = jnp.asarray(np.kron(np.eye(NA_H), np.ones((NA_D, NA_D))), BF16)
    tril = jnp.asarray(np.stack([np.tril(np.ones((ML_CHUNK, ML_CHUNK))), np.triu(np.ones((ML_CHUNK, ML_CHUNK)))]), BF16)
    pair_idx = jnp.asarray(_na_pair_index())
    rw = jnp.stack(_split_bf16(jnp.pad(router_w, ((0, 0), (0, 128 - N_EXPERTS))), 2))
    rb = jnp.pad(router_b, (0, 128 - N_EXPERTS), constant_values=NEG).reshape(1, 128)

    def pad_lanes(v, n):
        return jnp.pad(v, [(0, 0)] * (v.ndim - 1) + [(0, n - v.shape[-1])])

    for l in range(DEPTH):
        mx = mod_all[l, :B]
        my = jnp.broadcast_to(mod_all[l, B], (B, 6 * D))
        mods = jnp.stack([mx, my], axis=1).reshape(2 * B, 1, 6 * D)

        w_in_p = _in_proj_layout(w_in[l]).astype(BF16)
        pmla, pu, pg, pna = _in_proj(z, mods, norm1_g[l].reshape(1, D), w_in_p)

        wuq = pad_lanes(jnp.transpose(mla_w_uq[l].reshape(Q_RANK, MLA_H, QK_D), (1, 0, 2)), HP).astype(BF16)
        wukv = jnp.transpose(mla_w_ukv[l].reshape(KV_RANK, MLA_H, NOPE_D + V_D), (1, 0, 2))
        wuk = pad_lanes(wukv[..., :NOPE_D], HP).astype(BF16)
        wuv = pad_lanes(wukv[..., NOPE_D:], HP).astype(BF16)
        q, k, v = _mla_prep(pmla, cos, sin, mla_qnorm_g[l].reshape(1, Q_RANK), wuq,
                            mla_kvnorm_g[l].reshape(1, KV_RANK), wuk, wuv,
                            pad_lanes(mla_q_g[l].reshape(1, QK_D), HP), pad_lanes(mla_k_g[l].reshape(1, QK_D), HP))
        mla_o = _mla_attn(q, k, v)

        padh = lambda a: _pad_heads(a, ML_H, ML_D, HP)
        padw = lambda w_: jnp.pad(w_, ((0, 0), (0, HP - ML_D), (0, HP - ML_D))).astype(BF16)
        cw = jnp.pad(padh(ml_conv_w[l]), ((0, 8 - 3), (0, 0)))
        ml_o = _mlstm(pu, pg, cw, padh(ml_conv_b[l]).reshape(1, ML_WP),
                      padw(ml_w_q[l]), padw(jnp.swapaxes(ml_w_k[l], 1, 2) * (ML_D ** -0.5)), padw(ml_w_v[l]),
                      pad_lanes(_gate_order(ml_gate_b[l]).reshape(1, 4 * ML_H), 128),
                      padh(ml_norm_g[l]).reshape(1, ML_WP), padh(ml_skip[l]).reshape(1, ML_WP), tril)

        na_o = _na_attn(pair_idx, pna, _na_pair_tiles(na_rpb[l]), jnp.tile(na_q_g[l], NA_H).reshape(1, NA_W),
                        jnp.tile(na_k_g[l], NA_H).reshape(1, NA_W), seg)

        wo = w_out[l]
        wa = wo[:MLA_H * V_D].astype(BF16)
        wm = jnp.pad(wo[MLA_H * V_D:MLA_H * V_D + ML_W].reshape(ML_H, ML_D, D),
                     ((0, 0), (0, HP - ML_D), (0, 0))).reshape(ML_WP, D).astype(BF16)
        wn = wo[MLA_H * V_D + ML_W:].astype(BF16)
        z1, h2, gates = _out_proj(z, mods, mla_o, ml_o, na_o, wa, wm, wn, norm2_g[l].reshape(1, D), rw, rb)
        z = _moe(z1, h2, gates, mods, moe_w1, moe_w3, moe_w2, l)

    return z[:, :SEQ]
```

```python
import functools

import numpy as np
import jax
import jax.numpy as jnp
from jax import lax
from jax.experimental import pallas as pl
from jax.experimental.pallas import tpu as pltpu

F32 = jnp.float32
BF16 = jnp.bfloat16

D = 1024
SEQ = 2048
CTX = 256
S = SEQ + CTX
DEPTH = 4
GRID_W = 64
ROWS = SEQ // GRID_W
EPS = 1e-6

MLA_H = 6
Q_RANK = 256
KV_RANK = 128
NOPE_D = 64
ROPE_D = 32
V_D = 64
QK_D = NOPE_D + ROPE_D
ROPE_BASE = 10000.0

ML_H = 4
ML_D = 96
ML_W = ML_H * ML_D
HP = 128
ML_WP = ML_H * HP
ML_CHUNK = 128
N_CHUNK = S // ML_CHUNK
N_CTX_CHUNK = CTX // ML_CHUNK

NA_H = 4
NA_D = 64
NA_W = NA_H * NA_D
WIN_R = 8
WIN_C = 16
NA_QROWS = 4
NA_KROWS = 12
NA_QB = NA_QROWS * GRID_W
NA_KB = NA_KROWS * GRID_W
NA_NBLK = ROWS // NA_QROWS

N_EXPERTS = 16
N_GROUPS = 4
EPG = N_EXPERTS // N_GROUPS
D_FF = 256

TT = 256
NT = S // TT
OT = 768
MOE_T = 1152
NEG = -1e30

C_QC = 0
C_CKV = 256
C_KR = 384
C_U = 512
C_Z = C_U + ML_WP
C_G = C_Z + ML_WP
C_NA = C_G + 128
NP_IN = C_NA + 3 * NA_W

VMEM_LIMIT = 56 * 1024 * 1024


def _cparams(sem):
    return pltpu.CompilerParams(dimension_semantics=sem, vmem_limit_bytes=VMEM_LIMIT)


def _sigmoid(x):
    return 1.0 / (1.0 + jnp.exp(-x))


def _silu(x):
    return x * _sigmoid(x)


def _dot(a, b):
    return jnp.dot(a, b, preferred_element_type=F32)


def _dot_nt(a, b):
    return lax.dot_general(a, b, (((1,), (1,)), ((), ())), preferred_element_type=F32)


def _dot_tn(a, b):
    return lax.dot_general(a, b, (((0,), (0,)), ((), ())), preferred_element_type=F32)


def _dot_hi(a, b):
    return jnp.dot(a, b, preferred_element_type=F32, precision=lax.Precision.HIGHEST)


def _split_bf16(x, n):
    parts = []
    for _ in range(n):
        p = x.astype(BF16)
        parts.append(p)
        x = x - p.astype(F32)
    return parts


def _mod_rows(mod_ref, t):
    m = mod_ref[0]
    return [m[:, i * D:(i + 1) * D] for i in range(6)]


def _mod_kernel(c_ref, w_ref, b_ref, o_ref):
    sc = _silu(c_ref[...])
    o_ref[0] = _dot_hi(sc, w_ref[0]) + b_ref[0]


def _modulation(cc, w_mod, b_mod):
    nc = 6
    return pl.pallas_call(
        _mod_kernel,
        grid=(DEPTH, nc),
        in_specs=[pl.BlockSpec((16, D), lambda l, j: (0, 0)),
                  pl.BlockSpec((1, D, D), lambda l, j: (l, 0, j)),
                  pl.BlockSpec((1, 1, D), lambda l, j: (l, 0, j))],
        out_specs=pl.BlockSpec((1, 16, D), lambda l, j: (l, 0, j)),
        out_shape=jax.ShapeDtypeStruct((DEPTH, 16, 6 * D), F32),
        compiler_params=_cparams(("parallel", "parallel")),
        name="modulation",
    )(cc, w_mod, b_mod.reshape(DEPTH, 1, 6 * D))


def _in_proj_kernel(z_ref, mod_ref, g_ref, w_ref, pmla_ref, pu_ref, pg_ref, pna_ref):
    sh1, sc1 = _mod_rows(mod_ref, None)[:2]
    x = z_ref[0]
    xn = x * lax.rsqrt(jnp.mean(x * x, axis=-1, keepdims=True) + EPS) * g_ref[...]
    xn = xn * (1.0 + sc1) + sh1
    p = _dot(xn.astype(BF16), w_ref[...])
    pmla_ref[0] = p[:, :C_U].astype(BF16)
    pu_ref[0] = p[:, C_U:C_G].astype(BF16)
    pg_ref[0] = p[:, C_G:C_NA]
    pna_ref[0] = p[:, C_NA:].astype(BF16)


def _mod_spec():
    return pl.BlockSpec((1, 1, 6 * D), lambda b, t: (2 * b + t // (NT - 1), 0, 0))


def _in_proj(z, mods, g, w):
    B = z.shape[0]
    tok = lambda w_: pl.BlockSpec((1, TT, w_), lambda b, t: (b, t, 0))
    return pl.pallas_call(
        _in_proj_kernel,
        grid=(B, NT),
        in_specs=[tok(D), _mod_spec(),
                  pl.BlockSpec((1, D), lambda b, t: (0, 0)),
                  pl.BlockSpec((D, NP_IN), lambda b, t: (0, 0))],
        out_specs=[tok(C_U), tok(2 * ML_WP), tok(128), tok(3 * NA_W)],
        out_shape=[jax.ShapeDtypeStruct((B, S, C_U), BF16),
                   jax.ShapeDtypeStruct((B, S, 2 * ML_WP), BF16),
                   jax.ShapeDtypeStruct((B, S, 128), F32),
                   jax.ShapeDtypeStruct((B, S, 3 * NA_W), BF16)],
        compiler_params=_cparams(("parallel", "parallel")),
        name="in_proj",
    )(z, mods, g, w)


def _mla_prep_kernel(p_ref, tab_ref, qng_ref, wuq_ref, kvng_ref, wuk_ref, wuv_ref, rot_ref, ones_ref,
                     q_out, k_out, v_out):
    p = p_ref[0].astype(F32)
    qc = p[:, C_QC:C_CKV]
    ckv = p[:, C_CKV:C_KR]
    kr = p[:, C_KR:C_U]
    qcn = (qc * lax.rsqrt(jnp.mean(qc * qc, axis=-1, keepdims=True) + EPS) * qng_ref[...]).astype(BF16)
    ckvn = (ckv * lax.rsqrt(jnp.mean(ckv * ckv, axis=-1, keepdims=True) + EPS) * kvng_ref[...]).astype(BF16)
    lane = lax.broadcasted_iota(jnp.int32, (TT, HP), 1)
    ones = ones_ref[...]
    kr_rot = _dot(kr.astype(BF16), rot_ref[...])

    q_all = _dot(qcn, wuq_ref[...])
    k_all = _dot(ckvn, wuk_ref[...])
    v_all = _dot(ckvn, wuv_ref[...])
    qs = [q_all[:, 2 * h * HP:(2 * h + 1) * HP] for h in range(MLA_H)]
    q_rots = [q_all[:, (2 * h + 1) * HP:(2 * h + 2) * HP] for h in range(MLA_H)]
    ks = [k_all[:, h * HP:(h + 1) * HP] + kr for h in range(MLA_H)]
    ss_q = [_dot((x * x).astype(BF16), ones) for x in qs]
    ss_k = [_dot((x * x).astype(BF16), ones) for x in ks]

    def norm_rope(x, x_rot, ss, cos_g, sin_g):
        return lax.rsqrt(ss * (1.0 / QK_D) + EPS) * (x * cos_g + x_rot * sin_g)

    for h in range(MLA_H):
        q_out[0, h] = norm_rope(qs[h], q_rots[h], ss_q[h], tab_ref[0], tab_ref[1]).astype(BF16)
        k_out[0, h] = norm_rope(ks[h], kr_rot, ss_k[h], tab_ref[2], tab_ref[3]).astype(BF16)
        v_out[0, h] = jnp.where(lane < V_D, v_all[:, h * HP:(h + 1) * HP], 1.0).astype(BF16)


def _mla_prep(pmla, tabs, qng, wuq, kvng, wuk, wuv, rot, ones):
    B = pmla.shape[0]
    full = lambda a: pl.BlockSpec(a.shape, lambda b, t, _n=a.ndim: (0,) * _n)
    hd = lambda w_: pl.BlockSpec((1, MLA_H, TT, w_), lambda b, t: (b, 0, t, 0))
    return pl.pallas_call(
        _mla_prep_kernel,
        grid=(B, NT),
        in_specs=[pl.BlockSpec((1, TT, C_U), lambda b, t: (b, t, 0)),
                  pl.BlockSpec((4, TT, HP), lambda b, t: (0, t, 0)),
                  full(qng), full(wuq), full(kvng), full(wuk), full(wuv), full(rot), full(ones)],
        out_specs=[hd(HP), hd(HP), hd(HP)],
        out_shape=[jax.ShapeDtypeStruct((B, MLA_H, S, HP), BF16),
                   jax.ShapeDtypeStruct((B, MLA_H, S, HP), BF16),
                   jax.ShapeDtypeStruct((B, MLA_H, S, HP), BF16)],
        compiler_params=_cparams(("parallel", "parallel")),
        name="mla_prep",
    )(pmla, tabs, qng, wuq, kvng, wuk, wuv, rot, ones)


def _mla_attn_kernel(q_ref, k_ref, v_ref, o_ref):
    t = pl.program_id(1)

    def attend(k_lo, k_n):
        def scores(h):
            return _dot_nt(q_ref[0, h], k_ref[0, h, k_lo:k_lo + k_n, :])

        outs = []
        s_next = scores(0)
        for h in range(MLA_H):
            s = s_next
            if h + 1 < MLA_H:
                s_next = scores(h + 1)
            m = jnp.max(s, axis=-1, keepdims=True)
            p = jnp.exp2(s - m)
            pv = _dot(p.astype(BF16), v_ref[0, h, k_lo:k_lo + k_n, :])
            outs.append(pv[:, :V_D] / pv[:, V_D:V_D + 1])
        o_ref[0] = jnp.concatenate(outs, axis=-1).astype(BF16)

    @pl.when(t < NT - 1)
    def _():
        attend(0, S)

    @pl.when(t == NT - 1)
    def _():
        attend(SEQ, CTX)


def _mla_attn(q, k, v):
    B = q.shape[0]
    return pl.pallas_call(
        _mla_attn_kernel,
        grid=(B, NT),
        in_specs=[pl.BlockSpec((1, MLA_H, TT, HP), lambda b, t: (b, 0, t, 0)),
                  pl.BlockSpec((1, MLA_H, S, HP), lambda b, t: (b, 0, 0, 0)),
                  pl.BlockSpec((1, MLA_H, S, HP), lambda b, t: (b, 0, 0, 0))],
        out_specs=pl.BlockSpec((1, TT, MLA_H * V_D), lambda b, t: (b, t, 0)),
        out_shape=jax.ShapeDtypeStruct((B, S, MLA_H * V_D), BF16),
        compiler_params=_cparams(("parallel", "arbitrary")),
        name="mla_attn",
    )(q, k, v)


def _log_sigmoid(x):
    return jnp.minimum(x, 0.0) - jnp.log(1.0 + jnp.exp(-jnp.abs(x)))


def _mlstm_kernel(pu_ref, pg_ref, cw_ref, cb_ref, wq_ref, wk_ref, wv_ref, gb_ref, ng_ref, sk_ref,
                  tril_ref, o_ref, uc_s, q_s, kt_s, v_s, h_s, c_s, m_s, pm_s, b_s, rt_s):
    CA = 2 * ML_CHUNK
    row = lax.broadcasted_iota(jnp.int32, (CA, ML_WP), 0)

    def conv_body(i, carry):
        r0 = pl.multiple_of(i * CA, CA)
        x = pu_ref[0, pl.ds(r0, CA), 0:ML_WP].astype(F32)
        pr = pl.multiple_of(jnp.maximum(r0 - 16, 0), 16)
        nx = pl.multiple_of(jnp.minimum(r0 + CA, S - 16), 16)
        prev = pu_ref[0, pl.ds(pr, 16), 0:ML_WP].astype(F32)[15:16, :]
        nxt = pu_ref[0, pl.ds(nx, 16), 0:ML_WP].astype(F32)[0:1, :]
        seq_start = jnp.logical_or(r0 == 0, r0 == SEQ)
        seq_end = jnp.logical_or(r0 + CA == SEQ, r0 + CA == S)
        prev = jnp.where(seq_start, 0.0, prev)
        nxt = jnp.where(seq_end, 0.0, nxt)
        up = jnp.where(row == 0, prev, pltpu.roll(x, 1, 0))
        dn = jnp.where(row == CA - 1, nxt, pltpu.roll(x, CA - 1, 0))
        uc = _silu(cw_ref[0:1, :] * up + cw_ref[1:2, :] * x + cw_ref[2:3, :] * dn + cb_ref[...])
        ucb = uc.astype(BF16)
        uc_s[pl.ds(r0, CA), :] = ucb
        xb = x.astype(BF16)
        for h in range(ML_H):
            sl = slice(h * HP, (h + 1) * HP)
            q_s[pl.ds(r0, CA), sl] = _dot(ucb[:, sl], wq_ref[h]).astype(BF16)
            kt = _dot_nt(wk_ref[h], ucb[:, sl])
            kt_s[2 * i, sl, :] = kt[:, :ML_CHUNK].astype(BF16)
            kt_s[2 * i + 1, sl, :] = kt[:, ML_CHUNK:].astype(BF16)
            v_s[pl.ds(r0, CA), 2 * h * HP:(2 * h + 1) * HP] = _dot(xb[:, sl], wv_ref[h]).astype(BF16)
            v_s[pl.ds(r0, CA), (2 * h + 1) * HP:(2 * h + 2) * HP] = jnp.ones((CA, HP), BF16)

        for half in range(2):
            rows = pl.ds(r0 + half * ML_CHUNK, ML_CHUNK)
            g = pg_ref[0, rows, :] + gb_ref[...]
            parts = _split_bf16(_log_sigmoid(g), 3)
            cum_f = sum(_dot(tril_ref[0], part) for part in parts)
            cum_b = sum(_dot(tril_ref[1], part) for part in parts)
            bsh = pltpu.roll(jnp.where(bwd_lane, cum_b, cum_f), 128 - n_chain, 1)
            r = g - bsh
            pf = r
            pb = r
            k = 1
            while k < ML_CHUNK:
                pf = jnp.maximum(pf, jnp.where(ti >= k, pltpu.roll(pf, k, 0), NEG))
                pb = jnp.maximum(pb, jnp.where(ti < ML_CHUNK - k, pltpu.roll(pb, ML_CHUNK - k, 0), NEG))
                k *= 2
            pm_s[2 * i + half] = jnp.where(bwd_lane, pb, pf)
            b_s[2 * i + half] = bsh
            rt_s[2 * i + half] = r.T[0:n_chain, :]
        return carry

    n_chain = 2 * ML_H
    ti = lax.broadcasted_iota(jnp.int32, (ML_CHUNK, ML_CHUNK), 0)
    si = lax.broadcasted_iota(jnp.int32, (ML_CHUNK, ML_CHUNK), 1)
    bwd_lane = (si % n_chain) >= ML_H
    lax.fori_loop(0, S // CA, conv_body, 0)

    c_s[...] = jnp.zeros_like(c_s)
    m_s[...] = jnp.zeros_like(m_s)
    masks = (si <= ti, si >= ti)

    def scan_body(j, carry):
        chunk = (jnp.where(j < N_CTX_CHUNK, j + N_CHUNK - N_CTX_CHUNK, j - N_CTX_CHUNK), N_CHUNK - 1 - j)
        chains = []
        for d in range(2):
            r0 = pl.multiple_of(chunk[d] * ML_CHUNK, ML_CHUNK)
            p_col = pm_s[chunk[d]]
            bsh = b_s[chunk[d]]
            r_t = rt_s[chunk[d]]
            end = ML_CHUNK - 1 if d == 0 else 0
            for h in range(ML_H):
                c = d * ML_H + h
                sl = slice(h * HP, (h + 1) * HP)
                qc = q_s[pl.ds(r0, ML_CHUNK), sl]
                kt = kt_s[chunk[d], sl, :]
                vx = v_s[pl.ds(r0, ML_CHUNK), 2 * h * HP:(2 * h + 2) * HP]
                r_row = r_t[c:c + 1, :]
                m = m_s[c]
                st = c_s[c]
                big_m = jnp.maximum(m, jnp.broadcast_to(p_col[:, c:c + 1], (ML_CHUNK, HP)))
                b_b = jnp.broadcast_to(bsh[:, c:c + 1], (ML_CHUNK, HP))
                m_end = big_m[end:end + 1, :]
                ktw = (kt.astype(F32) * jnp.exp(r_row - m_end)).astype(BF16)
                chains.append(dict(d=d, r0=r0, sl=sl, c=c, vx=vx, r_row=r_row, m=m, st=st, big_m=big_m,
                                   b_b=b_b, m_end=m_end, end=end,
                                   qk=_dot(qc, kt), inter=_dot(qc, st.astype(BF16)), upd=_dot(ktw, vx)))
        for ch in chains:
            dw = jnp.exp(jnp.where(masks[ch["d"]], ch["r_row"] - ch["big_m"], NEG))
            ch["intra"] = _dot((ch["qk"] * dw).astype(BF16), ch["vx"])
        for ch in chains:
            m, big_m, inter, intra, end = ch["m"], ch["big_m"], ch["inter"], ch["intra"], ch["end"]
            iw = jnp.exp(m - big_m)
            num = iw * inter[:, :HP] + intra[:, :HP]
            nq = iw * inter[:, HP:] + intra[:, HP:]
            hv = num / jnp.maximum(jnp.abs(nq), jnp.exp(-(ch["b_b"] + big_m)))
            a = jnp.exp(m - ch["m_end"])
            ch["out"] = (hv, jnp.concatenate([a, a], axis=1) * ch["st"] + ch["upd"],
                         ch["b_b"][end:end + 1, :] + ch["m_end"])
        for ch in chains:
            hv, st_new, m_new = ch["out"]
            h_s[ch["d"], pl.ds(ch["r0"], ML_CHUNK), ch["sl"]] = hv
            c_s[ch["c"]] = st_new
            m_s[ch["c"]] = m_new
        return carry

    lax.fori_loop(0, N_CHUNK, scan_body, 0)

    live = (lax.broadcasted_iota(jnp.int32, (CA, HP), 1) < ML_D).astype(F32)

    def out_body(i, carry):
        r0 = pl.multiple_of(i * CA, CA)
        for h in range(ML_H):
            sl = slice(h * HP, (h + 1) * HP)
            hh = h_s[0, pl.ds(r0, CA), sl] + h_s[1, pl.ds(r0, CA), sl]
            mu = jnp.sum(hh, axis=-1, keepdims=True) * (1.0 / ML_D)
            dv = (hh - mu) * live
            var = jnp.sum(dv * dv, axis=-1, keepdims=True) * (1.0 / ML_D)
            hn = dv * lax.rsqrt(var + EPS) * ng_ref[:, sl]
            uc = uc_s[pl.ds(r0, CA), sl].astype(F32)
            zz = pu_ref[0, pl.ds(r0, CA), ML_WP + h * HP:ML_WP + (h + 1) * HP].astype(F32)
            o_ref[0, pl.ds(r0, CA), sl] = ((hn + sk_ref[:, sl] * uc) * _silu(zz)).astype(BF16)
        return carry

    lax.fori_loop(0, S // CA, out_body, 0)


def _mlstm(pu, pg, cw, cb, wq, wk, wv, gb, ng, sk, tril):
    B = pu.shape[0]
    full = lambda a: pl.BlockSpec(a.shape, lambda b, _n=a.ndim: (0,) * _n)
    n_chain = 2 * ML_H
    return pl.pallas_call(
        _mlstm_kernel,
        grid=(B,),
        in_specs=[pl.BlockSpec((1, S, 2 * ML_WP), lambda b: (b, 0, 0)),
                  pl.BlockSpec((1, S, 128), lambda b: (b, 0, 0)),
                  full(cw), full(cb), full(wq), full(wk), full(wv), full(gb), full(ng), full(sk), full(tril)],
        out_specs=pl.BlockSpec((1, S, ML_WP), lambda b: (b, 0, 0)),
        out_shape=jax.ShapeDtypeStruct((B, S, ML_WP), BF16),
        scratch_shapes=[pltpu.VMEM((S, ML_WP), BF16), pltpu.VMEM((S, ML_WP), BF16),
                        pltpu.VMEM((N_CHUNK, ML_WP, ML_CHUNK), BF16), pltpu.VMEM((S, 2 * ML_WP), BF16),
                        pltpu.VMEM((2, S, ML_WP), F32),
                        pltpu.VMEM((n_chain, HP, 2 * HP), F32),
                        pltpu.VMEM((n_chain, 1, HP), F32),
                        pltpu.VMEM((N_CHUNK, ML_CHUNK, 128), F32),
                        pltpu.VMEM((N_CHUNK, ML_CHUNK, 128), F32),
                        pltpu.VMEM((N_CHUNK, n_chain, ML_CHUNK), F32)],
        compiler_params=_cparams(("parallel",)),
        name="mlstm",
    )(pu, pg, cw, cb, wq, wk, wv, gb, ng, sk, tril)


def _na_kernel(idx_ref, p_ref, pt_ref, qg_ref, kg_ref, seg_ref, o_ref, kn_s, bias_s):
    j = pl.program_id(1)
    seg = seg_ref[...]

    def headnorm(x, g):
        ss = _dot((x * x).astype(BF16), seg)
        return x * lax.rsqrt(ss * (1.0 / NA_D) + EPS) * g

    @pl.when(j == 0)
    def _():
        def body(i, carry):
            r0 = pl.multiple_of(i * TT, TT)
            kk = p_ref[0, pl.ds(r0, TT), NA_W:2 * NA_W].astype(F32)
            kn_s[pl.ds(r0, TT), :] = headnorm(kk, kg_ref[...]).astype(BF16)
            return carry
        lax.fori_loop(0, NT, body, 0)

    scale = float(NA_D ** -0.5 * np.log2(np.e))
    kctx = kn_s[SEQ:S, :]
    vctx = p_ref[0, SEQ:S, 2 * NA_W:3 * NA_W]

    @pl.when(j < NA_NBLK)
    def _():
        q0 = pl.multiple_of(j * NA_QB, NA_QB)
        k0 = pl.multiple_of(jnp.clip(j * NA_QROWS - WIN_R // 2, 0, ROWS - NA_KROWS) * GRID_W, 256)
        q = headnorm(p_ref[0, pl.ds(q0, NA_QB), 0:NA_W].astype(F32), qg_ref[...]) * scale
        kl = kn_s[pl.ds(k0, NA_KB), :]
        vl = p_ref[0, pl.ds(k0, NA_KB), 2 * NA_W:3 * NA_W]
        head = lax.broadcasted_iota(jnp.int32, (NA_QB, NA_W), 1) // NA_D
        acc = jnp.zeros((NA_QB, NA_W), F32)

        def scores(h):
            qm = jnp.where(head == h, q, 0.0).astype(BF16)
            for i in range(NA_QROWS):
                for p in range(NA_KROWS // 2):
                    code = idx_ref[(j * NA_QROWS + i) * (NA_KROWS // 2) + p]
                    bias_s[i * GRID_W:(i + 1) * GRID_W, p * 2 * GRID_W:(p + 1) * 2 * GRID_W] = pt_ref[h, code]
            return _dot_nt(qm, kl) + bias_s[...].astype(F32), _dot_nt(qm, kctx)

        s_next = scores(0)
        for h in range(NA_H):
            s1, s2 = s_next
            if h + 1 < NA_H:
                s_next = scores(h + 1)
            m = jnp.maximum(jnp.max(s1, axis=-1, keepdims=True), jnp.max(s2, axis=-1, keepdims=True))
            p1 = jnp.exp2(s1 - m)
            p2 = jnp.exp2(s2 - m)
            l = jnp.sum(p1, axis=-1, keepdims=True) + jnp.sum(p2, axis=-1, keepdims=True)
            o = (_dot(p1.astype(BF16), vl) + _dot(p2.astype(BF16), vctx)) / l
            acc = jnp.where(head == h, o, acc)
        o_ref[0, pl.ds(q0, NA_QB), :] = acc.astype(BF16)

    @pl.when(j == NA_NBLK)
    def _():
        q = headnorm(p_ref[0, SEQ:S, 0:NA_W].astype(F32), qg_ref[...]) * scale
        head = lax.broadcasted_iota(jnp.int32, (CTX, NA_W), 1) // NA_D
        acc = jnp.zeros((CTX, NA_W), F32)
        for h in range(NA_H):
            qm = jnp.where(head == h, q, 0.0).astype(BF16)
            s2 = _dot_nt(qm, kctx)
            m = jnp.max(s2, axis=-1, keepdims=True)
            p2 = jnp.exp2(s2 - m)
            l = jnp.sum(p2, axis=-1, keepdims=True)
            o = _dot(p2.astype(BF16), vctx) / l
            acc = jnp.where(head == h, o, acc)
        o_ref[0, SEQ:S, :] = acc.astype(BF16)


def _na_attn(pair_idx, pna, pair_tiles, qg, kg, seg):
    B = pna.shape[0]
    return pl.pallas_call(
        _na_kernel,
        grid_spec=pltpu.PrefetchScalarGridSpec(
            num_scalar_prefetch=1,
            grid=(B, NA_NBLK + 1),
            in_specs=[pl.BlockSpec((1, S, 3 * NA_W), lambda b, j, idx: (b, 0, 0)),
                      pl.BlockSpec(pair_tiles.shape, lambda b, j, idx: (0, 0, 0, 0)),
                      pl.BlockSpec((1, NA_W), lambda b, j, idx: (0, 0)),
                      pl.BlockSpec((1, NA_W), lambda b, j, idx: (0, 0)),
                      pl.BlockSpec((NA_W, NA_W), lambda b, j, idx: (0, 0))],
            out_specs=pl.BlockSpec((1, S, NA_W), lambda b, j, idx: (b, 0, 0)),
            scratch_shapes=[pltpu.VMEM((S, NA_W), BF16), pltpu.VMEM((NA_QB, NA_KB), BF16)]),
        out_shape=jax.ShapeDtypeStruct((B, S, NA_W), BF16),
        compiler_params=_cparams(("parallel", "arbitrary")),
        name="na_attn",
    )(pair_idx, pna, pair_tiles, qg, kg, seg)


def _out_proj_kernel(z_ref, modx_ref, mody_ref, a_ref, m_ref, n_ref, wa_ref, wm_ref, wn_ref, g2_ref, rw_ref, rb_ref,
                     z1_ref, h2_ref, gate_ref):
    lane = lax.broadcasted_iota(jnp.int32, (TT, 128), 1)
    live = lane < N_EXPERTS
    groups = [slice(i * TT, (i + 1) * TT) for i in range(OT // TT)]

    splits = []
    for rows in groups:
        is_ctx = lax.broadcasted_iota(jnp.int32, (TT, 1), 0) + (pl.program_id(1) * OT + rows.start) >= SEQ
        mod = lambda i: jnp.where(is_ctx, mody_ref[0][:, i * D:(i + 1) * D], modx_ref[0][:, i * D:(i + 1) * D])
        mix = (_dot(a_ref[0, rows, :], wa_ref[...]) + _dot(m_ref[0, rows, :], wm_ref[...])
               + _dot(n_ref[0, rows, :], wn_ref[...]))
        x = z_ref[0, rows, :] + mod(2) * mix
        z1_ref[0, rows, :] = x
        hn = x * lax.rsqrt(jnp.mean(x * x, axis=-1, keepdims=True) + EPS) * g2_ref[...]
        hn = hn * (1.0 + mod(4)) + mod(3)
        h_hi, h_lo = _split_bf16(hn, 2)
        h2_ref[0, rows, :] = h_hi
        splits.append((h_hi, h_lo))

    affs = [_sigmoid(_dot(h_hi, rw_ref[0]) + (_dot(h_hi, rw_ref[1]) + _dot(h_lo, rw_ref[0])))
            for h_hi, h_lo in splits]

    def cyc(x, k, width):
        fwd = pltpu.roll(x, 128 - k, 1)
        back = pltpu.roll(x, width - k, 1)
        return jnp.where((lane % width) + k < width, fwd, back)

    def rank(x, width, step):
        r = jnp.zeros((TT, 128), F32)
        for k in range(1, width // step):
            y = cyc(x, k * step, width)
            wrapped = (lane % width) + k * step >= width
            beats = jnp.logical_or(y > x, jnp.logical_and(y == x, wrapped))
            r = r + beats.astype(F32)
        return r

    for rows, aff in zip(groups, affs):
        sel = aff + rb_ref[...]
        top2 = rank(sel, EPG, 1) < 2.0
        part = jnp.where(top2, sel, 0.0)
        gscore = part
        for k in range(1, EPG):
            gscore = gscore + cyc(part, k, EPG)
        best = rank(gscore, N_EXPERTS, EPG) < 1.0
        chosen = jnp.logical_and(jnp.logical_and(top2, best), live)
        w = jnp.where(chosen, aff, 0.0)
        gate_ref[0, rows, :] = w / jnp.sum(w, axis=-1, keepdims=True)


def _out_proj(z, mods, mla_o, ml_o, na_o, wa, wm, wn, g2, rw, rb):
    B = z.shape[0]
    tok = lambda w_: pl.BlockSpec((1, OT, w_), lambda b, t: (b, t, 0))
    full = lambda a: pl.BlockSpec(a.shape, lambda b, t, _n=a.ndim: (0,) * _n)
    return pl.pallas_call(
        _out_proj_kernel,
        grid=(B, S // OT),
        in_specs=[tok(D),
                  pl.BlockSpec((1, 1, 6 * D), lambda b, t: (2 * b, 0, 0)),
                  pl.BlockSpec((1, 1, 6 * D), lambda b, t: (2 * b + 1, 0, 0)),
                  tok(MLA_H * V_D), tok(ML_WP), tok(NA_W),
                  full(wa), full(wm), full(wn), full(g2), full(rw), full(rb)],
        out_specs=[tok(D), tok(D), tok(128)],
        out_shape=[jax.ShapeDtypeStruct((B, S, D), F32),
                   jax.ShapeDtypeStruct((B, S, D), BF16),
                   jax.ShapeDtypeStruct((B, S, 128), F32)],
        compiler_params=_cparams(("parallel", "parallel")),
        name="out_proj",
    )(z, mods, mods, mla_o, ml_o, na_o, wa, wm, wn, g2, rw, rb)


def _moe_kernel(z1_ref, h2_ref, gate_ref, modx_ref, mody_ref, w1_ref, w3_ref, w2_ref, o_ref, acc_s):
    e = pl.program_id(2)
    half = pl.program_id(1)

    @pl.when(e == 0)
    def _():
        acc_s[...] = jnp.zeros_like(acc_s)

    h = h2_ref[0]
    a = _dot(h, w1_ref[0, 0].astype(BF16))
    b = _dot(h, w3_ref[0, 0].astype(BF16))
    lane = lax.broadcasted_iota(jnp.int32, (MOE_T, 128), 1)
    ge = jnp.sum(jnp.where(lane == e, gate_ref[0], 0.0), axis=-1, keepdims=True)
    hid = _silu(a) * b * ge
    acc_s[...] += _dot(hid.astype(BF16), w2_ref[0, 0].astype(BF16))

    @pl.when(e == N_EXPERTS - 1)
    def _():
        rowi = lax.broadcasted_iota(jnp.int32, (MOE_T, 1), 0) + half * MOE_T
        g2 = jnp.where(rowi < SEQ, modx_ref[0][:, 5 * D:], mody_ref[0][:, 5 * D:])
        o_ref[0] = z1_ref[0] + g2 * acc_s[...]


def _moe(z1, h2, gates, mods, w1, w3, w2, l):
    B = z1.shape[0]
    nh = S // MOE_T
    tok = lambda w_: pl.BlockSpec((1, MOE_T, w_), lambda b, t, e: (b, t, 0))
    return pl.pallas_call(
        _moe_kernel,
        grid=(B, nh, N_EXPERTS),
        in_specs=[tok(D), tok(D), tok(128),
                  pl.BlockSpec((1, 1, 6 * D), lambda b, t, e: (2 * b, 0, 0)),
                  pl.BlockSpec((1, 1, 6 * D), lambda b, t, e: (2 * b + 1, 0, 0)),
                  pl.BlockSpec((1, 1, D, D_FF), lambda b, t, e: (l, e, 0, 0)),
                  pl.BlockSpec((1, 1, D, D_FF), lambda b, t, e: (l, e, 0, 0)),
                  pl.BlockSpec((1, 1, D_FF, D), lambda b, t, e: (l, e, 0, 0))],
        out_specs=tok(D),
        out_shape=jax.ShapeDtypeStruct((B, S, D), F32),
        scratch_shapes=[pltpu.VMEM((MOE_T, D), F32)],
        compiler_params=_cparams(("parallel", "parallel", "arbitrary")),
        name="moe",
    )(z1, h2, gates, mods, mods, w1, w3, w2)


def _in_proj_layout(w):
    cuts = np.cumsum([Q_RANK, KV_RANK, ROPE_D, ML_W, ML_W, 4 * ML_H])
    qc, ckv, kr, u, zz, g, na = jnp.split(w, [int(v) for v in cuts], axis=-1)
    zeros = lambda n: jnp.zeros((w.shape[0], n), w.dtype)
    out = jnp.concatenate([qc, ckv, zeros(NOPE_D), kr, zeros(HP - QK_D),
                           _pad_heads(u, ML_H, ML_D, HP), _pad_heads(zz, ML_H, ML_D, HP),
                           _gate_order(g), zeros(128 - 4 * ML_H), na], axis=-1)
    assert out.shape[-1] == NP_IN
    return out


def _gate_order(g):
    i_f, f_f, i_b, f_b = jnp.split(g, 4, axis=-1)
    return jnp.concatenate([i_f, i_b, f_f, f_b], axis=-1)


def _pad_heads(v, nh, d, dp):
    lead = v.shape[:-1]
    v = v.reshape(lead + (nh, d))
    v = jnp.pad(v, [(0, 0)] * len(lead) + [(0, 0), (0, dp - d)])
    return v.reshape(lead + (nh * dp,))


def _rope_tables():
    t = np.arange(SEQ)
    row = (t // GRID_W).astype(np.float32)
    col = (t % GRID_W).astype(np.float32)
    quarter = ROPE_D // 4
    inv = jnp.asarray(ROPE_BASE, F32) ** (-jnp.arange(quarter, dtype=F32) / quarter)
    ar = jnp.asarray(row)[:, None] * inv
    ac = jnp.asarray(col)[:, None] * inv
    ang = jnp.concatenate([ar, ar, ac, ac], axis=-1)
    cos = jnp.ones((S, HP), F32).at[:SEQ, NOPE_D:QK_D].set(jnp.cos(ang))
    sin = jnp.zeros((S, HP), F32).at[:SEQ, NOPE_D:QK_D].set(jnp.sin(ang))
    return cos, sin


def _rotate_half_index():
    q = ROPE_D // 4
    src = np.arange(QK_D)
    sign = np.zeros((QK_D,), np.float32)
    for blk in range(2):
        lo = NOPE_D + 2 * q * blk
        src[lo:lo + q] = np.arange(lo + q, lo + 2 * q)
        sign[lo:lo + q] = -1.0
        src[lo + q:lo + 2 * q] = np.arange(lo, lo + q)
        sign[lo + q:lo + 2 * q] = 1.0
    return src, sign


def _rotate_half(w):
    src, sign = _rotate_half_index()
    return w[..., src] * sign


NA_NDR = 2 * WIN_R - 1
NA_NPAIR = 3 * NA_NDR


def _na_pair_index():
    idx = np.zeros((NA_NBLK, NA_QROWS, NA_KROWS // 2), np.int32)
    for blk in range(NA_NBLK):
        k0 = int(np.clip(blk * NA_QROWS - WIN_R // 2, 0, ROWS - NA_KROWS))
        for i in range(NA_QROWS):
            qr = blk * NA_QROWS + i
            rs = int(np.clip(qr - WIN_R // 2, 0, ROWS - WIN_R))
            assert k0 <= rs and rs + WIN_R <= k0 + NA_KROWS
            for p in range(NA_KROWS // 2):
                kr = k0 + 2 * p
                dr = kr - qr + WIN_R - 1
                left = rs <= kr < rs + WIN_R
                right = rs <= kr + 1 < rs + WIN_R
                if left and right:
                    idx[blk, i, p] = 1 + dr
                elif left:
                    idx[blk, i, p] = NA_NDR + dr
                elif right:
                    idx[blk, i, p] = 2 * NA_NDR + dr + 1
    return idx.reshape(-1)


def _na_pair_tiles(rpb):
    cq = np.arange(GRID_W)
    cs = np.clip(cq - WIN_C // 2, 0, GRID_W - WIN_C)
    col_ok = (cq[None, :] >= cs[:, None]) & (cq[None, :] < cs[:, None] + WIN_C)
    dc = np.clip(cq[None, :] - cq[:, None], -(WIN_C - 1), WIN_C - 1) + (WIN_C - 1)
    onehot = jnp.asarray(np.eye(2 * WIN_C - 1, dtype=np.float32)[dc])
    tiles = jnp.einsum('hrc,qkc->hrqk', rpb, onehot, precision=lax.Precision.HIGHEST)
    tiles = jnp.where(jnp.asarray(col_ok), tiles * float(np.log2(np.e)), NEG)
    masked = jnp.full_like(tiles, NEG)
    both = jnp.concatenate([tiles[:, :-1], tiles[:, 1:]], axis=-1)
    left = jnp.concatenate([tiles, masked], axis=-1)
    right = jnp.concatenate([masked, tiles], axis=-1)
    none = jnp.concatenate([masked[:, :1], masked[:, :1]], axis=-1)
    out = jnp.concatenate([none, both, left, right], axis=1)
    assert out.shape[1] == NA_NPAIR
    return out.astype(BF16)


def kernel(x, c, ctx, c_ctx, w_mod, b_mod, norm1_g, norm2_g, w_in, w_out, mla_qnorm_g, mla_w_uq, mla_kvnorm_g, mla_w_ukv, mla_q_g, mla_k_g, ml_conv_w, ml_conv_b, ml_w_q, ml_w_k, ml_w_v, ml_gate_b, ml_norm_g, ml_skip, na_q_g, na_k_g, na_rpb, router_w, router_b, moe_w1, moe_w3, moe_w2):
    B = x.shape[0]
    z = jnp.concatenate([x, ctx], axis=1)
    cc = jnp.zeros((16, D), F32).at[:B].set(c).at[B].set(c_ctx)
    mod_all = _modulation(cc, w_mod, b_mod)
    cos, sin = _rope_tables()
    rot_src, rot_sign = _rotate_half_index()
    rot_np = np.zeros((HP, HP), np.float32)
    rot_np[rot_src, np.arange(QK_D)] = rot_sign
    rot_mat = jnp.asarray(rot_np, BF16)
    ones_hp = jnp.ones((HP, HP), BF16)
    seg = jnp.asarray(np.kron(np.eye(NA_H), np.ones((NA_D, NA_D))), BF16)
    tril = jnp.asarray(np.stack([np.tril(np.ones((ML_CHUNK, ML_CHUNK))), np.triu(np.ones((ML_CHUNK, ML_CHUNK)))]), BF16)
    pair_idx = jnp.asarray(_na_pair_index())
    rw = jnp.stack(_split_bf16(jnp.pad(router_w, ((0, 0), (0, 128 - N_EXPERTS))), 2))
    rb = jnp.pad(router_b, (0, 128 - N_EXPERTS), constant_values=NEG).reshape(1, 128)

    def pad_lanes(v, n):
        return jnp.pad(v, [(0, 0)] * (v.ndim - 1) + [(0, n - v.shape[-1])])

    for l in range(DEPTH):
        mx = mod_all[l, :B]
        my = jnp.broadcast_to(mod_all[l, B], (B, 6 * D))
        mods = jnp.stack([mx, my], axis=1).reshape(2 * B, 1, 6 * D)

        w_in_p = _in_proj_layout(w_in[l]).astype(BF16)
        pmla, pu, pg, pna = _in_proj(z, mods, norm1_g[l].reshape(1, D), w_in_p)

        wq = jnp.transpose(mla_w_uq[l].reshape(Q_RANK, MLA_H, QK_D), (1, 0, 2))
        wuq = jnp.concatenate([pad_lanes(wq, HP), pad_lanes(_rotate_half(wq), HP)], axis=-1).astype(BF16)
        wukv = jnp.transpose(mla_w_ukv[l].reshape(KV_RANK, MLA_H, NOPE_D + V_D), (1, 0, 2))
        wuk = pad_lanes(wukv[..., :NOPE_D], HP).astype(BF16)
        wuv = pad_lanes(wukv[..., NOPE_D:], HP).astype(BF16)
        qg, kg = mla_q_g[l], mla_k_g[l]
        q_scale = float(QK_D ** -0.5 * np.log2(np.e))
        tabs = jnp.stack([cos * pad_lanes(qg, HP) * q_scale, sin * pad_lanes(jnp.abs(rot_sign) * qg[rot_src], HP) * q_scale,
                          cos * pad_lanes(kg, HP), sin * pad_lanes(jnp.abs(rot_sign) * kg[rot_src], HP)])
        heads_last = lambda w_: jnp.transpose(w_, (1, 0, 2)).reshape(w_.shape[1], -1)
        q, k, v = _mla_prep(pmla, tabs, mla_qnorm_g[l].reshape(1, Q_RANK), heads_last(wuq),
                            mla_kvnorm_g[l].reshape(1, KV_RANK), heads_last(wuk), heads_last(wuv), rot_mat, ones_hp)
        mla_o = _mla_attn(q, k, v)

        padh = lambda a: _pad_heads(a, ML_H, ML_D, HP)
        padw = lambda w_: jnp.pad(w_, ((0, 0), (0, HP - ML_D), (0, HP - ML_D))).astype(BF16)
        cw = jnp.pad(padh(ml_conv_w[l]), ((0, 8 - 3), (0, 0)))
        ml_o = _mlstm(pu, pg, cw, padh(ml_conv_b[l]).reshape(1, ML_WP),
                      padw(ml_w_q[l]), padw(jnp.swapaxes(ml_w_k[l], 1, 2) * (ML_D ** -0.5)), padw(ml_w_v[l]),
                      pad_lanes(_gate_order(ml_gate_b[l]).reshape(1, 4 * ML_H), 128),
                      padh(ml_norm_g[l]).reshape(1, ML_WP), padh(ml_skip[l]).reshape(1, ML_WP), tril)

        na_o = _na_attn(pair_idx, pna, _na_pair_tiles(na_rpb[l]), jnp.tile(na_q_g[l], NA_H).reshape(1, NA_W),
                        jnp.tile(na_k_g[l], NA_H).reshape(1, NA_W), seg)

        wo = w_out[l]
        wa = wo[:MLA_H * V_D].astype(BF16)
        wm = jnp.pad(wo[MLA_H * V_D:MLA_H * V_D + ML_W].reshape(ML_H, ML_D, D),
                     ((0, 0), (0, HP - ML_D), (0, 0))).reshape(ML_WP, D).astype(BF16)
        wn = wo[MLA_H * V_D + ML_W:].astype(BF16)
        z1, h2, gates = _out_proj(z, mods, mla_o, ml_o, na_o, wa, wm, wn, norm2_g[l].reshape(1, D), rw, rb)
        z = _moe(z1, h2, gates, mods, moe_w1, moe_w3, moe_w2, l)

    return z[:, :SEQ]
```

```python
import functools

import numpy as np
import jax
import jax.numpy as jnp
from jax import lax
from jax.experimental import pallas as pl
from jax.experimental.pallas import tpu as pltpu

F32 = jnp.float32
BF16 = jnp.bfloat16

D = 1024
SEQ = 2048
CTX = 256
S = SEQ + CTX
DEPTH = 4
GRID_W = 64
ROWS = SEQ // GRID_W
EPS = 1e-6

MLA_H = 6
Q_RANK = 256
KV_RANK = 128
NOPE_D = 64
ROPE_D = 32
V_D = 64
QK_D = NOPE_D + ROPE_D
ROPE_BASE = 10000.0

ML_H = 4
ML_D = 96
ML_W = ML_H * ML_D
HP = 128
ML_WP = ML_H * HP
ML_CHUNK = 128
N_CHUNK = S // ML_CHUNK
N_CTX_CHUNK = CTX // ML_CHUNK

NA_H = 4
NA_D = 64
NA_W = NA_H * NA_D
WIN_R = 8
WIN_C = 16
NA_QROWS = 4
NA_KROWS = 12
NA_QB = NA_QROWS * GRID_W
NA_KB = NA_KROWS * GRID_W
NA_NBLK = ROWS // NA_QROWS

N_EXPERTS = 16
N_GROUPS = 4
EPG = N_EXPERTS // N_GROUPS
D_FF = 256

TT = 256
NT = S // TT
OT = 768
TOP_K = 2
XW = D + 128
DISPATCH_T = S
GROUP_T = 512
COMBINE_T = 768
NEG = -1e30

C_QC = 0
C_CKV = 256
C_KR = 384
C_U = 512
C_Z = C_U + ML_WP
C_G = C_Z + ML_WP
C_NA = C_G + 128
NP_IN = C_NA + 3 * NA_W

VMEM_LIMIT = 56 * 1024 * 1024


def _cparams(sem):
    return pltpu.CompilerParams(dimension_semantics=sem, vmem_limit_bytes=VMEM_LIMIT)


def _sigmoid(x):
    return 1.0 / (1.0 + jnp.exp(-x))


def _silu(x):
    return x * _sigmoid(x)


def _dot(a, b):
    return jnp.dot(a, b, preferred_element_type=F32)


def _dot_nt(a, b):
    return lax.dot_general(a, b, (((1,), (1,)), ((), ())), preferred_element_type=F32)


def _dot_tn(a, b):
    return lax.dot_general(a, b, (((0,), (0,)), ((), ())), preferred_element_type=F32)


def _dot_hi(a, b):
    return jnp.dot(a, b, preferred_element_type=F32, precision=lax.Precision.HIGHEST)


def _split_bf16(x, n):
    parts = []
    for _ in range(n):
        p = x.astype(BF16)
        parts.append(p)
        x = x - p.astype(F32)
    return parts


def _mod_rows(mod_ref, t):
    m = mod_ref[0]
    return [m[:, i * D:(i + 1) * D] for i in range(6)]


def _mod_kernel(c_ref, w_ref, b_ref, o_ref):
    sc = _silu(c_ref[...])
    o_ref[0] = _dot_hi(sc, w_ref[0]) + b_ref[0]


def _modulation(cc, w_mod, b_mod):
    nc = 6
    return pl.pallas_call(
        _mod_kernel,
        grid=(DEPTH, nc),
        in_specs=[pl.BlockSpec((16, D), lambda l, j: (0, 0)),
                  pl.BlockSpec((1, D, D), lambda l, j: (l, 0, j)),
                  pl.BlockSpec((1, 1, D), lambda l, j: (l, 0, j))],
        out_specs=pl.BlockSpec((1, 16, D), lambda l, j: (l, 0, j)),
        out_shape=jax.ShapeDtypeStruct((DEPTH, 16, 6 * D), F32),
        compiler_params=_cparams(("parallel", "parallel")),
        name="modulation",
    )(cc, w_mod, b_mod.reshape(DEPTH, 1, 6 * D))


def _in_proj_kernel(z_ref, mod_ref, g_ref, w_ref, pmla_ref, pu_ref, pg_ref, pna_ref):
    sh1, sc1 = _mod_rows(mod_ref, None)[:2]
    x = z_ref[0]
    xn = x * lax.rsqrt(jnp.mean(x * x, axis=-1, keepdims=True) + EPS) * g_ref[...]
    xn = xn * (1.0 + sc1) + sh1
    p = _dot(xn.astype(BF16), w_ref[...])
    pmla_ref[0] = p[:, :C_U].astype(BF16)
    pu_ref[0] = p[:, C_U:C_G].astype(BF16)
    pg_ref[0] = p[:, C_G:C_NA]
    pna_ref[0] = p[:, C_NA:].astype(BF16)


def _mod_spec():
    return pl.BlockSpec((1, 1, 6 * D), lambda b, t: (2 * b + t // (NT - 1), 0, 0))


def _in_proj(z, mods, g, w):
    B = z.shape[0]
    tok = lambda w_: pl.BlockSpec((1, TT, w_), lambda b, t: (b, t, 0))
    return pl.pallas_call(
        _in_proj_kernel,
        grid=(B, NT),
        in_specs=[tok(D), _mod_spec(),
                  pl.BlockSpec((1, D), lambda b, t: (0, 0)),
                  pl.BlockSpec((D, NP_IN), lambda b, t: (0, 0))],
        out_specs=[tok(C_U), tok(2 * ML_WP), tok(128), tok(3 * NA_W)],
        out_shape=[jax.ShapeDtypeStruct((B, S, C_U), BF16),
                   jax.ShapeDtypeStruct((B, S, 2 * ML_WP), BF16),
                   jax.ShapeDtypeStruct((B, S, 128), F32),
                   jax.ShapeDtypeStruct((B, S, 3 * NA_W), BF16)],
        compiler_params=_cparams(("parallel", "parallel")),
        name="in_proj",
    )(z, mods, g, w)


def _mla_prep_kernel(p_ref, tab_ref, qng_ref, wuq_ref, kvng_ref, wuk_ref, wuv_ref, rot_ref, ones_ref,
                     q_out, k_out, v_out):
    p = p_ref[0].astype(F32)
    qc = p[:, C_QC:C_CKV]
    ckv = p[:, C_CKV:C_KR]
    kr = p[:, C_KR:C_U]
    qcn = (qc * lax.rsqrt(jnp.mean(qc * qc, axis=-1, keepdims=True) + EPS) * qng_ref[...]).astype(BF16)
    ckvn = (ckv * lax.rsqrt(jnp.mean(ckv * ckv, axis=-1, keepdims=True) + EPS) * kvng_ref[...]).astype(BF16)
    lane = lax.broadcasted_iota(jnp.int32, (TT, HP), 1)
    ones = ones_ref[...]
    kr_rot = _dot(kr.astype(BF16), rot_ref[...])

    q_all = _dot(qcn, wuq_ref[...])
    k_all = _dot(ckvn, wuk_ref[...])
    v_all = _dot(ckvn, wuv_ref[...])
    qs = [q_all[:, 2 * h * HP:(2 * h + 1) * HP] for h in range(MLA_H)]
    q_rots = [q_all[:, (2 * h + 1) * HP:(2 * h + 2) * HP] for h in range(MLA_H)]
    ks = [k_all[:, h * HP:(h + 1) * HP] + kr for h in range(MLA_H)]
    ss_q = [_dot((x * x).astype(BF16), ones) for x in qs]
    ss_k = [_dot((x * x).astype(BF16), ones) for x in ks]

    def norm_rope(x, x_rot, ss, cos_g, sin_g):
        return lax.rsqrt(ss * (1.0 / QK_D) + EPS) * (x * cos_g + x_rot * sin_g)

    for h in range(MLA_H):
        q_out[0, h] = norm_rope(qs[h], q_rots[h], ss_q[h], tab_ref[0], tab_ref[1]).astype(BF16)
        k_out[0, h] = norm_rope(ks[h], kr_rot, ss_k[h], tab_ref[2], tab_ref[3]).astype(BF16)
        v_out[0, h] = jnp.where(lane < V_D, v_all[:, h * HP:(h + 1) * HP], 1.0).astype(BF16)


def _mla_prep(pmla, tabs, qng, wuq, kvng, wuk, wuv, rot, ones):
    B = pmla.shape[0]
    full = lambda a: pl.BlockSpec(a.shape, lambda b, t, _n=a.ndim: (0,) * _n)
    hd = lambda w_: pl.BlockSpec((1, MLA_H, TT, w_), lambda b, t: (b, 0, t, 0))
    return pl.pallas_call(
        _mla_prep_kernel,
        grid=(B, NT),
        in_specs=[pl.BlockSpec((1, TT, C_U), lambda b, t: (b, t, 0)),
                  pl.BlockSpec((4, TT, HP), lambda b, t: (0, t, 0)),
                  full(qng), full(wuq), full(kvng), full(wuk), full(wuv), full(rot), full(ones)],
        out_specs=[hd(HP), hd(HP), hd(HP)],
        out_shape=[jax.ShapeDtypeStruct((B, MLA_H, S, HP), BF16),
                   jax.ShapeDtypeStruct((B, MLA_H, S, HP), BF16),
                   jax.ShapeDtypeStruct((B, MLA_H, S, HP), BF16)],
        compiler_params=_cparams(("parallel", "parallel")),
        name="mla_prep",
    )(pmla, tabs, qng, wuq, kvng, wuk, wuv, rot, ones)


def _mla_attn_kernel(q_ref, k_ref, v_ref, o_ref):
    t = pl.program_id(1)

    def attend(k_lo, k_n):
        def scores(h):
            return _dot_nt(q_ref[0, h], k_ref[0, h, k_lo:k_lo + k_n, :])

        outs = []
        s_next = scores(0)
        for h in range(MLA_H):
            s = s_next
            if h + 1 < MLA_H:
                s_next = scores(h + 1)
            m = jnp.max(s, axis=-1, keepdims=True)
            p = jnp.exp2(s - m)
            pv = _dot(p.astype(BF16), v_ref[0, h, k_lo:k_lo + k_n, :])
            outs.append(pv[:, :V_D] / pv[:, V_D:V_D + 1])
        o_ref[0] = jnp.concatenate(outs, axis=-1).astype(BF16)

    @pl.when(t < NT - 1)
    def _():
        attend(0, S)

    @pl.when(t == NT - 1)
    def _():
        attend(SEQ, CTX)


def _mla_attn(q, k, v):
    B = q.shape[0]
    return pl.pallas_call(
        _mla_attn_kernel,
        grid=(B, NT),
        in_specs=[pl.BlockSpec((1, MLA_H, TT, HP), lambda b, t: (b, 0, t, 0)),
                  pl.BlockSpec((1, MLA_H, S, HP), lambda b, t: (b, 0, 0, 0)),
                  pl.BlockSpec((1, MLA_H, S, HP), lambda b, t: (b, 0, 0, 0))],
        out_specs=pl.BlockSpec((1, TT, MLA_H * V_D), lambda b, t: (b, t, 0)),
        out_shape=jax.ShapeDtypeStruct((B, S, MLA_H * V_D), BF16),
        compiler_params=_cparams(("parallel", "arbitrary")),
        name="mla_attn",
    )(q, k, v)


def _log_sigmoid(x):
    return jnp.minimum(x, 0.0) - jnp.log(1.0 + jnp.exp(-jnp.abs(x)))


def _mlstm_kernel(pu_ref, pg_ref, cw_ref, cb_ref, wq_ref, wk_ref, wv_ref, gb_ref, ng_ref, sk_ref,
                  tril_ref, o_ref, uc_s, q_s, kt_s, v_s, h_s, c_s, m_s, pm_s, b_s, rt_s):
    CA = 2 * ML_CHUNK
    row = lax.broadcasted_iota(jnp.int32, (CA, ML_WP), 0)

    def conv_body(i, carry):
        r0 = pl.multiple_of(i * CA, CA)
        x = pu_ref[0, pl.ds(r0, CA), 0:ML_WP].astype(F32)
        pr = pl.multiple_of(jnp.maximum(r0 - 16, 0), 16)
        nx = pl.multiple_of(jnp.minimum(r0 + CA, S - 16), 16)
        prev = pu_ref[0, pl.ds(pr, 16), 0:ML_WP].astype(F32)[15:16, :]
        nxt = pu_ref[0, pl.ds(nx, 16), 0:ML_WP].astype(F32)[0:1, :]
        seq_start = jnp.logical_or(r0 == 0, r0 == SEQ)
        seq_end = jnp.logical_or(r0 + CA == SEQ, r0 + CA == S)
        prev = jnp.where(seq_start, 0.0, prev)
        nxt = jnp.where(seq_end, 0.0, nxt)
        up = jnp.where(row == 0, prev, pltpu.roll(x, 1, 0))
        dn = jnp.where(row == CA - 1, nxt, pltpu.roll(x, CA - 1, 0))
        uc = _silu(cw_ref[0:1, :] * up + cw_ref[1:2, :] * x + cw_ref[2:3, :] * dn + cb_ref[...])
        ucb = uc.astype(BF16)
        uc_s[pl.ds(r0, CA), :] = ucb
        xb = x.astype(BF16)
        for h in range(ML_H):
            sl = slice(h * HP, (h + 1) * HP)
            q_s[pl.ds(r0, CA), sl] = _dot(ucb[:, sl], wq_ref[h]).astype(BF16)
            kt = _dot_nt(wk_ref[h], ucb[:, sl])
            kt_s[2 * i, sl, :] = kt[:, :ML_CHUNK].astype(BF16)
            kt_s[2 * i + 1, sl, :] = kt[:, ML_CHUNK:].astype(BF16)
            v_s[pl.ds(r0, CA), 2 * h * HP:(2 * h + 1) * HP] = _dot(xb[:, sl], wv_ref[h]).astype(BF16)
            v_s[pl.ds(r0, CA), (2 * h + 1) * HP:(2 * h + 2) * HP] = jnp.ones((CA, HP), BF16)

        for half in range(2):
            rows = pl.ds(r0 + half * ML_CHUNK, ML_CHUNK)
            g = pg_ref[0, rows, :] + gb_ref[...]
            parts = _split_bf16(_log_sigmoid(g), 3)
            cum_f = sum(_dot(tril_ref[0], part) for part in parts)
            cum_b = sum(_dot(tril_ref[1], part) for part in parts)
            bsh = pltpu.roll(jnp.where(bwd_lane, cum_b, cum_f), 128 - n_chain, 1)
            r = g - bsh
            pf = r
            pb = r
            k = 1
            while k < ML_CHUNK:
                pf = jnp.maximum(pf, jnp.where(ti >= k, pltpu.roll(pf, k, 0), NEG))
                pb = jnp.maximum(pb, jnp.where(ti < ML_CHUNK - k, pltpu.roll(pb, ML_CHUNK - k, 0), NEG))
                k *= 2
            pm_s[2 * i + half] = jnp.where(bwd_lane, pb, pf)
            b_s[2 * i + half] = bsh
            rt_s[2 * i + half] = r.T[0:n_chain, :]
        return carry

    n_chain = 2 * ML_H
    ti = lax.broadcasted_iota(jnp.int32, (ML_CHUNK, ML_CHUNK), 0)
    si = lax.broadcasted_iota(jnp.int32, (ML_CHUNK, ML_CHUNK), 1)
    bwd_lane = (si % n_chain) >= ML_H
    lax.fori_loop(0, S // CA, conv_body, 0)

    c_s[...] = jnp.zeros_like(c_s)
    m_s[...] = jnp.zeros_like(m_s)
    masks = (si <= ti, si >= ti)

    def scan_body(j, carry):
        chunk = (jnp.where(j < N_CTX_CHUNK, j + N_CHUNK - N_CTX_CHUNK, j - N_CTX_CHUNK), N_CHUNK - 1 - j)
        chains = []
        for d in range(2):
            r0 = pl.multiple_of(chunk[d] * ML_CHUNK, ML_CHUNK)
            p_col = pm_s[chunk[d]]
            bsh = b_s[chunk[d]]
            r_t = rt_s[chunk[d]]
            end = ML_CHUNK - 1 if d == 0 else 0
            for h in range(ML_H):
                c = d * ML_H + h
                sl = slice(h * HP, (h + 1) * HP)
                qc = q_s[pl.ds(r0, ML_CHUNK), sl]
                kt = kt_s[chunk[d], sl, :]
                vx = v_s[pl.ds(r0, ML_CHUNK), 2 * h * HP:(2 * h + 2) * HP]
                r_row = r_t[c:c + 1, :]
                m = m_s[c]
                st = c_s[c]
                big_m = jnp.maximum(m, jnp.broadcast_to(p_col[:, c:c + 1], (ML_CHUNK, HP)))
                b_b = jnp.broadcast_to(bsh[:, c:c + 1], (ML_CHUNK, HP))
                m_end = big_m[end:end + 1, :]
                ktw = (kt.astype(F32) * jnp.exp(r_row - m_end)).astype(BF16)
                chains.append(dict(d=d, r0=r0, sl=sl, c=c, vx=vx, r_row=r_row, m=m, st=st, big_m=big_m,
                                   b_b=b_b, m_end=m_end, end=end,
                                   qk=_dot(qc, kt), inter=_dot(qc, st.astype(BF16)), upd=_dot(ktw, vx)))
        for ch in chains:
            dw = jnp.exp(jnp.where(masks[ch["d"]], ch["r_row"] - ch["big_m"], NEG))
            ch["intra"] = _dot((ch["qk"] * dw).astype(BF16), ch["vx"])
        for ch in chains:
            m, big_m, inter, intra, end = ch["m"], ch["big_m"], ch["inter"], ch["intra"], ch["end"]
            iw = jnp.exp(m - big_m)
            num = iw * inter[:, :HP] + intra[:, :HP]
            nq = iw * inter[:, HP:] + intra[:, HP:]
            hv = num / jnp.maximum(jnp.abs(nq), jnp.exp(-(ch["b_b"] + big_m)))
            a = jnp.exp(m - ch["m_end"])
            ch["out"] = (hv, jnp.concatenate([a, a], axis=1) * ch["st"] + ch["upd"],
                         ch["b_b"][end:end + 1, :] + ch["m_end"])
        for ch in chains:
            hv, st_new, m_new = ch["out"]
            h_s[ch["d"], pl.ds(ch["r0"], ML_CHUNK), ch["sl"]] = hv
            c_s[ch["c"]] = st_new
            m_s[ch["c"]] = m_new
        return carry

    lax.fori_loop(0, N_CHUNK, scan_body, 0)

    live = (lax.broadcasted_iota(jnp.int32, (CA, HP), 1) < ML_D).astype(F32)

    def out_body(i, carry):
        r0 = pl.multiple_of(i * CA, CA)
        for h in range(ML_H):
            sl = slice(h * HP, (h + 1) * HP)
            hh = h_s[0, pl.ds(r0, CA), sl] + h_s[1, pl.ds(r0, CA), sl]
            mu = jnp.sum(hh, axis=-1, keepdims=True) * (1.0 / ML_D)
            dv = (hh - mu) * live
            var = jnp.sum(dv * dv, axis=-1, keepdims=True) * (1.0 / ML_D)
            hn = dv * lax.rsqrt(var + EPS) * ng_ref[:, sl]
            uc = uc_s[pl.ds(r0, CA), sl].astype(F32)
            zz = pu_ref[0, pl.ds(r0, CA), ML_WP + h * HP:ML_WP + (h + 1) * HP].astype(F32)
            o_ref[0, pl.ds(r0, CA), sl] = ((hn + sk_ref[:, sl] * uc) * _silu(zz)).astype(BF16)
        return carry

    lax.fori_loop(0, S // CA, out_body, 0)


def _mlstm(pu, pg, cw, cb, wq, wk, wv, gb, ng, sk, tril):
    B = pu.shape[0]
    full = lambda a: pl.BlockSpec(a.shape, lambda b, _n=a.ndim: (0,) * _n)
    n_chain = 2 * ML_H
    return pl.pallas_call(
        _mlstm_kernel,
        grid=(B,),
        in_specs=[pl.BlockSpec((1, S, 2 * ML_WP), lambda b: (b, 0, 0)),
                  pl.BlockSpec((1, S, 128), lambda b: (b, 0, 0)),
                  full(cw), full(cb), full(wq), full(wk), full(wv), full(gb), full(ng), full(sk), full(tril)],
        out_specs=pl.BlockSpec((1, S, ML_WP), lambda b: (b, 0, 0)),
        out_shape=jax.ShapeDtypeStruct((B, S, ML_WP), BF16),
        scratch_shapes=[pltpu.VMEM((S, ML_WP), BF16), pltpu.VMEM((S, ML_WP), BF16),
                        pltpu.VMEM((N_CHUNK, ML_WP, ML_CHUNK), BF16), pltpu.VMEM((S, 2 * ML_WP), BF16),
                        pltpu.VMEM((2, S, ML_WP), F32),
                        pltpu.VMEM((n_chain, HP, 2 * HP), F32),
                        pltpu.VMEM((n_chain, 1, HP), F32),
                        pltpu.VMEM((N_CHUNK, ML_CHUNK, 128), F32),
                        pltpu.VMEM((N_CHUNK, ML_CHUNK, 128), F32),
                        pltpu.VMEM((N_CHUNK, n_chain, ML_CHUNK), F32)],
        compiler_params=_cparams(("parallel",)),
        name="mlstm",
    )(pu, pg, cw, cb, wq, wk, wv, gb, ng, sk, tril)


def _na_kernel(idx_ref, p_ref, pt_ref, qg_ref, kg_ref, seg_ref, o_ref, kn_s, bias_s):
    j = pl.program_id(1)
    seg = seg_ref[...]

    def headnorm(x, g):
        ss = _dot((x * x).astype(BF16), seg)
        return x * lax.rsqrt(ss * (1.0 / NA_D) + EPS) * g

    @pl.when(j == 0)
    def _():
        def body(i, carry):
            r0 = pl.multiple_of(i * TT, TT)
            kk = p_ref[0, pl.ds(r0, TT), NA_W:2 * NA_W].astype(F32)
            kn_s[pl.ds(r0, TT), :] = headnorm(kk, kg_ref[...]).astype(BF16)
            return carry
        lax.fori_loop(0, NT, body, 0)

    scale = float(NA_D ** -0.5 * np.log2(np.e))
    kctx = kn_s[SEQ:S, :]
    vctx = p_ref[0, SEQ:S, 2 * NA_W:3 * NA_W]

    @pl.when(j < NA_NBLK)
    def _():
        q0 = pl.multiple_of(j * NA_QB, NA_QB)
        k0 = pl.multiple_of(jnp.clip(j * NA_QROWS - WIN_R // 2, 0, ROWS - NA_KROWS) * GRID_W, 256)
        q = headnorm(p_ref[0, pl.ds(q0, NA_QB), 0:NA_W].astype(F32), qg_ref[...]) * scale
        kl = kn_s[pl.ds(k0, NA_KB), :]
        vl = p_ref[0, pl.ds(k0, NA_KB), 2 * NA_W:3 * NA_W]
        head = lax.broadcasted_iota(jnp.int32, (NA_QB, NA_W), 1) // NA_D
        acc = jnp.zeros((NA_QB, NA_W), F32)

        def scores(h):
            qm = jnp.where(head == h, q, 0.0).astype(BF16)
            for i in range(NA_QROWS):
                for p in range(NA_KROWS // 2):
                    code = idx_ref[(j * NA_QROWS + i) * (NA_KROWS // 2) + p]
                    bias_s[i * GRID_W:(i + 1) * GRID_W, p * 2 * GRID_W:(p + 1) * 2 * GRID_W] = pt_ref[h, code]
            return _dot_nt(qm, kl) + bias_s[...].astype(F32), _dot_nt(qm, kctx)

        s_next = scores(0)
        for h in range(NA_H):
            s1, s2 = s_next
            if h + 1 < NA_H:
                s_next = scores(h + 1)
            m = jnp.maximum(jnp.max(s1, axis=-1, keepdims=True), jnp.max(s2, axis=-1, keepdims=True))
            p1 = jnp.exp2(s1 - m)
            p2 = jnp.exp2(s2 - m)
            l = jnp.sum(p1, axis=-1, keepdims=True) + jnp.sum(p2, axis=-1, keepdims=True)
            o = (_dot(p1.astype(BF16), vl) + _dot(p2.astype(BF16), vctx)) / l
            acc = jnp.where(head == h, o, acc)
        o_ref[0, pl.ds(q0, NA_QB), :] = acc.astype(BF16)

    @pl.when(j == NA_NBLK)
    def _():
        q = headnorm(p_ref[0, SEQ:S, 0:NA_W].astype(F32), qg_ref[...]) * scale
        head = lax.broadcasted_iota(jnp.int32, (CTX, NA_W), 1) // NA_D
        acc = jnp.zeros((CTX, NA_W), F32)
        for h in range(NA_H):
            qm = jnp.where(head == h, q, 0.0).astype(BF16)
            s2 = _dot_nt(qm, kctx)
            m = jnp.max(s2, axis=-1, keepdims=True)
            p2 = jnp.exp2(s2 - m)
            l = jnp.sum(p2, axis=-1, keepdims=True)
            o = _dot(p2.astype(BF16), vctx) / l
            acc = jnp.where(head == h, o, acc)
        o_ref[0, SEQ:S, :] = acc.astype(BF16)


def _na_attn(pair_idx, pna, pair_tiles, qg, kg, seg):
    B = pna.shape[0]
    return pl.pallas_call(
        _na_kernel,
        grid_spec=pltpu.PrefetchScalarGridSpec(
            num_scalar_prefetch=1,
            grid=(B, NA_NBLK + 1),
            in_specs=[pl.BlockSpec((1, S, 3 * NA_W), lambda b, j, idx: (b, 0, 0)),
                      pl.BlockSpec(pair_tiles.shape, lambda b, j, idx: (0, 0, 0, 0)),
                      pl.BlockSpec((1, NA_W), lambda b, j, idx: (0, 0)),
                      pl.BlockSpec((1, NA_W), lambda b, j, idx: (0, 0)),
                      pl.BlockSpec((NA_W, NA_W), lambda b, j, idx: (0, 0))],
            out_specs=pl.BlockSpec((1, S, NA_W), lambda b, j, idx: (b, 0, 0)),
            scratch_shapes=[pltpu.VMEM((S, NA_W), BF16), pltpu.VMEM((NA_QB, NA_KB), BF16)]),
        out_shape=jax.ShapeDtypeStruct((B, S, NA_W), BF16),
        compiler_params=_cparams(("parallel", "arbitrary")),
        name="na_attn",
    )(pair_idx, pna, pair_tiles, qg, kg, seg)


def _out_proj_kernel(z_ref, modx_ref, mody_ref, a_ref, m_ref, n_ref, wa_ref, wm_ref, wn_ref, g2_ref, rw_ref, rb_ref,
                     z1_ref, xg_ref):
    lane = lax.broadcasted_iota(jnp.int32, (TT, 128), 1)
    live = lane < N_EXPERTS
    groups = [slice(i * TT, (i + 1) * TT) for i in range(OT // TT)]

    splits = []
    for rows in groups:
        is_ctx = lax.broadcasted_iota(jnp.int32, (TT, 1), 0) + (pl.program_id(1) * OT + rows.start) >= SEQ
        mod = lambda i: jnp.where(is_ctx, mody_ref[0][:, i * D:(i + 1) * D], modx_ref[0][:, i * D:(i + 1) * D])
        mix = (_dot(a_ref[0, rows, :], wa_ref[...]) + _dot(m_ref[0, rows, :], wm_ref[...])
               + _dot(n_ref[0, rows, :], wn_ref[...]))
        x = z_ref[0, rows, :] + mod(2) * mix
        z1_ref[0, rows, :] = x
        hn = x * lax.rsqrt(jnp.mean(x * x, axis=-1, keepdims=True) + EPS) * g2_ref[...]
        hn = hn * (1.0 + mod(4)) + mod(3)
        h_hi, h_lo = _split_bf16(hn, 2)
        xg_ref[0, rows, :D] = hn
        splits.append((h_hi, h_lo))

    affs = [_sigmoid(_dot(h_hi, rw_ref[0]) + (_dot(h_hi, rw_ref[1]) + _dot(h_lo, rw_ref[0])))
            for h_hi, h_lo in splits]

    def cyc(x, k, width):
        fwd = pltpu.roll(x, 128 - k, 1)
        back = pltpu.roll(x, width - k, 1)
        return jnp.where((lane % width) + k < width, fwd, back)

    def rank(x, width, step):
        r = jnp.zeros((TT, 128), F32)
        for k in range(1, width // step):
            y = cyc(x, k * step, width)
            wrapped = (lane % width) + k * step >= width
            beats = jnp.logical_or(y > x, jnp.logical_and(y == x, wrapped))
            r = r + beats.astype(F32)
        return r

    for rows, aff in zip(groups, affs):
        sel = aff + rb_ref[...]
        top2 = rank(sel, EPG, 1) < 2.0
        part = jnp.where(top2, sel, 0.0)
        gscore = part
        for k in range(1, EPG):
            gscore = gscore + cyc(part, k, EPG)
        best = rank(gscore, N_EXPERTS, EPG) < 1.0
        chosen = jnp.logical_and(jnp.logical_and(top2, best), live)
        w = jnp.where(chosen, aff, 0.0)
        group = jnp.sum(jnp.where(chosen, (lane // EPG).astype(F32), 0.0), axis=-1, keepdims=True) * (1.0 / TOP_K)
        xg_ref[0, rows, D:] = jnp.where(lane == N_EXPERTS, group, w / jnp.sum(w, axis=-1, keepdims=True))


def _out_proj(z, mods, mla_o, ml_o, na_o, wa, wm, wn, g2, rw, rb):
    B = z.shape[0]
    tok = lambda w_: pl.BlockSpec((1, OT, w_), lambda b, t: (b, t, 0))
    full = lambda a: pl.BlockSpec(a.shape, lambda b, t, _n=a.ndim: (0,) * _n)
    return pl.pallas_call(
        _out_proj_kernel,
        grid=(B, S // OT),
        in_specs=[tok(D),
                  pl.BlockSpec((1, 1, 6 * D), lambda b, t: (2 * b, 0, 0)),
                  pl.BlockSpec((1, 1, 6 * D), lambda b, t: (2 * b + 1, 0, 0)),
                  tok(MLA_H * V_D), tok(ML_WP), tok(NA_W),
                  full(wa), full(wm), full(wn), full(g2), full(rw), full(rb)],
        out_specs=[tok(D), tok(XW)],
        out_shape=[jax.ShapeDtypeStruct((B, S, D), F32),
                   jax.ShapeDtypeStruct((B, S, XW), F32)],
        compiler_params=_cparams(("parallel", "parallel")),
        name="out_proj",
    )(z, mods, mods, mla_o, ml_o, na_o, wa, wm, wn, g2, rw, rb)


ROW_WAIT = 128


def _row_copy_all(n, row_copy, slab_copy):
    def issue(i, carry):
        row_copy(i).start()
        return carry

    def drain(i, carry):
        slab_copy(ROW_WAIT).wait()
        return carry

    lax.fori_loop(0, n, issue, 0, unroll=8)
    lax.fori_loop(0, n // ROW_WAIT, drain, 0)


def _dispatch_kernel(pos_ref, src_ref, init_ref, dst_ref, sem):
    del init_ref
    base = pl.program_id(0) * DISPATCH_T
    _row_copy_all(
        DISPATCH_T,
        lambda i: pltpu.make_async_copy(src_ref.at[pl.ds(base + i, 1)], dst_ref.at[pl.ds(pos_ref[base + i], 1)], sem),
        lambda k: pltpu.make_async_copy(src_ref.at[pl.ds(0, k)], dst_ref.at[pl.ds(0, k)], sem))


def _dispatch(pos, rows, n_rows):
    n = rows.shape[0]
    return pl.pallas_call(
        _dispatch_kernel,
        grid_spec=pltpu.PrefetchScalarGridSpec(
            num_scalar_prefetch=1,
            grid=(n // DISPATCH_T,),
            in_specs=[pl.BlockSpec(memory_space=pl.ANY), pl.BlockSpec(memory_space=pl.ANY)],
            out_specs=pl.BlockSpec(memory_space=pl.ANY),
            scratch_shapes=[pltpu.SemaphoreType.DMA(())]),
        out_shape=jax.ShapeDtypeStruct((n_rows, XW), F32),
        input_output_aliases={2: 0},
        compiler_params=_cparams(("arbitrary",)),
        name="moe_dispatch",
    )(pos, rows, jnp.zeros((n_rows, XW), F32))


def _experts_kernel(tg_ref, xs_ref, w1_ref, w3_ref, w2_ref, ys_ref):
    g = tg_ref[pl.program_id(0)]
    x = xs_ref[:, :D].astype(BF16)
    gates = xs_ref[:, D:]
    lane = lax.broadcasted_iota(jnp.int32, (GROUP_T, 128), 1)
    ups = [(_dot(x, w1_ref[0, e].astype(BF16)), _dot(x, w3_ref[0, e].astype(BF16))) for e in range(EPG)]
    acc = jnp.zeros((GROUP_T, D), F32)
    for e in range(EPG):
        ge = jnp.sum(jnp.where(lane == g * EPG + e, gates, 0.0), axis=-1, keepdims=True)
        a, b = ups[e]
        acc = acc + _dot((_silu(a) * b * ge).astype(BF16), w2_ref[0, e].astype(BF16))
    ys_ref[...] = acc


def _experts(tile_group, xs, w1, w3, w2, l):
    n_rows = xs.shape[0]
    wspec = lambda k, n: pl.BlockSpec((1, EPG, k, n), lambda i, tg: (l, tg[i], 0, 0))
    return pl.pallas_call(
        _experts_kernel,
        grid_spec=pltpu.PrefetchScalarGridSpec(
            num_scalar_prefetch=1,
            grid=(n_rows // GROUP_T,),
            in_specs=[pl.BlockSpec((GROUP_T, XW), lambda i, tg: (i, 0)),
                      wspec(D, D_FF), wspec(D, D_FF), wspec(D_FF, D)],
            out_specs=pl.BlockSpec((GROUP_T, D), lambda i, tg: (i, 0))),
        out_shape=jax.ShapeDtypeStruct((n_rows, D), F32),
        compiler_params=_cparams(("arbitrary",)),
        name="moe_experts",
    )(tile_group, xs, w1, w3, w2)


def _combine_kernel(pos_ref, z1_ref, modx_ref, mody_ref, ys_ref, o_ref, buf, sem):
    t = pl.program_id(1)
    base = pl.program_id(0) * S + t * COMBINE_T
    _row_copy_all(
        COMBINE_T,
        lambda i: pltpu.make_async_copy(ys_ref.at[pl.ds(pos_ref[base + i], 1)], buf.at[pl.ds(i, 1)], sem),
        lambda k: pltpu.make_async_copy(ys_ref.at[pl.ds(0, k)], buf.at[pl.ds(0, k)], sem))
    is_ctx = lax.broadcasted_iota(jnp.int32, (COMBINE_T, 1), 0) + t * COMBINE_T >= SEQ
    g2 = jnp.where(is_ctx, mody_ref[0][:, 5 * D:], modx_ref[0][:, 5 * D:])
    o_ref[0] = z1_ref[0] + g2 * buf[...]


def _combine(pos, z1, mods, ys):
    B = z1.shape[0]
    tok = pl.BlockSpec((1, COMBINE_T, D), lambda b, t, pos_: (b, t, 0))
    return pl.pallas_call(
        _combine_kernel,
        grid_spec=pltpu.PrefetchScalarGridSpec(
            num_scalar_prefetch=1,
            grid=(B, S // COMBINE_T),
            in_specs=[tok,
                      pl.BlockSpec((1, 1, 6 * D), lambda b, t, pos_: (2 * b, 0, 0)),
                      pl.BlockSpec((1, 1, 6 * D), lambda b, t, pos_: (2 * b + 1, 0, 0)),
                      pl.BlockSpec(memory_space=pl.ANY)],
            out_specs=tok,
            scratch_shapes=[pltpu.VMEM((COMBINE_T, D), F32), pltpu.SemaphoreType.DMA(())]),
        out_shape=jax.ShapeDtypeStruct((B, S, D), F32),
        compiler_params=_cparams(("arbitrary", "arbitrary")),
        name="moe_combine",
    )(pos, z1, mods, mods, ys)


def _group_layout(gid, n_rows):
    onehot = (gid[:, None] == jnp.arange(N_GROUPS)[None, :]).astype(jnp.int32)
    rank = jnp.cumsum(onehot, axis=0) - onehot
    count = jnp.sum(onehot, axis=0)
    padded = (count + GROUP_T - 1) // GROUP_T * GROUP_T
    end = jnp.cumsum(padded)
    pos = jnp.sum(onehot * (end - padded + rank), axis=1)
    n_tiles = n_rows // GROUP_T
    tile_group = jnp.sum(jnp.arange(n_tiles)[:, None] * GROUP_T >= end[None, :], axis=1)
    return pos.astype(jnp.int32), jnp.minimum(tile_group, N_GROUPS - 1).astype(jnp.int32)


def _moe(z1, xg, mods, w1, w3, w2, l):
    B = z1.shape[0]
    n = B * S
    rows = xg.reshape(n, XW)
    gid = rows[:, D + N_EXPERTS].astype(jnp.int32)
    n_rows = (-(-n // GROUP_T) + N_GROUPS) * GROUP_T
    pos, tile_group = _group_layout(gid, n_rows)
    xs = _dispatch(pos, rows, n_rows)
    ys = _experts(tile_group, xs, w1, w3, w2, l)
    return _combine(pos, z1, mods, ys)


def _in_proj_layout(w):
    cuts = np.cumsum([Q_RANK, KV_RANK, ROPE_D, ML_W, ML_W, 4 * ML_H])
    qc, ckv, kr, u, zz, g, na = jnp.split(w, [int(v) for v in cuts], axis=-1)
    zeros = lambda n: jnp.zeros((w.shape[0], n), w.dtype)
    out = jnp.concatenate([qc, ckv, zeros(NOPE_D), kr, zeros(HP - QK_D),
                           _pad_heads(u, ML_H, ML_D, HP), _pad_heads(zz, ML_H, ML_D, HP),
                           _gate_order(g), zeros(128 - 4 * ML_H), na], axis=-1)
    assert out.shape[-1] == NP_IN
    return out


def _gate_order(g):
    i_f, f_f, i_b, f_b = jnp.split(g, 4, axis=-1)
    return jnp.concatenate([i_f, i_b, f_f, f_b], axis=-1)


def _pad_heads(v, nh, d, dp):
    lead = v.shape[:-1]
    v = v.reshape(lead + (nh, d))
    v = jnp.pad(v, [(0, 0)] * len(lead) + [(0, 0), (0, dp - d)])
    return v.reshape(lead + (nh * dp,))


def _rope_tables():
    t = np.arange(SEQ)
    row = (t // GRID_W).astype(np.float32)
    col = (t % GRID_W).astype(np.float32)
    quarter = ROPE_D // 4
    inv = jnp.asarray(ROPE_BASE, F32) ** (-jnp.arange(quarter, dtype=F32) / quarter)
    ar = jnp.asarray(row)[:, None] * inv
    ac = jnp.asarray(col)[:, None] * inv
    ang = jnp.concatenate([ar, ar, ac, ac], axis=-1)
    cos = jnp.ones((S, HP), F32).at[:SEQ, NOPE_D:QK_D].set(jnp.cos(ang))
    sin = jnp.zeros((S, HP), F32).at[:SEQ, NOPE_D:QK_D].set(jnp.sin(ang))
    return cos, sin


def _rotate_half_index():
    q = ROPE_D // 4
    src = np.arange(QK_D)
    sign = np.zeros((QK_D,), np.float32)
    for blk in range(2):
        lo = NOPE_D + 2 * q * blk
        src[lo:lo + q] = np.arange(lo + q, lo + 2 * q)
        sign[lo:lo + q] = -1.0
        src[lo + q:lo + 2 * q] = np.arange(lo, lo + q)
        sign[lo + q:lo + 2 * q] = 1.0
    return src, sign


def _rotate_half(w):
    src, sign = _rotate_half_index()
    return w[..., src] * sign


NA_NDR = 2 * WIN_R - 1
NA_NPAIR = 3 * NA_NDR


def _na_pair_index():
    idx = np.zeros((NA_NBLK, NA_QROWS, NA_KROWS // 2), np.int32)
    for blk in range(NA_NBLK):
        k0 = int(np.clip(blk * NA_QROWS - WIN_R // 2, 0, ROWS - NA_KROWS))
        for i in range(NA_QROWS):
            qr = blk * NA_QROWS + i
            rs = int(np.clip(qr - WIN_R // 2, 0, ROWS - WIN_R))
            assert k0 <= rs and rs + WIN_R <= k0 + NA_KROWS
            for p in range(NA_KROWS // 2):
                kr = k0 + 2 * p
                dr = kr - qr + WIN_R - 1
                left = rs <= kr < rs + WIN_R
                right = rs <= kr + 1 < rs + WIN_R
                if left and right:
                    idx[blk, i, p] = 1 + dr
                elif left:
                    idx[blk, i, p] = NA_NDR + dr
                elif right:
                    idx[blk, i, p] = 2 * NA_NDR + dr + 1
    return idx.reshape(-1)


def _na_pair_tiles(rpb):
    cq = np.arange(GRID_W)
    cs = np.clip(cq - WIN_C // 2, 0, GRID_W - WIN_C)
    col_ok = (cq[None, :] >= cs[:, None]) & (cq[None, :] < cs[:, None] + WIN_C)
    dc = np.clip(cq[None, :] - cq[:, None], -(WIN_C - 1), WIN_C - 1) + (WIN_C - 1)
    onehot = jnp.asarray(np.eye(2 * WIN_C - 1, dtype=np.float32)[dc])
    tiles = jnp.einsum('hrc,qkc->hrqk', rpb, onehot, precision=lax.Precision.HIGHEST)
    tiles = jnp.where(jnp.asarray(col_ok), tiles * float(np.log2(np.e)), NEG)
    masked = jnp.full_like(tiles, NEG)
    both = jnp.concatenate([tiles[:, :-1], tiles[:, 1:]], axis=-1)
    left = jnp.concatenate([tiles, masked], axis=-1)
    right = jnp.concatenate([masked, tiles], axis=-1)
    none = jnp.concatenate([masked[:, :1], masked[:, :1]], axis=-1)
    out = jnp.concatenate([none, both, left, right], axis=1)
    assert out.shape[1] == NA_NPAIR
    return out.astype(BF16)


def kernel(x, c, ctx, c_ctx, w_mod, b_mod, norm1_g, norm2_g, w_in, w_out, mla_qnorm_g, mla_w_uq, mla_kvnorm_g, mla_w_ukv, mla_q_g, mla_k_g, ml_conv_w, ml_conv_b, ml_w_q, ml_w_k, ml_w_v, ml_gate_b, ml_norm_g, ml_skip, na_q_g, na_k_g, na_rpb, router_w, router_b, moe_w1, moe_w3, moe_w2):
    B = x.shape[0]
    z = jnp.concatenate([x, ctx], axis=1)
    cc = jnp.zeros((16, D), F32).at[:B].set(c).at[B].set(c_ctx)
    mod_all = _modulation(cc, w_mod, b_mod)
    cos, sin = _rope_tables()
    rot_src, rot_sign = _rotate_half_index()
    rot_np = np.zeros((HP, HP), np.float32)
    rot_np[rot_src, np.arange(QK_D)] = rot_sign
    rot_mat = jnp.asarray(rot_np, BF16)
    ones_hp = jnp.ones((HP, HP), BF16)
    seg = jnp.asarray(np.kron(np.eye(NA_H), np.ones((NA_D, NA_D))), BF16)
    tril = jnp.asarray(np.stack([np.tril(np.ones((ML_CHUNK, ML_CHUNK))), np.triu(np.ones((ML_CHUNK, ML_CHUNK)))]), BF16)
    pair_idx = jnp.asarray(_na_pair_index())
    rw = jnp.stack(_split_bf16(jnp.pad(router_w, ((0, 0), (0, 128 - N_EXPERTS))), 2))
    rb = jnp.pad(router_b, (0, 128 - N_EXPERTS), constant_values=NEG).reshape(1, 128)

    def pad_lanes(v, n):
        return jnp.pad(v, [(0, 0)] * (v.ndim - 1) + [(0, n - v.shape[-1])])

    for l in range(DEPTH):
        mx = mod_all[l, :B]
        my = jnp.broadcast_to(mod_all[l, B], (B, 6 * D))
        mods = jnp.stack([mx, my], axis=1).reshape(2 * B, 1, 6 * D)

        w_in_p = _in_proj_layout(w_in[l]).astype(BF16)
        pmla, pu, pg, pna = _in_proj(z, mods, norm1_g[l].reshape(1, D), w_in_p)

        wq = jnp.transpose(mla_w_uq[l].reshape(Q_RANK, MLA_H, QK_D), (1, 0, 2))
        wuq = jnp.concatenate([pad_lanes(wq, HP), pad_lanes(_rotate_half(wq), HP)], axis=-1).astype(BF16)
        wukv = jnp.transpose(mla_w_ukv[l].reshape(KV_RANK, MLA_H, NOPE_D + V_D), (1, 0, 2))
        wuk = pad_lanes(wukv[..., :NOPE_D], HP).astype(BF16)
        wuv = pad_lanes(wukv[..., NOPE_D:], HP).astype(BF16)
        qg, kg = mla_q_g[l], mla_k_g[l]
        q_scale = float(QK_D ** -0.5 * np.log2(np.e))
        tabs = jnp.stack([cos * pad_lanes(qg, HP) * q_scale, sin * pad_lanes(jnp.abs(rot_sign) * qg[rot_src], HP) * q_scale,
                          cos * pad_lanes(kg, HP), sin * pad_lanes(jnp.abs(rot_sign) * kg[rot_src], HP)])
        heads_last = lambda w_: jnp.transpose(w_, (1, 0, 2)).reshape(w_.shape[1], -1)
        q, k, v = _mla_prep(pmla, tabs, mla_qnorm_g[l].reshape(1, Q_RANK), heads_last(wuq),
                            mla_kvnorm_g[l].reshape(1, KV_RANK), heads_last(wuk), heads_last(wuv), rot_mat, ones_hp)
        mla_o = _mla_attn(q, k, v)

        padh = lambda a: _pad_heads(a, ML_H, ML_D, HP)
        padw = lambda w_: jnp.pad(w_, ((0, 0), (0, HP - ML_D), (0, HP - ML_D))).astype(BF16)
        cw = jnp.pad(padh(ml_conv_w[l]), ((0, 8 - 3), (0, 0)))
        ml_o = _mlstm(pu, pg, cw, padh(ml_conv_b[l]).reshape(1, ML_WP),
                      padw(ml_w_q[l]), padw(jnp.swapaxes(ml_w_k[l], 1, 2) * (ML_D ** -0.5)), padw(ml_w_v[l]),
                      pad_lanes(_gate_order(ml_gate_b[l]).reshape(1, 4 * ML_H), 128),
                      padh(ml_norm_g[l]).reshape(1, ML_WP), padh(ml_skip[l]).reshape(1, ML_WP), tril)

        na_o = _na_attn(pair_idx, pna, _na_pair_tiles(na_rpb[l]), jnp.tile(na_q_g[l], NA_H).reshape(1, NA_W),
                        jnp.tile(na_k_g[l], NA_H).reshape(1, NA_W), seg)

        wo = w_out[l]
        wa = wo[:MLA_H * V_D].astype(BF16)
        wm = jnp.pad(wo[MLA_H * V_D:MLA_H * V_D + ML_W].reshape(ML_H, ML_D, D),
                     ((0, 0), (0, HP - ML_D), (0, 0))).reshape(ML_WP, D).astype(BF16)
        wn = wo[MLA_H * V_D + ML_W:].astype(BF16)
        z1, xg = _out_proj(z, mods, mla_o, ml_o, na_o, wa, wm, wn, norm2_g[l].reshape(1, D), rw, rb)
        z = _moe(z1, xg, mods, moe_w1, moe_w3, moe_w2, l)

    return z[:, :SEQ]
```

```python
import functools

import numpy as np
import jax
import jax.numpy as jnp
from jax import lax
from jax.experimental import pallas as pl
from jax.experimental.pallas import tpu as pltpu

F32 = jnp.float32
BF16 = jnp.bfloat16

D = 1024
SEQ = 2048
CTX = 256
S = SEQ + CTX
DEPTH = 4
GRID_W = 64
ROWS = SEQ // GRID_W
EPS = 1e-6

MLA_H = 6
Q_RANK = 256
KV_RANK = 128
NOPE_D = 64
ROPE_D = 32
V_D = 64
QK_D = NOPE_D + ROPE_D
ROPE_BASE = 10000.0

ML_H = 4
ML_D = 96
ML_W = ML_H * ML_D
HP = 128
ML_WP = ML_H * HP
ML_CHUNK = 128
N_CHUNK = S // ML_CHUNK
N_CTX_CHUNK = CTX // ML_CHUNK

NA_H = 4
NA_D = 64
NA_W = NA_H * NA_D
WIN_R = 8
WIN_C = 16
NA_QROWS = 4
NA_KROWS = 12
NA_QB = NA_QROWS * GRID_W
NA_KB = NA_KROWS * GRID_W
NA_NBLK = ROWS // NA_QROWS

N_EXPERTS = 16
N_GROUPS = 4
EPG = N_EXPERTS // N_GROUPS
D_FF = 256

TT = 256
NT = S // TT
OT = 768
TOP_K = 2
XW = D + 128
GROUP_T = 512
COMBINE_T = 768
NEG = -1e30

C_QC = 0
C_CKV = 256
C_KR = 384
C_U = 512
C_Z = C_U + ML_WP
C_G = C_Z + ML_WP
C_NA = C_G + 128
NP_IN = C_NA + 3 * NA_W

VMEM_LIMIT = 56 * 1024 * 1024


def _cparams(sem):
    return pltpu.CompilerParams(dimension_semantics=sem, vmem_limit_bytes=VMEM_LIMIT)


def _sigmoid(x):
    return 1.0 / (1.0 + jnp.exp(-x))


def _silu(x):
    return x * _sigmoid(x)


def _dot(a, b):
    return jnp.dot(a, b, preferred_element_type=F32)


def _dot_nt(a, b):
    return lax.dot_general(a, b, (((1,), (1,)), ((), ())), preferred_element_type=F32)


def _dot_tn(a, b):
    return lax.dot_general(a, b, (((0,), (0,)), ((), ())), preferred_element_type=F32)


def _dot_hi(a, b):
    return jnp.dot(a, b, preferred_element_type=F32, precision=lax.Precision.HIGHEST)


def _split_bf16(x, n):
    parts = []
    for _ in range(n):
        p = x.astype(BF16)
        parts.append(p)
        x = x - p.astype(F32)
    return parts


def _mod_rows(mod_ref, t):
    m = mod_ref[0]
    return [m[:, i * D:(i + 1) * D] for i in range(6)]


def _mod_kernel(c_ref, w_ref, b_ref, o_ref):
    sc = _silu(c_ref[...])
    o_ref[0] = _dot_hi(sc, w_ref[0]) + b_ref[0]


def _modulation(cc, w_mod, b_mod):
    nc = 6
    return pl.pallas_call(
        _mod_kernel,
        grid=(DEPTH, nc),
        in_specs=[pl.BlockSpec((16, D), lambda l, j: (0, 0)),
                  pl.BlockSpec((1, D, D), lambda l, j: (l, 0, j)),
                  pl.BlockSpec((1, 1, D), lambda l, j: (l, 0, j))],
        out_specs=pl.BlockSpec((1, 16, D), lambda l, j: (l, 0, j)),
        out_shape=jax.ShapeDtypeStruct((DEPTH, 16, 6 * D), F32),
        compiler_params=_cparams(("parallel", "parallel")),
        name="modulation",
    )(cc, w_mod, b_mod.reshape(DEPTH, 1, 6 * D))


def _in_proj_kernel(z_ref, mod_ref, g_ref, w_ref, pmla_ref, pu_ref, pg_ref, pna_ref):
    sh1, sc1 = _mod_rows(mod_ref, None)[:2]
    x = z_ref[0]
    xn = x * lax.rsqrt(jnp.mean(x * x, axis=-1, keepdims=True) + EPS) * g_ref[...]
    xn = xn * (1.0 + sc1) + sh1
    p = _dot(xn.astype(BF16), w_ref[...])
    pmla_ref[0] = p[:, :C_U].astype(BF16)
    pu_ref[0] = p[:, C_U:C_G].astype(BF16)
    pg_ref[0] = p[:, C_G:C_NA]
    pna_ref[0] = p[:, C_NA:].astype(BF16)


def _mod_spec():
    return pl.BlockSpec((1, 1, 6 * D), lambda b, t: (2 * b + t // (NT - 1), 0, 0))


def _in_proj(z, mods, g, w):
    B = z.shape[0]
    tok = lambda w_: pl.BlockSpec((1, TT, w_), lambda b, t: (b, t, 0))
    return pl.pallas_call(
        _in_proj_kernel,
        grid=(B, NT),
        in_specs=[tok(D), _mod_spec(),
                  pl.BlockSpec((1, D), lambda b, t: (0, 0)),
                  pl.BlockSpec((D, NP_IN), lambda b, t: (0, 0))],
        out_specs=[tok(C_U), tok(2 * ML_WP), tok(128), tok(3 * NA_W)],
        out_shape=[jax.ShapeDtypeStruct((B, S, C_U), BF16),
                   jax.ShapeDtypeStruct((B, S, 2 * ML_WP), BF16),
                   jax.ShapeDtypeStruct((B, S, 128), F32),
                   jax.ShapeDtypeStruct((B, S, 3 * NA_W), BF16)],
        compiler_params=_cparams(("parallel", "parallel")),
        name="in_proj",
    )(z, mods, g, w)


def _mla_prep_kernel(p_ref, tab_ref, qng_ref, wuq_ref, kvng_ref, wuk_ref, wuv_ref, rot_ref, ones_ref,
                     q_out, k_out, v_out):
    p = p_ref[0].astype(F32)
    qc = p[:, C_QC:C_CKV]
    ckv = p[:, C_CKV:C_KR]
    kr = p[:, C_KR:C_U]
    qcn = (qc * lax.rsqrt(jnp.mean(qc * qc, axis=-1, keepdims=True) + EPS) * qng_ref[...]).astype(BF16)
    ckvn = (ckv * lax.rsqrt(jnp.mean(ckv * ckv, axis=-1, keepdims=True) + EPS) * kvng_ref[...]).astype(BF16)
    lane = lax.broadcasted_iota(jnp.int32, (TT, HP), 1)
    ones = ones_ref[...]
    kr_rot = _dot(kr.astype(BF16), rot_ref[...])

    q_all = _dot(qcn, wuq_ref[...])
    k_all = _dot(ckvn, wuk_ref[...])
    v_all = _dot(ckvn, wuv_ref[...])
    qs = [q_all[:, 2 * h * HP:(2 * h + 1) * HP] for h in range(MLA_H)]
    q_rots = [q_all[:, (2 * h + 1) * HP:(2 * h + 2) * HP] for h in range(MLA_H)]
    ks = [k_all[:, h * HP:(h + 1) * HP] + kr for h in range(MLA_H)]
    ss_q = [_dot((x * x).astype(BF16), ones) for x in qs]
    ss_k = [_dot((x * x).astype(BF16), ones) for x in ks]

    def norm_rope(x, x_rot, ss, cos_g, sin_g):
        return lax.rsqrt(ss * (1.0 / QK_D) + EPS) * (x * cos_g + x_rot * sin_g)

    for h in range(MLA_H):
        q_out[0, h] = norm_rope(qs[h], q_rots[h], ss_q[h], tab_ref[0], tab_ref[1]).astype(BF16)
        k_out[0, h] = norm_rope(ks[h], kr_rot, ss_k[h], tab_ref[2], tab_ref[3]).astype(BF16)
        v_out[0, h] = jnp.where(lane < V_D, v_all[:, h * HP:(h + 1) * HP], 1.0).astype(BF16)


def _mla_prep(pmla, tabs, qng, wuq, kvng, wuk, wuv, rot, ones):
    B = pmla.shape[0]
    full = lambda a: pl.BlockSpec(a.shape, lambda b, t, _n=a.ndim: (0,) * _n)
    hd = lambda w_: pl.BlockSpec((1, MLA_H, TT, w_), lambda b, t: (b, 0, t, 0))
    return pl.pallas_call(
        _mla_prep_kernel,
        grid=(B, NT),
        in_specs=[pl.BlockSpec((1, TT, C_U), lambda b, t: (b, t, 0)),
                  pl.BlockSpec((4, TT, HP), lambda b, t: (0, t, 0)),
                  full(qng), full(wuq), full(kvng), full(wuk), full(wuv), full(rot), full(ones)],
        out_specs=[hd(HP), hd(HP), hd(HP)],
        out_shape=[jax.ShapeDtypeStruct((B, MLA_H, S, HP), BF16),
                   jax.ShapeDtypeStruct((B, MLA_H, S, HP), BF16),
                   jax.ShapeDtypeStruct((B, MLA_H, S, HP), BF16)],
        compiler_params=_cparams(("parallel", "parallel")),
        name="mla_prep",
    )(pmla, tabs, qng, wuq, kvng, wuk, wuv, rot, ones)


def _mla_attn_kernel(q_ref, k_ref, v_ref, o_ref):
    t = pl.program_id(1)

    def attend(k_lo, k_n):
        def scores(h):
            return _dot_nt(q_ref[0, h], k_ref[0, h, k_lo:k_lo + k_n, :])

        outs = []
        s_next = scores(0)
        for h in range(MLA_H):
            s = s_next
            if h + 1 < MLA_H:
                s_next = scores(h + 1)
            m = jnp.max(s, axis=-1, keepdims=True)
            p = jnp.exp2(s - m)
            pv = _dot(p.astype(BF16), v_ref[0, h, k_lo:k_lo + k_n, :])
            outs.append(pv[:, :V_D] / pv[:, V_D:V_D + 1])
        o_ref[0] = jnp.concatenate(outs, axis=-1).astype(BF16)

    @pl.when(t < NT - 1)
    def _():
        attend(0, S)

    @pl.when(t == NT - 1)
    def _():
        attend(SEQ, CTX)


def _mla_attn(q, k, v):
    B = q.shape[0]
    return pl.pallas_call(
        _mla_attn_kernel,
        grid=(B, NT),
        in_specs=[pl.BlockSpec((1, MLA_H, TT, HP), lambda b, t: (b, 0, t, 0)),
                  pl.BlockSpec((1, MLA_H, S, HP), lambda b, t: (b, 0, 0, 0)),
                  pl.BlockSpec((1, MLA_H, S, HP), lambda b, t: (b, 0, 0, 0))],
        out_specs=pl.BlockSpec((1, TT, MLA_H * V_D), lambda b, t: (b, t, 0)),
        out_shape=jax.ShapeDtypeStruct((B, S, MLA_H * V_D), BF16),
        compiler_params=_cparams(("parallel", "arbitrary")),
        name="mla_attn",
    )(q, k, v)


def _log_sigmoid(x):
    return jnp.minimum(x, 0.0) - jnp.log(1.0 + jnp.exp(-jnp.abs(x)))


def _mlstm_kernel(pu_ref, pg_ref, cw_ref, cb_ref, wq_ref, wk_ref, wv_ref, gb_ref, ng_ref, sk_ref,
                  tril_ref, o_ref, uc_s, q_s, kt_s, v_s, h_s, c_s, m_s, pm_s, b_s, rt_s):
    CA = 2 * ML_CHUNK
    row = lax.broadcasted_iota(jnp.int32, (CA, ML_WP), 0)

    def conv_body(i, carry):
        r0 = pl.multiple_of(i * CA, CA)
        x = pu_ref[0, pl.ds(r0, CA), 0:ML_WP].astype(F32)
        pr = pl.multiple_of(jnp.maximum(r0 - 16, 0), 16)
        nx = pl.multiple_of(jnp.minimum(r0 + CA, S - 16), 16)
        prev = pu_ref[0, pl.ds(pr, 16), 0:ML_WP].astype(F32)[15:16, :]
        nxt = pu_ref[0, pl.ds(nx, 16), 0:ML_WP].astype(F32)[0:1, :]
        seq_start = jnp.logical_or(r0 == 0, r0 == SEQ)
        seq_end = jnp.logical_or(r0 + CA == SEQ, r0 + CA == S)
        prev = jnp.where(seq_start, 0.0, prev)
        nxt = jnp.where(seq_end, 0.0, nxt)
        up = jnp.where(row == 0, prev, pltpu.roll(x, 1, 0))
        dn = jnp.where(row == CA - 1, nxt, pltpu.roll(x, CA - 1, 0))
        uc = _silu(cw_ref[0:1, :] * up + cw_ref[1:2, :] * x + cw_ref[2:3, :] * dn + cb_ref[...])
        ucb = uc.astype(BF16)
        uc_s[pl.ds(r0, CA), :] = ucb
        xb = x.astype(BF16)
        for h in range(ML_H):
            sl = slice(h * HP, (h + 1) * HP)
            q_s[pl.ds(r0, CA), sl] = _dot(ucb[:, sl], wq_ref[h]).astype(BF16)
            kt = _dot_nt(wk_ref[h], ucb[:, sl])
            kt_s[2 * i, sl, :] = kt[:, :ML_CHUNK].astype(BF16)
            kt_s[2 * i + 1, sl, :] = kt[:, ML_CHUNK:].astype(BF16)
            v_s[pl.ds(r0, CA), 2 * h * HP:(2 * h + 1) * HP] = _dot(xb[:, sl], wv_ref[h]).astype(BF16)
            v_s[pl.ds(r0, CA), (2 * h + 1) * HP:(2 * h + 2) * HP] = jnp.ones((CA, HP), BF16)

        for half in range(2):
            rows = pl.ds(r0 + half * ML_CHUNK, ML_CHUNK)
            g = pg_ref[0, rows, :] + gb_ref[...]
            parts = _split_bf16(_log_sigmoid(g), 3)
            cum_f = sum(_dot(tril_ref[0], part) for part in parts)
            cum_b = sum(_dot(tril_ref[1], part) for part in parts)
            bsh = pltpu.roll(jnp.where(bwd_lane, cum_b, cum_f), 128 - n_chain, 1)
            r = g - bsh
            pf = r
            pb = r
            k = 1
            while k < ML_CHUNK:
                pf = jnp.maximum(pf, jnp.where(ti >= k, pltpu.roll(pf, k, 0), NEG))
                pb = jnp.maximum(pb, jnp.where(ti < ML_CHUNK - k, pltpu.roll(pb, ML_CHUNK - k, 0), NEG))
                k *= 2
            pm_s[2 * i + half] = jnp.where(bwd_lane, pb, pf)
            b_s[2 * i + half] = bsh
            rt_s[2 * i + half] = r.T[0:n_chain, :]
        return carry

    n_chain = 2 * ML_H
    ti = lax.broadcasted_iota(jnp.int32, (ML_CHUNK, ML_CHUNK), 0)
    si = lax.broadcasted_iota(jnp.int32, (ML_CHUNK, ML_CHUNK), 1)
    bwd_lane = (si % n_chain) >= ML_H
    lax.fori_loop(0, S // CA, conv_body, 0)

    c_s[...] = jnp.zeros_like(c_s)
    m_s[...] = jnp.zeros_like(m_s)
    masks = (si <= ti, si >= ti)

    def scan_body(j, carry):
        chunk = (jnp.where(j < N_CTX_CHUNK, j + N_CHUNK - N_CTX_CHUNK, j - N_CTX_CHUNK), N_CHUNK - 1 - j)
        chains = []
        for d in range(2):
            r0 = pl.multiple_of(chunk[d] * ML_CHUNK, ML_CHUNK)
            p_col = pm_s[chunk[d]]
            bsh = b_s[chunk[d]]
            r_t = rt_s[chunk[d]]
            end = ML_CHUNK - 1 if d == 0 else 0
            for h in range(ML_H):
                c = d * ML_H + h
                sl = slice(h * HP, (h + 1) * HP)
                qc = q_s[pl.ds(r0, ML_CHUNK), sl]
                kt = kt_s[chunk[d], sl, :]
                vx = v_s[pl.ds(r0, ML_CHUNK), 2 * h * HP:(2 * h + 2) * HP]
                r_row = r_t[c:c + 1, :]
                m = m_s[c]
                st = c_s[c]
                big_m = jnp.maximum(m, jnp.broadcast_to(p_col[:, c:c + 1], (ML_CHUNK, HP)))
                b_b = jnp.broadcast_to(bsh[:, c:c + 1], (ML_CHUNK, HP))
                m_end = big_m[end:end + 1, :]
                ktw = (kt.astype(F32) * jnp.exp(r_row - m_end)).astype(BF16)
                chains.append(dict(d=d, r0=r0, sl=sl, c=c, vx=vx, r_row=r_row, m=m, st=st, big_m=big_m,
                                   b_b=b_b, m_end=m_end, end=end,
                                   qk=_dot(qc, kt), inter=_dot(qc, st.astype(BF16)), upd=_dot(ktw, vx)))
        for ch in chains:
            dw = jnp.exp(jnp.where(masks[ch["d"]], ch["r_row"] - ch["big_m"], NEG))
            ch["intra"] = _dot((ch["qk"] * dw).astype(BF16), ch["vx"])
        for ch in chains:
            m, big_m, inter, intra, end = ch["m"], ch["big_m"], ch["inter"], ch["intra"], ch["end"]
            iw = jnp.exp(m - big_m)
            num = iw * inter[:, :HP] + intra[:, :HP]
            nq = iw * inter[:, HP:] + intra[:, HP:]
            hv = num / jnp.maximum(jnp.abs(nq), jnp.exp(-(ch["b_b"] + big_m)))
            a = jnp.exp(m - ch["m_end"])
            ch["out"] = (hv, jnp.concatenate([a, a], axis=1) * ch["st"] + ch["upd"],
                         ch["b_b"][end:end + 1, :] + ch["m_end"])
        for ch in chains:
            hv, st_new, m_new = ch["out"]
            h_s[ch["d"], pl.ds(ch["r0"], ML_CHUNK), ch["sl"]] = hv
            c_s[ch["c"]] = st_new
            m_s[ch["c"]] = m_new
        return carry

    lax.fori_loop(0, N_CHUNK, scan_body, 0)

    live = (lax.broadcasted_iota(jnp.int32, (CA, HP), 1) < ML_D).astype(F32)

    def out_body(i, carry):
        r0 = pl.multiple_of(i * CA, CA)
        for h in range(ML_H):
            sl = slice(h * HP, (h + 1) * HP)
            hh = h_s[0, pl.ds(r0, CA), sl] + h_s[1, pl.ds(r0, CA), sl]
            mu = jnp.sum(hh, axis=-1, keepdims=True) * (1.0 / ML_D)
            dv = (hh - mu) * live
            var = jnp.sum(dv * dv, axis=-1, keepdims=True) * (1.0 / ML_D)
            hn = dv * lax.rsqrt(var + EPS) * ng_ref[:, sl]
            uc = uc_s[pl.ds(r0, CA), sl].astype(F32)
            zz = pu_ref[0, pl.ds(r0, CA), ML_WP + h * HP:ML_WP + (h + 1) * HP].astype(F32)
            o_ref[0, pl.ds(r0, CA), sl] = ((hn + sk_ref[:, sl] * uc) * _silu(zz)).astype(BF16)
        return carry

    lax.fori_loop(0, S // CA, out_body, 0)


def _mlstm(pu, pg, cw, cb, wq, wk, wv, gb, ng, sk, tril):
    B = pu.shape[0]
    full = lambda a: pl.BlockSpec(a.shape, lambda b, _n=a.ndim: (0,) * _n)
    n_chain = 2 * ML_H
    return pl.pallas_call(
        _mlstm_kernel,
        grid=(B,),
        in_specs=[pl.BlockSpec((1, S, 2 * ML_WP), lambda b: (b, 0, 0)),
                  pl.BlockSpec((1, S, 128), lambda b: (b, 0, 0)),
                  full(cw), full(cb), full(wq), full(wk), full(wv), full(gb), full(ng), full(sk), full(tril)],
        out_specs=pl.BlockSpec((1, S, ML_WP), lambda b: (b, 0, 0)),
        out_shape=jax.ShapeDtypeStruct((B, S, ML_WP), BF16),
        scratch_shapes=[pltpu.VMEM((S, ML_WP), BF16), pltpu.VMEM((S, ML_WP), BF16),
                        pltpu.VMEM((N_CHUNK, ML_WP, ML_CHUNK), BF16), pltpu.VMEM((S, 2 * ML_WP), BF16),
                        pltpu.VMEM((2, S, ML_WP), F32),
                        pltpu.VMEM((n_chain, HP, 2 * HP), F32),
                        pltpu.VMEM((n_chain, 1, HP), F32),
                        pltpu.VMEM((N_CHUNK, ML_CHUNK, 128), F32),
                        pltpu.VMEM((N_CHUNK, ML_CHUNK, 128), F32),
                        pltpu.VMEM((N_CHUNK, n_chain, ML_CHUNK), F32)],
        compiler_params=_cparams(("parallel",)),
        name="mlstm",
    )(pu, pg, cw, cb, wq, wk, wv, gb, ng, sk, tril)


def _na_kernel(idx_ref, p_ref, pt_ref, qg_ref, kg_ref, seg_ref, o_ref, kn_s, bias_s):
    j = pl.program_id(1)
    seg = seg_ref[...]

    def headnorm(x, g):
        ss = _dot((x * x).astype(BF16), seg)
        return x * lax.rsqrt(ss * (1.0 / NA_D) + EPS) * g

    @pl.when(j == 0)
    def _():
        def body(i, carry):
            r0 = pl.multiple_of(i * TT, TT)
            kk = p_ref[0, pl.ds(r0, TT), NA_W:2 * NA_W].astype(F32)
            kn_s[pl.ds(r0, TT), :] = headnorm(kk, kg_ref[...]).astype(BF16)
            return carry
        lax.fori_loop(0, NT, body, 0)

    scale = float(NA_D ** -0.5 * np.log2(np.e))
    kctx = kn_s[SEQ:S, :]
    vctx = p_ref[0, SEQ:S, 2 * NA_W:3 * NA_W]

    @pl.when(j < NA_NBLK)
    def _():
        q0 = pl.multiple_of(j * NA_QB, NA_QB)
        k0 = pl.multiple_of(jnp.clip(j * NA_QROWS - WIN_R // 2, 0, ROWS - NA_KROWS) * GRID_W, 256)
        q = headnorm(p_ref[0, pl.ds(q0, NA_QB), 0:NA_W].astype(F32), qg_ref[...]) * scale
        kl = kn_s[pl.ds(k0, NA_KB), :]
        vl = p_ref[0, pl.ds(k0, NA_KB), 2 * NA_W:3 * NA_W]
        head = lax.broadcasted_iota(jnp.int32, (NA_QB, NA_W), 1) // NA_D
        acc = jnp.zeros((NA_QB, NA_W), F32)

        def scores(h):
            qm = jnp.where(head == h, q, 0.0).astype(BF16)
            for i in range(NA_QROWS):
                for p in range(NA_KROWS // 2):
                    code = idx_ref[(j * NA_QROWS + i) * (NA_KROWS // 2) + p]
                    bias_s[i * GRID_W:(i + 1) * GRID_W, p * 2 * GRID_W:(p + 1) * 2 * GRID_W] = pt_ref[h, code]
            return _dot_nt(qm, kl) + bias_s[...].astype(F32), _dot_nt(qm, kctx)

        s_next = scores(0)
        for h in range(NA_H):
            s1, s2 = s_next
            if h + 1 < NA_H:
                s_next = scores(h + 1)
            m = jnp.maximum(jnp.max(s1, axis=-1, keepdims=True), jnp.max(s2, axis=-1, keepdims=True))
            p1 = jnp.exp2(s1 - m)
            p2 = jnp.exp2(s2 - m)
            l = jnp.sum(p1, axis=-1, keepdims=True) + jnp.sum(p2, axis=-1, keepdims=True)
            o = (_dot(p1.astype(BF16), vl) + _dot(p2.astype(BF16), vctx)) / l
            acc = jnp.where(head == h, o, acc)
        o_ref[0, pl.ds(q0, NA_QB), :] = acc.astype(BF16)

    @pl.when(j == NA_NBLK)
    def _():
        q = headnorm(p_ref[0, SEQ:S, 0:NA_W].astype(F32), qg_ref[...]) * scale
        head = lax.broadcasted_iota(jnp.int32, (CTX, NA_W), 1) // NA_D
        acc = jnp.zeros((CTX, NA_W), F32)
        for h in range(NA_H):
            qm = jnp.where(head == h, q, 0.0).astype(BF16)
            s2 = _dot_nt(qm, kctx)
            m = jnp.max(s2, axis=-1, keepdims=True)
            p2 = jnp.exp2(s2 - m)
            l = jnp.sum(p2, axis=-1, keepdims=True)
            o = _dot(p2.astype(BF16), vctx) / l
            acc = jnp.where(head == h, o, acc)
        o_ref[0, SEQ:S, :] = acc.astype(BF16)


def _na_attn(pair_idx, pna, pair_tiles, qg, kg, seg):
    B = pna.shape[0]
    return pl.pallas_call(
        _na_kernel,
        grid_spec=pltpu.PrefetchScalarGridSpec(
            num_scalar_prefetch=1,
            grid=(B, NA_NBLK + 1),
            in_specs=[pl.BlockSpec((1, S, 3 * NA_W), lambda b, j, idx: (b, 0, 0)),
                      pl.BlockSpec(pair_tiles.shape, lambda b, j, idx: (0, 0, 0, 0)),
                      pl.BlockSpec((1, NA_W), lambda b, j, idx: (0, 0)),
                      pl.BlockSpec((1, NA_W), lambda b, j, idx: (0, 0)),
                      pl.BlockSpec((NA_W, NA_W), lambda b, j, idx: (0, 0))],
            out_specs=pl.BlockSpec((1, S, NA_W), lambda b, j, idx: (b, 0, 0)),
            scratch_shapes=[pltpu.VMEM((S, NA_W), BF16), pltpu.VMEM((NA_QB, NA_KB), BF16)]),
        out_shape=jax.ShapeDtypeStruct((B, S, NA_W), BF16),
        compiler_params=_cparams(("parallel", "arbitrary")),
        name="na_attn",
    )(pair_idx, pna, pair_tiles, qg, kg, seg)


def _out_proj_kernel(z_ref, modx_ref, mody_ref, a_ref, m_ref, n_ref, wa_ref, wm_ref, wn_ref, g2_ref, rw_ref, rb_ref,
                     z1_ref, xg_ref):
    lane = lax.broadcasted_iota(jnp.int32, (TT, 128), 1)
    live = lane < N_EXPERTS
    groups = [slice(i * TT, (i + 1) * TT) for i in range(OT // TT)]

    splits = []
    for rows in groups:
        is_ctx = lax.broadcasted_iota(jnp.int32, (TT, 1), 0) + (pl.program_id(1) * OT + rows.start) >= SEQ
        mod = lambda i: jnp.where(is_ctx, mody_ref[0][:, i * D:(i + 1) * D], modx_ref[0][:, i * D:(i + 1) * D])
        mix = (_dot(a_ref[0, rows, :], wa_ref[...]) + _dot(m_ref[0, rows, :], wm_ref[...])
               + _dot(n_ref[0, rows, :], wn_ref[...]))
        x = z_ref[0, rows, :] + mod(2) * mix
        z1_ref[0, rows, :] = x
        hn = x * lax.rsqrt(jnp.mean(x * x, axis=-1, keepdims=True) + EPS) * g2_ref[...]
        hn = hn * (1.0 + mod(4)) + mod(3)
        h_hi, h_lo = _split_bf16(hn, 2)
        xg_ref[0, rows, :D] = hn
        splits.append((h_hi, h_lo))

    affs = [_sigmoid(_dot(h_hi, rw_ref[0]) + (_dot(h_hi, rw_ref[1]) + _dot(h_lo, rw_ref[0])))
            for h_hi, h_lo in splits]

    def cyc(x, k, width):
        fwd = pltpu.roll(x, 128 - k, 1)
        back = pltpu.roll(x, width - k, 1)
        return jnp.where((lane % width) + k < width, fwd, back)

    def rank(x, width, step):
        r = jnp.zeros((TT, 128), F32)
        for k in range(1, width // step):
            y = cyc(x, k * step, width)
            wrapped = (lane % width) + k * step >= width
            beats = jnp.logical_or(y > x, jnp.logical_and(y == x, wrapped))
            r = r + beats.astype(F32)
        return r

    for rows, aff in zip(groups, affs):
        sel = aff + rb_ref[...]
        top2 = rank(sel, EPG, 1) < 2.0
        part = jnp.where(top2, sel, 0.0)
        gscore = part
        for k in range(1, EPG):
            gscore = gscore + cyc(part, k, EPG)
        best = rank(gscore, N_EXPERTS, EPG) < 1.0
        chosen = jnp.logical_and(jnp.logical_and(top2, best), live)
        w = jnp.where(chosen, aff, 0.0)
        group = jnp.sum(jnp.where(chosen, (lane // EPG).astype(F32), 0.0), axis=-1, keepdims=True) * (1.0 / TOP_K)
        xg_ref[0, rows, D:] = jnp.where(lane == N_EXPERTS, group, w / jnp.sum(w, axis=-1, keepdims=True))


def _out_proj(z, mods, mla_o, ml_o, na_o, wa, wm, wn, g2, rw, rb):
    B = z.shape[0]
    tok = lambda w_: pl.BlockSpec((1, OT, w_), lambda b, t: (b, t, 0))
    full = lambda a: pl.BlockSpec(a.shape, lambda b, t, _n=a.ndim: (0,) * _n)
    return pl.pallas_call(
        _out_proj_kernel,
        grid=(B, S // OT),
        in_specs=[tok(D),
                  pl.BlockSpec((1, 1, 6 * D), lambda b, t: (2 * b, 0, 0)),
                  pl.BlockSpec((1, 1, 6 * D), lambda b, t: (2 * b + 1, 0, 0)),
                  tok(MLA_H * V_D), tok(ML_WP), tok(NA_W),
                  full(wa), full(wm), full(wn), full(g2), full(rw), full(rb)],
        out_specs=[tok(D), tok(XW)],
        out_shape=[jax.ShapeDtypeStruct((B, S, D), F32),
                   jax.ShapeDtypeStruct((B, S, XW), F32)],
        compiler_params=_cparams(("parallel", "parallel")),
        name="out_proj",
    )(z, mods, mods, mla_o, ml_o, na_o, wa, wm, wn, g2, rw, rb)


ROW_WAIT = 128


def _row_copy_start(n, row_copy):
    def issue(i, carry):
        row_copy(i).start()
        return carry

    lax.fori_loop(0, n, issue, 0, unroll=8)


def _row_copy_wait(n, slab_copy):
    def drain(i, carry):
        slab_copy(ROW_WAIT).wait()
        return carry

    lax.fori_loop(0, n // ROW_WAIT, drain, 0)


def _experts_kernel(pos_ref, tg_ref, rows_ref, w1_ref, w3_ref, w2_ref, ys_ref, src_s, buf, sems):
    i = pl.program_id(0)
    n_rows = src_s.shape[0]

    def gather(tile, slot):
        return (lambda r: pltpu.make_async_copy(rows_ref.at[pl.ds(src_s[tile * GROUP_T + r], 1)],
                                                buf.at[slot, pl.ds(r, 1)], sems.at[slot]),
                lambda k: pltpu.make_async_copy(rows_ref.at[pl.ds(0, k)], buf.at[slot, pl.ds(0, k)], sems.at[slot]))

    @pl.when(i == 0)
    def _():
        def clear(r, carry):
            src_s[r] = 0
            return carry

        def place(t, carry):
            src_s[pos_ref[t]] = t
            return carry

        lax.fori_loop(0, n_rows, clear, 0, unroll=8)
        lax.fori_loop(0, pos_ref.shape[0], place, 0, unroll=8)
        _row_copy_start(GROUP_T, gather(0, 0)[0])

    slot = i % 2
    _row_copy_wait(GROUP_T, gather(i, slot)[1])

    @pl.when(i + 1 < pl.num_programs(0))
    def _():
        _row_copy_start(GROUP_T, gather(i + 1, 1 - slot)[0])

    g = tg_ref[i]
    x = buf[slot, :, :D].astype(BF16)
    gates = buf[slot, :, D:]
    lane = lax.broadcasted_iota(jnp.int32, (GROUP_T, 128), 1)
    ups = [(_dot(x, w1_ref[0, e].astype(BF16)), _dot(x, w3_ref[0, e].astype(BF16))) for e in range(EPG)]
    acc = jnp.zeros((GROUP_T, D), F32)
    for e in range(EPG):
        ge = jnp.sum(jnp.where(lane == g * EPG + e, gates, 0.0), axis=-1, keepdims=True)
        a, b = ups[e]
        acc = acc + _dot((_silu(a) * b * ge).astype(BF16), w2_ref[0, e].astype(BF16))
    ys_ref[...] = acc


def _experts(pos, tile_group, rows, n_rows, w1, w3, w2, l):
    wspec = lambda k, n: pl.BlockSpec((1, EPG, k, n), lambda i, pos_, tg: (l, tg[i], 0, 0))
    return pl.pallas_call(
        _experts_kernel,
        grid_spec=pltpu.PrefetchScalarGridSpec(
            num_scalar_prefetch=2,
            grid=(n_rows // GROUP_T,),
            in_specs=[pl.BlockSpec(memory_space=pl.ANY),
                      wspec(D, D_FF), wspec(D, D_FF), wspec(D_FF, D)],
            out_specs=pl.BlockSpec((GROUP_T, D), lambda i, pos_, tg: (i, 0)),
            scratch_shapes=[pltpu.SMEM((n_rows,), jnp.int32),
                            pltpu.VMEM((2, GROUP_T, XW), F32),
                            pltpu.SemaphoreType.DMA((2,))]),
        out_shape=jax.ShapeDtypeStruct((n_rows, D), F32),
        compiler_params=_cparams(("arbitrary",)),
        name="moe_experts",
    )(pos, tile_group, rows, w1, w3, w2)


def _combine_kernel(pos_ref, z1_ref, modx_ref, mody_ref, ys_ref, o_ref, buf, sem):
    t = pl.program_id(1)
    base = pl.program_id(0) * S + t * COMBINE_T
    _row_copy_start(COMBINE_T, lambda i: pltpu.make_async_copy(
        ys_ref.at[pl.ds(pos_ref[base + i], 1)], buf.at[pl.ds(i, 1)], sem))
    _row_copy_wait(COMBINE_T, lambda k: pltpu.make_async_copy(ys_ref.at[pl.ds(0, k)], buf.at[pl.ds(0, k)], sem))
    is_ctx = lax.broadcasted_iota(jnp.int32, (COMBINE_T, 1), 0) + t * COMBINE_T >= SEQ
    g2 = jnp.where(is_ctx, mody_ref[0][:, 5 * D:], modx_ref[0][:, 5 * D:])
    o_ref[0] = z1_ref[0] + g2 * buf[...]


def _combine(pos, z1, mods, ys):
    B = z1.shape[0]
    tok = pl.BlockSpec((1, COMBINE_T, D), lambda b, t, pos_: (b, t, 0))
    return pl.pallas_call(
        _combine_kernel,
        grid_spec=pltpu.PrefetchScalarGridSpec(
            num_scalar_prefetch=1,
            grid=(B, S // COMBINE_T),
            in_specs=[tok,
                      pl.BlockSpec((1, 1, 6 * D), lambda b, t, pos_: (2 * b, 0, 0)),
                      pl.BlockSpec((1, 1, 6 * D), lambda b, t, pos_: (2 * b + 1, 0, 0)),
                      pl.BlockSpec(memory_space=pl.ANY)],
            out_specs=tok,
            scratch_shapes=[pltpu.VMEM((COMBINE_T, D), F32), pltpu.SemaphoreType.DMA(())]),
        out_shape=jax.ShapeDtypeStruct((B, S, D), F32),
        compiler_params=_cparams(("arbitrary", "arbitrary")),
        name="moe_combine",
    )(pos, z1, mods, mods, ys)


def _group_layout(gid, n_rows):
    onehot = (gid[:, None] == jnp.arange(N_GROUPS)[None, :]).astype(jnp.int32)
    rank = jnp.cumsum(onehot, axis=0) - onehot
    count = jnp.sum(onehot, axis=0)
    padded = (count + GROUP_T - 1) // GROUP_T * GROUP_T
    end = jnp.cumsum(padded)
    pos = jnp.sum(onehot * (end - padded + rank), axis=1)
    n_tiles = n_rows // GROUP_T
    tile_group = jnp.sum(jnp.arange(n_tiles)[:, None] * GROUP_T >= end[None, :], axis=1)
    return pos.astype(jnp.int32), jnp.minimum(tile_group, N_GROUPS - 1).astype(jnp.int32)


def _moe(z1, xg, mods, w1, w3, w2, l):
    B = z1.shape[0]
    n = B * S
    rows = xg.reshape(n, XW)
    gid = rows[:, D + N_EXPERTS].astype(jnp.int32)
    n_rows = (-(-n // GROUP_T) + N_GROUPS) * GROUP_T
    pos, tile_group = _group_layout(gid, n_rows)
    ys = _experts(pos, tile_group, rows, n_rows, w1, w3, w2, l)
    return _combine(pos, z1, mods, ys)


def _in_proj_layout(w):
    cuts = np.cumsum([Q_RANK, KV_RANK, ROPE_D, ML_W, ML_W, 4 * ML_H])
    qc, ckv, kr, u, zz, g, na = jnp.split(w, [int(v) for v in cuts], axis=-1)
    zeros = lambda n: jnp.zeros((w.shape[0], n), w.dtype)
    out = jnp.concatenate([qc, ckv, zeros(NOPE_D), kr, zeros(HP - QK_D),
                           _pad_heads(u, ML_H, ML_D, HP), _pad_heads(zz, ML_H, ML_D, HP),
                           _gate_order(g), zeros(128 - 4 * ML_H), na], axis=-1)
    assert out.shape[-1] == NP_IN
    return out


def _gate_order(g):
    i_f, f_f, i_b, f_b = jnp.split(g, 4, axis=-1)
    return jnp.concatenate([i_f, i_b, f_f, f_b], axis=-1)


def _pad_heads(v, nh, d, dp):
    lead = v.shape[:-1]
    v = v.reshape(lead + (nh, d))
    v = jnp.pad(v, [(0, 0)] * len(lead) + [(0, 0), (0, dp - d)])
    return v.reshape(lead + (nh * dp,))


def _rope_tables():
    t = np.arange(SEQ)
    row = (t // GRID_W).astype(np.float32)
    col = (t % GRID_W).astype(np.float32)
    quarter = ROPE_D // 4
    inv = jnp.asarray(ROPE_BASE, F32) ** (-jnp.arange(quarter, dtype=F32) / quarter)
    ar = jnp.asarray(row)[:, None] * inv
    ac = jnp.asarray(col)[:, None] * inv
    ang = jnp.concatenate([ar, ar, ac, ac], axis=-1)
    cos = jnp.ones((S, HP), F32).at[:SEQ, NOPE_D:QK_D].set(jnp.cos(ang))
    sin = jnp.zeros((S, HP), F32).at[:SEQ, NOPE_D:QK_D].set(jnp.sin(ang))
    return cos, sin


def _rotate_half_index():
    q = ROPE_D // 4
    src = np.arange(QK_D)
    sign = np.zeros((QK_D,), np.float32)
    for blk in range(2):
        lo = NOPE_D + 2 * q * blk
        src[lo:lo + q] = np.arange(lo + q, lo + 2 * q)
        sign[lo:lo + q] = -1.0
        src[lo + q:lo + 2 * q] = np.arange(lo, lo + q)
        sign[lo + q:lo + 2 * q] = 1.0
    return src, sign


def _rotate_half(w):
    src, sign = _rotate_half_index()
    return w[..., src] * sign


NA_NDR = 2 * WIN_R - 1
NA_NPAIR = 3 * NA_NDR


def _na_pair_index():
    idx = np.zeros((NA_NBLK, NA_QROWS, NA_KROWS // 2), np.int32)
    for blk in range(NA_NBLK):
        k0 = int(np.clip(blk * NA_QROWS - WIN_R // 2, 0, ROWS - NA_KROWS))
        for i in range(NA_QROWS):
            qr = blk * NA_QROWS + i
            rs = int(np.clip(qr - WIN_R // 2, 0, ROWS - WIN_R))
            assert k0 <= rs and rs + WIN_R <= k0 + NA_KROWS
            for p in range(NA_KROWS // 2):
                kr = k0 + 2 * p
                dr = kr - qr + WIN_R - 1
                left = rs <= kr < rs + WIN_R
                right = rs <= kr + 1 < rs + WIN_R
                if left and right:
                    idx[blk, i, p] = 1 + dr
                elif left:
                    idx[blk, i, p] = NA_NDR + dr
                elif right:
                    idx[blk, i, p] = 2 * NA_NDR + dr + 1
    return idx.reshape(-1)


def _na_pair_tiles(rpb):
    cq = np.arange(GRID_W)
    cs = np.clip(cq - WIN_C // 2, 0, GRID_W - WIN_C)
    col_ok = (cq[None, :] >= cs[:, None]) & (cq[None, :] < cs[:, None] + WIN_C)
    dc = np.clip(cq[None, :] - cq[:, None], -(WIN_C - 1), WIN_C - 1) + (WIN_C - 1)
    onehot = jnp.asarray(np.eye(2 * WIN_C - 1, dtype=np.float32)[dc])
    tiles = jnp.einsum('hrc,qkc->hrqk', rpb, onehot, precision=lax.Precision.HIGHEST)
    tiles = jnp.where(jnp.asarray(col_ok), tiles * float(np.log2(np.e)), NEG)
    masked = jnp.full_like(tiles, NEG)
    both = jnp.concatenate([tiles[:, :-1], tiles[:, 1:]], axis=-1)
    left = jnp.concatenate([tiles, masked], axis=-1)
    right = jnp.concatenate([masked, tiles], axis=-1)
    none = jnp.concatenate([masked[:, :1], masked[:, :1]], axis=-1)
    out = jnp.concatenate([none, both, left, right], axis=1)
    assert out.shape[1] == NA_NPAIR
    return out.astype(BF16)


def kernel(x, c, ctx, c_ctx, w_mod, b_mod, norm1_g, norm2_g, w_in, w_out, mla_qnorm_g, mla_w_uq, mla_kvnorm_g, mla_w_ukv, mla_q_g, mla_k_g, ml_conv_w, ml_conv_b, ml_w_q, ml_w_k, ml_w_v, ml_gate_b, ml_norm_g, ml_skip, na_q_g, na_k_g, na_rpb, router_w, router_b, moe_w1, moe_w3, moe_w2):
    B = x.shape[0]
    z = jnp.concatenate([x, ctx], axis=1)
    cc = jnp.zeros((16, D), F32).at[:B].set(c).at[B].set(c_ctx)
    mod_all = _modulation(cc, w_mod, b_mod)
    cos, sin = _rope_tables()
    rot_src, rot_sign = _rotate_half_index()
    rot_np = np.zeros((HP, HP), np.float32)
    rot_np[rot_src, np.arange(QK_D)] = rot_sign
    rot_mat = jnp.asarray(rot_np, BF16)
    ones_hp = jnp.ones((HP, HP), BF16)
    seg = jnp.asarray(np.kron(np.eye(NA_H), np.ones((NA_D, NA_D))), BF16)
    tril = jnp.asarray(np.stack([np.tril(np.ones((ML_CHUNK, ML_CHUNK))), np.triu(np.ones((ML_CHUNK, ML_CHUNK)))]), BF16)
    pair_idx = jnp.asarray(_na_pair_index())
    rw = jnp.stack(_split_bf16(jnp.pad(router_w, ((0, 0), (0, 128 - N_EXPERTS))), 2))
    rb = jnp.pad(router_b, (0, 128 - N_EXPERTS), constant_values=NEG).reshape(1, 128)

    def pad_lanes(v, n):
        return jnp.pad(v, [(0, 0)] * (v.ndim - 1) + [(0, n - v.shape[-1])])

    for l in range(DEPTH):
        mx = mod_all[l, :B]
        my = jnp.broadcast_to(mod_all[l, B], (B, 6 * D))
        mods = jnp.stack([mx, my], axis=1).reshape(2 * B, 1, 6 * D)

        w_in_p = _in_proj_layout(w_in[l]).astype(BF16)
        pmla, pu, pg, pna = _in_proj(z, mods, norm1_g[l].reshape(1, D), w_in_p)

        wq = jnp.transpose(mla_w_uq[l].reshape(Q_RANK, MLA_H, QK_D), (1, 0, 2))
        wuq = jnp.concatenate([pad_lanes(wq, HP), pad_lanes(_rotate_half(wq), HP)], axis=-1).astype(BF16)
        wukv = jnp.transpose(mla_w_ukv[l].reshape(KV_RANK, MLA_H, NOPE_D + V_D), (1, 0, 2))
        wuk = pad_lanes(wukv[..., :NOPE_D], HP).astype(BF16)
        wuv = pad_lanes(wukv[..., NOPE_D:], HP).astype(BF16)
        qg, kg = mla_q_g[l], mla_k_g[l]
        q_scale = float(QK_D ** -0.5 * np.log2(np.e))
        tabs = jnp.stack([cos * pad_lanes(qg, HP) * q_scale, sin * pad_lanes(jnp.abs(rot_sign) * qg[rot_src], HP) * q_scale,
                          cos * pad_lanes(kg, HP), sin * pad_lanes(jnp.abs(rot_sign) * kg[rot_src], HP)])
        heads_last = lambda w_: jnp.transpose(w_, (1, 0, 2)).reshape(w_.shape[1], -1)
        q, k, v = _mla_prep(pmla, tabs, mla_qnorm_g[l].reshape(1, Q_RANK), heads_last(wuq),
                            mla_kvnorm_g[l].reshape(1, KV_RANK), heads_last(wuk), heads_last(wuv), rot_mat, ones_hp)
        mla_o = _mla_attn(q, k, v)

        padh = lambda a: _pad_heads(a, ML_H, ML_D, HP)
        padw = lambda w_: jnp.pad(w_, ((0, 0), (0, HP - ML_D), (0, HP - ML_D))).astype(BF16)
        cw = jnp.pad(padh(ml_conv_w[l]), ((0, 8 - 3), (0, 0)))
        ml_o = _mlstm(pu, pg, cw, padh(ml_conv_b[l]).reshape(1, ML_WP),
                      padw(ml_w_q[l]), padw(jnp.swapaxes(ml_w_k[l], 1, 2) * (ML_D ** -0.5)), padw(ml_w_v[l]),
                      pad_lanes(_gate_order(ml_gate_b[l]).reshape(1, 4 * ML_H), 128),
                      padh(ml_norm_g[l]).reshape(1, ML_WP), padh(ml_skip[l]).reshape(1, ML_WP), tril)

        na_o = _na_attn(pair_idx, pna, _na_pair_tiles(na_rpb[l]), jnp.tile(na_q_g[l], NA_H).reshape(1, NA_W),
                        jnp.tile(na_k_g[l], NA_H).reshape(1, NA_W), seg)

        wo = w_out[l]
        wa = wo[:MLA_H * V_D].astype(BF16)
        wm = jnp.pad(wo[MLA_H * V_D:MLA_H * V_D + ML_W].reshape(ML_H, ML_D, D),
                     ((0, 0), (0, HP - ML_D), (0, 0))).reshape(ML_WP, D).astype(BF16)
        wn = wo[MLA_H * V_D + ML_W:].astype(BF16)
        z1, xg = _out_proj(z, mods, mla_o, ml_o, na_o, wa, wm, wn, norm2_g[l].reshape(1, D), rw, rb)
        z = _moe(z1, xg, mods, moe_w1, moe_w3, moe_w2, l)

    return z[:, :SEQ]
```

```python
import functools

import numpy as np
import jax
import jax.numpy as jnp
from jax import lax
from jax.experimental import pallas as pl
from jax.experimental.pallas import tpu as pltpu

F32 = jnp.float32
BF16 = jnp.bfloat16

D = 1024
SEQ = 2048
CTX = 256
S = SEQ + CTX
DEPTH = 4
GRID_W = 64
ROWS = SEQ // GRID_W
EPS = 1e-6

MLA_H = 6
Q_RANK = 256
KV_RANK = 128
NOPE_D = 64
ROPE_D = 32
V_D = 64
QK_D = NOPE_D + ROPE_D
ROPE_BASE = 10000.0

ML_H = 4
ML_D = 96
ML_W = ML_H * ML_D
HP = 128
ML_WP = ML_H * HP
ML_CHUNK = 128
N_CHUNK = S // ML_CHUNK
N_CTX_CHUNK = CTX // ML_CHUNK

NA_H = 4
NA_D = 64
NA_W = NA_H * NA_D
WIN_R = 8
WIN_C = 16
NA_QROWS = 4
NA_KROWS = 12
NA_QB = NA_QROWS * GRID_W
NA_KB = NA_KROWS * GRID_W
NA_NBLK = ROWS // NA_QROWS

N_EXPERTS = 16
N_GROUPS = 4
EPG = N_EXPERTS // N_GROUPS
D_FF = 256

TT = 256
NT = S // TT
OT = 768
TOP_K = 2
XW = D + 128
GROUP_T = 512
COMBINE_T = 768
COMBINE_LAST_T = 512
NEG = -1e30

C_QC = 0
C_CKV = 256
C_KR = 384
C_U = 512
C_Z = C_U + ML_WP
C_G = C_Z + ML_WP
C_NA = C_G + 128
NP_IN = C_NA + 3 * NA_W

VMEM_LIMIT = 56 * 1024 * 1024


def _cparams(sem):
    return pltpu.CompilerParams(dimension_semantics=sem, vmem_limit_bytes=VMEM_LIMIT)


def _sigmoid(x):
    return 1.0 / (1.0 + jnp.exp(-x))


def _silu(x):
    return x * _sigmoid(x)


def _dot(a, b):
    return jnp.dot(a, b, preferred_element_type=F32)


def _dot_nt(a, b):
    return lax.dot_general(a, b, (((1,), (1,)), ((), ())), preferred_element_type=F32)


def _dot_tn(a, b):
    return lax.dot_general(a, b, (((0,), (0,)), ((), ())), preferred_element_type=F32)


def _dot_hi(a, b):
    return jnp.dot(a, b, preferred_element_type=F32, precision=lax.Precision.HIGHEST)


def _split_bf16(x, n):
    parts = []
    for _ in range(n):
        p = x.astype(BF16)
        parts.append(p)
        x = x - p.astype(F32)
    return parts


def _mod_rows(mod_ref, t):
    m = mod_ref[0]
    return [m[:, i * D:(i + 1) * D] for i in range(6)]


def _mod_kernel(c_ref, w_ref, b_ref, o_ref):
    sc = _silu(c_ref[...])
    o_ref[0] = _dot_hi(sc, w_ref[0]) + b_ref[0]


def _modulation(cc, w_mod, b_mod):
    nc = 6
    return pl.pallas_call(
        _mod_kernel,
        grid=(DEPTH, nc),
        in_specs=[pl.BlockSpec((16, D), lambda l, j: (0, 0)),
                  pl.BlockSpec((1, D, D), lambda l, j: (l, 0, j)),
                  pl.BlockSpec((1, 1, D), lambda l, j: (l, 0, j))],
        out_specs=pl.BlockSpec((1, 16, D), lambda l, j: (l, 0, j)),
        out_shape=jax.ShapeDtypeStruct((DEPTH, 16, 6 * D), F32),
        compiler_params=_cparams(("parallel", "parallel")),
        name="modulation",
    )(cc, w_mod, b_mod.reshape(DEPTH, 1, 6 * D))


def _in_proj_kernel(z_ref, mod_ref, g_ref, w_ref, pmla_ref, pu_ref, pg_ref, pna_ref):
    sh1, sc1 = _mod_rows(mod_ref, None)[:2]
    x = z_ref[0]
    xn = x * lax.rsqrt(jnp.mean(x * x, axis=-1, keepdims=True) + EPS) * g_ref[...]
    xn = xn * (1.0 + sc1) + sh1
    p = _dot(xn.astype(BF16), w_ref[...])
    pmla_ref[0] = p[:, :C_U].astype(BF16)
    pu_ref[0] = p[:, C_U:C_G].astype(BF16)
    pg_ref[0] = p[:, C_G:C_NA]
    pna_ref[0] = p[:, C_NA:].astype(BF16)


def _mod_spec():
    return pl.BlockSpec((1, 1, 6 * D), lambda b, t: (2 * b + t // (NT - 1), 0, 0))


def _in_proj(z, mods, g, w):
    B = z.shape[0]
    tok = lambda w_: pl.BlockSpec((1, TT, w_), lambda b, t: (b, t, 0))
    return pl.pallas_call(
        _in_proj_kernel,
        grid=(B, NT),
        in_specs=[tok(D), _mod_spec(),
                  pl.BlockSpec((1, D), lambda b, t: (0, 0)),
                  pl.BlockSpec((D, NP_IN), lambda b, t: (0, 0))],
        out_specs=[tok(C_U), tok(2 * ML_WP), tok(128), tok(3 * NA_W)],
        out_shape=[jax.ShapeDtypeStruct((B, S, C_U), BF16),
                   jax.ShapeDtypeStruct((B, S, 2 * ML_WP), BF16),
                   jax.ShapeDtypeStruct((B, S, 128), F32),
                   jax.ShapeDtypeStruct((B, S, 3 * NA_W), BF16)],
        compiler_params=_cparams(("parallel", "parallel")),
        name="in_proj",
    )(z, mods, g, w)


def _mla_prep_kernel(p_ref, tab_ref, qng_ref, wuq_ref, kvng_ref, wuk_ref, wuv_ref, rot_ref, ones_ref,
                     q_out, k_out, v_out):
    p = p_ref[0].astype(F32)
    qc = p[:, C_QC:C_CKV]
    ckv = p[:, C_CKV:C_KR]
    kr = p[:, C_KR:C_U]
    qcn = (qc * lax.rsqrt(jnp.mean(qc * qc, axis=-1, keepdims=True) + EPS) * qng_ref[...]).astype(BF16)
    ckvn = (ckv * lax.rsqrt(jnp.mean(ckv * ckv, axis=-1, keepdims=True) + EPS) * kvng_ref[...]).astype(BF16)
    lane = lax.broadcasted_iota(jnp.int32, (TT, HP), 1)
    ones = ones_ref[...]
    kr_rot = _dot(kr.astype(BF16), rot_ref[...])

    q_all = _dot(qcn, wuq_ref[...])
    k_all = _dot(ckvn, wuk_ref[...])
    v_all = _dot(ckvn, wuv_ref[...])
    qs = [q_all[:, 2 * h * HP:(2 * h + 1) * HP] for h in range(MLA_H)]
    q_rots = [q_all[:, (2 * h + 1) * HP:(2 * h + 2) * HP] for h in range(MLA_H)]
    ks = [k_all[:, h * HP:(h + 1) * HP] + kr for h in range(MLA_H)]
    ss_q = [_dot((x * x).astype(BF16), ones) for x in qs]
    ss_k = [_dot((x * x).astype(BF16), ones) for x in ks]

    def norm_rope(x, x_rot, ss, cos_g, sin_g):
        return lax.rsqrt(ss * (1.0 / QK_D) + EPS) * (x * cos_g + x_rot * sin_g)

    for h in range(MLA_H):
        q_out[0, h] = norm_rope(qs[h], q_rots[h], ss_q[h], tab_ref[0], tab_ref[1]).astype(BF16)
        k_out[0, h] = norm_rope(ks[h], kr_rot, ss_k[h], tab_ref[2], tab_ref[3]).astype(BF16)
        v_out[0, h] = jnp.where(lane < V_D, v_all[:, h * HP:(h + 1) * HP], 1.0).astype(BF16)


def _mla_prep(pmla, tabs, qng, wuq, kvng, wuk, wuv, rot, ones):
    B = pmla.shape[0]
    full = lambda a: pl.BlockSpec(a.shape, lambda b, t, _n=a.ndim: (0,) * _n)
    hd = lambda w_: pl.BlockSpec((1, MLA_H, TT, w_), lambda b, t: (b, 0, t, 0))
    return pl.pallas_call(
        _mla_prep_kernel,
        grid=(B, NT),
        in_specs=[pl.BlockSpec((1, TT, C_U), lambda b, t: (b, t, 0)),
                  pl.BlockSpec((4, TT, HP), lambda b, t: (0, t, 0)),
                  full(qng), full(wuq), full(kvng), full(wuk), full(wuv), full(rot), full(ones)],
        out_specs=[hd(HP), hd(HP), hd(HP)],
        out_shape=[jax.ShapeDtypeStruct((B, MLA_H, S, HP), BF16),
                   jax.ShapeDtypeStruct((B, MLA_H, S, HP), BF16),
                   jax.ShapeDtypeStruct((B, MLA_H, S, HP), BF16)],
        compiler_params=_cparams(("parallel", "parallel")),
        name="mla_prep",
    )(pmla, tabs, qng, wuq, kvng, wuk, wuv, rot, ones)


def _mla_attn_kernel(q_ref, k_ref, v_ref, o_ref):
    t = pl.program_id(1)

    def attend(k_lo, k_n):
        def scores(h):
            return _dot_nt(q_ref[0, h], k_ref[0, h, k_lo:k_lo + k_n, :])

        outs = []
        s_next = scores(0)
        for h in range(MLA_H):
            s = s_next
            if h + 1 < MLA_H:
                s_next = scores(h + 1)
            m = jnp.max(s, axis=-1, keepdims=True)
            p = jnp.exp2(s - m)
            pv = _dot(p.astype(BF16), v_ref[0, h, k_lo:k_lo + k_n, :])
            outs.append(pv[:, :V_D] / pv[:, V_D:V_D + 1])
        o_ref[0] = jnp.concatenate(outs, axis=-1).astype(BF16)

    @pl.when(t < NT - 1)
    def _():
        attend(0, S)

    @pl.when(t == NT - 1)
    def _():
        attend(SEQ, CTX)


def _mla_attn(q, k, v):
    B = q.shape[0]
    return pl.pallas_call(
        _mla_attn_kernel,
        grid=(B, NT),
        in_specs=[pl.BlockSpec((1, MLA_H, TT, HP), lambda b, t: (b, 0, t, 0)),
                  pl.BlockSpec((1, MLA_H, S, HP), lambda b, t: (b, 0, 0, 0)),
                  pl.BlockSpec((1, MLA_H, S, HP), lambda b, t: (b, 0, 0, 0))],
        out_specs=pl.BlockSpec((1, TT, MLA_H * V_D), lambda b, t: (b, t, 0)),
        out_shape=jax.ShapeDtypeStruct((B, S, MLA_H * V_D), BF16),
        compiler_params=_cparams(("parallel", "arbitrary")),
        name="mla_attn",
    )(q, k, v)


def _log_sigmoid(x):
    return jnp.minimum(x, 0.0) - jnp.log(1.0 + jnp.exp(-jnp.abs(x)))


def _mlstm_kernel(pu_ref, pg_ref, cw_ref, cb_ref, wq_ref, wk_ref, wv_ref, gb_ref, ng_ref, sk_ref,
                  tril_ref, o_ref, uc_s, q_s, kt_s, v_s, h_s, c_s, m_s, pm_s, b_s, rt_s):
    CA = 2 * ML_CHUNK
    row = lax.broadcasted_iota(jnp.int32, (CA, ML_WP), 0)

    def conv_body(i, carry):
        r0 = pl.multiple_of(i * CA, CA)
        x = pu_ref[0, pl.ds(r0, CA), 0:ML_WP].astype(F32)
        pr = pl.multiple_of(jnp.maximum(r0 - 16, 0), 16)
        nx = pl.multiple_of(jnp.minimum(r0 + CA, S - 16), 16)
        prev = pu_ref[0, pl.ds(pr, 16), 0:ML_WP].astype(F32)[15:16, :]
        nxt = pu_ref[0, pl.ds(nx, 16), 0:ML_WP].astype(F32)[0:1, :]
        seq_start = jnp.logical_or(r0 == 0, r0 == SEQ)
        seq_end = jnp.logical_or(r0 + CA == SEQ, r0 + CA == S)
        prev = jnp.where(seq_start, 0.0, prev)
        nxt = jnp.where(seq_end, 0.0, nxt)
        up = jnp.where(row == 0, prev, pltpu.roll(x, 1, 0))
        dn = jnp.where(row == CA - 1, nxt, pltpu.roll(x, CA - 1, 0))
        uc = _silu(cw_ref[0:1, :] * up + cw_ref[1:2, :] * x + cw_ref[2:3, :] * dn + cb_ref[...])
        ucb = uc.astype(BF16)
        uc_s[pl.ds(r0, CA), :] = ucb
        xb = x.astype(BF16)
        for h in range(ML_H):
            sl = slice(h * HP, (h + 1) * HP)
            q_s[pl.ds(r0, CA), sl] = _dot(ucb[:, sl], wq_ref[h]).astype(BF16)
            kt = _dot_nt(wk_ref[h], ucb[:, sl])
            kt_s[2 * i, sl, :] = kt[:, :ML_CHUNK].astype(BF16)
            kt_s[2 * i + 1, sl, :] = kt[:, ML_CHUNK:].astype(BF16)
            v_s[pl.ds(r0, CA), 2 * h * HP:(2 * h + 1) * HP] = _dot(xb[:, sl], wv_ref[h]).astype(BF16)
            v_s[pl.ds(r0, CA), (2 * h + 1) * HP:(2 * h + 2) * HP] = jnp.ones((CA, HP), BF16)

        for half in range(2):
            rows = pl.ds(r0 + half * ML_CHUNK, ML_CHUNK)
            g = pg_ref[0, rows, :] + gb_ref[...]
            parts = _split_bf16(_log_sigmoid(g), 3)
            cum_f = sum(_dot(tril_ref[0], part) for part in parts)
            cum_b = sum(_dot(tril_ref[1], part) for part in parts)
            bsh = pltpu.roll(jnp.where(bwd_lane, cum_b, cum_f), 128 - n_chain, 1)
            r = g - bsh
            pf = r
            pb = r
            k = 1
            while k < ML_CHUNK:
                pf = jnp.maximum(pf, jnp.where(ti >= k, pltpu.roll(pf, k, 0), NEG))
                pb = jnp.maximum(pb, jnp.where(ti < ML_CHUNK - k, pltpu.roll(pb, ML_CHUNK - k, 0), NEG))
                k *= 2
            pm_s[2 * i + half] = jnp.where(bwd_lane, pb, pf)
            b_s[2 * i + half] = bsh
            rt_s[2 * i + half] = r.T[0:n_chain, :]
        return carry

    n_chain = 2 * ML_H
    ti = lax.broadcasted_iota(jnp.int32, (ML_CHUNK, ML_CHUNK), 0)
    si = lax.broadcasted_iota(jnp.int32, (ML_CHUNK, ML_CHUNK), 1)
    bwd_lane = (si % n_chain) >= ML_H
    lax.fori_loop(0, S // CA, conv_body, 0)

    c_s[...] = jnp.zeros_like(c_s)
    m_s[...] = jnp.zeros_like(m_s)
    masks = (si <= ti, si >= ti)

    def scan_body(j, carry):
        chunk = (jnp.where(j < N_CTX_CHUNK, j + N_CHUNK - N_CTX_CHUNK, j - N_CTX_CHUNK), N_CHUNK - 1 - j)
        chains = []
        for d in range(2):
            r0 = pl.multiple_of(chunk[d] * ML_CHUNK, ML_CHUNK)
            p_col = pm_s[chunk[d]]
            bsh = b_s[chunk[d]]
            r_t = rt_s[chunk[d]]
            end = ML_CHUNK - 1 if d == 0 else 0
            for h in range(ML_H):
                c = d * ML_H + h
                sl = slice(h * HP, (h + 1) * HP)
                qc = q_s[pl.ds(r0, ML_CHUNK), sl]
                kt = kt_s[chunk[d], sl, :]
                vx = v_s[pl.ds(r0, ML_CHUNK), 2 * h * HP:(2 * h + 2) * HP]
                r_row = r_t[c:c + 1, :]
                m = m_s[c]
                st = c_s[c]
                big_m = jnp.maximum(m, jnp.broadcast_to(p_col[:, c:c + 1], (ML_CHUNK, HP)))
                b_b = jnp.broadcast_to(bsh[:, c:c + 1], (ML_CHUNK, HP))
                m_end = big_m[end:end + 1, :]
                ktw = (kt.astype(F32) * jnp.exp(r_row - m_end)).astype(BF16)
                chains.append(dict(d=d, r0=r0, sl=sl, c=c, vx=vx, r_row=r_row, m=m, st=st, big_m=big_m,
                                   b_b=b_b, m_end=m_end, end=end,
                                   qk=_dot(qc, kt), inter=_dot(qc, st.astype(BF16)), upd=_dot(ktw, vx)))
        for ch in chains:
            dw = jnp.exp(jnp.where(masks[ch["d"]], ch["r_row"] - ch["big_m"], NEG))
            ch["intra"] = _dot((ch["qk"] * dw).astype(BF16), ch["vx"])
        for ch in chains:
            m, big_m, inter, intra, end = ch["m"], ch["big_m"], ch["inter"], ch["intra"], ch["end"]
            iw = jnp.exp(m - big_m)
            num = iw * inter[:, :HP] + intra[:, :HP]
            nq = iw * inter[:, HP:] + intra[:, HP:]
            hv = num / jnp.maximum(jnp.abs(nq), jnp.exp(-(ch["b_b"] + big_m)))
            a = jnp.exp(m - ch["m_end"])
            ch["out"] = (hv, jnp.concatenate([a, a], axis=1) * ch["st"] + ch["upd"],
                         ch["b_b"][end:end + 1, :] + ch["m_end"])
        for ch in chains:
            hv, st_new, m_new = ch["out"]
            h_s[ch["d"], pl.ds(ch["r0"], ML_CHUNK), ch["sl"]] = hv
            c_s[ch["c"]] = st_new
            m_s[ch["c"]] = m_new
        return carry

    lax.fori_loop(0, N_CHUNK, scan_body, 0)

    live = (lax.broadcasted_iota(jnp.int32, (CA, HP), 1) < ML_D).astype(F32)

    def out_body(i, carry):
        r0 = pl.multiple_of(i * CA, CA)
        for h in range(ML_H):
            sl = slice(h * HP, (h + 1) * HP)
            hh = h_s[0, pl.ds(r0, CA), sl] + h_s[1, pl.ds(r0, CA), sl]
            mu = jnp.sum(hh, axis=-1, keepdims=True) * (1.0 / ML_D)
            dv = (hh - mu) * live
            var = jnp.sum(dv * dv, axis=-1, keepdims=True) * (1.0 / ML_D)
            hn = dv * lax.rsqrt(var + EPS) * ng_ref[:, sl]
            uc = uc_s[pl.ds(r0, CA), sl].astype(F32)
            zz = pu_ref[0, pl.ds(r0, CA), ML_WP + h * HP:ML_WP + (h + 1) * HP].astype(F32)
            o_ref[0, pl.ds(r0, CA), sl] = ((hn + sk_ref[:, sl] * uc) * _silu(zz)).astype(BF16)
        return carry

    lax.fori_loop(0, S // CA, out_body, 0)


def _mlstm(pu, pg, cw, cb, wq, wk, wv, gb, ng, sk, tril):
    B = pu.shape[0]
    full = lambda a: pl.BlockSpec(a.shape, lambda b, _n=a.ndim: (0,) * _n)
    n_chain = 2 * ML_H
    return pl.pallas_call(
        _mlstm_kernel,
        grid=(B,),
        in_specs=[pl.BlockSpec((1, S, 2 * ML_WP), lambda b: (b, 0, 0)),
                  pl.BlockSpec((1, S, 128), lambda b: (b, 0, 0)),
                  full(cw), full(cb), full(wq), full(wk), full(wv), full(gb), full(ng), full(sk), full(tril)],
        out_specs=pl.BlockSpec((1, S, ML_WP), lambda b: (b, 0, 0)),
        out_shape=jax.ShapeDtypeStruct((B, S, ML_WP), BF16),
        scratch_shapes=[pltpu.VMEM((S, ML_WP), BF16), pltpu.VMEM((S, ML_WP), BF16),
                        pltpu.VMEM((N_CHUNK, ML_WP, ML_CHUNK), BF16), pltpu.VMEM((S, 2 * ML_WP), BF16),
                        pltpu.VMEM((2, S, ML_WP), F32),
                        pltpu.VMEM((n_chain, HP, 2 * HP), F32),
                        pltpu.VMEM((n_chain, 1, HP), F32),
                        pltpu.VMEM((N_CHUNK, ML_CHUNK, 128), F32),
                        pltpu.VMEM((N_CHUNK, ML_CHUNK, 128), F32),
                        pltpu.VMEM((N_CHUNK, n_chain, ML_CHUNK), F32)],
        compiler_params=_cparams(("parallel",)),
        name="mlstm",
    )(pu, pg, cw, cb, wq, wk, wv, gb, ng, sk, tril)


def _na_kernel(idx_ref, p_ref, pt_ref, qg_ref, kg_ref, seg_ref, o_ref, kn_s, bias_s):
    j = pl.program_id(1)
    seg = seg_ref[...]

    def headnorm(x, g):
        ss = _dot((x * x).astype(BF16), seg)
        return x * lax.rsqrt(ss * (1.0 / NA_D) + EPS) * g

    @pl.when(j == 0)
    def _():
        def body(i, carry):
            r0 = pl.multiple_of(i * TT, TT)
            kk = p_ref[0, pl.ds(r0, TT), NA_W:2 * NA_W].astype(F32)
            kn_s[pl.ds(r0, TT), :] = headnorm(kk, kg_ref[...]).astype(BF16)
            return carry
        lax.fori_loop(0, NT, body, 0)

    scale = float(NA_D ** -0.5 * np.log2(np.e))
    kctx = kn_s[SEQ:S, :]
    vctx = p_ref[0, SEQ:S, 2 * NA_W:3 * NA_W]

    @pl.when(j < NA_NBLK)
    def _():
        q0 = pl.multiple_of(j * NA_QB, NA_QB)
        k0 = pl.multiple_of(jnp.clip(j * NA_QROWS - WIN_R // 2, 0, ROWS - NA_KROWS) * GRID_W, 256)
        q = headnorm(p_ref[0, pl.ds(q0, NA_QB), 0:NA_W].astype(F32), qg_ref[...]) * scale
        kl = kn_s[pl.ds(k0, NA_KB), :]
        vl = p_ref[0, pl.ds(k0, NA_KB), 2 * NA_W:3 * NA_W]
        head = lax.broadcasted_iota(jnp.int32, (NA_QB, NA_W), 1) // NA_D
        acc = jnp.zeros((NA_QB, NA_W), F32)

        def scores(h):
            qm = jnp.where(head == h, q, 0.0).astype(BF16)
            for i in range(NA_QROWS):
                for p in range(NA_KROWS // 2):
                    code = idx_ref[(j * NA_QROWS + i) * (NA_KROWS // 2) + p]
                    bias_s[i * GRID_W:(i + 1) * GRID_W, p * 2 * GRID_W:(p + 1) * 2 * GRID_W] = pt_ref[h, code]
            return _dot_nt(qm, kl) + bias_s[...].astype(F32), _dot_nt(qm, kctx)

        s_next = scores(0)
        for h in range(NA_H):
            s1, s2 = s_next
            if h + 1 < NA_H:
                s_next = scores(h + 1)
            m = jnp.maximum(jnp.max(s1, axis=-1, keepdims=True), jnp.max(s2, axis=-1, keepdims=True))
            p1 = jnp.exp2(s1 - m)
            p2 = jnp.exp2(s2 - m)
            l = jnp.sum(p1, axis=-1, keepdims=True) + jnp.sum(p2, axis=-1, keepdims=True)
            o = (_dot(p1.astype(BF16), vl) + _dot(p2.astype(BF16), vctx)) / l
            acc = jnp.where(head == h, o, acc)
        o_ref[0, pl.ds(q0, NA_QB), :] = acc.astype(BF16)

    @pl.when(j == NA_NBLK)
    def _():
        q = headnorm(p_ref[0, SEQ:S, 0:NA_W].astype(F32), qg_ref[...]) * scale
        head = lax.broadcasted_iota(jnp.int32, (CTX, NA_W), 1) // NA_D
        acc = jnp.zeros((CTX, NA_W), F32)
        for h in range(NA_H):
            qm = jnp.where(head == h, q, 0.0).astype(BF16)
            s2 = _dot_nt(qm, kctx)
            m = jnp.max(s2, axis=-1, keepdims=True)
            p2 = jnp.exp2(s2 - m)
            l = jnp.sum(p2, axis=-1, keepdims=True)
            o = _dot(p2.astype(BF16), vctx) / l
            acc = jnp.where(head == h, o, acc)
        o_ref[0, SEQ:S, :] = acc.astype(BF16)


def _na_attn(pair_idx, pna, pair_tiles, qg, kg, seg):
    B = pna.shape[0]
    return pl.pallas_call(
        _na_kernel,
        grid_spec=pltpu.PrefetchScalarGridSpec(
            num_scalar_prefetch=1,
            grid=(B, NA_NBLK + 1),
            in_specs=[pl.BlockSpec((1, S, 3 * NA_W), lambda b, j, idx: (b, 0, 0)),
                      pl.BlockSpec(pair_tiles.shape, lambda b, j, idx: (0, 0, 0, 0)),
                      pl.BlockSpec((1, NA_W), lambda b, j, idx: (0, 0)),
                      pl.BlockSpec((1, NA_W), lambda b, j, idx: (0, 0)),
                      pl.BlockSpec((NA_W, NA_W), lambda b, j, idx: (0, 0))],
            out_specs=pl.BlockSpec((1, S, NA_W), lambda b, j, idx: (b, 0, 0)),
            scratch_shapes=[pltpu.VMEM((S, NA_W), BF16), pltpu.VMEM((NA_QB, NA_KB), BF16)]),
        out_shape=jax.ShapeDtypeStruct((B, S, NA_W), BF16),
        compiler_params=_cparams(("parallel", "arbitrary")),
        name="na_attn",
    )(pair_idx, pna, pair_tiles, qg, kg, seg)


def _out_proj_kernel(z_ref, modx_ref, mody_ref, a_ref, m_ref, n_ref, wa_ref, wm_ref, wn_ref, g2_ref, rw_ref, rb_ref,
                     z1_ref, xg_ref):
    lane = lax.broadcasted_iota(jnp.int32, (TT, 128), 1)
    live = lane < N_EXPERTS
    groups = [slice(i * TT, (i + 1) * TT) for i in range(OT // TT)]

    splits = []
    for rows in groups:
        is_ctx = lax.broadcasted_iota(jnp.int32, (TT, 1), 0) + (pl.program_id(1) * OT + rows.start) >= SEQ
        mod = lambda i: jnp.where(is_ctx, mody_ref[0][:, i * D:(i + 1) * D], modx_ref[0][:, i * D:(i + 1) * D])
        mix = (_dot(a_ref[0, rows, :], wa_ref[...]) + _dot(m_ref[0, rows, :], wm_ref[...])
               + _dot(n_ref[0, rows, :], wn_ref[...]))
        x = z_ref[0, rows, :] + mod(2) * mix
        z1_ref[0, rows, :] = x
        hn = x * lax.rsqrt(jnp.mean(x * x, axis=-1, keepdims=True) + EPS) * g2_ref[...]
        hn = hn * (1.0 + mod(4)) + mod(3)
        h_hi, h_lo = _split_bf16(hn, 2)
        xg_ref[0, rows, :D] = hn
        splits.append((h_hi, h_lo))

    affs = [_sigmoid(_dot(h_hi, rw_ref[0]) + (_dot(h_hi, rw_ref[1]) + _dot(h_lo, rw_ref[0])))
            for h_hi, h_lo in splits]

    def cyc(x, k, width):
        fwd = pltpu.roll(x, 128 - k, 1)
        back = pltpu.roll(x, width - k, 1)
        return jnp.where((lane % width) + k < width, fwd, back)

    def rank(x, width, step):
        r = jnp.zeros((TT, 128), F32)
        for k in range(1, width // step):
            y = cyc(x, k * step, width)
            wrapped = (lane % width) + k * step >= width
            beats = jnp.logical_or(y > x, jnp.logical_and(y == x, wrapped))
            r = r + beats.astype(F32)
        return r

    for rows, aff in zip(groups, affs):
        sel = aff + rb_ref[...]
        top2 = rank(sel, EPG, 1) < 2.0
        part = jnp.where(top2, sel, 0.0)
        gscore = part
        for k in range(1, EPG):
            gscore = gscore + cyc(part, k, EPG)
        best = rank(gscore, N_EXPERTS, EPG) < 1.0
        chosen = jnp.logical_and(jnp.logical_and(top2, best), live)
        w = jnp.where(chosen, aff, 0.0)
        group = jnp.sum(jnp.where(chosen, (lane // EPG).astype(F32), 0.0), axis=-1, keepdims=True) * (1.0 / TOP_K)
        xg_ref[0, rows, D:] = jnp.where(lane == N_EXPERTS, group, w / jnp.sum(w, axis=-1, keepdims=True))


def _out_proj(z, mods, mla_o, ml_o, na_o, wa, wm, wn, g2, rw, rb):
    B = z.shape[0]
    tok = lambda w_: pl.BlockSpec((1, OT, w_), lambda b, t: (b, t, 0))
    full = lambda a: pl.BlockSpec(a.shape, lambda b, t, _n=a.ndim: (0,) * _n)
    return pl.pallas_call(
        _out_proj_kernel,
        grid=(B, S // OT),
        in_specs=[tok(D),
                  pl.BlockSpec((1, 1, 6 * D), lambda b, t: (2 * b, 0, 0)),
                  pl.BlockSpec((1, 1, 6 * D), lambda b, t: (2 * b + 1, 0, 0)),
                  tok(MLA_H * V_D), tok(ML_WP), tok(NA_W),
                  full(wa), full(wm), full(wn), full(g2), full(rw), full(rb)],
        out_specs=[tok(D), tok(XW)],
        out_shape=[jax.ShapeDtypeStruct((B, S, D), F32),
                   jax.ShapeDtypeStruct((B, S, XW), F32)],
        compiler_params=_cparams(("parallel", "parallel")),
        name="out_proj",
    )(z, mods, mods, mla_o, ml_o, na_o, wa, wm, wn, g2, rw, rb)


ROW_WAIT = 128


def _row_copy_start(n, row_copy):
    def issue(i, carry):
        row_copy(i).start()
        return carry

    lax.fori_loop(0, n, issue, 0, unroll=8)


def _row_copy_wait(n, slab_copy):
    def drain(i, carry):
        slab_copy(ROW_WAIT).wait()
        return carry

    lax.fori_loop(0, n // ROW_WAIT, drain, 0)


def _experts_kernel(pos_ref, tg_ref, rows_ref, w1_ref, w3_ref, w2_ref, ys_ref, src_s, buf0, buf1, sems):
    i = pl.program_id(0)
    last = pl.num_programs(0) - 1
    n_rows = src_s.shape[0]
    bufs = (buf0, buf1)

    def row_copy(tile, slot, r):
        return pltpu.make_async_copy(rows_ref.at[pl.ds(src_s[tile * GROUP_T + r], 1)],
                                     bufs[slot].at[pl.ds(r, 1)], sems.at[slot])

    def slab_copy(slot):
        return lambda k: pltpu.make_async_copy(rows_ref.at[pl.ds(0, k)], bufs[slot].at[pl.ds(0, k)], sems.at[slot])

    @pl.when(i == 0)
    def _():
        def clear(r, carry):
            src_s[r] = 0
            return carry

        def place(t, carry):
            src_s[pos_ref[t]] = t
            return carry

        lax.fori_loop(0, n_rows, clear, 0, unroll=8)
        lax.fori_loop(0, pos_ref.shape[0], place, 0, unroll=8)
        _row_copy_start(GROUP_T, lambda r: row_copy(0, 0, r))

    def step(slot):
        nxt = jnp.minimum(i + 1, last)
        pending = iter(range(GROUP_T))

        def fetch(count):
            for _ in range(count):
                row_copy(nxt, 1 - slot, next(pending)).start()

        _row_copy_wait(GROUP_T, slab_copy(slot))
        buf = bufs[slot]
        g = tg_ref[i]
        x = buf[:, :D].astype(BF16)
        gates = buf[:, D:]
        lane = lax.broadcasted_iota(jnp.int32, (GROUP_T, 128), 1)
        per_dot = GROUP_T // (3 * EPG)
        ups = []
        for e in range(EPG):
            a = _dot(x, w1_ref[0, e].astype(BF16))
            fetch(per_dot)
            b = _dot(x, w3_ref[0, e].astype(BF16))
            fetch(per_dot)
            ups.append((a, b))
        acc = jnp.zeros((GROUP_T, D), F32)
        for e in range(EPG):
            ge = jnp.sum(jnp.where(lane == g * EPG + e, gates, 0.0), axis=-1, keepdims=True)
            a, b = ups[e]
            acc = acc + _dot((_silu(a) * b * ge).astype(BF16), w2_ref[0, e].astype(BF16))
            fetch(per_dot if e + 1 < EPG else GROUP_T - (3 * EPG - 1) * per_dot)
        ys_ref[...] = acc

        @pl.when(i == last)
        def _():
            _row_copy_wait(GROUP_T, slab_copy(1 - slot))

    for slot in range(2):
        pl.when(i % 2 == slot)(functools.partial(step, slot))


def _experts(pos, tile_group, rows, n_rows, w1, w3, w2, l):
    wspec = lambda k, n: pl.BlockSpec((1, EPG, k, n), lambda i, pos_, tg: (l, tg[i], 0, 0))
    return pl.pallas_call(
        _experts_kernel,
        grid_spec=pltpu.PrefetchScalarGridSpec(
            num_scalar_prefetch=2,
            grid=(n_rows // GROUP_T,),
            in_specs=[pl.BlockSpec(memory_space=pl.ANY),
                      wspec(D, D_FF), wspec(D, D_FF), wspec(D_FF, D)],
            out_specs=pl.BlockSpec((GROUP_T, D), lambda i, pos_, tg: (i, 0)),
            scratch_shapes=[pltpu.SMEM((n_rows,), jnp.int32),
                            pltpu.VMEM((GROUP_T, XW), F32), pltpu.VMEM((GROUP_T, XW), F32),
                            pltpu.SemaphoreType.DMA((2,))]),
        out_shape=jax.ShapeDtypeStruct((n_rows, D), F32),
        compiler_params=_cparams(("arbitrary",)),
        name="moe_experts",
    )(pos, tile_group, rows, w1, w3, w2)


def _combine_kernel(pos_ref, z1_ref, modx_ref, mody_ref, ys_ref, o_ref, buf, sem, *, tile):
    t = pl.program_id(1)
    base = pl.program_id(0) * S + t * tile
    _row_copy_start(tile, lambda i: pltpu.make_async_copy(
        ys_ref.at[pl.ds(pos_ref[base + i], 1)], buf.at[pl.ds(i, 1)], sem))
    _row_copy_wait(tile, lambda k: pltpu.make_async_copy(ys_ref.at[pl.ds(0, k)], buf.at[pl.ds(0, k)], sem))
    is_ctx = lax.broadcasted_iota(jnp.int32, (tile, 1), 0) + t * tile >= SEQ
    g2 = jnp.where(is_ctx, mody_ref[0][:, 5 * D:], modx_ref[0][:, 5 * D:])
    o_ref[0] = z1_ref[0] + g2 * buf[...]


def _combine(pos, z1, mods, ys, rows_out, tile):
    B = z1.shape[0]
    tok = pl.BlockSpec((1, tile, D), lambda b, t, pos_: (b, t, 0))
    return pl.pallas_call(
        functools.partial(_combine_kernel, tile=tile),
        grid_spec=pltpu.PrefetchScalarGridSpec(
            num_scalar_prefetch=1,
            grid=(B, rows_out // tile),
            in_specs=[tok,
                      pl.BlockSpec((1, 1, 6 * D), lambda b, t, pos_: (2 * b, 0, 0)),
                      pl.BlockSpec((1, 1, 6 * D), lambda b, t, pos_: (2 * b + 1, 0, 0)),
                      pl.BlockSpec(memory_space=pl.ANY)],
            out_specs=tok,
            scratch_shapes=[pltpu.VMEM((tile, D), F32), pltpu.SemaphoreType.DMA(())]),
        out_shape=jax.ShapeDtypeStruct((B, rows_out, D), F32),
        compiler_params=_cparams(("arbitrary", "arbitrary")),
        name="moe_combine",
    )(pos, z1, mods, mods, ys)


def _group_layout(gid, n_rows):
    onehot = (gid[:, None] == jnp.arange(N_GROUPS)[None, :]).astype(jnp.int32)
    rank = jnp.cumsum(onehot, axis=0) - onehot
    count = jnp.sum(onehot, axis=0)
    padded = (count + GROUP_T - 1) // GROUP_T * GROUP_T
    end = jnp.cumsum(padded)
    pos = jnp.sum(onehot * (end - padded + rank), axis=1)
    n_tiles = n_rows // GROUP_T
    tile_group = jnp.sum(jnp.arange(n_tiles)[:, None] * GROUP_T >= end[None, :], axis=1)
    return pos.astype(jnp.int32), jnp.minimum(tile_group, N_GROUPS - 1).astype(jnp.int32)


def _moe(z1, xg, mods, w1, w3, w2, l):
    B = z1.shape[0]
    n = B * S
    rows = xg.reshape(n, XW)
    gid = rows[:, D + N_EXPERTS].astype(jnp.int32)
    n_rows = (-(-n // GROUP_T) + N_GROUPS) * GROUP_T
    pos, tile_group = _group_layout(gid, n_rows)
    ys = _experts(pos, tile_group, rows, n_rows, w1, w3, w2, l)
    if l == DEPTH - 1:
        return _combine(pos, z1, mods, ys, SEQ, COMBINE_LAST_T)
    return _combine(pos, z1, mods, ys, S, COMBINE_T)


def _in_proj_layout(w):
    cuts = np.cumsum([Q_RANK, KV_RANK, ROPE_D, ML_W, ML_W, 4 * ML_H])
    qc, ckv, kr, u, zz, g, na = jnp.split(w, [int(v) for v in cuts], axis=-1)
    zeros = lambda n: jnp.zeros((w.shape[0], n), w.dtype)
    out = jnp.concatenate([qc, ckv, zeros(NOPE_D), kr, zeros(HP - QK_D),
                           _pad_heads(u, ML_H, ML_D, HP), _pad_heads(zz, ML_H, ML_D, HP),
                           _gate_order(g), zeros(128 - 4 * ML_H), na], axis=-1)
    assert out.shape[-1] == NP_IN
    return out


def _gate_order(g):
    i_f, f_f, i_b, f_b = jnp.split(g, 4, axis=-1)
    return jnp.concatenate([i_f, i_b, f_f, f_b], axis=-1)


def _pad_heads(v, nh, d, dp):
    lead = v.shape[:-1]
    v = v.reshape(lead + (nh, d))
    v = jnp.pad(v, [(0, 0)] * len(lead) + [(0, 0), (0, dp - d)])
    return v.reshape(lead + (nh * dp,))


def _rope_tables():
    t = np.arange(SEQ)
    row = (t // GRID_W).astype(np.float32)
    col = (t % GRID_W).astype(np.float32)
    quarter = ROPE_D // 4
    inv = jnp.asarray(ROPE_BASE, F32) ** (-jnp.arange(quarter, dtype=F32) / quarter)
    ar = jnp.asarray(row)[:, None] * inv
    ac = jnp.asarray(col)[:, None] * inv
    ang = jnp.concatenate([ar, ar, ac, ac], axis=-1)
    cos = jnp.ones((S, HP), F32).at[:SEQ, NOPE_D:QK_D].set(jnp.cos(ang))
    sin = jnp.zeros((S, HP), F32).at[:SEQ, NOPE_D:QK_D].set(jnp.sin(ang))
    return cos, sin


def _rotate_half_index():
    q = ROPE_D // 4
    src = np.arange(QK_D)
    sign = np.zeros((QK_D,), np.float32)
    for blk in range(2):
        lo = NOPE_D + 2 * q * blk
        src[lo:lo + q] = np.arange(lo + q, lo + 2 * q)
        sign[lo:lo + q] = -1.0
        src[lo + q:lo + 2 * q] = np.arange(lo, lo + q)
        sign[lo + q:lo + 2 * q] = 1.0
    return src, sign


def _rotate_half(w):
    src, sign = _rotate_half_index()
    return w[..., src] * sign


NA_NDR = 2 * WIN_R - 1
NA_NPAIR = 3 * NA_NDR


def _na_pair_index():
    idx = np.zeros((NA_NBLK, NA_QROWS, NA_KROWS // 2), np.int32)
    for blk in range(NA_NBLK):
        k0 = int(np.clip(blk * NA_QROWS - WIN_R // 2, 0, ROWS - NA_KROWS))
        for i in range(NA_QROWS):
            qr = blk * NA_QROWS + i
            rs = int(np.clip(qr - WIN_R // 2, 0, ROWS - WIN_R))
            assert k0 <= rs and rs + WIN_R <= k0 + NA_KROWS
            for p in range(NA_KROWS // 2):
                kr = k0 + 2 * p
                dr = kr - qr + WIN_R - 1
                left = rs <= kr < rs + WIN_R
                right = rs <= kr + 1 < rs + WIN_R
                if left and right:
                    idx[blk, i, p] = 1 + dr
                elif left:
                    idx[blk, i, p] = NA_NDR + dr
                elif right:
                    idx[blk, i, p] = 2 * NA_NDR + dr + 1
    return idx.reshape(-1)


def _na_pair_tiles(rpb):
    cq = np.arange(GRID_W)
    cs = np.clip(cq - WIN_C // 2, 0, GRID_W - WIN_C)
    col_ok = (cq[None, :] >= cs[:, None]) & (cq[None, :] < cs[:, None] + WIN_C)
    dc = np.clip(cq[None, :] - cq[:, None], -(WIN_C - 1), WIN_C - 1) + (WIN_C - 1)
    onehot = jnp.asarray(np.eye(2 * WIN_C - 1, dtype=np.float32)[dc])
    tiles = jnp.einsum('hrc,qkc->hrqk', rpb, onehot, precision=lax.Precision.HIGHEST)
    tiles = jnp.where(jnp.asarray(col_ok), tiles * float(np.log2(np.e)), NEG)
    masked = jnp.full_like(tiles, NEG)
    both = jnp.concatenate([tiles[:, :-1], tiles[:, 1:]], axis=-1)
    left = jnp.concatenate([tiles, masked], axis=-1)
    right = jnp.concatenate([masked, tiles], axis=-1)
    none = jnp.concatenate([masked[:, :1], masked[:, :1]], axis=-1)
    out = jnp.concatenate([none, both, left, right], axis=1)
    assert out.shape[1] == NA_NPAIR
    return out.astype(BF16)


def kernel(x, c, ctx, c_ctx, w_mod, b_mod, norm1_g, norm2_g, w_in, w_out, mla_qnorm_g, mla_w_uq, mla_kvnorm_g, mla_w_ukv, mla_q_g, mla_k_g, ml_conv_w, ml_conv_b, ml_w_q, ml_w_k, ml_w_v, ml_gate_b, ml_norm_g, ml_skip, na_q_g, na_k_g, na_rpb, router_w, router_b, moe_w1, moe_w3, moe_w2):
    B = x.shape[0]
    z = jnp.concatenate([x, ctx], axis=1)
    cc = jnp.zeros((16, D), F32).at[:B].set(c).at[B].set(c_ctx)
    mod_all = _modulation(cc, w_mod, b_mod)
    cos, sin = _rope_tables()
    rot_src, rot_sign = _rotate_half_index()
    rot_np = np.zeros((HP, HP), np.float32)
    rot_np[rot_src, np.arange(QK_D)] = rot_sign
    rot_mat = jnp.asarray(rot_np, BF16)
    ones_hp = jnp.ones((HP, HP), BF16)
    seg = jnp.asarray(np.kron(np.eye(NA_H), np.ones((NA_D, NA_D))), BF16)
    tril = jnp.asarray(np.stack([np.tril(np.ones((ML_CHUNK, ML_CHUNK))), np.triu(np.ones((ML_CHUNK, ML_CHUNK)))]), BF16)
    pair_idx = jnp.asarray(_na_pair_index())
    rw = jnp.stack(_split_bf16(jnp.pad(router_w, ((0, 0), (0, 128 - N_EXPERTS))), 2))
    rb = jnp.pad(router_b, (0, 128 - N_EXPERTS), constant_values=NEG).reshape(1, 128)

    def pad_lanes(v, n):
        return jnp.pad(v, [(0, 0)] * (v.ndim - 1) + [(0, n - v.shape[-1])])

    for l in range(DEPTH):
        mx = mod_all[l, :B]
        my = jnp.broadcast_to(mod_all[l, B], (B, 6 * D))
        mods = jnp.stack([mx, my], axis=1).reshape(2 * B, 1, 6 * D)

        w_in_p = _in_proj_layout(w_in[l]).astype(BF16)
        pmla, pu, pg, pna = _in_proj(z, mods, norm1_g[l].reshape(1, D), w_in_p)

        wq = jnp.transpose(mla_w_uq[l].reshape(Q_RANK, MLA_H, QK_D), (1, 0, 2))
        wuq = jnp.concatenate([pad_lanes(wq, HP), pad_lanes(_rotate_half(wq), HP)], axis=-1).astype(BF16)
        wukv = jnp.transpose(mla_w_ukv[l].reshape(KV_RANK, MLA_H, NOPE_D + V_D), (1, 0, 2))
        wuk = pad_lanes(wukv[..., :NOPE_D], HP).astype(BF16)
        wuv = pad_lanes(wukv[..., NOPE_D:], HP).astype(BF16)
        qg, kg = mla_q_g[l], mla_k_g[l]
        q_scale = float(QK_D ** -0.5 * np.log2(np.e))
        tabs = jnp.stack([cos * pad_lanes(qg, HP) * q_scale, sin * pad_lanes(jnp.abs(rot_sign) * qg[rot_src], HP) * q_scale,
                          cos * pad_lanes(kg, HP), sin * pad_lanes(jnp.abs(rot_sign) * kg[rot_src], HP)])
        heads_last = lambda w_: jnp.transpose(w_, (1, 0, 2)).reshape(w_.shape[1], -1)
        q, k, v = _mla_prep(pmla, tabs, mla_qnorm_g[l].reshape(1, Q_RANK), heads_last(wuq),
                            mla_kvnorm_g[l].reshape(1, KV_RANK), heads_last(wuk), heads_last(wuv), rot_mat, ones_hp)
        mla_o = _mla_attn(q, k, v)

        padh = lambda a: _pad_heads(a, ML_H, ML_D, HP)
        padw = lambda w_: jnp.pad(w_, ((0, 0), (0, HP - ML_D), (0, HP - ML_D))).astype(BF16)
        cw = jnp.pad(padh(ml_conv_w[l]), ((0, 8 - 3), (0, 0)))
        ml_o = _mlstm(pu, pg, cw, padh(ml_conv_b[l]).reshape(1, ML_WP),
                      padw(ml_w_q[l]), padw(jnp.swapaxes(ml_w_k[l], 1, 2) * (ML_D ** -0.5)), padw(ml_w_v[l]),
                      pad_lanes(_gate_order(ml_gate_b[l]).reshape(1, 4 * ML_H), 128),
                      padh(ml_norm_g[l]).reshape(1, ML_WP), padh(ml_skip[l]).reshape(1, ML_WP), tril)

        na_o = _na_attn(pair_idx, pna, _na_pair_tiles(na_rpb[l]), jnp.tile(na_q_g[l], NA_H).reshape(1, NA_W),
                        jnp.tile(na_k_g[l], NA_H).reshape(1, NA_W), seg)

        wo = w_out[l]
        wa = wo[:MLA_H * V_D].astype(BF16)
        wm = jnp.pad(wo[MLA_H * V_D:MLA_H * V_D + ML_W].reshape(ML_H, ML_D, D),
                     ((0, 0), (0, HP - ML_D), (0, 0))).reshape(ML_WP, D).astype(BF16)
        wn = wo[MLA_H * V_D + ML_W:].astype(BF16)
        z1, xg = _out_proj(z, mods, mla_o, ml_o, na_o, wa, wm, wn, norm2_g[l].reshape(1, D), rw, rb)
        z = _moe(z1, xg, mods, moe_w1, moe_w3, moe_w2, l)

    return z
```

```python
import functools

import numpy as np
import jax
import jax.numpy as jnp
from jax import lax
from jax.experimental import pallas as pl
from jax.experimental.pallas import tpu as pltpu

F32 = jnp.float32
BF16 = jnp.bfloat16

D = 1024
SEQ = 2048
CTX = 256
S = SEQ + CTX
DEPTH = 4
GRID_W = 64
ROWS = SEQ // GRID_W
EPS = 1e-6

MLA_H = 6
Q_RANK = 256
KV_RANK = 128
NOPE_D = 64
ROPE_D = 32
V_D = 64
QK_D = NOPE_D + ROPE_D
ROPE_BASE = 10000.0

ML_H = 4
ML_D = 96
ML_W = ML_H * ML_D
HP = 128
ML_WP = ML_H * HP
ML_CHUNK = 128
N_CHUNK = S // ML_CHUNK
N_CTX_CHUNK = CTX // ML_CHUNK

NA_H = 4
NA_D = 64
NA_W = NA_H * NA_D
WIN_R = 8
WIN_C = 16
NA_QROWS = 4
NA_KROWS = 12
NA_QB = NA_QROWS * GRID_W
NA_KB = NA_KROWS * GRID_W
NA_NBLK = ROWS // NA_QROWS

N_EXPERTS = 16
N_GROUPS = 4
EPG = N_EXPERTS // N_GROUPS
D_FF = 256

TT = 256
NT = S // TT
OT = 768
TOP_K = 2
XW = D + 128
GROUP_T = 512
COMBINE_T = 768
COMBINE_LAST_T = 512
NEG = -1e30

C_QC = 0
C_CKV = 256
C_KR = 384
C_U = 512
C_Z = C_U + ML_WP
C_G = C_Z + ML_WP
C_NA = C_G + 128
NP_IN = C_NA + 3 * NA_W

VMEM_LIMIT = 56 * 1024 * 1024


def _cparams(sem):
    return pltpu.CompilerParams(dimension_semantics=sem, vmem_limit_bytes=VMEM_LIMIT)


def _sigmoid(x):
    return 1.0 / (1.0 + jnp.exp(-x))


def _silu(x):
    return x * _sigmoid(x)


def _dot(a, b):
    return jnp.dot(a, b, preferred_element_type=F32)


def _dot_nt(a, b):
    return lax.dot_general(a, b, (((1,), (1,)), ((), ())), preferred_element_type=F32)


def _dot_tn(a, b):
    return lax.dot_general(a, b, (((0,), (0,)), ((), ())), preferred_element_type=F32)


def _dot_hi(a, b):
    return jnp.dot(a, b, preferred_element_type=F32, precision=lax.Precision.HIGHEST)


def _split_bf16(x, n):
    parts = []
    for _ in range(n):
        p = x.astype(BF16)
        parts.append(p)
        x = x - p.astype(F32)
    return parts


def _mod_rows(mod_ref, t):
    m = mod_ref[0]
    return [m[:, i * D:(i + 1) * D] for i in range(6)]


def _mod_kernel(c_ref, w_ref, b_ref, o_ref):
    sc = _silu(c_ref[...])
    o_ref[0] = _dot_hi(sc, w_ref[0]) + b_ref[0]


def _modulation(cc, w_mod, b_mod):
    nc = 6
    return pl.pallas_call(
        _mod_kernel,
        grid=(DEPTH, nc),
        in_specs=[pl.BlockSpec((16, D), lambda l, j: (0, 0)),
                  pl.BlockSpec((1, D, D), lambda l, j: (l, 0, j)),
                  pl.BlockSpec((1, 1, D), lambda l, j: (l, 0, j))],
        out_specs=pl.BlockSpec((1, 16, D), lambda l, j: (l, 0, j)),
        out_shape=jax.ShapeDtypeStruct((DEPTH, 16, 6 * D), F32),
        compiler_params=_cparams(("parallel", "parallel")),
        name="modulation",
    )(cc, w_mod, b_mod.reshape(DEPTH, 1, 6 * D))


def _in_proj_kernel(z_ref, mod_ref, g_ref, w_ref, pmla_ref, pu_ref, pg_ref, pna_ref):
    sh1, sc1 = _mod_rows(mod_ref, None)[:2]
    x = z_ref[0]
    xn = x * lax.rsqrt(jnp.mean(x * x, axis=-1, keepdims=True) + EPS) * g_ref[...]
    xn = xn * (1.0 + sc1) + sh1
    p = _dot(xn.astype(BF16), w_ref[...])
    pmla_ref[0] = p[:, :C_U].astype(BF16)
    pu_ref[0] = p[:, C_U:C_G].astype(BF16)
    pg_ref[0] = p[:, C_G:C_NA]
    pna_ref[0] = p[:, C_NA:].astype(BF16)


def _mod_spec():
    return pl.BlockSpec((1, 1, 6 * D), lambda b, t: (2 * b + t // (NT - 1), 0, 0))


def _in_proj(z, mods, g, w):
    B = z.shape[0]
    tok = lambda w_: pl.BlockSpec((1, TT, w_), lambda b, t: (b, t, 0))
    return pl.pallas_call(
        _in_proj_kernel,
        grid=(B, NT),
        in_specs=[tok(D), _mod_spec(),
                  pl.BlockSpec((1, D), lambda b, t: (0, 0)),
                  pl.BlockSpec((D, NP_IN), lambda b, t: (0, 0))],
        out_specs=[tok(C_U), tok(2 * ML_WP), tok(128), tok(3 * NA_W)],
        out_shape=[jax.ShapeDtypeStruct((B, S, C_U), BF16),
                   jax.ShapeDtypeStruct((B, S, 2 * ML_WP), BF16),
                   jax.ShapeDtypeStruct((B, S, 128), F32),
                   jax.ShapeDtypeStruct((B, S, 3 * NA_W), BF16)],
        compiler_params=_cparams(("parallel", "parallel")),
        name="in_proj",
    )(z, mods, g, w)


def _mla_prep_kernel(p_ref, tab_ref, qng_ref, wuq_ref, kvng_ref, wuk_ref, wuv_ref, rot_ref, ones_ref,
                     q_out, k_out, v_out):
    p = p_ref[0].astype(F32)
    qc = p[:, C_QC:C_CKV]
    ckv = p[:, C_CKV:C_KR]
    kr = p[:, C_KR:C_U]
    qcn = (qc * lax.rsqrt(jnp.mean(qc * qc, axis=-1, keepdims=True) + EPS) * qng_ref[...]).astype(BF16)
    ckvn = (ckv * lax.rsqrt(jnp.mean(ckv * ckv, axis=-1, keepdims=True) + EPS) * kvng_ref[...]).astype(BF16)
    lane = lax.broadcasted_iota(jnp.int32, (TT, HP), 1)
    ones = ones_ref[...]
    kr_rot = _dot(kr.astype(BF16), rot_ref[...])

    q_all = _dot(qcn, wuq_ref[...])
    k_all = _dot(ckvn, wuk_ref[...])
    v_all = _dot(ckvn, wuv_ref[...])
    qs = [q_all[:, 2 * h * HP:(2 * h + 1) * HP] for h in range(MLA_H)]
    q_rots = [q_all[:, (2 * h + 1) * HP:(2 * h + 2) * HP] for h in range(MLA_H)]
    ks = [k_all[:, h * HP:(h + 1) * HP] + kr for h in range(MLA_H)]
    ss_q = [_dot((x * x).astype(BF16), ones) for x in qs]
    ss_k = [_dot((x * x).astype(BF16), ones) for x in ks]

    def norm_rope(x, x_rot, ss, cos_g, sin_g):
        return lax.rsqrt(ss * (1.0 / QK_D) + EPS) * (x * cos_g + x_rot * sin_g)

    for h in range(MLA_H):
        q_out[0, h] = norm_rope(qs[h], q_rots[h], ss_q[h], tab_ref[0], tab_ref[1]).astype(BF16)
        k_out[0, h] = norm_rope(ks[h], kr_rot, ss_k[h], tab_ref[2], tab_ref[3]).astype(BF16)
        v_out[0, h] = jnp.where(lane < V_D, v_all[:, h * HP:(h + 1) * HP], 1.0).astype(BF16)


def _mla_prep(pmla, tabs, qng, wuq, kvng, wuk, wuv, rot, ones):
    B = pmla.shape[0]
    full = lambda a: pl.BlockSpec(a.shape, lambda b, t, _n=a.ndim: (0,) * _n)
    hd = lambda w_: pl.BlockSpec((1, MLA_H, TT, w_), lambda b, t: (b, 0, t, 0))
    return pl.pallas_call(
        _mla_prep_kernel,
        grid=(B, NT),
        in_specs=[pl.BlockSpec((1, TT, C_U), lambda b, t: (b, t, 0)),
                  pl.BlockSpec((4, TT, HP), lambda b, t: (0, t, 0)),
                  full(qng), full(wuq), full(kvng), full(wuk), full(wuv), full(rot), full(ones)],
        out_specs=[hd(HP), hd(HP), hd(HP)],
        out_shape=[jax.ShapeDtypeStruct((B, MLA_H, S, HP), BF16),
                   jax.ShapeDtypeStruct((B, MLA_H, S, HP), BF16),
                   jax.ShapeDtypeStruct((B, MLA_H, S, HP), BF16)],
        compiler_params=_cparams(("parallel", "parallel")),
        name="mla_prep",
    )(pmla, tabs, qng, wuq, kvng, wuk, wuv, rot, ones)


def _mla_attn_kernel(q_ref, k_ref, v_ref, o_ref):
    t = pl.program_id(1)

    def attend(k_lo, k_n):
        def scores(h):
            return _dot_nt(q_ref[0, h], k_ref[0, h, k_lo:k_lo + k_n, :])

        outs = []
        s_next = scores(0)
        for h in range(MLA_H):
            s = s_next
            if h + 1 < MLA_H:
                s_next = scores(h + 1)
            m = jnp.max(s, axis=-1, keepdims=True)
            p = jnp.exp2(s - m)
            pv = _dot(p.astype(BF16), v_ref[0, h, k_lo:k_lo + k_n, :])
            outs.append(pv[:, :V_D] / pv[:, V_D:V_D + 1])
        o_ref[0] = jnp.concatenate(outs, axis=-1).astype(BF16)

    @pl.when(t < NT - 1)
    def _():
        attend(0, S)

    @pl.when(t == NT - 1)
    def _():
        attend(SEQ, CTX)


def _mla_attn(q, k, v):
    B = q.shape[0]
    return pl.pallas_call(
        _mla_attn_kernel,
        grid=(B, NT),
        in_specs=[pl.BlockSpec((1, MLA_H, TT, HP), lambda b, t: (b, 0, t, 0)),
                  pl.BlockSpec((1, MLA_H, S, HP), lambda b, t: (b, 0, 0, 0)),
                  pl.BlockSpec((1, MLA_H, S, HP), lambda b, t: (b, 0, 0, 0))],
        out_specs=pl.BlockSpec((1, TT, MLA_H * V_D), lambda b, t: (b, t, 0)),
        out_shape=jax.ShapeDtypeStruct((B, S, MLA_H * V_D), BF16),
        compiler_params=_cparams(("parallel", "arbitrary")),
        name="mla_attn",
    )(q, k, v)


def _log_sigmoid(x):
    return jnp.minimum(x, 0.0) - jnp.log(1.0 + jnp.exp(-jnp.abs(x)))


def _mlstm_kernel(pu_ref, pg_ref, cw_ref, cb_ref, wq_ref, wk_ref, wv_ref, gb_ref, ng_ref, sk_ref,
                  tril_ref, o_ref, uc_s, q_s, kt_s, v_s, h_s, c_s, m_s, pm_s, b_s, rt_s):
    CA = 2 * ML_CHUNK
    row = lax.broadcasted_iota(jnp.int32, (CA, ML_WP), 0)

    def conv_body(i, carry):
        r0 = pl.multiple_of(i * CA, CA)
        x = pu_ref[0, pl.ds(r0, CA), 0:ML_WP].astype(F32)
        pr = pl.multiple_of(jnp.maximum(r0 - 16, 0), 16)
        nx = pl.multiple_of(jnp.minimum(r0 + CA, S - 16), 16)
        prev = pu_ref[0, pl.ds(pr, 16), 0:ML_WP].astype(F32)[15:16, :]
        nxt = pu_ref[0, pl.ds(nx, 16), 0:ML_WP].astype(F32)[0:1, :]
        seq_start = jnp.logical_or(r0 == 0, r0 == SEQ)
        seq_end = jnp.logical_or(r0 + CA == SEQ, r0 + CA == S)
        prev = jnp.where(seq_start, 0.0, prev)
        nxt = jnp.where(seq_end, 0.0, nxt)
        up = jnp.where(row == 0, prev, pltpu.roll(x, 1, 0))
        dn = jnp.where(row == CA - 1, nxt, pltpu.roll(x, CA - 1, 0))
        uc = _silu(cw_ref[0:1, :] * up + cw_ref[1:2, :] * x + cw_ref[2:3, :] * dn + cb_ref[...])
        ucb = uc.astype(BF16)
        uc_s[pl.ds(r0, CA), :] = ucb
        xb = x.astype(BF16)
        for h in range(ML_H):
            sl = slice(h * HP, (h + 1) * HP)
            q_s[pl.ds(r0, CA), sl] = _dot(ucb[:, sl], wq_ref[h]).astype(BF16)
            kt = _dot_nt(wk_ref[h], ucb[:, sl])
            kt_s[2 * i, sl, :] = kt[:, :ML_CHUNK].astype(BF16)
            kt_s[2 * i + 1, sl, :] = kt[:, ML_CHUNK:].astype(BF16)
            v_s[pl.ds(r0, CA), 2 * h * HP:(2 * h + 1) * HP] = _dot(xb[:, sl], wv_ref[h]).astype(BF16)
            v_s[pl.ds(r0, CA), (2 * h + 1) * HP:(2 * h + 2) * HP] = jnp.ones((CA, HP), BF16)

        for half in range(2):
            rows = pl.ds(r0 + half * ML_CHUNK, ML_CHUNK)
            g = pg_ref[0, rows, :] + gb_ref[...]
            parts = _split_bf16(_log_sigmoid(g), 3)
            cum_f = sum(_dot(tril_ref[0], part) for part in parts)
            cum_b = sum(_dot(tril_ref[1], part) for part in parts)
            bsh = pltpu.roll(jnp.where(bwd_lane, cum_b, cum_f), 128 - n_chain, 1)
            r = g - bsh
            pf = r
            pb = r
            k = 1
            while k < ML_CHUNK:
                pf = jnp.maximum(pf, jnp.where(ti >= k, pltpu.roll(pf, k, 0), NEG))
                pb = jnp.maximum(pb, jnp.where(ti < ML_CHUNK - k, pltpu.roll(pb, ML_CHUNK - k, 0), NEG))
                k *= 2
            pm_s[2 * i + half] = jnp.where(bwd_lane, pb, pf)
            b_s[2 * i + half] = bsh
            rt_s[2 * i + half] = r.T[0:n_chain, :]
        return carry

    n_chain = 2 * ML_H
    ti = lax.broadcasted_iota(jnp.int32, (ML_CHUNK, ML_CHUNK), 0)
    si = lax.broadcasted_iota(jnp.int32, (ML_CHUNK, ML_CHUNK), 1)
    bwd_lane = (si % n_chain) >= ML_H
    lax.fori_loop(0, S // CA, conv_body, 0)

    c_s[...] = jnp.zeros_like(c_s)
    m_s[...] = jnp.zeros_like(m_s)
    masks = (si <= ti, si >= ti)

    def scan_body(j, carry):
        chunk = (jnp.where(j < N_CTX_CHUNK, j + N_CHUNK - N_CTX_CHUNK, j - N_CTX_CHUNK), N_CHUNK - 1 - j)
        chains = []
        for d in range(2):
            r0 = pl.multiple_of(chunk[d] * ML_CHUNK, ML_CHUNK)
            p_col = pm_s[chunk[d]]
            bsh = b_s[chunk[d]]
            r_t = rt_s[chunk[d]]
            end = ML_CHUNK - 1 if d == 0 else 0
            for h in range(ML_H):
                c = d * ML_H + h
                sl = slice(h * HP, (h + 1) * HP)
                qc = q_s[pl.ds(r0, ML_CHUNK), sl]
                kt = kt_s[chunk[d], sl, :]
                vx = v_s[pl.ds(r0, ML_CHUNK), 2 * h * HP:(2 * h + 2) * HP]
                r_row = r_t[c:c + 1, :]
                m = m_s[c]
                st = c_s[c]
                big_m = jnp.maximum(m, jnp.broadcast_to(p_col[:, c:c + 1], (ML_CHUNK, HP)))
                b_b = jnp.broadcast_to(bsh[:, c:c + 1], (ML_CHUNK, HP))
                m_end = big_m[end:end + 1, :]
                ktw = (kt.astype(F32) * jnp.exp(r_row - m_end)).astype(BF16)
                chains.append(dict(d=d, r0=r0, sl=sl, c=c, vx=vx, r_row=r_row, m=m, st=st, big_m=big_m,
                                   b_b=b_b, m_end=m_end, end=end,
                                   qk=_dot(qc, kt), inter=_dot(qc, st.astype(BF16)), upd=_dot(ktw, vx)))
        for ch in chains:
            dw = jnp.exp(jnp.where(masks[ch["d"]], ch["r_row"] - ch["big_m"], NEG))
            ch["intra"] = _dot((ch["qk"] * dw).astype(BF16), ch["vx"])
        for ch in chains:
            m, big_m, inter, intra, end = ch["m"], ch["big_m"], ch["inter"], ch["intra"], ch["end"]
            iw = jnp.exp(m - big_m)
            num = iw * inter[:, :HP] + intra[:, :HP]
            nq = iw * inter[:, HP:] + intra[:, HP:]
            hv = num / jnp.maximum(jnp.abs(nq), jnp.exp(-(ch["b_b"] + big_m)))
            a = jnp.exp(m - ch["m_end"])
            ch["out"] = (hv, jnp.concatenate([a, a], axis=1) * ch["st"] + ch["upd"],
                         ch["b_b"][end:end + 1, :] + ch["m_end"])
        for ch in chains:
            hv, st_new, m_new = ch["out"]
            h_s[ch["d"], pl.ds(ch["r0"], ML_CHUNK), ch["sl"]] = hv
            c_s[ch["c"]] = st_new
            m_s[ch["c"]] = m_new
        return carry

    lax.fori_loop(0, N_CHUNK, scan_body, 0)

    live = (lax.broadcasted_iota(jnp.int32, (CA, HP), 1) < ML_D).astype(F32)

    def out_body(i, carry):
        r0 = pl.multiple_of(i * CA, CA)
        for h in range(ML_H):
            sl = slice(h * HP, (h + 1) * HP)
            hh = h_s[0, pl.ds(r0, CA), sl] + h_s[1, pl.ds(r0, CA), sl]
            mu = jnp.sum(hh, axis=-1, keepdims=True) * (1.0 / ML_D)
            dv = (hh - mu) * live
            var = jnp.sum(dv * dv, axis=-1, keepdims=True) * (1.0 / ML_D)
            hn = dv * lax.rsqrt(var + EPS) * ng_ref[:, sl]
            uc = uc_s[pl.ds(r0, CA), sl].astype(F32)
            zz = pu_ref[0, pl.ds(r0, CA), ML_WP + h * HP:ML_WP + (h + 1) * HP].astype(F32)
            o_ref[0, pl.ds(r0, CA), sl] = ((hn + sk_ref[:, sl] * uc) * _silu(zz)).astype(BF16)
        return carry

    lax.fori_loop(0, S // CA, out_body, 0)


def _mlstm(pu, pg, cw, cb, wq, wk, wv, gb, ng, sk, tril):
    B = pu.shape[0]
    full = lambda a: pl.BlockSpec(a.shape, lambda b, _n=a.ndim: (0,) * _n)
    n_chain = 2 * ML_H
    return pl.pallas_call(
        _mlstm_kernel,
        grid=(B,),
        in_specs=[pl.BlockSpec((1, S, 2 * ML_WP), lambda b: (b, 0, 0)),
                  pl.BlockSpec((1, S, 128), lambda b: (b, 0, 0)),
                  full(cw), full(cb), full(wq), full(wk), full(wv), full(gb), full(ng), full(sk), full(tril)],
        out_specs=pl.BlockSpec((1, S, ML_WP), lambda b: (b, 0, 0)),
        out_shape=jax.ShapeDtypeStruct((B, S, ML_WP), BF16),
        scratch_shapes=[pltpu.VMEM((S, ML_WP), BF16), pltpu.VMEM((S, ML_WP), BF16),
                        pltpu.VMEM((N_CHUNK, ML_WP, ML_CHUNK), BF16), pltpu.VMEM((S, 2 * ML_WP), BF16),
                        pltpu.VMEM((2, S, ML_WP), F32),
                        pltpu.VMEM((n_chain, HP, 2 * HP), F32),
                        pltpu.VMEM((n_chain, 1, HP), F32),
                        pltpu.VMEM((N_CHUNK, ML_CHUNK, 128), F32),
                        pltpu.VMEM((N_CHUNK, ML_CHUNK, 128), F32),
                        pltpu.VMEM((N_CHUNK, n_chain, ML_CHUNK), F32)],
        compiler_params=_cparams(("parallel",)),
        name="mlstm",
    )(pu, pg, cw, cb, wq, wk, wv, gb, ng, sk, tril)


def _na_kernel(idx_ref, p_ref, pt_ref, qg_ref, kg_ref, seg_ref, o_ref, kn_s, bias_s):
    j = pl.program_id(1)
    seg = seg_ref[...]

    def headnorm(x, g):
        ss = _dot((x * x).astype(BF16), seg)
        return x * lax.rsqrt(ss * (1.0 / NA_D) + EPS) * g

    @pl.when(j == 0)
    def _():
        def body(i, carry):
            r0 = pl.multiple_of(i * TT, TT)
            kk = p_ref[0, pl.ds(r0, TT), NA_W:2 * NA_W].astype(F32)
            kn_s[pl.ds(r0, TT), :] = headnorm(kk, kg_ref[...]).astype(BF16)
            return carry
        lax.fori_loop(0, NT, body, 0)

    scale = float(NA_D ** -0.5 * np.log2(np.e))
    kctx = kn_s[SEQ:S, :]
    vctx = p_ref[0, SEQ:S, 2 * NA_W:3 * NA_W]

    @pl.when(j < NA_NBLK)
    def _():
        q0 = pl.multiple_of(j * NA_QB, NA_QB)
        k0 = pl.multiple_of(jnp.clip(j * NA_QROWS - WIN_R // 2, 0, ROWS - NA_KROWS) * GRID_W, 256)
        q = headnorm(p_ref[0, pl.ds(q0, NA_QB), 0:NA_W].astype(F32), qg_ref[...]) * scale
        kl = kn_s[pl.ds(k0, NA_KB), :]
        vl = p_ref[0, pl.ds(k0, NA_KB), 2 * NA_W:3 * NA_W]
        head = lax.broadcasted_iota(jnp.int32, (NA_QB, NA_W), 1) // NA_D
        acc = jnp.zeros((NA_QB, NA_W), F32)

        def scores(h):
            qm = jnp.where(head == h, q, 0.0).astype(BF16)
            for i in range(NA_QROWS):
                for p in range(NA_KROWS // 2):
                    code = idx_ref[(j * NA_QROWS + i) * (NA_KROWS // 2) + p]
                    bias_s[i * GRID_W:(i + 1) * GRID_W, p * 2 * GRID_W:(p + 1) * 2 * GRID_W] = pt_ref[h, code]
            return _dot_nt(qm, kl) + bias_s[...].astype(F32), _dot_nt(qm, kctx)

        s_next = scores(0)
        for h in range(NA_H):
            s1, s2 = s_next
            if h + 1 < NA_H:
                s_next = scores(h + 1)
            m = jnp.maximum(jnp.max(s1, axis=-1, keepdims=True), jnp.max(s2, axis=-1, keepdims=True))
            p1 = jnp.exp2(s1 - m)
            p2 = jnp.exp2(s2 - m)
            l = jnp.sum(p1, axis=-1, keepdims=True) + jnp.sum(p2, axis=-1, keepdims=True)
            o = (_dot(p1.astype(BF16), vl) + _dot(p2.astype(BF16), vctx)) / l
            acc = jnp.where(head == h, o, acc)
        o_ref[0, pl.ds(q0, NA_QB), :] = acc.astype(BF16)

    @pl.when(j == NA_NBLK)
    def _():
        q = headnorm(p_ref[0, SEQ:S, 0:NA_W].astype(F32), qg_ref[...]) * scale
        head = lax.broadcasted_iota(jnp.int32, (CTX, NA_W), 1) // NA_D
        acc = jnp.zeros((CTX, NA_W), F32)
        for h in range(NA_H):
            qm = jnp.where(head == h, q, 0.0).astype(BF16)
            s2 = _dot_nt(qm, kctx)
            m = jnp.max(s2, axis=-1, keepdims=True)
            p2 = jnp.exp2(s2 - m)
            l = jnp.sum(p2, axis=-1, keepdims=True)
            o = _dot(p2.astype(BF16), vctx) / l
            acc = jnp.where(head == h, o, acc)
        o_ref[0, SEQ:S, :] = acc.astype(BF16)


def _na_attn(pair_idx, pna, pair_tiles, qg, kg, seg):
    B = pna.shape[0]
    return pl.pallas_call(
        _na_kernel,
        grid_spec=pltpu.PrefetchScalarGridSpec(
            num_scalar_prefetch=1,
            grid=(B, NA_NBLK + 1),
            in_specs=[pl.BlockSpec((1, S, 3 * NA_W), lambda b, j, idx: (b, 0, 0)),
                      pl.BlockSpec(pair_tiles.shape, lambda b, j, idx: (0, 0, 0, 0)),
                      pl.BlockSpec((1, NA_W), lambda b, j, idx: (0, 0)),
                      pl.BlockSpec((1, NA_W), lambda b, j, idx: (0, 0)),
                      pl.BlockSpec((NA_W, NA_W), lambda b, j, idx: (0, 0))],
            out_specs=pl.BlockSpec((1, S, NA_W), lambda b, j, idx: (b, 0, 0)),
            scratch_shapes=[pltpu.VMEM((S, NA_W), BF16), pltpu.VMEM((NA_QB, NA_KB), BF16)]),
        out_shape=jax.ShapeDtypeStruct((B, S, NA_W), BF16),
        compiler_params=_cparams(("parallel", "arbitrary")),
        name="na_attn",
    )(pair_idx, pna, pair_tiles, qg, kg, seg)


def _out_proj_kernel(z_ref, modx_ref, mody_ref, a_ref, m_ref, n_ref, wa_ref, wm_ref, wn_ref, g2_ref, rw_ref, rb_ref,
                     z1_ref, xg_ref):
    lane = lax.broadcasted_iota(jnp.int32, (TT, 128), 1)
    live = lane < N_EXPERTS
    groups = [slice(i * TT, (i + 1) * TT) for i in range(OT // TT)]

    splits = []
    for rows in groups:
        is_ctx = lax.broadcasted_iota(jnp.int32, (TT, 1), 0) + (pl.program_id(1) * OT + rows.start) >= SEQ
        mod = lambda i: jnp.where(is_ctx, mody_ref[0][:, i * D:(i + 1) * D], modx_ref[0][:, i * D:(i + 1) * D])
        mix = (_dot(a_ref[0, rows, :], wa_ref[...]) + _dot(m_ref[0, rows, :], wm_ref[...])
               + _dot(n_ref[0, rows, :], wn_ref[...]))
        x = z_ref[0, rows, :] + mod(2) * mix
        z1_ref[0, rows, :] = x
        hn = x * lax.rsqrt(jnp.mean(x * x, axis=-1, keepdims=True) + EPS) * g2_ref[...]
        hn = hn * (1.0 + mod(4)) + mod(3)
        h_hi, h_lo = _split_bf16(hn, 2)
        xg_ref[0, rows, :D] = hn
        splits.append((h_hi, h_lo))

    affs = [_sigmoid(_dot(h_hi, rw_ref[0]) + (_dot(h_hi, rw_ref[1]) + _dot(h_lo, rw_ref[0])))
            for h_hi, h_lo in splits]

    def cyc(x, k, width):
        fwd = pltpu.roll(x, 128 - k, 1)
        back = pltpu.roll(x, width - k, 1)
        return jnp.where((lane % width) + k < width, fwd, back)

    def rank(x, width, step):
        r = jnp.zeros((TT, 128), F32)
        for k in range(1, width // step):
            y = cyc(x, k * step, width)
            wrapped = (lane % width) + k * step >= width
            beats = jnp.logical_or(y > x, jnp.logical_and(y == x, wrapped))
            r = r + beats.astype(F32)
        return r

    for rows, aff in zip(groups, affs):
        sel = aff + rb_ref[...]
        top2 = rank(sel, EPG, 1) < 2.0
        part = jnp.where(top2, sel, 0.0)
        gscore = part
        for k in range(1, EPG):
            gscore = gscore + cyc(part, k, EPG)
        best = rank(gscore, N_EXPERTS, EPG) < 1.0
        chosen = jnp.logical_and(jnp.logical_and(top2, best), live)
        w = jnp.where(chosen, aff, 0.0)
        group = jnp.sum(jnp.where(chosen, (lane // EPG).astype(F32), 0.0), axis=-1, keepdims=True) * (1.0 / TOP_K)
        xg_ref[0, rows, D:] = jnp.where(lane == N_EXPERTS, group, w / jnp.sum(w, axis=-1, keepdims=True))


def _out_proj(z, mods, mla_o, ml_o, na_o, wa, wm, wn, g2, rw, rb):
    B = z.shape[0]
    tok = lambda w_: pl.BlockSpec((1, OT, w_), lambda b, t: (b, t, 0))
    full = lambda a: pl.BlockSpec(a.shape, lambda b, t, _n=a.ndim: (0,) * _n)
    return pl.pallas_call(
        _out_proj_kernel,
        grid=(B, S // OT),
        in_specs=[tok(D),
                  pl.BlockSpec((1, 1, 6 * D), lambda b, t: (2 * b, 0, 0)),
                  pl.BlockSpec((1, 1, 6 * D), lambda b, t: (2 * b + 1, 0, 0)),
                  tok(MLA_H * V_D), tok(ML_WP), tok(NA_W),
                  full(wa), full(wm), full(wn), full(g2), full(rw), full(rb)],
        out_specs=[tok(D), tok(XW)],
        out_shape=[jax.ShapeDtypeStruct((B, S, D), F32),
                   jax.ShapeDtypeStruct((B, S, XW), F32)],
        compiler_params=_cparams(("parallel", "parallel")),
        name="out_proj",
    )(z, mods, mods, mla_o, ml_o, na_o, wa, wm, wn, g2, rw, rb)


ROW_WAIT = 128


def _row_copy_start(n, row_copy):
    def issue(i, carry):
        row_copy(2 * i).start(priority=0)
        row_copy(2 * i + 1).start(priority=1)
        return carry

    lax.fori_loop(0, n // 2, issue, 0, unroll=4)


def _row_copy_wait(n, slab_copy):
    def drain(i, carry):
        slab_copy(ROW_WAIT).wait()
        return carry

    lax.fori_loop(0, n // ROW_WAIT, drain, 0)


def _experts_kernel(pos_ref, tg_ref, rows_ref, w1_ref, w3_ref, w2_ref, ys_ref, src_s, buf0, buf1, sems):
    i = pl.program_id(0)
    last = pl.num_programs(0) - 1
    n_rows = src_s.shape[0]
    bufs = (buf0, buf1)

    def row_copy(tile, slot, r):
        return pltpu.make_async_copy(rows_ref.at[pl.ds(src_s[tile * GROUP_T + r], 1)],
                                     bufs[slot].at[pl.ds(r, 1)], sems.at[slot])

    def slab_copy(slot):
        return lambda k: pltpu.make_async_copy(rows_ref.at[pl.ds(0, k)], bufs[slot].at[pl.ds(0, k)], sems.at[slot])

    @pl.when(i == 0)
    def _():
        def clear(r, carry):
            src_s[r] = 0
            return carry

        def place(t, carry):
            src_s[pos_ref[t]] = t
            return carry

        lax.fori_loop(0, n_rows, clear, 0, unroll=8)
        lax.fori_loop(0, pos_ref.shape[0], place, 0, unroll=8)
        _row_copy_start(GROUP_T, lambda r: row_copy(0, 0, r))

    def step(slot):
        nxt = jnp.minimum(i + 1, last)
        pending = iter(range(GROUP_T))

        def fetch(count):
            for _ in range(count):
                r = next(pending)
                row_copy(nxt, 1 - slot, r).start(priority=r % 2)

        _row_copy_wait(GROUP_T, slab_copy(slot))
        buf = bufs[slot]
        g = tg_ref[i]
        x = buf[:, :D].astype(BF16)
        gates = buf[:, D:]
        lane = lax.broadcasted_iota(jnp.int32, (GROUP_T, 128), 1)
        per_dot = GROUP_T // (3 * EPG)
        ups = []
        for e in range(EPG):
            a = _dot(x, w1_ref[0, e].astype(BF16))
            fetch(per_dot)
            b = _dot(x, w3_ref[0, e].astype(BF16))
            fetch(per_dot)
            ups.append((a, b))
        acc = jnp.zeros((GROUP_T, D), F32)
        for e in range(EPG):
            ge = jnp.sum(jnp.where(lane == g * EPG + e, gates, 0.0), axis=-1, keepdims=True)
            a, b = ups[e]
            acc = acc + _dot((_silu(a) * b * ge).astype(BF16), w2_ref[0, e].astype(BF16))
            fetch(per_dot if e + 1 < EPG else GROUP_T - (3 * EPG - 1) * per_dot)
        ys_ref[...] = acc

        @pl.when(i == last)
        def _():
            _row_copy_wait(GROUP_T, slab_copy(1 - slot))

    for slot in range(2):
        pl.when(i % 2 == slot)(functools.partial(step, slot))


def _experts(pos, tile_group, rows, n_rows, w1, w3, w2, l):
    wspec = lambda k, n: pl.BlockSpec((1, EPG, k, n), lambda i, pos_, tg: (l, tg[i], 0, 0))
    return pl.pallas_call(
        _experts_kernel,
        grid_spec=pltpu.PrefetchScalarGridSpec(
            num_scalar_prefetch=2,
            grid=(n_rows // GROUP_T,),
            in_specs=[pl.BlockSpec(memory_space=pl.ANY),
                      wspec(D, D_FF), wspec(D, D_FF), wspec(D_FF, D)],
            out_specs=pl.BlockSpec((GROUP_T, D), lambda i, pos_, tg: (i, 0)),
            scratch_shapes=[pltpu.SMEM((n_rows,), jnp.int32),
                            pltpu.VMEM((GROUP_T, XW), F32), pltpu.VMEM((GROUP_T, XW), F32),
                            pltpu.SemaphoreType.DMA((2,))]),
        out_shape=jax.ShapeDtypeStruct((n_rows, D), F32),
        compiler_params=_cparams(("arbitrary",)),
        name="moe_experts",
    )(pos, tile_group, rows, w1, w3, w2)


def _combine_kernel(pos_ref, z1_ref, modx_ref, mody_ref, ys_ref, o_ref, buf, sem, *, tile):
    t = pl.program_id(1)
    base = pl.program_id(0) * S + t * tile
    _row_copy_start(tile, lambda i: pltpu.make_async_copy(
        ys_ref.at[pl.ds(pos_ref[base + i], 1)], buf.at[pl.ds(i, 1)], sem))
    _row_copy_wait(tile, lambda k: pltpu.make_async_copy(ys_ref.at[pl.ds(0, k)], buf.at[pl.ds(0, k)], sem))
    is_ctx = lax.broadcasted_iota(jnp.int32, (tile, 1), 0) + t * tile >= SEQ
    g2 = jnp.where(is_ctx, mody_ref[0][:, 5 * D:], modx_ref[0][:, 5 * D:])
    o_ref[0] = z1_ref[0] + g2 * buf[...]


def _combine(pos, z1, mods, ys, rows_out, tile):
    B = z1.shape[0]
    tok = pl.BlockSpec((1, tile, D), lambda b, t, pos_: (b, t, 0))
    return pl.pallas_call(
        functools.partial(_combine_kernel, tile=tile),
        grid_spec=pltpu.PrefetchScalarGridSpec(
            num_scalar_prefetch=1,
            grid=(B, rows_out // tile),
            in_specs=[tok,
                      pl.BlockSpec((1, 1, 6 * D), lambda b, t, pos_: (2 * b, 0, 0)),
                      pl.BlockSpec((1, 1, 6 * D), lambda b, t, pos_: (2 * b + 1, 0, 0)),
                      pl.BlockSpec(memory_space=pl.ANY)],
            out_specs=tok,
            scratch_shapes=[pltpu.VMEM((tile, D), F32), pltpu.SemaphoreType.DMA(())]),
        out_shape=jax.ShapeDtypeStruct((B, rows_out, D), F32),
        compiler_params=_cparams(("arbitrary", "arbitrary")),
        name="moe_combine",
    )(pos, z1, mods, mods, ys)


def _group_layout(gid, n_rows):
    onehot = (gid[:, None] == jnp.arange(N_GROUPS)[None, :]).astype(jnp.int32)
    rank = jnp.cumsum(onehot, axis=0) - onehot
    count = jnp.sum(onehot, axis=0)
    padded = (count + GROUP_T - 1) // GROUP_T * GROUP_T
    end = jnp.cumsum(padded)
    pos = jnp.sum(onehot * (end - padded + rank), axis=1)
    n_tiles = n_rows // GROUP_T
    tile_group = jnp.sum(jnp.arange(n_tiles)[:, None] * GROUP_T >= end[None, :], axis=1)
    return pos.astype(jnp.int32), jnp.minimum(tile_group, N_GROUPS - 1).astype(jnp.int32)


def _moe(z1, xg, mods, w1, w3, w2, l):
    B = z1.shape[0]
    n = B * S
    rows = xg.reshape(n, XW)
    gid = rows[:, D + N_EXPERTS].astype(jnp.int32)
    n_rows = (-(-n // GROUP_T) + N_GROUPS) * GROUP_T
    pos, tile_group = _group_layout(gid, n_rows)
    ys = _experts(pos, tile_group, rows, n_rows, w1, w3, w2, l)
    if l == DEPTH - 1:
        return _combine(pos, z1, mods, ys, SEQ, COMBINE_LAST_T)
    return _combine(pos, z1, mods, ys, S, COMBINE_T)


def _in_proj_layout(w):
    cuts = np.cumsum([Q_RANK, KV_RANK, ROPE_D, ML_W, ML_W, 4 * ML_H])
    qc, ckv, kr, u, zz, g, na = jnp.split(w, [int(v) for v in cuts], axis=-1)
    zeros = lambda n: jnp.zeros((w.shape[0], n), w.dtype)
    out = jnp.concatenate([qc, ckv, zeros(NOPE_D), kr, zeros(HP - QK_D),
                           _pad_heads(u, ML_H, ML_D, HP), _pad_heads(zz, ML_H, ML_D, HP),
                           _gate_order(g), zeros(128 - 4 * ML_H), na], axis=-1)
    assert out.shape[-1] == NP_IN
    return out


def _gate_order(g):
    i_f, f_f, i_b, f_b = jnp.split(g, 4, axis=-1)
    return jnp.concatenate([i_f, i_b, f_f, f_b], axis=-1)


def _pad_heads(v, nh, d, dp):
    lead = v.shape[:-1]
    v = v.reshape(lead + (nh, d))
    v = jnp.pad(v, [(0, 0)] * len(lead) + [(0, 0), (0, dp - d)])
    return v.reshape(lead + (nh * dp,))


def _rope_tables():
    t = np.arange(SEQ)
    row = (t // GRID_W).astype(np.float32)
    col = (t % GRID_W).astype(np.float32)
    quarter = ROPE_D // 4
    inv = jnp.asarray(ROPE_BASE, F32) ** (-jnp.arange(quarter, dtype=F32) / quarter)
    ar = jnp.asarray(row)[:, None] * inv
    ac = jnp.asarray(col)[:, None] * inv
    ang = jnp.concatenate([ar, ar, ac, ac], axis=-1)
    cos = jnp.ones((S, HP), F32).at[:SEQ, NOPE_D:QK_D].set(jnp.cos(ang))
    sin = jnp.zeros((S, HP), F32).at[:SEQ, NOPE_D:QK_D].set(jnp.sin(ang))
    return cos, sin


def _rotate_half_index():
    q = ROPE_D // 4
    src = np.arange(QK_D)
    sign = np.zeros((QK_D,), np.float32)
    for blk in range(2):
        lo = NOPE_D + 2 * q * blk
        src[lo:lo + q] = np.arange(lo + q, lo + 2 * q)
        sign[lo:lo + q] = -1.0
        src[lo + q:lo + 2 * q] = np.arange(lo, lo + q)
        sign[lo + q:lo + 2 * q] = 1.0
    return src, sign


def _rotate_half(w):
    src, sign = _rotate_half_index()
    return w[..., src] * sign


NA_NDR = 2 * WIN_R - 1
NA_NPAIR = 3 * NA_NDR


def _na_pair_index():
    idx = np.zeros((NA_NBLK, NA_QROWS, NA_KROWS // 2), np.int32)
    for blk in range(NA_NBLK):
        k0 = int(np.clip(blk * NA_QROWS - WIN_R // 2, 0, ROWS - NA_KROWS))
        for i in range(NA_QROWS):
            qr = blk * NA_QROWS + i
            rs = int(np.clip(qr - WIN_R // 2, 0, ROWS - WIN_R))
            assert k0 <= rs and rs + WIN_R <= k0 + NA_KROWS
            for p in range(NA_KROWS // 2):
                kr = k0 + 2 * p
                dr = kr - qr + WIN_R - 1
                left = rs <= kr < rs + WIN_R
                right = rs <= kr + 1 < rs + WIN_R
                if left and right:
                    idx[blk, i, p] = 1 + dr
                elif left:
                    idx[blk, i, p] = NA_NDR + dr
                elif right:
                    idx[blk, i, p] = 2 * NA_NDR + dr + 1
    return idx.reshape(-1)


def _na_pair_tiles(rpb):
    cq = np.arange(GRID_W)
    cs = np.clip(cq - WIN_C // 2, 0, GRID_W - WIN_C)
    col_ok = (cq[None, :] >= cs[:, None]) & (cq[None, :] < cs[:, None] + WIN_C)
    dc = np.clip(cq[None, :] - cq[:, None], -(WIN_C - 1), WIN_C - 1) + (WIN_C - 1)
    onehot = jnp.asarray(np.eye(2 * WIN_C - 1, dtype=np.float32)[dc])
    tiles = jnp.einsum('hrc,qkc->hrqk', rpb, onehot, precision=lax.Precision.HIGHEST)
    tiles = jnp.where(jnp.asarray(col_ok), tiles * float(np.log2(np.e)), NEG)
    masked = jnp.full_like(tiles, NEG)
    both = jnp.concatenate([tiles[:, :-1], tiles[:, 1:]], axis=-1)
    left = jnp.concatenate([tiles, masked], axis=-1)
    right = jnp.concatenate([masked, tiles], axis=-1)
    none = jnp.concatenate([masked[:, :1], masked[:, :1]], axis=-1)
    out = jnp.concatenate([none, both, left, right], axis=1)
    assert out.shape[1] == NA_NPAIR
    return out.astype(BF16)


def kernel(x, c, ctx, c_ctx, w_mod, b_mod, norm1_g, norm2_g, w_in, w_out, mla_qnorm_g, mla_w_uq, mla_kvnorm_g, mla_w_ukv, mla_q_g, mla_k_g, ml_conv_w, ml_conv_b, ml_w_q, ml_w_k, ml_w_v, ml_gate_b, ml_norm_g, ml_skip, na_q_g, na_k_g, na_rpb, router_w, router_b, moe_w1, moe_w3, moe_w2):
    B = x.shape[0]
    z = jnp.concatenate([x, ctx], axis=1)
    cc = jnp.zeros((16, D), F32).at[:B].set(c).at[B].set(c_ctx)
    mod_all = _modulation(cc, w_mod, b_mod)
    cos, sin = _rope_tables()
    rot_src, rot_sign = _rotate_half_index()
    rot_np = np.zeros((HP, HP), np.float32)
    rot_np[rot_src, np.arange(QK_D)] = rot_sign
    rot_mat = jnp.asarray(rot_np, BF16)
    ones_hp = jnp.ones((HP, HP), BF16)
    seg = jnp.asarray(np.kron(np.eye(NA_H), np.ones((NA_D, NA_D))), BF16)
    tril = jnp.asarray(np.stack([np.tril(np.ones((ML_CHUNK, ML_CHUNK))), np.triu(np.ones((ML_CHUNK, ML_CHUNK)))]), BF16)
    pair_idx = jnp.asarray(_na_pair_index())
    rw = jnp.stack(_split_bf16(jnp.pad(router_w, ((0, 0), (0, 128 - N_EXPERTS))), 2))
    rb = jnp.pad(router_b, (0, 128 - N_EXPERTS), constant_values=NEG).reshape(1, 128)

    def pad_lanes(v, n):
        return jnp.pad(v, [(0, 0)] * (v.ndim - 1) + [(0, n - v.shape[-1])])

    for l in range(DEPTH):
        mx = mod_all[l, :B]
        my = jnp.broadcast_to(mod_all[l, B], (B, 6 * D))
        mods = jnp.stack([mx, my], axis=1).reshape(2 * B, 1, 6 * D)

        w_in_p = _in_proj_layout(w_in[l]).astype(BF16)
        pmla, pu, pg, pna = _in_proj(z, mods, norm1_g[l].reshape(1, D), w_in_p)

        wq = jnp.transpose(mla_w_uq[l].reshape(Q_RANK, MLA_H, QK_D), (1, 0, 2))
        wuq = jnp.concatenate([pad_lanes(wq, HP), pad_lanes(_rotate_half(wq), HP)], axis=-1).astype(BF16)
        wukv = jnp.transpose(mla_w_ukv[l].reshape(KV_RANK, MLA_H, NOPE_D + V_D), (1, 0, 2))
        wuk = pad_lanes(wukv[..., :NOPE_D], HP).astype(BF16)
        wuv = pad_lanes(wukv[..., NOPE_D:], HP).astype(BF16)
        qg, kg = mla_q_g[l], mla_k_g[l]
        q_scale = float(QK_D ** -0.5 * np.log2(np.e))
        tabs = jnp.stack([cos * pad_lanes(qg, HP) * q_scale, sin * pad_lanes(jnp.abs(rot_sign) * qg[rot_src], HP) * q_scale,
                          cos * pad_lanes(kg, HP), sin * pad_lanes(jnp.abs(rot_sign) * kg[rot_src], HP)])
        heads_last = lambda w_: jnp.transpose(w_, (1, 0, 2)).reshape(w_.shape[1], -1)
        q, k, v = _mla_prep(pmla, tabs, mla_qnorm_g[l].reshape(1, Q_RANK), heads_last(wuq),
                            mla_kvnorm_g[l].reshape(1, KV_RANK), heads_last(wuk), heads_last(wuv), rot_mat, ones_hp)
        mla_o = _mla_attn(q, k, v)

        padh = lambda a: _pad_heads(a, ML_H, ML_D, HP)
        padw = lambda w_: jnp.pad(w_, ((0, 0), (0, HP - ML_D), (0, HP - ML_D))).astype(BF16)
        cw = jnp.pad(padh(ml_conv_w[l]), ((0, 8 - 3), (0, 0)))
        ml_o = _mlstm(pu, pg, cw, padh(ml_conv_b[l]).reshape(1, ML_WP),
                      padw(ml_w_q[l]), padw(jnp.swapaxes(ml_w_k[l], 1, 2) * (ML_D ** -0.5)), padw(ml_w_v[l]),
                      pad_lanes(_gate_order(ml_gate_b[l]).reshape(1, 4 * ML_H), 128),
                      padh(ml_norm_g[l]).reshape(1, ML_WP), padh(ml_skip[l]).reshape(1, ML_WP), tril)

        na_o = _na_attn(pair_idx, pna, _na_pair_tiles(na_rpb[l]), jnp.tile(na_q_g[l], NA_H).reshape(1, NA_W),
                        jnp.tile(na_k_g[l], NA_H).reshape(1, NA_W), seg)

        wo = w_out[l]
        wa = wo[:MLA_H * V_D].astype(BF16)
        wm = jnp.pad(wo[MLA_H * V_D:MLA_H * V_D + ML_W].reshape(ML_H, ML_D, D),
                     ((0, 0), (0, HP - ML_D), (0, 0))).reshape(ML_WP, D).astype(BF16)
        wn = wo[MLA_H * V_D + ML_W:].astype(BF16)
        z1, xg = _out_proj(z, mods, mla_o, ml_o, na_o, wa, wm, wn, norm2_g[l].reshape(1, D), rw, rb)
        z = _moe(z1, xg, mods, moe_w1, moe_w3, moe_w2, l)

    return z
```

```python
import functools

import numpy as np
import jax
import jax.numpy as jnp
from jax import lax
from jax.experimental import pallas as pl
from jax.experimental.pallas import tpu as pltpu

F32 = jnp.float32
BF16 = jnp.bfloat16

D = 1024
SEQ = 2048
CTX = 256
S = SEQ + CTX
DEPTH = 4
GRID_W = 64
ROWS = SEQ // GRID_W
EPS = 1e-6

MLA_H = 6
Q_RANK = 256
KV_RANK = 128
NOPE_D = 64
ROPE_D = 32
V_D = 64
QK_D = NOPE_D + ROPE_D
ROPE_BASE = 10000.0

ML_H = 4
ML_D = 96
ML_W = ML_H * ML_D
HP = 128
ML_WP = ML_H * HP
ML_CHUNK = 128
N_CHUNK = S // ML_CHUNK
N_CTX_CHUNK = CTX // ML_CHUNK

NA_H = 4
NA_D = 64
NA_W = NA_H * NA_D
WIN_R = 8
WIN_C = 16
NA_QROWS = 4
NA_KROWS = 12
NA_QB = NA_QROWS * GRID_W
NA_KB = NA_KROWS * GRID_W
NA_NBLK = ROWS // NA_QROWS

N_EXPERTS = 16
N_GROUPS = 4
EPG = N_EXPERTS // N_GROUPS
D_FF = 256

TT = 256
NT = S // TT
OT = 768
TOP_K = 2
SORT_BLK = 8
SORT_ROWS = 320
GROUP_T = 512
COMBINE_T = 768
COMBINE_LAST_T = 512
NEG = -1e30

C_QC = 0
C_CKV = 256
C_KR = 384
C_U = 512
C_Z = C_U + ML_WP
C_G = C_Z + ML_WP
C_NA = C_G + 128
NP_IN = C_NA + 3 * NA_W

VMEM_LIMIT = 56 * 1024 * 1024


def _cparams(sem):
    return pltpu.CompilerParams(dimension_semantics=sem, vmem_limit_bytes=VMEM_LIMIT)


def _sigmoid(x):
    return 1.0 / (1.0 + jnp.exp(-x))


def _silu(x):
    return x * _sigmoid(x)


def _dot(a, b):
    return jnp.dot(a, b, preferred_element_type=F32)


def _dot_nt(a, b):
    return lax.dot_general(a, b, (((1,), (1,)), ((), ())), preferred_element_type=F32)


def _dot_tn(a, b):
    return lax.dot_general(a, b, (((0,), (0,)), ((), ())), preferred_element_type=F32)


def _dot_hi(a, b):
    return jnp.dot(a, b, preferred_element_type=F32, precision=lax.Precision.HIGHEST)


def _split_bf16(x, n):
    parts = []
    for _ in range(n):
        p = x.astype(BF16)
        parts.append(p)
        x = x - p.astype(F32)
    return parts


def _mod_rows(mod_ref, t):
    m = mod_ref[0]
    return [m[:, i * D:(i + 1) * D] for i in range(6)]


def _mod_kernel(c_ref, w_ref, b_ref, o_ref):
    sc = _silu(c_ref[...])
    o_ref[0] = _dot_hi(sc, w_ref[0]) + b_ref[0]


def _modulation(cc, w_mod, b_mod):
    nc = 6
    return pl.pallas_call(
        _mod_kernel,
        grid=(DEPTH, nc),
        in_specs=[pl.BlockSpec((16, D), lambda l, j: (0, 0)),
                  pl.BlockSpec((1, D, D), lambda l, j: (l, 0, j)),
                  pl.BlockSpec((1, 1, D), lambda l, j: (l, 0, j))],
        out_specs=pl.BlockSpec((1, 16, D), lambda l, j: (l, 0, j)),
        out_shape=jax.ShapeDtypeStruct((DEPTH, 16, 6 * D), F32),
        compiler_params=_cparams(("parallel", "parallel")),
        name="modulation",
    )(cc, w_mod, b_mod.reshape(DEPTH, 1, 6 * D))


def _in_proj_kernel(z_ref, mod_ref, g_ref, w_ref, pmla_ref, pu_ref, pg_ref, pna_ref):
    sh1, sc1 = _mod_rows(mod_ref, None)[:2]
    x = z_ref[0]
    xn = x * lax.rsqrt(jnp.mean(x * x, axis=-1, keepdims=True) + EPS) * g_ref[...]
    xn = xn * (1.0 + sc1) + sh1
    p = _dot(xn.astype(BF16), w_ref[...])
    pmla_ref[0] = p[:, :C_U].astype(BF16)
    pu_ref[0] = p[:, C_U:C_G].astype(BF16)
    pg_ref[0] = p[:, C_G:C_NA]
    pna_ref[0] = p[:, C_NA:].astype(BF16)


def _mod_spec():
    return pl.BlockSpec((1, 1, 6 * D), lambda b, t: (2 * b + t // (NT - 1), 0, 0))


def _in_proj(z, mods, g, w):
    B = z.shape[0]
    tok = lambda w_: pl.BlockSpec((1, TT, w_), lambda b, t: (b, t, 0))
    return pl.pallas_call(
        _in_proj_kernel,
        grid=(B, NT),
        in_specs=[tok(D), _mod_spec(),
                  pl.BlockSpec((1, D), lambda b, t: (0, 0)),
                  pl.BlockSpec((D, NP_IN), lambda b, t: (0, 0))],
        out_specs=[tok(C_U), tok(2 * ML_WP), tok(128), tok(3 * NA_W)],
        out_shape=[jax.ShapeDtypeStruct((B, S, C_U), BF16),
                   jax.ShapeDtypeStruct((B, S, 2 * ML_WP), BF16),
                   jax.ShapeDtypeStruct((B, S, 128), F32),
                   jax.ShapeDtypeStruct((B, S, 3 * NA_W), BF16)],
        compiler_params=_cparams(("parallel", "parallel")),
        name="in_proj",
    )(z, mods, g, w)


def _mla_prep_kernel(p_ref, tab_ref, qng_ref, wuq_ref, kvng_ref, wuk_ref, wuv_ref, rot_ref, ones_ref,
                     q_out, k_out, v_out):
    p = p_ref[0].astype(F32)
    qc = p[:, C_QC:C_CKV]
    ckv = p[:, C_CKV:C_KR]
    kr = p[:, C_KR:C_U]
    qcn = (qc * lax.rsqrt(jnp.mean(qc * qc, axis=-1, keepdims=True) + EPS) * qng_ref[...]).astype(BF16)
    ckvn = (ckv * lax.rsqrt(jnp.mean(ckv * ckv, axis=-1, keepdims=True) + EPS) * kvng_ref[...]).astype(BF16)
    lane = lax.broadcasted_iota(jnp.int32, (TT, HP), 1)
    ones = ones_ref[...]
    kr_rot = _dot(kr.astype(BF16), rot_ref[...])

    q_all = _dot(qcn, wuq_ref[...])
    k_all = _dot(ckvn, wuk_ref[...])
    v_all = _dot(ckvn, wuv_ref[...])
    qs = [q_all[:, 2 * h * HP:(2 * h + 1) * HP] for h in range(MLA_H)]
    q_rots = [q_all[:, (2 * h + 1) * HP:(2 * h + 2) * HP] for h in range(MLA_H)]
    ks = [k_all[:, h * HP:(h + 1) * HP] + kr for h in range(MLA_H)]
    ss_q = [_dot((x * x).astype(BF16), ones) for x in qs]
    ss_k = [_dot((x * x).astype(BF16), ones) for x in ks]

    def norm_rope(x, x_rot, ss, cos_g, sin_g):
        return lax.rsqrt(ss * (1.0 / QK_D) + EPS) * (x * cos_g + x_rot * sin_g)

    for h in range(MLA_H):
        q_out[0, h] = norm_rope(qs[h], q_rots[h], ss_q[h], tab_ref[0], tab_ref[1]).astype(BF16)
        k_out[0, h] = norm_rope(ks[h], kr_rot, ss_k[h], tab_ref[2], tab_ref[3]).astype(BF16)
        v_out[0, h] = jnp.where(lane < V_D, v_all[:, h * HP:(h + 1) * HP], 1.0).astype(BF16)


def _mla_prep(pmla, tabs, qng, wuq, kvng, wuk, wuv, rot, ones):
    B = pmla.shape[0]
    full = lambda a: pl.BlockSpec(a.shape, lambda b, t, _n=a.ndim: (0,) * _n)
    hd = lambda w_: pl.BlockSpec((1, MLA_H, TT, w_), lambda b, t: (b, 0, t, 0))
    return pl.pallas_call(
        _mla_prep_kernel,
        grid=(B, NT),
        in_specs=[pl.BlockSpec((1, TT, C_U), lambda b, t: (b, t, 0)),
                  pl.BlockSpec((4, TT, HP), lambda b, t: (0, t, 0)),
                  full(qng), full(wuq), full(kvng), full(wuk), full(wuv), full(rot), full(ones)],
        out_specs=[hd(HP), hd(HP), hd(HP)],
        out_shape=[jax.ShapeDtypeStruct((B, MLA_H, S, HP), BF16),
                   jax.ShapeDtypeStruct((B, MLA_H, S, HP), BF16),
                   jax.ShapeDtypeStruct((B, MLA_H, S, HP), BF16)],
        compiler_params=_cparams(("parallel", "parallel")),
        name="mla_prep",
    )(pmla, tabs, qng, wuq, kvng, wuk, wuv, rot, ones)


def _mla_attn_kernel(q_ref, k_ref, v_ref, o_ref):
    t = pl.program_id(1)

    def attend(k_lo, k_n):
        def scores(h):
            return _dot_nt(q_ref[0, h], k_ref[0, h, k_lo:k_lo + k_n, :])

        outs = []
        s_next = scores(0)
        for h in range(MLA_H):
            s = s_next
            if h + 1 < MLA_H:
                s_next = scores(h + 1)
            m = jnp.max(s, axis=-1, keepdims=True)
            p = jnp.exp2(s - m)
            pv = _dot(p.astype(BF16), v_ref[0, h, k_lo:k_lo + k_n, :])
            outs.append(pv[:, :V_D] / pv[:, V_D:V_D + 1])
        o_ref[0] = jnp.concatenate(outs, axis=-1).astype(BF16)

    @pl.when(t < NT - 1)
    def _():
        attend(0, S)

    @pl.when(t == NT - 1)
    def _():
        attend(SEQ, CTX)


def _mla_attn(q, k, v):
    B = q.shape[0]
    return pl.pallas_call(
        _mla_attn_kernel,
        grid=(B, NT),
        in_specs=[pl.BlockSpec((1, MLA_H, TT, HP), lambda b, t: (b, 0, t, 0)),
                  pl.BlockSpec((1, MLA_H, S, HP), lambda b, t: (b, 0, 0, 0)),
                  pl.BlockSpec((1, MLA_H, S, HP), lambda b, t: (b, 0, 0, 0))],
        out_specs=pl.BlockSpec((1, TT, MLA_H * V_D), lambda b, t: (b, t, 0)),
        out_shape=jax.ShapeDtypeStruct((B, S, MLA_H * V_D), BF16),
        compiler_params=_cparams(("parallel", "arbitrary")),
        name="mla_attn",
    )(q, k, v)


def _log_sigmoid(x):
    return jnp.minimum(x, 0.0) - jnp.log(1.0 + jnp.exp(-jnp.abs(x)))


def _mlstm_kernel(pu_ref, pg_ref, cw_ref, cb_ref, wq_ref, wk_ref, wv_ref, gb_ref, ng_ref, sk_ref,
                  tril_ref, o_ref, uc_s, q_s, kt_s, v_s, h_s, c_s, m_s, pm_s, b_s, rt_s):
    CA = 2 * ML_CHUNK
    row = lax.broadcasted_iota(jnp.int32, (CA, ML_WP), 0)

    def conv_body(i, carry):
        r0 = pl.multiple_of(i * CA, CA)
        x = pu_ref[0, pl.ds(r0, CA), 0:ML_WP].astype(F32)
        pr = pl.multiple_of(jnp.maximum(r0 - 16, 0), 16)
        nx = pl.multiple_of(jnp.minimum(r0 + CA, S - 16), 16)
        prev = pu_ref[0, pl.ds(pr, 16), 0:ML_WP].astype(F32)[15:16, :]
        nxt = pu_ref[0, pl.ds(nx, 16), 0:ML_WP].astype(F32)[0:1, :]
        seq_start = jnp.logical_or(r0 == 0, r0 == SEQ)
        seq_end = jnp.logical_or(r0 + CA == SEQ, r0 + CA == S)
        prev = jnp.where(seq_start, 0.0, prev)
        nxt = jnp.where(seq_end, 0.0, nxt)
        up = jnp.where(row == 0, prev, pltpu.roll(x, 1, 0))
        dn = jnp.where(row == CA - 1, nxt, pltpu.roll(x, CA - 1, 0))
        uc = _silu(cw_ref[0:1, :] * up + cw_ref[1:2, :] * x + cw_ref[2:3, :] * dn + cb_ref[...])
        ucb = uc.astype(BF16)
        uc_s[pl.ds(r0, CA), :] = ucb
        xb = x.astype(BF16)
        for h in range(ML_H):
            sl = slice(h * HP, (h + 1) * HP)
            q_s[pl.ds(r0, CA), sl] = _dot(ucb[:, sl], wq_ref[h]).astype(BF16)
            kt = _dot_nt(wk_ref[h], ucb[:, sl])
            kt_s[2 * i, sl, :] = kt[:, :ML_CHUNK].astype(BF16)
            kt_s[2 * i + 1, sl, :] = kt[:, ML_CHUNK:].astype(BF16)
            v_s[pl.ds(r0, CA), 2 * h * HP:(2 * h + 1) * HP] = _dot(xb[:, sl], wv_ref[h]).astype(BF16)
            v_s[pl.ds(r0, CA), (2 * h + 1) * HP:(2 * h + 2) * HP] = jnp.ones((CA, HP), BF16)

        for half in range(2):
            rows = pl.ds(r0 + half * ML_CHUNK, ML_CHUNK)
            g = pg_ref[0, rows, :] + gb_ref[...]
            parts = _split_bf16(_log_sigmoid(g), 3)
            cum_f = sum(_dot(tril_ref[0], part) for part in parts)
            cum_b = sum(_dot(tril_ref[1], part) for part in parts)
            bsh = pltpu.roll(jnp.where(bwd_lane, cum_b, cum_f), 128 - n_chain, 1)
            r = g - bsh
            pf = r
            pb = r
            k = 1
            while k < ML_CHUNK:
                pf = jnp.maximum(pf, jnp.where(ti >= k, pltpu.roll(pf, k, 0), NEG))
                pb = jnp.maximum(pb, jnp.where(ti < ML_CHUNK - k, pltpu.roll(pb, ML_CHUNK - k, 0), NEG))
                k *= 2
            pm_s[2 * i + half] = jnp.where(bwd_lane, pb, pf)
            b_s[2 * i + half] = bsh
            rt_s[2 * i + half] = r.T[0:n_chain, :]
        return carry

    n_chain = 2 * ML_H
    ti = lax.broadcasted_iota(jnp.int32, (ML_CHUNK, ML_CHUNK), 0)
    si = lax.broadcasted_iota(jnp.int32, (ML_CHUNK, ML_CHUNK), 1)
    bwd_lane = (si % n_chain) >= ML_H
    lax.fori_loop(0, S // CA, conv_body, 0)

    c_s[...] = jnp.zeros_like(c_s)
    m_s[...] = jnp.zeros_like(m_s)
    masks = (si <= ti, si >= ti)

    def scan_body(j, carry):
        chunk = (jnp.where(j < N_CTX_CHUNK, j + N_CHUNK - N_CTX_CHUNK, j - N_CTX_CHUNK), N_CHUNK - 1 - j)
        chains = []
        for d in range(2):
            r0 = pl.multiple_of(chunk[d] * ML_CHUNK, ML_CHUNK)
            p_col = pm_s[chunk[d]]
            bsh = b_s[chunk[d]]
            r_t = rt_s[chunk[d]]
            end = ML_CHUNK - 1 if d == 0 else 0
            for h in range(ML_H):
                c = d * ML_H + h
                sl = slice(h * HP, (h + 1) * HP)
                qc = q_s[pl.ds(r0, ML_CHUNK), sl]
                kt = kt_s[chunk[d], sl, :]
                vx = v_s[pl.ds(r0, ML_CHUNK), 2 * h * HP:(2 * h + 2) * HP]
                r_row = r_t[c:c + 1, :]
                m = m_s[c]
                st = c_s[c]
                big_m = jnp.maximum(m, jnp.broadcast_to(p_col[:, c:c + 1], (ML_CHUNK, HP)))
                b_b = jnp.broadcast_to(bsh[:, c:c + 1], (ML_CHUNK, HP))
                m_end = big_m[end:end + 1, :]
                ktw = (kt.astype(F32) * jnp.exp(r_row - m_end)).astype(BF16)
                chains.append(dict(d=d, r0=r0, sl=sl, c=c, vx=vx, r_row=r_row, m=m, st=st, big_m=big_m,
                                   b_b=b_b, m_end=m_end, end=end,
                                   qk=_dot(qc, kt), inter=_dot(qc, st.astype(BF16)), upd=_dot(ktw, vx)))
        for ch in chains:
            dw = jnp.exp(jnp.where(masks[ch["d"]], ch["r_row"] - ch["big_m"], NEG))
            ch["intra"] = _dot((ch["qk"] * dw).astype(BF16), ch["vx"])
        for ch in chains:
            m, big_m, inter, intra, end = ch["m"], ch["big_m"], ch["inter"], ch["intra"], ch["end"]
            iw = jnp.exp(m - big_m)
            num = iw * inter[:, :HP] + intra[:, :HP]
            nq = iw * inter[:, HP:] + intra[:, HP:]
            hv = num / jnp.maximum(jnp.abs(nq), jnp.exp(-(ch["b_b"] + big_m)))
            a = jnp.exp(m - ch["m_end"])
            ch["out"] = (hv, jnp.concatenate([a, a], axis=1) * ch["st"] + ch["upd"],
                         ch["b_b"][end:end + 1, :] + ch["m_end"])
        for ch in chains:
            hv, st_new, m_new = ch["out"]
            h_s[ch["d"], pl.ds(ch["r0"], ML_CHUNK), ch["sl"]] = hv
            c_s[ch["c"]] = st_new
            m_s[ch["c"]] = m_new
        return carry

    lax.fori_loop(0, N_CHUNK, scan_body, 0)

    live = (lax.broadcasted_iota(jnp.int32, (CA, HP), 1) < ML_D).astype(F32)

    def out_body(i, carry):
        r0 = pl.multiple_of(i * CA, CA)
        for h in range(ML_H):
            sl = slice(h * HP, (h + 1) * HP)
            hh = h_s[0, pl.ds(r0, CA), sl] + h_s[1, pl.ds(r0, CA), sl]
            mu = jnp.sum(hh, axis=-1, keepdims=True) * (1.0 / ML_D)
            dv = (hh - mu) * live
            var = jnp.sum(dv * dv, axis=-1, keepdims=True) * (1.0 / ML_D)
            hn = dv * lax.rsqrt(var + EPS) * ng_ref[:, sl]
            uc = uc_s[pl.ds(r0, CA), sl].astype(F32)
            zz = pu_ref[0, pl.ds(r0, CA), ML_WP + h * HP:ML_WP + (h + 1) * HP].astype(F32)
            o_ref[0, pl.ds(r0, CA), sl] = ((hn + sk_ref[:, sl] * uc) * _silu(zz)).astype(BF16)
        return carry

    lax.fori_loop(0, S // CA, out_body, 0)


def _mlstm(pu, pg, cw, cb, wq, wk, wv, gb, ng, sk, tril):
    B = pu.shape[0]
    full = lambda a: pl.BlockSpec(a.shape, lambda b, _n=a.ndim: (0,) * _n)
    n_chain = 2 * ML_H
    return pl.pallas_call(
        _mlstm_kernel,
        grid=(B,),
        in_specs=[pl.BlockSpec((1, S, 2 * ML_WP), lambda b: (b, 0, 0)),
                  pl.BlockSpec((1, S, 128), lambda b: (b, 0, 0)),
                  full(cw), full(cb), full(wq), full(wk), full(wv), full(gb), full(ng), full(sk), full(tril)],
        out_specs=pl.BlockSpec((1, S, ML_WP), lambda b: (b, 0, 0)),
        out_shape=jax.ShapeDtypeStruct((B, S, ML_WP), BF16),
        scratch_shapes=[pltpu.VMEM((S, ML_WP), BF16), pltpu.VMEM((S, ML_WP), BF16),
                        pltpu.VMEM((N_CHUNK, ML_WP, ML_CHUNK), BF16), pltpu.VMEM((S, 2 * ML_WP), BF16),
                        pltpu.VMEM((2, S, ML_WP), F32),
                        pltpu.VMEM((n_chain, HP, 2 * HP), F32),
                        pltpu.VMEM((n_chain, 1, HP), F32),
                        pltpu.VMEM((N_CHUNK, ML_CHUNK, 128), F32),
                        pltpu.VMEM((N_CHUNK, ML_CHUNK, 128), F32),
                        pltpu.VMEM((N_CHUNK, n_chain, ML_CHUNK), F32)],
        compiler_params=_cparams(("parallel",)),
        name="mlstm",
    )(pu, pg, cw, cb, wq, wk, wv, gb, ng, sk, tril)


def _na_kernel(idx_ref, p_ref, pt_ref, qg_ref, kg_ref, seg_ref, o_ref, kn_s, bias_s):
    j = pl.program_id(1)
    seg = seg_ref[...]

    def headnorm(x, g):
        ss = _dot((x * x).astype(BF16), seg)
        return x * lax.rsqrt(ss * (1.0 / NA_D) + EPS) * g

    @pl.when(j == 0)
    def _():
        def body(i, carry):
            r0 = pl.multiple_of(i * TT, TT)
            kk = p_ref[0, pl.ds(r0, TT), NA_W:2 * NA_W].astype(F32)
            kn_s[pl.ds(r0, TT), :] = headnorm(kk, kg_ref[...]).astype(BF16)
            return carry
        lax.fori_loop(0, NT, body, 0)

    scale = float(NA_D ** -0.5 * np.log2(np.e))
    kctx = kn_s[SEQ:S, :]
    vctx = p_ref[0, SEQ:S, 2 * NA_W:3 * NA_W]

    @pl.when(j < NA_NBLK)
    def _():
        q0 = pl.multiple_of(j * NA_QB, NA_QB)
        k0 = pl.multiple_of(jnp.clip(j * NA_QROWS - WIN_R // 2, 0, ROWS - NA_KROWS) * GRID_W, 256)
        q = headnorm(p_ref[0, pl.ds(q0, NA_QB), 0:NA_W].astype(F32), qg_ref[...]) * scale
        kl = kn_s[pl.ds(k0, NA_KB), :]
        vl = p_ref[0, pl.ds(k0, NA_KB), 2 * NA_W:3 * NA_W]
        head = lax.broadcasted_iota(jnp.int32, (NA_QB, NA_W), 1) // NA_D
        acc = jnp.zeros((NA_QB, NA_W), F32)

        def scores(h):
            qm = jnp.where(head == h, q, 0.0).astype(BF16)
            for i in range(NA_QROWS):
                for p in range(NA_KROWS // 2):
                    code = idx_ref[(j * NA_QROWS + i) * (NA_KROWS // 2) + p]
                    bias_s[i * GRID_W:(i + 1) * GRID_W, p * 2 * GRID_W:(p + 1) * 2 * GRID_W] = pt_ref[h, code]
            return _dot_nt(qm, kl) + bias_s[...].astype(F32), _dot_nt(qm, kctx)

        s_next = scores(0)
        for h in range(NA_H):
            s1, s2 = s_next
            if h + 1 < NA_H:
                s_next = scores(h + 1)
            m = jnp.maximum(jnp.max(s1, axis=-1, keepdims=True), jnp.max(s2, axis=-1, keepdims=True))
            p1 = jnp.exp2(s1 - m)
            p2 = jnp.exp2(s2 - m)
            l = jnp.sum(p1, axis=-1, keepdims=True) + jnp.sum(p2, axis=-1, keepdims=True)
            o = (_dot(p1.astype(BF16), vl) + _dot(p2.astype(BF16), vctx)) / l
            acc = jnp.where(head == h, o, acc)
        o_ref[0, pl.ds(q0, NA_QB), :] = acc.astype(BF16)

    @pl.when(j == NA_NBLK)
    def _():
        q = headnorm(p_ref[0, SEQ:S, 0:NA_W].astype(F32), qg_ref[...]) * scale
        head = lax.broadcasted_iota(jnp.int32, (CTX, NA_W), 1) // NA_D
        acc = jnp.zeros((CTX, NA_W), F32)
        for h in range(NA_H):
            qm = jnp.where(head == h, q, 0.0).astype(BF16)
            s2 = _dot_nt(qm, kctx)
            m = jnp.max(s2, axis=-1, keepdims=True)
            p2 = jnp.exp2(s2 - m)
            l = jnp.sum(p2, axis=-1, keepdims=True)
            o = _dot(p2.astype(BF16), vctx) / l
            acc = jnp.where(head == h, o, acc)
        o_ref[0, SEQ:S, :] = acc.astype(BF16)


def _na_attn(pair_idx, pna, pair_tiles, qg, kg, seg):
    B = pna.shape[0]
    return pl.pallas_call(
        _na_kernel,
        grid_spec=pltpu.PrefetchScalarGridSpec(
            num_scalar_prefetch=1,
            grid=(B, NA_NBLK + 1),
            in_specs=[pl.BlockSpec((1, S, 3 * NA_W), lambda b, j, idx: (b, 0, 0)),
                      pl.BlockSpec(pair_tiles.shape, lambda b, j, idx: (0, 0, 0, 0)),
                      pl.BlockSpec((1, NA_W), lambda b, j, idx: (0, 0)),
                      pl.BlockSpec((1, NA_W), lambda b, j, idx: (0, 0)),
                      pl.BlockSpec((NA_W, NA_W), lambda b, j, idx: (0, 0))],
            out_specs=pl.BlockSpec((1, S, NA_W), lambda b, j, idx: (b, 0, 0)),
            scratch_shapes=[pltpu.VMEM((S, NA_W), BF16), pltpu.VMEM((NA_QB, NA_KB), BF16)]),
        out_shape=jax.ShapeDtypeStruct((B, S, NA_W), BF16),
        compiler_params=_cparams(("parallel", "arbitrary")),
        name="na_attn",
    )(pair_idx, pna, pair_tiles, qg, kg, seg)


def _out_proj_kernel(z_ref, modx_ref, mody_ref, a_ref, m_ref, n_ref, wa_ref, wm_ref, wn_ref, g2_ref, rw_ref, rb_ref,
                     before_ref, below_ref, z1_ref, hs_ref, gs_ref, tm_ref, sm_ref):
    lane = lax.broadcasted_iota(jnp.int32, (TT, 128), 1)
    live = lane < N_EXPERTS
    groups = [slice(i * TT, (i + 1) * TT) for i in range(OT // TT)]

    splits = []
    for rows in groups:
        is_ctx = lax.broadcasted_iota(jnp.int32, (TT, 1), 0) + (pl.program_id(1) * OT + rows.start) >= SEQ
        mod = lambda i: jnp.where(is_ctx, mody_ref[0][:, i * D:(i + 1) * D], modx_ref[0][:, i * D:(i + 1) * D])
        mix = (_dot(a_ref[0, rows, :], wa_ref[...]) + _dot(m_ref[0, rows, :], wm_ref[...])
               + _dot(n_ref[0, rows, :], wn_ref[...]))
        x = z_ref[0, rows, :] + mod(2) * mix
        z1_ref[0, rows, :] = x
        hn = x * lax.rsqrt(jnp.mean(x * x, axis=-1, keepdims=True) + EPS) * g2_ref[...]
        hn = hn * (1.0 + mod(4)) + mod(3)
        h_hi, h_lo = _split_bf16(hn, 2)
        splits.append((h_hi, h_lo))

    affs = [_sigmoid(_dot(h_hi, rw_ref[0]) + (_dot(h_hi, rw_ref[1]) + _dot(h_lo, rw_ref[0])))
            for h_hi, h_lo in splits]

    def cyc(x, k, width):
        fwd = pltpu.roll(x, 128 - k, 1)
        back = pltpu.roll(x, width - k, 1)
        return jnp.where((lane % width) + k < width, fwd, back)

    def rank(x, width, step):
        r = jnp.zeros((TT, 128), F32)
        for k in range(1, width // step):
            y = cyc(x, k * step, width)
            wrapped = (lane % width) + k * step >= width
            beats = jnp.logical_or(y > x, jnp.logical_and(y == x, wrapped))
            r = r + beats.astype(F32)
        return r

    routed = []
    for aff in affs:
        sel = aff + rb_ref[...]
        top2 = rank(sel, EPG, 1) < 2.0
        part = jnp.where(top2, sel, 0.0)
        gscore = part
        for k in range(1, EPG):
            gscore = gscore + cyc(part, k, EPG)
        best = rank(gscore, N_EXPERTS, EPG) < 1.0
        chosen = jnp.logical_and(jnp.logical_and(top2, best), live)
        w = jnp.where(chosen, aff, 0.0)
        group = jnp.sum(jnp.where(chosen, (lane // EPG).astype(F32), 0.0), axis=-1, keepdims=True) * (1.0 / TOP_K)
        routed.append((w / jnp.sum(w, axis=-1, keepdims=True), group))

    sub = lax.broadcasted_iota(jnp.int32, (128, TT), 0)
    row_id = lax.broadcasted_iota(jnp.int32, (SORT_ROWS, TT), 0).astype(F32)
    for gi, ((gate, group), (h_hi, _)) in enumerate(zip(routed, splits)):
        member_t = (lane.astype(F32) == group).astype(F32).T
        member_t = jnp.where(sub < N_GROUPS, member_t, 0.0)
        ahead = _dot(member_t.astype(BF16), before_ref[...])
        count = jnp.sum(member_t, axis=1, keepdims=True)
        padded = jnp.floor((count + (SORT_BLK - 1)) * (1.0 / SORT_BLK)) * SORT_BLK
        start = _dot(below_ref[...], jnp.broadcast_to(padded, (128, 128)).astype(BF16))
        where_t = jnp.sum(member_t * (start[:, 0:1] + ahead), axis=0, keepdims=True)
        perm = (row_id == where_t).astype(BF16)
        hs_ref[0, gi] = _dot(perm, h_hi)
        gs_ref[0, gi] = sum(_dot(perm, part) for part in _split_bf16(gate, 3))
        group_t = jnp.sum(member_t * sub.astype(F32), axis=0, keepdims=True)
        tm_ref[0, gi, 0:1, :] = group_t
        tm_ref[0, gi, 1:2, :] = where_t
        tm_ref[0, gi, 2:8, :] = jnp.zeros((6, TT), F32)
        lane_m = lax.broadcasted_iota(jnp.int32, (128, 128), 1)
        sm_ref[0, gi] = jnp.where(lane_m == 0, jnp.broadcast_to(padded, (128, 128)), start)


def _out_proj(z, mods, mla_o, ml_o, na_o, wa, wm, wn, g2, rw, rb):
    B = z.shape[0]
    tok = lambda w_: pl.BlockSpec((1, OT, w_), lambda b, t: (b, t, 0))
    full = lambda a: pl.BlockSpec(a.shape, lambda b, t, _n=a.ndim: (0,) * _n)
    per_group = lambda r, w_: pl.BlockSpec((1, OT // TT, r, w_), lambda b, t: (b, t, 0, 0))
    grouped = lambda r, w_: jax.ShapeDtypeStruct((B, NT, r, w_), F32)
    before = jnp.asarray(np.triu(np.ones((TT, TT)), 1), BF16)
    below = jnp.asarray(np.tril(np.ones((128, 128)), -1), BF16)
    return pl.pallas_call(
        _out_proj_kernel,
        grid=(B, S // OT),
        in_specs=[tok(D),
                  pl.BlockSpec((1, 1, 6 * D), lambda b, t: (2 * b, 0, 0)),
                  pl.BlockSpec((1, 1, 6 * D), lambda b, t: (2 * b + 1, 0, 0)),
                  tok(MLA_H * V_D), tok(ML_WP), tok(NA_W),
                  full(wa), full(wm), full(wn), full(g2), full(rw), full(rb), full(before), full(below)],
        out_specs=[tok(D), per_group(SORT_ROWS, D), per_group(SORT_ROWS, 128), per_group(8, TT), per_group(128, 128)],
        out_shape=[jax.ShapeDtypeStruct((B, S, D), F32),
                   grouped(SORT_ROWS, D), grouped(SORT_ROWS, 128), grouped(8, TT), grouped(128, 128)],
        compiler_params=_cparams(("parallel", "parallel")),
        name="out_proj",
    )(z, mods, mods, mla_o, ml_o, na_o, wa, wm, wn, g2, rw, rb, before, below)


ROW_WAIT = 128


def _row_copy_start(n, row_copy):
    def issue(i, carry):
        row_copy(i).start()
        return carry

    lax.fori_loop(0, n, issue, 0, unroll=8)


def _row_copy_wait(n, slab_copy):
    def drain(i, carry):
        slab_copy(ROW_WAIT).wait()
        return carry

    lax.fori_loop(0, n // ROW_WAIT, drain, 0)


def _experts_kernel(blk_ref, tg_ref, hs_ref, gs_ref, w1_ref, w3_ref, w2_ref, ys_ref, hbuf, gbuf, sems):
    i = pl.program_id(0)
    n_blk = GROUP_T // SORT_BLK

    def fetch(tile, slot):
        def src(q):
            return pl.ds(pl.multiple_of(blk_ref[tile * n_blk + q] * SORT_BLK, SORT_BLK), SORT_BLK)

        def dst(q):
            return pl.ds(pl.multiple_of(q * SORT_BLK, SORT_BLK), SORT_BLK)

        _row_copy_start(n_blk, lambda q: pltpu.make_async_copy(hs_ref.at[src(q)], hbuf.at[slot, dst(q)], sems.at[0, slot]))
        _row_copy_start(n_blk, lambda q: pltpu.make_async_copy(gs_ref.at[src(q)], gbuf.at[slot, dst(q)], sems.at[1, slot]))

    @pl.when(i == 0)
    def _():
        fetch(0, 0)

    slot = i % 2
    _row_copy_wait(GROUP_T, lambda k: pltpu.make_async_copy(
        hs_ref.at[pl.ds(0, k)], hbuf.at[slot, pl.ds(0, k)], sems.at[0, slot]))
    _row_copy_wait(GROUP_T, lambda k: pltpu.make_async_copy(
        gs_ref.at[pl.ds(0, k)], gbuf.at[slot, pl.ds(0, k)], sems.at[1, slot]))

    @pl.when(i + 1 < pl.num_programs(0))
    def _():
        fetch(i + 1, 1 - slot)

    g = tg_ref[i]
    x = hbuf[slot].astype(BF16)
    gates = gbuf[slot]
    lane = lax.broadcasted_iota(jnp.int32, (GROUP_T, 128), 1)
    ups = [(_dot(x, w1_ref[0, e].astype(BF16)), _dot(x, w3_ref[0, e].astype(BF16))) for e in range(EPG)]
    acc = jnp.zeros((GROUP_T, D), F32)
    for e in range(EPG):
        ge = jnp.sum(jnp.where(lane == g * EPG + e, gates, 0.0), axis=-1, keepdims=True)
        a, b = ups[e]
        acc = acc + _dot((_silu(a) * b * ge).astype(BF16), w2_ref[0, e].astype(BF16))
    ys_ref[...] = acc


def _experts(src_blk, tile_group, hs, gs, n_rows, w1, w3, w2, l):
    wspec = lambda k, n: pl.BlockSpec((1, EPG, k, n), lambda i, blk, tg: (l, tg[i], 0, 0))
    return pl.pallas_call(
        _experts_kernel,
        grid_spec=pltpu.PrefetchScalarGridSpec(
            num_scalar_prefetch=2,
            grid=(n_rows // GROUP_T,),
            in_specs=[pl.BlockSpec(memory_space=pl.ANY), pl.BlockSpec(memory_space=pl.ANY),
                      wspec(D, D_FF), wspec(D, D_FF), wspec(D_FF, D)],
            out_specs=pl.BlockSpec((GROUP_T, D), lambda i, blk, tg: (i, 0)),
            scratch_shapes=[pltpu.VMEM((2, GROUP_T, D), F32), pltpu.VMEM((2, GROUP_T, 128), F32),
                            pltpu.SemaphoreType.DMA((2, 2))]),
        out_shape=jax.ShapeDtypeStruct((n_rows, D), F32),
        compiler_params=_cparams(("arbitrary",)),
        name="moe_experts",
    )(src_blk, tile_group, hs, gs, w1, w3, w2)


def _combine_kernel(pos_ref, z1_ref, modx_ref, mody_ref, ys_ref, o_ref, buf, sem, *, tile):
    t = pl.program_id(1)
    base = pl.program_id(0) * S + t * tile
    _row_copy_start(tile, lambda i: pltpu.make_async_copy(
        ys_ref.at[pl.ds(pos_ref[base + i], 1)], buf.at[pl.ds(i, 1)], sem))
    _row_copy_wait(tile, lambda k: pltpu.make_async_copy(ys_ref.at[pl.ds(0, k)], buf.at[pl.ds(0, k)], sem))
    is_ctx = lax.broadcasted_iota(jnp.int32, (tile, 1), 0) + t * tile >= SEQ
    g2 = jnp.where(is_ctx, mody_ref[0][:, 5 * D:], modx_ref[0][:, 5 * D:])
    o_ref[0] = z1_ref[0] + g2 * buf[...]


def _combine(pos, z1, mods, ys, rows_out, tile):
    B = z1.shape[0]
    tok = pl.BlockSpec((1, tile, D), lambda b, t, pos_: (b, t, 0))
    return pl.pallas_call(
        functools.partial(_combine_kernel, tile=tile),
        grid_spec=pltpu.PrefetchScalarGridSpec(
            num_scalar_prefetch=1,
            grid=(B, rows_out // tile),
            in_specs=[tok,
                      pl.BlockSpec((1, 1, 6 * D), lambda b, t, pos_: (2 * b, 0, 0)),
                      pl.BlockSpec((1, 1, 6 * D), lambda b, t, pos_: (2 * b + 1, 0, 0)),
                      pl.BlockSpec(memory_space=pl.ANY)],
            out_specs=tok,
            scratch_shapes=[pltpu.VMEM((tile, D), F32), pltpu.SemaphoreType.DMA(())]),
        out_shape=jax.ShapeDtypeStruct((B, rows_out, D), F32),
        compiler_params=_cparams(("arbitrary", "arbitrary")),
        name="moe_combine",
    )(pos, z1, mods, mods, ys)


def _group_layout(tok_meta, seg_meta, n_rows):
    n_tg = tok_meta.shape[0]
    seg_len = seg_meta[:, :N_GROUPS, 0].astype(jnp.int32).T
    seg_local = seg_meta[:, :N_GROUPS, 1].astype(jnp.int32).T
    total = jnp.sum(seg_len, axis=1)
    padded = (total + GROUP_T - 1) // GROUP_T * GROUP_T
    group_end = jnp.cumsum(padded)
    seg_start = (group_end - padded)[:, None] + jnp.cumsum(seg_len, axis=1) - seg_len
    flat_start, flat_len, flat_local = seg_start.reshape(-1), seg_len.reshape(-1), seg_local.reshape(-1)

    row0 = jnp.arange(n_rows // SORT_BLK, dtype=jnp.int32) * SORT_BLK
    seg = jnp.clip(jnp.searchsorted(flat_start + flat_len, row0, side='right'), 0, flat_start.shape[0] - 1)
    filled = (row0 >= flat_start[seg]) & (row0 < flat_start[seg] + flat_len[seg])
    src_row = (seg % n_tg) * SORT_ROWS + flat_local[seg] + row0 - flat_start[seg]
    src_blk = jnp.where(filled, src_row // SORT_BLK, SORT_ROWS // SORT_BLK - 1)

    tile_group = jnp.sum(jnp.arange(n_rows // GROUP_T)[:, None] * GROUP_T >= group_end[None, :], axis=1)
    gid = tok_meta[:, 0, :].astype(jnp.int32)
    local = tok_meta[:, 1, :].astype(jnp.int32)
    tg = jnp.arange(n_tg)[:, None]
    pos = seg_start[gid, tg] + local - seg_local[gid, tg]
    return (src_blk.astype(jnp.int32), jnp.minimum(tile_group, N_GROUPS - 1).astype(jnp.int32),
            pos.reshape(-1).astype(jnp.int32))


def _moe(z1, hs, gs, tok_meta, seg_meta, mods, w1, w3, w2, l):
    B = z1.shape[0]
    n_tg = B * NT
    n_rows = (-(-(n_tg * (TT + N_GROUPS * (SORT_BLK - 1))) // GROUP_T) + N_GROUPS) * GROUP_T
    src_blk, tile_group, pos = _group_layout(tok_meta.reshape(n_tg, 8, TT), seg_meta.reshape(n_tg, 128, 128), n_rows)
    ys = _experts(src_blk, tile_group, hs.reshape(n_tg * SORT_ROWS, D), gs.reshape(n_tg * SORT_ROWS, 128),
                  n_rows, w1, w3, w2, l)
    if l == DEPTH - 1:
        return _combine(pos, z1, mods, ys, SEQ, COMBINE_LAST_T)
    return _combine(pos, z1, mods, ys, S, COMBINE_T)


def _in_proj_layout(w):
    cuts = np.cumsum([Q_RANK, KV_RANK, ROPE_D, ML_W, ML_W, 4 * ML_H])
    qc, ckv, kr, u, zz, g, na = jnp.split(w, [int(v) for v in cuts], axis=-1)
    zeros = lambda n: jnp.zeros((w.shape[0], n), w.dtype)
    out = jnp.concatenate([qc, ckv, zeros(NOPE_D), kr, zeros(HP - QK_D),
                           _pad_heads(u, ML_H, ML_D, HP), _pad_heads(zz, ML_H, ML_D, HP),
                           _gate_order(g), zeros(128 - 4 * ML_H), na], axis=-1)
    assert out.shape[-1] == NP_IN
    return out


def _gate_order(g):
    i_f, f_f, i_b, f_b = jnp.split(g, 4, axis=-1)
    return jnp.concatenate([i_f, i_b, f_f, f_b], axis=-1)


def _pad_heads(v, nh, d, dp):
    lead = v.shape[:-1]
    v = v.reshape(lead + (nh, d))
    v = jnp.pad(v, [(0, 0)] * len(lead) + [(0, 0), (0, dp - d)])
    return v.reshape(lead + (nh * dp,))


def _rope_tables():
    t = np.arange(SEQ)
    row = (t // GRID_W).astype(np.float32)
    col = (t % GRID_W).astype(np.float32)
    quarter = ROPE_D // 4
    inv = jnp.asarray(ROPE_BASE, F32) ** (-jnp.arange(quarter, dtype=F32) / quarter)
    ar = jnp.asarray(row)[:, None] * inv
    ac = jnp.asarray(col)[:, None] * inv
    ang = jnp.concatenate([ar, ar, ac, ac], axis=-1)
    cos = jnp.ones((S, HP), F32).at[:SEQ, NOPE_D:QK_D].set(jnp.cos(ang))
    sin = jnp.zeros((S, HP), F32).at[:SEQ, NOPE_D:QK_D].set(jnp.sin(ang))
    return cos, sin


def _rotate_half_index():
    q = ROPE_D // 4
    src = np.arange(QK_D)
    sign = np.zeros((QK_D,), np.float32)
    for blk in range(2):
        lo = NOPE_D + 2 * q * blk
        src[lo:lo + q] = np.arange(lo + q, lo + 2 * q)
        sign[lo:lo + q] = -1.0
        src[lo + q:lo + 2 * q] = np.arange(lo, lo + q)
        sign[lo + q:lo + 2 * q] = 1.0
    return src, sign


def _rotate_half(w):
    src, sign = _rotate_half_index()
    return w[..., src] * sign


NA_NDR = 2 * WIN_R - 1
NA_NPAIR = 3 * NA_NDR


def _na_pair_index():
    idx = np.zeros((NA_NBLK, NA_QROWS, NA_KROWS // 2), np.int32)
    for blk in range(NA_NBLK):
        k0 = int(np.clip(blk * NA_QROWS - WIN_R // 2, 0, ROWS - NA_KROWS))
        for i in range(NA_QROWS):
            qr = blk * NA_QROWS + i
            rs = int(np.clip(qr - WIN_R // 2, 0, ROWS - WIN_R))
            assert k0 <= rs and rs + WIN_R <= k0 + NA_KROWS
            for p in range(NA_KROWS // 2):
                kr = k0 + 2 * p
                dr = kr - qr + WIN_R - 1
                left = rs <= kr < rs + WIN_R
                right = rs <= kr + 1 < rs + WIN_R
                if left and right:
                    idx[blk, i, p] = 1 + dr
                elif left:
                    idx[blk, i, p] = NA_NDR + dr
                elif right:
                    idx[blk, i, p] = 2 * NA_NDR + dr + 1
    return idx.reshape(-1)


def _na_pair_tiles(rpb):
    cq = np.arange(GRID_W)
    cs = np.clip(cq - WIN_C // 2, 0, GRID_W - WIN_C)
    col_ok = (cq[None, :] >= cs[:, None]) & (cq[None, :] < cs[:, None] + WIN_C)
    dc = np.clip(cq[None, :] - cq[:, None], -(WIN_C - 1), WIN_C - 1) + (WIN_C - 1)
    onehot = jnp.asarray(np.eye(2 * WIN_C - 1, dtype=np.float32)[dc])
    tiles = jnp.einsum('hrc,qkc->hrqk', rpb, onehot, precision=lax.Precision.HIGHEST)
    tiles = jnp.where(jnp.asarray(col_ok), tiles * float(np.log2(np.e)), NEG)
    masked = jnp.full_like(tiles, NEG)
    both = jnp.concatenate([tiles[:, :-1], tiles[:, 1:]], axis=-1)
    left = jnp.concatenate([tiles, masked], axis=-1)
    right = jnp.concatenate([masked, tiles], axis=-1)
    none = jnp.concatenate([masked[:, :1], masked[:, :1]], axis=-1)
    out = jnp.concatenate([none, both, left, right], axis=1)
    assert out.shape[1] == NA_NPAIR
    return out.astype(BF16)


def kernel(x, c, ctx, c_ctx, w_mod, b_mod, norm1_g, norm2_g, w_in, w_out, mla_qnorm_g, mla_w_uq, mla_kvnorm_g, mla_w_ukv, mla_q_g, mla_k_g, ml_conv_w, ml_conv_b, ml_w_q, ml_w_k, ml_w_v, ml_gate_b, ml_norm_g, ml_skip, na_q_g, na_k_g, na_rpb, router_w, router_b, moe_w1, moe_w3, moe_w2):
    B = x.shape[0]
    z = jnp.concatenate([x, ctx], axis=1)
    cc = jnp.zeros((16, D), F32).at[:B].set(c).at[B].set(c_ctx)
    mod_all = _modulation(cc, w_mod, b_mod)
    cos, sin = _rope_tables()
    rot_src, rot_sign = _rotate_half_index()
    rot_np = np.zeros((HP, HP), np.float32)
    rot_np[rot_src, np.arange(QK_D)] = rot_sign
    rot_mat = jnp.asarray(rot_np, BF16)
    ones_hp = jnp.ones((HP, HP), BF16)
    seg = jnp.asarray(np.kron(np.eye(NA_H), np.ones((NA_D, NA_D))), BF16)
    tril = jnp.asarray(np.stack([np.tril(np.ones((ML_CHUNK, ML_CHUNK))), np.triu(np.ones((ML_CHUNK, ML_CHUNK)))]), BF16)
    pair_idx = jnp.asarray(_na_pair_index())
    rw = jnp.stack(_split_bf16(jnp.pad(router_w, ((0, 0), (0, 128 - N_EXPERTS))), 2))
    rb = jnp.pad(router_b, (0, 128 - N_EXPERTS), constant_values=NEG).reshape(1, 128)

    def pad_lanes(v, n):
        return jnp.pad(v, [(0, 0)] * (v.ndim - 1) + [(0, n - v.shape[-1])])

    for l in range(DEPTH):
        mx = mod_all[l, :B]
        my = jnp.broadcast_to(mod_all[l, B], (B, 6 * D))
        mods = jnp.stack([mx, my], axis=1).reshape(2 * B, 1, 6 * D)

        w_in_p = _in_proj_layout(w_in[l]).astype(BF16)
        pmla, pu, pg, pna = _in_proj(z, mods, norm1_g[l].reshape(1, D), w_in_p)

        wq = jnp.transpose(mla_w_uq[l].reshape(Q_RANK, MLA_H, QK_D), (1, 0, 2))
        wuq = jnp.concatenate([pad_lanes(wq, HP), pad_lanes(_rotate_half(wq), HP)], axis=-1).astype(BF16)
        wukv = jnp.transpose(mla_w_ukv[l].reshape(KV_RANK, MLA_H, NOPE_D + V_D), (1, 0, 2))
        wuk = pad_lanes(wukv[..., :NOPE_D], HP).astype(BF16)
        wuv = pad_lanes(wukv[..., NOPE_D:], HP).astype(BF16)
        qg, kg = mla_q_g[l], mla_k_g[l]
        q_scale = float(QK_D ** -0.5 * np.log2(np.e))
        tabs = jnp.stack([cos * pad_lanes(qg, HP) * q_scale, sin * pad_lanes(jnp.abs(rot_sign) * qg[rot_src], HP) * q_scale,
                          cos * pad_lanes(kg, HP), sin * pad_lanes(jnp.abs(rot_sign) * kg[rot_src], HP)])
        heads_last = lambda w_: jnp.transpose(w_, (1, 0, 2)).reshape(w_.shape[1], -1)
        q, k, v = _mla_prep(pmla, tabs, mla_qnorm_g[l].reshape(1, Q_RANK), heads_last(wuq),
                            mla_kvnorm_g[l].reshape(1, KV_RANK), heads_last(wuk), heads_last(wuv), rot_mat, ones_hp)
        mla_o = _mla_attn(q, k, v)

        padh = lambda a: _pad_heads(a, ML_H, ML_D, HP)
        padw = lambda w_: jnp.pad(w_, ((0, 0), (0, HP - ML_D), (0, HP - ML_D))).astype(BF16)
        cw = jnp.pad(padh(ml_conv_w[l]), ((0, 8 - 3), (0, 0)))
        ml_o = _mlstm(pu, pg, cw, padh(ml_conv_b[l]).reshape(1, ML_WP),
                      padw(ml_w_q[l]), padw(jnp.swapaxes(ml_w_k[l], 1, 2) * (ML_D ** -0.5)), padw(ml_w_v[l]),
                      pad_lanes(_gate_order(ml_gate_b[l]).reshape(1, 4 * ML_H), 128),
                      padh(ml_norm_g[l]).reshape(1, ML_WP), padh(ml_skip[l]).reshape(1, ML_WP), tril)

        na_o = _na_attn(pair_idx, pna, _na_pair_tiles(na_rpb[l]), jnp.tile(na_q_g[l], NA_H).reshape(1, NA_W),
                        jnp.tile(na_k_g[l], NA_H).reshape(1, NA_W), seg)

        wo = w_out[l]
        wa = wo[:MLA_H * V_D].astype(BF16)
        wm = jnp.pad(wo[MLA_H * V_D:MLA_H * V_D + ML_W].reshape(ML_H, ML_D, D),
                     ((0, 0), (0, HP - ML_D), (0, 0))).reshape(ML_WP, D).astype(BF16)
        wn = wo[MLA_H * V_D + ML_W:].astype(BF16)
        z1, hs, gs, tok_meta, seg_meta = _out_proj(z, mods, mla_o, ml_o, na_o, wa, wm, wn,
                                                   norm2_g[l].reshape(1, D), rw, rb)
        z = _moe(z1, hs, gs, tok_meta, seg_meta, mods, moe_w1, moe_w3, moe_w2, l)

    return z
```

```python
import functools

import numpy as np
import jax
import jax.numpy as jnp
from jax import lax
from jax.experimental import pallas as pl
from jax.experimental.pallas import tpu as pltpu

F32 = jnp.float32
BF16 = jnp.bfloat16

D = 1024
SEQ = 2048
CTX = 256
S = SEQ + CTX
DEPTH = 4
GRID_W = 64
ROWS = SEQ // GRID_W
EPS = 1e-6

MLA_H = 6
Q_RANK = 256
KV_RANK = 128
NOPE_D = 64
ROPE_D = 32
V_D = 64
QK_D = NOPE_D + ROPE_D
ROPE_BASE = 10000.0

ML_H = 4
ML_D = 96
ML_W = ML_H * ML_D
HP = 128
ML_WP = ML_H * HP
ML_CHUNK = 128
N_CHUNK = S // ML_CHUNK
N_CTX_CHUNK = CTX // ML_CHUNK

NA_H = 4
NA_D = 64
NA_W = NA_H * NA_D
WIN_R = 8
WIN_C = 16
NA_QROWS = 4
NA_KROWS = 12
NA_QB = NA_QROWS * GRID_W
NA_KB = NA_KROWS * GRID_W
NA_NBLK = ROWS // NA_QROWS

N_EXPERTS = 16
N_GROUPS = 4
EPG = N_EXPERTS // N_GROUPS
D_FF = 256

TT = 256
NT = S // TT
OT = 768
TOP_K = 2
SORT_BLK = 8
SORT_ROWS = 320
GROUP_T = 512
COMBINE_T = 768
COMBINE_LAST_T = 512
NEG = -1e30

C_QC = 0
C_CKV = 256
C_KR = 384
C_U = 512
C_Z = C_U + ML_WP
C_G = C_Z + ML_WP
C_NA = C_G + 128
NP_IN = C_NA + 3 * NA_W

VMEM_LIMIT = 56 * 1024 * 1024


def _cparams(sem):
    return pltpu.CompilerParams(dimension_semantics=sem, vmem_limit_bytes=VMEM_LIMIT)


def _sigmoid(x):
    return 1.0 / (1.0 + jnp.exp(-x))


def _silu(x):
    return x * _sigmoid(x)


def _dot(a, b):
    return jnp.dot(a, b, preferred_element_type=F32)


def _dot_nt(a, b):
    return lax.dot_general(a, b, (((1,), (1,)), ((), ())), preferred_element_type=F32)


def _dot_tn(a, b):
    return lax.dot_general(a, b, (((0,), (0,)), ((), ())), preferred_element_type=F32)


def _dot_hi(a, b):
    return jnp.dot(a, b, preferred_element_type=F32, precision=lax.Precision.HIGHEST)


def _split_bf16(x, n):
    parts = []
    for _ in range(n):
        p = x.astype(BF16)
        parts.append(p)
        x = x - p.astype(F32)
    return parts


def _mod_rows(mod_ref, t):
    m = mod_ref[0]
    return [m[:, i * D:(i + 1) * D] for i in range(6)]


def _mod_kernel(c_ref, w_ref, b_ref, o_ref):
    sc = _silu(c_ref[...])
    o_ref[0] = _dot_hi(sc, w_ref[0]) + b_ref[0]


def _modulation(cc, w_mod, b_mod):
    nc = 6
    return pl.pallas_call(
        _mod_kernel,
        grid=(DEPTH, nc),
        in_specs=[pl.BlockSpec((16, D), lambda l, j: (0, 0)),
                  pl.BlockSpec((1, D, D), lambda l, j: (l, 0, j)),
                  pl.BlockSpec((1, 1, D), lambda l, j: (l, 0, j))],
        out_specs=pl.BlockSpec((1, 16, D), lambda l, j: (l, 0, j)),
        out_shape=jax.ShapeDtypeStruct((DEPTH, 16, 6 * D), F32),
        compiler_params=_cparams(("parallel", "parallel")),
        name="modulation",
    )(cc, w_mod, b_mod.reshape(DEPTH, 1, 6 * D))


def _in_proj_kernel(z_ref, mod_ref, g_ref, w_ref, pmla_ref, pu_ref, pg_ref, pna_ref):
    sh1, sc1 = _mod_rows(mod_ref, None)[:2]
    x = z_ref[0]
    xn = x * lax.rsqrt(jnp.mean(x * x, axis=-1, keepdims=True) + EPS) * g_ref[...]
    xn = xn * (1.0 + sc1) + sh1
    p = _dot(xn.astype(BF16), w_ref[...])
    pmla_ref[0] = p[:, :C_U].astype(BF16)
    pu_ref[0] = p[:, C_U:C_G].astype(BF16)
    pg_ref[0] = p[:, C_G:C_NA]
    pna_ref[0] = p[:, C_NA:].astype(BF16)


def _mod_spec():
    return pl.BlockSpec((1, 1, 6 * D), lambda b, t: (2 * b + t // (NT - 1), 0, 0))


def _in_proj(z, mods, g, w):
    B = z.shape[0]
    tok = lambda w_: pl.BlockSpec((1, TT, w_), lambda b, t: (b, t, 0))
    return pl.pallas_call(
        _in_proj_kernel,
        grid=(B, NT),
        in_specs=[tok(D), _mod_spec(),
                  pl.BlockSpec((1, D), lambda b, t: (0, 0)),
                  pl.BlockSpec((D, NP_IN), lambda b, t: (0, 0))],
        out_specs=[tok(C_U), tok(2 * ML_WP), tok(128), tok(3 * NA_W)],
        out_shape=[jax.ShapeDtypeStruct((B, S, C_U), BF16),
                   jax.ShapeDtypeStruct((B, S, 2 * ML_WP), BF16),
                   jax.ShapeDtypeStruct((B, S, 128), F32),
                   jax.ShapeDtypeStruct((B, S, 3 * NA_W), BF16)],
        compiler_params=_cparams(("parallel", "parallel")),
        name="in_proj",
    )(z, mods, g, w)


def _mla_prep_kernel(p_ref, tab_ref, qng_ref, wuq_ref, kvng_ref, wuk_ref, wuv_ref, rot_ref, ones_ref,
                     q_out, k_out, v_out):
    p = p_ref[0].astype(F32)
    qc = p[:, C_QC:C_CKV]
    ckv = p[:, C_CKV:C_KR]
    kr = p[:, C_KR:C_U]
    qcn = (qc * lax.rsqrt(jnp.mean(qc * qc, axis=-1, keepdims=True) + EPS) * qng_ref[...]).astype(BF16)
    ckvn = (ckv * lax.rsqrt(jnp.mean(ckv * ckv, axis=-1, keepdims=True) + EPS) * kvng_ref[...]).astype(BF16)
    lane = lax.broadcasted_iota(jnp.int32, (TT, HP), 1)
    ones = ones_ref[...]
    kr_rot = _dot(kr.astype(BF16), rot_ref[...])

    q_all = _dot(qcn, wuq_ref[...])
    k_all = _dot(ckvn, wuk_ref[...])
    v_all = _dot(ckvn, wuv_ref[...])
    qs = [q_all[:, 2 * h * HP:(2 * h + 1) * HP] for h in range(MLA_H)]
    q_rots = [q_all[:, (2 * h + 1) * HP:(2 * h + 2) * HP] for h in range(MLA_H)]
    ks = [k_all[:, h * HP:(h + 1) * HP] + kr for h in range(MLA_H)]
    ss_q = [_dot((x * x).astype(BF16), ones) for x in qs]
    ss_k = [_dot((x * x).astype(BF16), ones) for x in ks]

    def norm_rope(x, x_rot, ss, cos_g, sin_g):
        return lax.rsqrt(ss * (1.0 / QK_D) + EPS) * (x * cos_g + x_rot * sin_g)

    for h in range(MLA_H):
        q_out[0, h] = norm_rope(qs[h], q_rots[h], ss_q[h], tab_ref[0], tab_ref[1]).astype(BF16)
        k_out[0, h] = norm_rope(ks[h], kr_rot, ss_k[h], tab_ref[2], tab_ref[3]).astype(BF16)
        v_out[0, h] = jnp.where(lane < V_D, v_all[:, h * HP:(h + 1) * HP], 1.0).astype(BF16)


def _mla_prep(pmla, tabs, qng, wuq, kvng, wuk, wuv, rot, ones):
    B = pmla.shape[0]
    full = lambda a: pl.BlockSpec(a.shape, lambda b, t, _n=a.ndim: (0,) * _n)
    hd = lambda w_: pl.BlockSpec((1, MLA_H, TT, w_), lambda b, t: (b, 0, t, 0))
    return pl.pallas_call(
        _mla_prep_kernel,
        grid=(B, NT),
        in_specs=[pl.BlockSpec((1, TT, C_U), lambda b, t: (b, t, 0)),
                  pl.BlockSpec((4, TT, HP), lambda b, t: (0, t, 0)),
                  full(qng), full(wuq), full(kvng), full(wuk), full(wuv), full(rot), full(ones)],
        out_specs=[hd(HP), hd(HP), hd(HP)],
        out_shape=[jax.ShapeDtypeStruct((B, MLA_H, S, HP), BF16),
                   jax.ShapeDtypeStruct((B, MLA_H, S, HP), BF16),
                   jax.ShapeDtypeStruct((B, MLA_H, S, HP), BF16)],
        compiler_params=_cparams(("parallel", "parallel")),
        name="mla_prep",
    )(pmla, tabs, qng, wuq, kvng, wuk, wuv, rot, ones)


def _mla_attn_kernel(q_ref, k_ref, v_ref, o_ref):
    t = pl.program_id(1)

    def attend(k_lo, k_n):
        def scores(h):
            return _dot_nt(q_ref[0, h], k_ref[0, h, k_lo:k_lo + k_n, :])

        outs = []
        s_next = scores(0)
        for h in range(MLA_H):
            s = s_next
            if h + 1 < MLA_H:
                s_next = scores(h + 1)
            m = jnp.max(s, axis=-1, keepdims=True)
            p = jnp.exp2(s - m)
            pv = _dot(p.astype(BF16), v_ref[0, h, k_lo:k_lo + k_n, :])
            outs.append(pv[:, :V_D] / pv[:, V_D:V_D + 1])
        o_ref[0] = jnp.concatenate(outs, axis=-1).astype(BF16)

    @pl.when(t < NT - 1)
    def _():
        attend(0, S)

    @pl.when(t == NT - 1)
    def _():
        attend(SEQ, CTX)


def _mla_attn(q, k, v):
    B = q.shape[0]
    return pl.pallas_call(
        _mla_attn_kernel,
        grid=(B, NT),
        in_specs=[pl.BlockSpec((1, MLA_H, TT, HP), lambda b, t: (b, 0, t, 0)),
                  pl.BlockSpec((1, MLA_H, S, HP), lambda b, t: (b, 0, 0, 0)),
                  pl.BlockSpec((1, MLA_H, S, HP), lambda b, t: (b, 0, 0, 0))],
        out_specs=pl.BlockSpec((1, TT, MLA_H * V_D), lambda b, t: (b, t, 0)),
        out_shape=jax.ShapeDtypeStruct((B, S, MLA_H * V_D), BF16),
        compiler_params=_cparams(("parallel", "arbitrary")),
        name="mla_attn",
    )(q, k, v)


def _log_sigmoid(x):
    return jnp.minimum(x, 0.0) - jnp.log(1.0 + jnp.exp(-jnp.abs(x)))


def _mlstm_kernel(pu_ref, pg_ref, cw_ref, cb_ref, wq_ref, wk_ref, wv_ref, gb_ref, ng_ref, sk_ref,
                  tril_ref, o_ref, uc_s, q_s, kt_s, v_s, h_s, c_s, m_s, pm_s, b_s, rt_s):
    CA = 2 * ML_CHUNK
    row = lax.broadcasted_iota(jnp.int32, (CA, ML_WP), 0)

    def conv_body(i, carry):
        r0 = pl.multiple_of(i * CA, CA)
        x = pu_ref[0, pl.ds(r0, CA), 0:ML_WP].astype(F32)
        pr = pl.multiple_of(jnp.maximum(r0 - 16, 0), 16)
        nx = pl.multiple_of(jnp.minimum(r0 + CA, S - 16), 16)
        prev = pu_ref[0, pl.ds(pr, 16), 0:ML_WP].astype(F32)[15:16, :]
        nxt = pu_ref[0, pl.ds(nx, 16), 0:ML_WP].astype(F32)[0:1, :]
        seq_start = jnp.logical_or(r0 == 0, r0 == SEQ)
        seq_end = jnp.logical_or(r0 + CA == SEQ, r0 + CA == S)
        prev = jnp.where(seq_start, 0.0, prev)
        nxt = jnp.where(seq_end, 0.0, nxt)
        up = jnp.where(row == 0, prev, pltpu.roll(x, 1, 0))
        dn = jnp.where(row == CA - 1, nxt, pltpu.roll(x, CA - 1, 0))
        uc = _silu(cw_ref[0:1, :] * up + cw_ref[1:2, :] * x + cw_ref[2:3, :] * dn + cb_ref[...])
        ucb = uc.astype(BF16)
        uc_s[pl.ds(r0, CA), :] = ucb
        xb = x.astype(BF16)
        for h in range(ML_H):
            sl = slice(h * HP, (h + 1) * HP)
            q_s[pl.ds(r0, CA), sl] = _dot(ucb[:, sl], wq_ref[h]).astype(BF16)
            kt = _dot_nt(wk_ref[h], ucb[:, sl])
            kt_s[2 * i, sl, :] = kt[:, :ML_CHUNK].astype(BF16)
            kt_s[2 * i + 1, sl, :] = kt[:, ML_CHUNK:].astype(BF16)
            v_s[pl.ds(r0, CA), 2 * h * HP:(2 * h + 1) * HP] = _dot(xb[:, sl], wv_ref[h]).astype(BF16)
            v_s[pl.ds(r0, CA), (2 * h + 1) * HP:(2 * h + 2) * HP] = jnp.ones((CA, HP), BF16)

        for half in range(2):
            rows = pl.ds(r0 + half * ML_CHUNK, ML_CHUNK)
            g = pg_ref[0, rows, :] + gb_ref[...]
            parts = _split_bf16(_log_sigmoid(g), 3)
            cum_f = sum(_dot(tril_ref[0], part) for part in parts)
            cum_b = sum(_dot(tril_ref[1], part) for part in parts)
            bsh = pltpu.roll(jnp.where(bwd_lane, cum_b, cum_f), 128 - n_chain, 1)
            r = g - bsh
            pf = r
            pb = r
            k = 1
            while k < ML_CHUNK:
                pf = jnp.maximum(pf, jnp.where(ti >= k, pltpu.roll(pf, k, 0), NEG))
                pb = jnp.maximum(pb, jnp.where(ti < ML_CHUNK - k, pltpu.roll(pb, ML_CHUNK - k, 0), NEG))
                k *= 2
            pm_s[2 * i + half] = jnp.where(bwd_lane, pb, pf)
            b_s[2 * i + half] = bsh
            rt_s[2 * i + half] = r.T[0:n_chain, :]
        return carry

    n_chain = 2 * ML_H
    ti = lax.broadcasted_iota(jnp.int32, (ML_CHUNK, ML_CHUNK), 0)
    si = lax.broadcasted_iota(jnp.int32, (ML_CHUNK, ML_CHUNK), 1)
    bwd_lane = (si % n_chain) >= ML_H
    lax.fori_loop(0, S // CA, conv_body, 0)

    c_s[...] = jnp.zeros_like(c_s)
    m_s[...] = jnp.zeros_like(m_s)
    masks = (si <= ti, si >= ti)

    def scan_body(j, carry):
        chunk = (jnp.where(j < N_CTX_CHUNK, j + N_CHUNK - N_CTX_CHUNK, j - N_CTX_CHUNK), N_CHUNK - 1 - j)
        chains = []
        for d in range(2):
            r0 = pl.multiple_of(chunk[d] * ML_CHUNK, ML_CHUNK)
            p_col = pm_s[chunk[d]]
            bsh = b_s[chunk[d]]
            r_t = rt_s[chunk[d]]
            end = ML_CHUNK - 1 if d == 0 else 0
            for h in range(ML_H):
                c = d * ML_H + h
                sl = slice(h * HP, (h + 1) * HP)
                qc = q_s[pl.ds(r0, ML_CHUNK), sl]
                kt = kt_s[chunk[d], sl, :]
                vx = v_s[pl.ds(r0, ML_CHUNK), 2 * h * HP:(2 * h + 2) * HP]
                r_row = r_t[c:c + 1, :]
                m = m_s[c]
                st = c_s[c]
                big_m = jnp.maximum(m, jnp.broadcast_to(p_col[:, c:c + 1], (ML_CHUNK, HP)))
                b_b = jnp.broadcast_to(bsh[:, c:c + 1], (ML_CHUNK, HP))
                m_end = big_m[end:end + 1, :]
                ktw = (kt.astype(F32) * jnp.exp(r_row - m_end)).astype(BF16)
                chains.append(dict(d=d, r0=r0, sl=sl, c=c, vx=vx, r_row=r_row, m=m, st=st, big_m=big_m,
                                   b_b=b_b, m_end=m_end, end=end,
                                   qk=_dot(qc, kt), inter=_dot(qc, st.astype(BF16)), upd=_dot(ktw, vx)))
        for ch in chains:
            dw = jnp.exp(jnp.where(masks[ch["d"]], ch["r_row"] - ch["big_m"], NEG))
            ch["intra"] = _dot((ch["qk"] * dw).astype(BF16), ch["vx"])
        for ch in chains:
            m, big_m, inter, intra, end = ch["m"], ch["big_m"], ch["inter"], ch["intra"], ch["end"]
            iw = jnp.exp(m - big_m)
            num = iw * inter[:, :HP] + intra[:, :HP]
            nq = iw * inter[:, HP:] + intra[:, HP:]
            hv = num / jnp.maximum(jnp.abs(nq), jnp.exp(-(ch["b_b"] + big_m)))
            a = jnp.exp(m - ch["m_end"])
            ch["out"] = (hv, jnp.concatenate([a, a], axis=1) * ch["st"] + ch["upd"],
                         ch["b_b"][end:end + 1, :] + ch["m_end"])
        for ch in chains:
            hv, st_new, m_new = ch["out"]
            h_s[ch["d"], pl.ds(ch["r0"], ML_CHUNK), ch["sl"]] = hv
            c_s[ch["c"]] = st_new
            m_s[ch["c"]] = m_new
        return carry

    lax.fori_loop(0, N_CHUNK, scan_body, 0)

    live = (lax.broadcasted_iota(jnp.int32, (CA, HP), 1) < ML_D).astype(F32)

    def out_body(i, carry):
        r0 = pl.multiple_of(i * CA, CA)
        for h in range(ML_H):
            sl = slice(h * HP, (h + 1) * HP)
            hh = h_s[0, pl.ds(r0, CA), sl] + h_s[1, pl.ds(r0, CA), sl]
            mu = jnp.sum(hh, axis=-1, keepdims=True) * (1.0 / ML_D)
            dv = (hh - mu) * live
            var = jnp.sum(dv * dv, axis=-1, keepdims=True) * (1.0 / ML_D)
            hn = dv * lax.rsqrt(var + EPS) * ng_ref[:, sl]
            uc = uc_s[pl.ds(r0, CA), sl].astype(F32)
            zz = pu_ref[0, pl.ds(r0, CA), ML_WP + h * HP:ML_WP + (h + 1) * HP].astype(F32)
            o_ref[0, pl.ds(r0, CA), sl] = ((hn + sk_ref[:, sl] * uc) * _silu(zz)).astype(BF16)
        return carry

    lax.fori_loop(0, S // CA, out_body, 0)


def _mlstm(pu, pg, cw, cb, wq, wk, wv, gb, ng, sk, tril):
    B = pu.shape[0]
    full = lambda a: pl.BlockSpec(a.shape, lambda b, _n=a.ndim: (0,) * _n)
    n_chain = 2 * ML_H
    return pl.pallas_call(
        _mlstm_kernel,
        grid=(B,),
        in_specs=[pl.BlockSpec((1, S, 2 * ML_WP), lambda b: (b, 0, 0)),
                  pl.BlockSpec((1, S, 128), lambda b: (b, 0, 0)),
                  full(cw), full(cb), full(wq), full(wk), full(wv), full(gb), full(ng), full(sk), full(tril)],
        out_specs=pl.BlockSpec((1, S, ML_WP), lambda b: (b, 0, 0)),
        out_shape=jax.ShapeDtypeStruct((B, S, ML_WP), BF16),
        scratch_shapes=[pltpu.VMEM((S, ML_WP), BF16), pltpu.VMEM((S, ML_WP), BF16),
                        pltpu.VMEM((N_CHUNK, ML_WP, ML_CHUNK), BF16), pltpu.VMEM((S, 2 * ML_WP), BF16),
                        pltpu.VMEM((2, S, ML_WP), F32),
                        pltpu.VMEM((n_chain, HP, 2 * HP), F32),
                        pltpu.VMEM((n_chain, 1, HP), F32),
                        pltpu.VMEM((N_CHUNK, ML_CHUNK, 128), F32),
                        pltpu.VMEM((N_CHUNK, ML_CHUNK, 128), F32),
                        pltpu.VMEM((N_CHUNK, n_chain, ML_CHUNK), F32)],
        compiler_params=_cparams(("parallel",)),
        name="mlstm",
    )(pu, pg, cw, cb, wq, wk, wv, gb, ng, sk, tril)


def _na_kernel(idx_ref, p_ref, pt_ref, qg_ref, kg_ref, seg_ref, o_ref, kn_s, bias_s):
    j = pl.program_id(1)
    seg = seg_ref[...]

    def headnorm(x, g):
        ss = _dot((x * x).astype(BF16), seg)
        return x * lax.rsqrt(ss * (1.0 / NA_D) + EPS) * g

    @pl.when(j == 0)
    def _():
        def body(i, carry):
            r0 = pl.multiple_of(i * TT, TT)
            kk = p_ref[0, pl.ds(r0, TT), NA_W:2 * NA_W].astype(F32)
            kn_s[pl.ds(r0, TT), :] = headnorm(kk, kg_ref[...]).astype(BF16)
            return carry
        lax.fori_loop(0, NT, body, 0)

    scale = float(NA_D ** -0.5 * np.log2(np.e))
    kctx = kn_s[SEQ:S, :]
    vctx = p_ref[0, SEQ:S, 2 * NA_W:3 * NA_W]

    @pl.when(j < NA_NBLK)
    def _():
        q0 = pl.multiple_of(j * NA_QB, NA_QB)
        k0 = pl.multiple_of(jnp.clip(j * NA_QROWS - WIN_R // 2, 0, ROWS - NA_KROWS) * GRID_W, 256)
        q = headnorm(p_ref[0, pl.ds(q0, NA_QB), 0:NA_W].astype(F32), qg_ref[...]) * scale
        kl = kn_s[pl.ds(k0, NA_KB), :]
        vl = p_ref[0, pl.ds(k0, NA_KB), 2 * NA_W:3 * NA_W]
        head = lax.broadcasted_iota(jnp.int32, (NA_QB, NA_W), 1) // NA_D
        acc = jnp.zeros((NA_QB, NA_W), F32)

        def scores(h):
            qm = jnp.where(head == h, q, 0.0).astype(BF16)
            for i in range(NA_QROWS):
                for p in range(NA_KROWS // 2):
                    code = idx_ref[(j * NA_QROWS + i) * (NA_KROWS // 2) + p]
                    bias_s[i * GRID_W:(i + 1) * GRID_W, p * 2 * GRID_W:(p + 1) * 2 * GRID_W] = pt_ref[h, code]
            return _dot_nt(qm, kl) + bias_s[...].astype(F32), _dot_nt(qm, kctx)

        s_next = scores(0)
        for h in range(NA_H):
            s1, s2 = s_next
            if h + 1 < NA_H:
                s_next = scores(h + 1)
            m = jnp.maximum(jnp.max(s1, axis=-1, keepdims=True), jnp.max(s2, axis=-1, keepdims=True))
            p1 = jnp.exp2(s1 - m)
            p2 = jnp.exp2(s2 - m)
            l = jnp.sum(p1, axis=-1, keepdims=True) + jnp.sum(p2, axis=-1, keepdims=True)
            o = (_dot(p1.astype(BF16), vl) + _dot(p2.astype(BF16), vctx)) / l
            acc = jnp.where(head == h, o, acc)
        o_ref[0, pl.ds(q0, NA_QB), :] = acc.astype(BF16)

    @pl.when(j == NA_NBLK)
    def _():
        q = headnorm(p_ref[0, SEQ:S, 0:NA_W].astype(F32), qg_ref[...]) * scale
        head = lax.broadcasted_iota(jnp.int32, (CTX, NA_W), 1) // NA_D
        acc = jnp.zeros((CTX, NA_W), F32)
        for h in range(NA_H):
            qm = jnp.where(head == h, q, 0.0).astype(BF16)
            s2 = _dot_nt(qm, kctx)
            m = jnp.max(s2, axis=-1, keepdims=True)
            p2 = jnp.exp2(s2 - m)
            l = jnp.sum(p2, axis=-1, keepdims=True)
            o = _dot(p2.astype(BF16), vctx) / l
            acc = jnp.where(head == h, o, acc)
        o_ref[0, SEQ:S, :] = acc.astype(BF16)


def _na_attn(pair_idx, pna, pair_tiles, qg, kg, seg):
    B = pna.shape[0]
    return pl.pallas_call(
        _na_kernel,
        grid_spec=pltpu.PrefetchScalarGridSpec(
            num_scalar_prefetch=1,
            grid=(B, NA_NBLK + 1),
            in_specs=[pl.BlockSpec((1, S, 3 * NA_W), lambda b, j, idx: (b, 0, 0)),
                      pl.BlockSpec(pair_tiles.shape, lambda b, j, idx: (0, 0, 0, 0)),
                      pl.BlockSpec((1, NA_W), lambda b, j, idx: (0, 0)),
                      pl.BlockSpec((1, NA_W), lambda b, j, idx: (0, 0)),
                      pl.BlockSpec((NA_W, NA_W), lambda b, j, idx: (0, 0))],
            out_specs=pl.BlockSpec((1, S, NA_W), lambda b, j, idx: (b, 0, 0)),
            scratch_shapes=[pltpu.VMEM((S, NA_W), BF16), pltpu.VMEM((NA_QB, NA_KB), BF16)]),
        out_shape=jax.ShapeDtypeStruct((B, S, NA_W), BF16),
        compiler_params=_cparams(("parallel", "arbitrary")),
        name="na_attn",
    )(pair_idx, pna, pair_tiles, qg, kg, seg)


def _out_proj_kernel(z_ref, modx_ref, mody_ref, a_ref, m_ref, n_ref, wa_ref, wm_ref, wn_ref, g2_ref, rw_ref, rb_ref,
                     before_ref, below_ref, z1_ref, hs_ref, gs_ref, tm_ref, sm_ref):
    lane = lax.broadcasted_iota(jnp.int32, (TT, 128), 1)
    live = lane < N_EXPERTS
    groups = [slice(i * TT, (i + 1) * TT) for i in range(OT // TT)]

    splits = []
    for rows in groups:
        is_ctx = lax.broadcasted_iota(jnp.int32, (TT, 1), 0) + (pl.program_id(1) * OT + rows.start) >= SEQ
        mod = lambda i: jnp.where(is_ctx, mody_ref[0][:, i * D:(i + 1) * D], modx_ref[0][:, i * D:(i + 1) * D])
        mix = (_dot(a_ref[0, rows, :], wa_ref[...]) + _dot(m_ref[0, rows, :], wm_ref[...])
               + _dot(n_ref[0, rows, :], wn_ref[...]))
        x = z_ref[0, rows, :] + mod(2) * mix
        z1_ref[0, rows, :] = x
        hn = x * lax.rsqrt(jnp.mean(x * x, axis=-1, keepdims=True) + EPS) * g2_ref[...]
        hn = hn * (1.0 + mod(4)) + mod(3)
        h_hi, h_lo = _split_bf16(hn, 2)
        splits.append((h_hi, h_lo))

    affs = [_sigmoid(_dot(h_hi, rw_ref[0]) + (_dot(h_hi, rw_ref[1]) + _dot(h_lo, rw_ref[0])))
            for h_hi, h_lo in splits]

    def cyc(x, k, width):
        fwd = pltpu.roll(x, 128 - k, 1)
        back = pltpu.roll(x, width - k, 1)
        return jnp.where((lane % width) + k < width, fwd, back)

    def rank(x, width, step):
        r = jnp.zeros((TT, 128), F32)
        for k in range(1, width // step):
            y = cyc(x, k * step, width)
            wrapped = (lane % width) + k * step >= width
            beats = jnp.logical_or(y > x, jnp.logical_and(y == x, wrapped))
            r = r + beats.astype(F32)
        return r

    routed = []
    for aff in affs:
        sel = aff + rb_ref[...]
        top2 = rank(sel, EPG, 1) < 2.0
        part = jnp.where(top2, sel, 0.0)
        gscore = part
        for k in range(1, EPG):
            gscore = gscore + cyc(part, k, EPG)
        best = rank(gscore, N_EXPERTS, EPG) < 1.0
        chosen = jnp.logical_and(jnp.logical_and(top2, best), live)
        w = jnp.where(chosen, aff, 0.0)
        group = jnp.sum(jnp.where(chosen, (lane // EPG).astype(F32), 0.0), axis=-1, keepdims=True) * (1.0 / TOP_K)
        routed.append((w / jnp.sum(w, axis=-1, keepdims=True), group))

    sub = lax.broadcasted_iota(jnp.int32, (128, TT), 0)
    row_id = lax.broadcasted_iota(jnp.int32, (SORT_ROWS, TT), 0).astype(F32)
    for gi, ((gate, group), (h_hi, _)) in enumerate(zip(routed, splits)):
        member_t = (lane.astype(F32) == group).astype(F32).T
        member_t = jnp.where(sub < N_GROUPS, member_t, 0.0)
        ahead = _dot(member_t.astype(BF16), before_ref[...])
        count = jnp.sum(member_t, axis=1, keepdims=True)
        padded = ((count.astype(jnp.int32) + (SORT_BLK - 1)) & -SORT_BLK).astype(F32)
        start = _dot(below_ref[...], jnp.broadcast_to(padded, (128, 128)).astype(BF16))
        where_t = jnp.sum(member_t * (start[:, 0:1] + ahead), axis=0, keepdims=True)
        perm = (row_id == where_t).astype(BF16)
        hs_ref[0, gi] = _dot(perm, h_hi)
        gs_ref[0, gi] = sum(_dot(perm, part) for part in _split_bf16(gate, 3))
        group_t = jnp.sum(member_t * sub.astype(F32), axis=0, keepdims=True)
        tm_ref[0, gi, 0:1, :] = group_t
        tm_ref[0, gi, 1:2, :] = where_t
        tm_ref[0, gi, 2:8, :] = jnp.zeros((6, TT), F32)
        lane_m = lax.broadcasted_iota(jnp.int32, (128, 128), 1)
        sm_ref[0, gi] = jnp.where(lane_m == 0, jnp.broadcast_to(padded, (128, 128)), start)


def _out_proj(z, mods, mla_o, ml_o, na_o, wa, wm, wn, g2, rw, rb):
    B = z.shape[0]
    tok = lambda w_: pl.BlockSpec((1, OT, w_), lambda b, t: (b, t, 0))
    full = lambda a: pl.BlockSpec(a.shape, lambda b, t, _n=a.ndim: (0,) * _n)
    per_group = lambda r, w_: pl.BlockSpec((1, OT // TT, r, w_), lambda b, t: (b, t, 0, 0))
    grouped = lambda r, w_: jax.ShapeDtypeStruct((B, NT, r, w_), F32)
    before = jnp.asarray(np.triu(np.ones((TT, TT)), 1), BF16)
    below = jnp.asarray(np.tril(np.ones((128, 128)), -1), BF16)
    return pl.pallas_call(
        _out_proj_kernel,
        grid=(B, S // OT),
        in_specs=[tok(D),
                  pl.BlockSpec((1, 1, 6 * D), lambda b, t: (2 * b, 0, 0)),
                  pl.BlockSpec((1, 1, 6 * D), lambda b, t: (2 * b + 1, 0, 0)),
                  tok(MLA_H * V_D), tok(ML_WP), tok(NA_W),
                  full(wa), full(wm), full(wn), full(g2), full(rw), full(rb), full(before), full(below)],
        out_specs=[tok(D), per_group(SORT_ROWS, D), per_group(SORT_ROWS, 128), per_group(8, TT), per_group(128, 128)],
        out_shape=[jax.ShapeDtypeStruct((B, S, D), F32),
                   grouped(SORT_ROWS, D), grouped(SORT_ROWS, 128), grouped(8, TT), grouped(128, 128)],
        compiler_params=_cparams(("parallel", "parallel")),
        name="out_proj",
    )(z, mods, mods, mla_o, ml_o, na_o, wa, wm, wn, g2, rw, rb, before, below)


ROW_WAIT = 128


def _row_copy_start(n, row_copy):
    def issue(i, carry):
        row_copy(i).start()
        return carry

    lax.fori_loop(0, n, issue, 0, unroll=8)


def _row_copy_wait(n, slab_copy):
    def drain(i, carry):
        slab_copy(ROW_WAIT).wait()
        return carry

    lax.fori_loop(0, n // ROW_WAIT, drain, 0)


def _experts_kernel(blk_ref, tg_ref, hs_ref, gs_ref, w1_ref, w3_ref, w2_ref, ys_ref, hbuf, gbuf, sems):
    i = pl.program_id(0)
    n_blk = GROUP_T // SORT_BLK

    def fetch(tile, slot):
        def src(q):
            return pl.ds(pl.multiple_of(blk_ref[tile * n_blk + q] * SORT_BLK, SORT_BLK), SORT_BLK)

        def dst(q):
            return pl.ds(pl.multiple_of(q * SORT_BLK, SORT_BLK), SORT_BLK)

        _row_copy_start(n_blk, lambda q: pltpu.make_async_copy(hs_ref.at[src(q)], hbuf.at[slot, dst(q)], sems.at[0, slot]))
        _row_copy_start(n_blk, lambda q: pltpu.make_async_copy(gs_ref.at[src(q)], gbuf.at[slot, dst(q)], sems.at[1, slot]))

    @pl.when(i == 0)
    def _():
        fetch(0, 0)

    slot = i % 2
    _row_copy_wait(GROUP_T, lambda k: pltpu.make_async_copy(
        hs_ref.at[pl.ds(0, k)], hbuf.at[slot, pl.ds(0, k)], sems.at[0, slot]))
    _row_copy_wait(GROUP_T, lambda k: pltpu.make_async_copy(
        gs_ref.at[pl.ds(0, k)], gbuf.at[slot, pl.ds(0, k)], sems.at[1, slot]))

    @pl.when(i + 1 < pl.num_programs(0))
    def _():
        fetch(i + 1, 1 - slot)

    g = tg_ref[i]
    x = hbuf[slot].astype(BF16)
    gates = gbuf[slot]
    lane = lax.broadcasted_iota(jnp.int32, (GROUP_T, 128), 1)
    ups = [(_dot(x, w1_ref[0, e].astype(BF16)), _dot(x, w3_ref[0, e].astype(BF16))) for e in range(EPG)]
    acc = jnp.zeros((GROUP_T, D), F32)
    for e in range(EPG):
        ge = jnp.sum(jnp.where(lane == g * EPG + e, gates, 0.0), axis=-1, keepdims=True)
        a, b = ups[e]
        acc = acc + _dot((_silu(a) * b * ge).astype(BF16), w2_ref[0, e].astype(BF16))
    ys_ref[...] = acc


def _experts(src_blk, tile_group, hs, gs, n_rows, w1, w3, w2, l):
    wspec = lambda k, n: pl.BlockSpec((1, EPG, k, n), lambda i, blk, tg: (l, tg[i], 0, 0))
    return pl.pallas_call(
        _experts_kernel,
        grid_spec=pltpu.PrefetchScalarGridSpec(
            num_scalar_prefetch=2,
            grid=(n_rows // GROUP_T,),
            in_specs=[pl.BlockSpec(memory_space=pl.ANY), pl.BlockSpec(memory_space=pl.ANY),
                      wspec(D, D_FF), wspec(D, D_FF), wspec(D_FF, D)],
            out_specs=pl.BlockSpec((GROUP_T, D), lambda i, blk, tg: (i, 0)),
            scratch_shapes=[pltpu.VMEM((2, GROUP_T, D), F32), pltpu.VMEM((2, GROUP_T, 128), F32),
                            pltpu.SemaphoreType.DMA((2, 2))]),
        out_shape=jax.ShapeDtypeStruct((n_rows, D), F32),
        compiler_params=_cparams(("arbitrary",)),
        name="moe_experts",
    )(src_blk, tile_group, hs, gs, w1, w3, w2)


def _combine_kernel(pos_ref, z1_ref, modx_ref, mody_ref, ys_ref, o_ref, buf, sem, *, tile):
    t = pl.program_id(1)
    base = pl.program_id(0) * S + t * tile
    _row_copy_start(tile, lambda i: pltpu.make_async_copy(
        ys_ref.at[pl.ds(pos_ref[base + i], 1)], buf.at[pl.ds(i, 1)], sem))
    _row_copy_wait(tile, lambda k: pltpu.make_async_copy(ys_ref.at[pl.ds(0, k)], buf.at[pl.ds(0, k)], sem))
    is_ctx = lax.broadcasted_iota(jnp.int32, (tile, 1), 0) + t * tile >= SEQ
    g2 = jnp.where(is_ctx, mody_ref[0][:, 5 * D:], modx_ref[0][:, 5 * D:])
    o_ref[0] = z1_ref[0] + g2 * buf[...]


def _combine(pos, z1, mods, ys, rows_out, tile):
    B = z1.shape[0]
    tok = pl.BlockSpec((1, tile, D), lambda b, t, pos_: (b, t, 0))
    return pl.pallas_call(
        functools.partial(_combine_kernel, tile=tile),
        grid_spec=pltpu.PrefetchScalarGridSpec(
            num_scalar_prefetch=1,
            grid=(B, rows_out // tile),
            in_specs=[tok,
                      pl.BlockSpec((1, 1, 6 * D), lambda b, t, pos_: (2 * b, 0, 0)),
                      pl.BlockSpec((1, 1, 6 * D), lambda b, t, pos_: (2 * b + 1, 0, 0)),
                      pl.BlockSpec(memory_space=pl.ANY)],
            out_specs=tok,
            scratch_shapes=[pltpu.VMEM((tile, D), F32), pltpu.SemaphoreType.DMA(())]),
        out_shape=jax.ShapeDtypeStruct((B, rows_out, D), F32),
        compiler_params=_cparams(("arbitrary", "arbitrary")),
        name="moe_combine",
    )(pos, z1, mods, mods, ys)


def _group_layout(tok_meta, seg_meta, n_rows):
    n_tg = tok_meta.shape[0]
    seg_len = seg_meta[:, :N_GROUPS, 0].astype(jnp.int32).T
    seg_local = seg_meta[:, :N_GROUPS, 1].astype(jnp.int32).T
    total = jnp.sum(seg_len, axis=1)
    padded = (total + GROUP_T - 1) // GROUP_T * GROUP_T
    group_end = jnp.cumsum(padded)
    seg_start = (group_end - padded)[:, None] + jnp.cumsum(seg_len, axis=1) - seg_len
    flat_start, flat_len, flat_local = seg_start.reshape(-1), seg_len.reshape(-1), seg_local.reshape(-1)

    row0 = (jnp.arange(n_rows // SORT_BLK, dtype=jnp.int32) * SORT_BLK)[:, None]
    covers = (row0 >= flat_start[None, :]) & (row0 < (flat_start + flat_len)[None, :])
    token_group = (jnp.arange(flat_start.shape[0], dtype=jnp.int32) % n_tg)[None, :]
    src_row = jnp.sum(jnp.where(covers, token_group * SORT_ROWS + flat_local[None, :] + row0 - flat_start[None, :], 0),
                      axis=1)
    filled = jnp.any(covers, axis=1)
    src_blk = jnp.where(filled, src_row // SORT_BLK, SORT_ROWS // SORT_BLK - 1)

    tile_group = jnp.sum(jnp.arange(n_rows // GROUP_T)[:, None] * GROUP_T >= group_end[None, :], axis=1)
    gid = tok_meta[:, 0, :].astype(jnp.int32)
    local = tok_meta[:, 1, :].astype(jnp.int32)
    shift = (seg_start - seg_local).T
    pos = local + sum(jnp.where(gid == g, shift[:, g:g + 1], 0) for g in range(N_GROUPS))
    return (src_blk.astype(jnp.int32), jnp.minimum(tile_group, N_GROUPS - 1).astype(jnp.int32),
            pos.reshape(-1).astype(jnp.int32))


def _moe(z1, hs, gs, tok_meta, seg_meta, mods, w1, w3, w2, l):
    B = z1.shape[0]
    n_tg = B * NT
    n_rows = (-(-(n_tg * (TT + N_GROUPS * (SORT_BLK - 1))) // GROUP_T) + N_GROUPS) * GROUP_T
    src_blk, tile_group, pos = _group_layout(tok_meta.reshape(n_tg, 8, TT), seg_meta.reshape(n_tg, 128, 128), n_rows)
    ys = _experts(src_blk, tile_group, hs.reshape(n_tg * SORT_ROWS, D), gs.reshape(n_tg * SORT_ROWS, 128),
                  n_rows, w1, w3, w2, l)
    if l == DEPTH - 1:
        return _combine(pos, z1, mods, ys, SEQ, COMBINE_LAST_T)
    return _combine(pos, z1, mods, ys, S, COMBINE_T)


def _in_proj_layout(w):
    cuts = np.cumsum([Q_RANK, KV_RANK, ROPE_D, ML_W, ML_W, 4 * ML_H])
    qc, ckv, kr, u, zz, g, na = jnp.split(w, [int(v) for v in cuts], axis=-1)
    zeros = lambda n: jnp.zeros((w.shape[0], n), w.dtype)
    out = jnp.concatenate([qc, ckv, zeros(NOPE_D), kr, zeros(HP - QK_D),
                           _pad_heads(u, ML_H, ML_D, HP), _pad_heads(zz, ML_H, ML_D, HP),
                           _gate_order(g), zeros(128 - 4 * ML_H), na], axis=-1)
    assert out.shape[-1] == NP_IN
    return out


def _gate_order(g):
    i_f, f_f, i_b, f_b = jnp.split(g, 4, axis=-1)
    return jnp.concatenate([i_f, i_b, f_f, f_b], axis=-1)


def _pad_heads(v, nh, d, dp):
    lead = v.shape[:-1]
    v = v.reshape(lead + (nh, d))
    v = jnp.pad(v, [(0, 0)] * len(lead) + [(0, 0), (0, dp - d)])
    return v.reshape(lead + (nh * dp,))


def _rope_tables():
    t = np.arange(SEQ)
    row = (t // GRID_W).astype(np.float32)
    col = (t % GRID_W).astype(np.float32)
    quarter = ROPE_D // 4
    inv = jnp.asarray(ROPE_BASE, F32) ** (-jnp.arange(quarter, dtype=F32) / quarter)
    ar = jnp.asarray(row)[:, None] * inv
    ac = jnp.asarray(col)[:, None] * inv
    ang = jnp.concatenate([ar, ar, ac, ac], axis=-1)
    cos = jnp.ones((S, HP), F32).at[:SEQ, NOPE_D:QK_D].set(jnp.cos(ang))
    sin = jnp.zeros((S, HP), F32).at[:SEQ, NOPE_D:QK_D].set(jnp.sin(ang))
    return cos, sin


def _rotate_half_index():
    q = ROPE_D // 4
    src = np.arange(QK_D)
    sign = np.zeros((QK_D,), np.float32)
    for blk in range(2):
        lo = NOPE_D + 2 * q * blk
        src[lo:lo + q] = np.arange(lo + q, lo + 2 * q)
        sign[lo:lo + q] = -1.0
        src[lo + q:lo + 2 * q] = np.arange(lo, lo + q)
        sign[lo + q:lo + 2 * q] = 1.0
    return src, sign


def _rotate_half(w):
    src, sign = _rotate_half_index()
    return w[..., src] * sign


NA_NDR = 2 * WIN_R - 1
NA_NPAIR = 3 * NA_NDR


def _na_pair_index():
    idx = np.zeros((NA_NBLK, NA_QROWS, NA_KROWS // 2), np.int32)
    for blk in range(NA_NBLK):
        k0 = int(np.clip(blk * NA_QROWS - WIN_R // 2, 0, ROWS - NA_KROWS))
        for i in range(NA_QROWS):
            qr = blk * NA_QROWS + i
            rs = int(np.clip(qr - WIN_R // 2, 0, ROWS - WIN_R))
            assert k0 <= rs and rs + WIN_R <= k0 + NA_KROWS
            for p in range(NA_KROWS // 2):
                kr = k0 + 2 * p
                dr = kr - qr + WIN_R - 1
                left = rs <= kr < rs + WIN_R
                right = rs <= kr + 1 < rs + WIN_R
                if left and right:
                    idx[blk, i, p] = 1 + dr
                elif left:
                    idx[blk, i, p] = NA_NDR + dr
                elif right:
                    idx[blk, i, p] = 2 * NA_NDR + dr + 1
    return idx.reshape(-1)


def _na_pair_tiles(rpb):
    cq = np.arange(GRID_W)
    cs = np.clip(cq - WIN_C // 2, 0, GRID_W - WIN_C)
    col_ok = (cq[None, :] >= cs[:, None]) & (cq[None, :] < cs[:, None] + WIN_C)
    dc = np.clip(cq[None, :] - cq[:, None], -(WIN_C - 1), WIN_C - 1) + (WIN_C - 1)
    onehot = jnp.asarray(np.eye(2 * WIN_C - 1, dtype=np.float32)[dc])
    tiles = jnp.einsum('hrc,qkc->hrqk', rpb, onehot, precision=lax.Precision.HIGHEST)
    tiles = jnp.where(jnp.asarray(col_ok), tiles * float(np.log2(np.e)), NEG)
    masked = jnp.full_like(tiles, NEG)
    both = jnp.concatenate([tiles[:, :-1], tiles[:, 1:]], axis=-1)
    left = jnp.concatenate([tiles, masked], axis=-1)
    right = jnp.concatenate([masked, tiles], axis=-1)
    none = jnp.concatenate([masked[:, :1], masked[:, :1]], axis=-1)
    out = jnp.concatenate([none, both, left, right], axis=1)
    assert out.shape[1] == NA_NPAIR
    return out.astype(BF16)


def kernel(x, c, ctx, c_ctx, w_mod, b_mod, norm1_g, norm2_g, w_in, w_out, mla_qnorm_g, mla_w_uq, mla_kvnorm_g, mla_w_ukv, mla_q_g, mla_k_g, ml_conv_w, ml_conv_b, ml_w_q, ml_w_k, ml_w_v, ml_gate_b, ml_norm_g, ml_skip, na_q_g, na_k_g, na_rpb, router_w, router_b, moe_w1, moe_w3, moe_w2):
    B = x.shape[0]
    z = jnp.concatenate([x, ctx], axis=1)
    cc = jnp.zeros((16, D), F32).at[:B].set(c).at[B].set(c_ctx)
    mod_all = _modulation(cc, w_mod, b_mod)
    cos, sin = _rope_tables()
    rot_src, rot_sign = _rotate_half_index()
    rot_np = np.zeros((HP, HP), np.float32)
    rot_np[rot_src, np.arange(QK_D)] = rot_sign
    rot_mat = jnp.asarray(rot_np, BF16)
    ones_hp = jnp.ones((HP, HP), BF16)
    seg = jnp.asarray(np.kron(np.eye(NA_H), np.ones((NA_D, NA_D))), BF16)
    tril = jnp.asarray(np.stack([np.tril(np.ones((ML_CHUNK, ML_CHUNK))), np.triu(np.ones((ML_CHUNK, ML_CHUNK)))]), BF16)
    pair_idx = jnp.asarray(_na_pair_index())
    rw = jnp.stack(_split_bf16(jnp.pad(router_w, ((0, 0), (0, 128 - N_EXPERTS))), 2))
    rb = jnp.pad(router_b, (0, 128 - N_EXPERTS), constant_values=NEG).reshape(1, 128)

    def pad_lanes(v, n):
        return jnp.pad(v, [(0, 0)] * (v.ndim - 1) + [(0, n - v.shape[-1])])

    for l in range(DEPTH):
        mx = mod_all[l, :B]
        my = jnp.broadcast_to(mod_all[l, B], (B, 6 * D))
        mods = jnp.stack([mx, my], axis=1).reshape(2 * B, 1, 6 * D)

        w_in_p = _in_proj_layout(w_in[l]).astype(BF16)
        pmla, pu, pg, pna = _in_proj(z, mods, norm1_g[l].reshape(1, D), w_in_p)

        wq = jnp.transpose(mla_w_uq[l].reshape(Q_RANK, MLA_H, QK_D), (1, 0, 2))
        wuq = jnp.concatenate([pad_lanes(wq, HP), pad_lanes(_rotate_half(wq), HP)], axis=-1).astype(BF16)
        wukv = jnp.transpose(mla_w_ukv[l].reshape(KV_RANK, MLA_H, NOPE_D + V_D), (1, 0, 2))
        wuk = pad_lanes(wukv[..., :NOPE_D], HP).astype(BF16)
        wuv = pad_lanes(wukv[..., NOPE_D:], HP).astype(BF16)
        qg, kg = mla_q_g[l], mla_k_g[l]
        q_scale = float(QK_D ** -0.5 * np.log2(np.e))
        tabs = jnp.stack([cos * pad_lanes(qg, HP) * q_scale, sin * pad_lanes(jnp.abs(rot_sign) * qg[rot_src], HP) * q_scale,
                          cos * pad_lanes(kg, HP), sin * pad_lanes(jnp.abs(rot_sign) * kg[rot_src], HP)])
        heads_last = lambda w_: jnp.transpose(w_, (1, 0, 2)).reshape(w_.shape[1], -1)
        q, k, v = _mla_prep(pmla, tabs, mla_qnorm_g[l].reshape(1, Q_RANK), heads_last(wuq),
                            mla_kvnorm_g[l].reshape(1, KV_RANK), heads_last(wuk), heads_last(wuv), rot_mat, ones_hp)
        mla_o = _mla_attn(q, k, v)

        padh = lambda a: _pad_heads(a, ML_H, ML_D, HP)
        padw = lambda w_: jnp.pad(w_, ((0, 0), (0, HP - ML_D), (0, HP - ML_D))).astype(BF16)
        cw = jnp.pad(padh(ml_conv_w[l]), ((0, 8 - 3), (0, 0)))
        ml_o = _mlstm(pu, pg, cw, padh(ml_conv_b[l]).reshape(1, ML_WP),
                      padw(ml_w_q[l]), padw(jnp.swapaxes(ml_w_k[l], 1, 2) * (ML_D ** -0.5)), padw(ml_w_v[l]),
                      pad_lanes(_gate_order(ml_gate_b[l]).reshape(1, 4 * ML_H), 128),
                      padh(ml_norm_g[l]).reshape(1, ML_WP), padh(ml_skip[l]).reshape(1, ML_WP), tril)

        na_o = _na_attn(pair_idx, pna, _na_pair_tiles(na_rpb[l]), jnp.tile(na_q_g[l], NA_H).reshape(1, NA_W),
                        jnp.tile(na_k_g[l], NA_H).reshape(1, NA_W), seg)

        wo = w_out[l]
        wa = wo[:MLA_H * V_D].astype(BF16)
        wm = jnp.pad(wo[MLA_H * V_D:MLA_H * V_D + ML_W].reshape(ML_H, ML_D, D),
                     ((0, 0), (0, HP - ML_D), (0, 0))).reshape(ML_WP, D).astype(BF16)
        wn = wo[MLA_H * V_D + ML_W:].astype(BF16)
        z1, hs, gs, tok_meta, seg_meta = _out_proj(z, mods, mla_o, ml_o, na_o, wa, wm, wn,
                                                   norm2_g[l].reshape(1, D), rw, rb)
        z = _moe(z1, hs, gs, tok_meta, seg_meta, mods, moe_w1, moe_w3, moe_w2, l)

    return z
```

```python
import functools

import numpy as np
import jax
import jax.numpy as jnp
from jax import lax
from jax.experimental import pallas as pl
from jax.experimental.pallas import tpu as pltpu

F32 = jnp.float32
BF16 = jnp.bfloat16

D = 1024
SEQ = 2048
CTX = 256
S = SEQ + CTX
DEPTH = 4
GRID_W = 64
ROWS = SEQ // GRID_W
EPS = 1e-6

MLA_H = 6
Q_RANK = 256
KV_RANK = 128
NOPE_D = 64
ROPE_D = 32
V_D = 64
QK_D = NOPE_D + ROPE_D
ROPE_BASE = 10000.0

ML_H = 4
ML_D = 96
ML_W = ML_H * ML_D
HP = 128
ML_WP = ML_H * HP
ML_CHUNK = 128
N_CHUNK = S // ML_CHUNK
N_CTX_CHUNK = CTX // ML_CHUNK

NA_H = 4
NA_D = 64
NA_W = NA_H * NA_D
WIN_R = 8
WIN_C = 16
NA_QROWS = 4
NA_KROWS = 12
NA_QB = NA_QROWS * GRID_W
NA_KB = NA_KROWS * GRID_W
NA_NBLK = ROWS // NA_QROWS

N_EXPERTS = 16
N_GROUPS = 4
EPG = N_EXPERTS // N_GROUPS
D_FF = 256

TT = 256
NT = S // TT
OT = 768
TOP_K = 2
SORT_BLK = 8
SORT_ROWS = 320
GROUP_T = 512
NEG = -1e30

C_QC = 0
C_CKV = 256
C_KR = 384
C_U = 512
C_Z = C_U + ML_WP
C_G = C_Z + ML_WP
C_NA = C_G + 128
NP_IN = C_NA + 3 * NA_W

VMEM_LIMIT = 56 * 1024 * 1024


def _cparams(sem):
    return pltpu.CompilerParams(dimension_semantics=sem, vmem_limit_bytes=VMEM_LIMIT)


def _sigmoid(x):
    return 1.0 / (1.0 + jnp.exp(-x))


def _silu(x):
    return x * _sigmoid(x)


def _dot(a, b):
    return jnp.dot(a, b, preferred_element_type=F32)


def _dot_nt(a, b):
    return lax.dot_general(a, b, (((1,), (1,)), ((), ())), preferred_element_type=F32)


def _dot_tn(a, b):
    return lax.dot_general(a, b, (((0,), (0,)), ((), ())), preferred_element_type=F32)


def _dot_hi(a, b):
    return jnp.dot(a, b, preferred_element_type=F32, precision=lax.Precision.HIGHEST)


def _split_bf16(x, n):
    parts = []
    for _ in range(n):
        p = x.astype(BF16)
        parts.append(p)
        x = x - p.astype(F32)
    return parts


def _mod_rows(mod_ref, t):
    m = mod_ref[0]
    return [m[:, i * D:(i + 1) * D] for i in range(6)]


def _mod_kernel(c_ref, w_ref, b_ref, o_ref):
    sc = _silu(c_ref[...])
    o_ref[0] = _dot_hi(sc, w_ref[0]) + b_ref[0]


def _modulation(cc, w_mod, b_mod):
    nc = 6
    return pl.pallas_call(
        _mod_kernel,
        grid=(DEPTH, nc),
        in_specs=[pl.BlockSpec((16, D), lambda l, j: (0, 0)),
                  pl.BlockSpec((1, D, D), lambda l, j: (l, 0, j)),
                  pl.BlockSpec((1, 1, D), lambda l, j: (l, 0, j))],
        out_specs=pl.BlockSpec((1, 16, D), lambda l, j: (l, 0, j)),
        out_shape=jax.ShapeDtypeStruct((DEPTH, 16, 6 * D), F32),
        compiler_params=_cparams(("parallel", "parallel")),
        name="modulation",
    )(cc, w_mod, b_mod.reshape(DEPTH, 1, 6 * D))


def _in_proj_kernel(z_ref, mod_ref, g_ref, w_ref, pmla_ref, pu_ref, pg_ref, pna_ref):
    sh1, sc1 = _mod_rows(mod_ref, None)[:2]
    x = z_ref[0]
    xn = x * lax.rsqrt(jnp.mean(x * x, axis=-1, keepdims=True) + EPS) * g_ref[...]
    xn = xn * (1.0 + sc1) + sh1
    p = _dot(xn.astype(BF16), w_ref[...])
    pmla_ref[0] = p[:, :C_U].astype(BF16)
    pu_ref[0] = p[:, C_U:C_G].astype(BF16)
    pg_ref[0] = p[:, C_G:C_NA]
    pna_ref[0] = p[:, C_NA:].astype(BF16)


def _mod_spec():
    return pl.BlockSpec((1, 1, 6 * D), lambda b, t: (2 * b + t // (NT - 1), 0, 0))


def _in_proj(z, mods, g, w):
    B = z.shape[0]
    tok = lambda w_: pl.BlockSpec((1, TT, w_), lambda b, t: (b, t, 0))
    return pl.pallas_call(
        _in_proj_kernel,
        grid=(B, NT),
        in_specs=[tok(D), _mod_spec(),
                  pl.BlockSpec((1, D), lambda b, t: (0, 0)),
                  pl.BlockSpec((D, NP_IN), lambda b, t: (0, 0))],
        out_specs=[tok(C_U), tok(2 * ML_WP), tok(128), tok(3 * NA_W)],
        out_shape=[jax.ShapeDtypeStruct((B, S, C_U), BF16),
                   jax.ShapeDtypeStruct((B, S, 2 * ML_WP), BF16),
                   jax.ShapeDtypeStruct((B, S, 128), F32),
                   jax.ShapeDtypeStruct((B, S, 3 * NA_W), BF16)],
        compiler_params=_cparams(("parallel", "parallel")),
        name="in_proj",
    )(z, mods, g, w)


def _mla_prep_kernel(p_ref, tab_ref, qng_ref, wuq_ref, kvng_ref, wuk_ref, wuv_ref, rot_ref, ones_ref,
                     q_out, k_out, v_out):
    p = p_ref[0].astype(F32)
    qc = p[:, C_QC:C_CKV]
    ckv = p[:, C_CKV:C_KR]
    kr = p[:, C_KR:C_U]
    qcn = (qc * lax.rsqrt(jnp.mean(qc * qc, axis=-1, keepdims=True) + EPS) * qng_ref[...]).astype(BF16)
    ckvn = (ckv * lax.rsqrt(jnp.mean(ckv * ckv, axis=-1, keepdims=True) + EPS) * kvng_ref[...]).astype(BF16)
    lane = lax.broadcasted_iota(jnp.int32, (TT, HP), 1)
    ones = ones_ref[...]
    kr_rot = _dot(kr.astype(BF16), rot_ref[...])

    q_all = _dot(qcn, wuq_ref[...])
    k_all = _dot(ckvn, wuk_ref[...])
    v_all = _dot(ckvn, wuv_ref[...])
    qs = [q_all[:, 2 * h * HP:(2 * h + 1) * HP] for h in range(MLA_H)]
    q_rots = [q_all[:, (2 * h + 1) * HP:(2 * h + 2) * HP] for h in range(MLA_H)]
    ks = [k_all[:, h * HP:(h + 1) * HP] + kr for h in range(MLA_H)]
    ss_q = [_dot((x * x).astype(BF16), ones) for x in qs]
    ss_k = [_dot((x * x).astype(BF16), ones) for x in ks]

    def norm_rope(x, x_rot, ss, cos_g, sin_g):
        return lax.rsqrt(ss * (1.0 / QK_D) + EPS) * (x * cos_g + x_rot * sin_g)

    for h in range(MLA_H):
        q_out[0, h] = norm_rope(qs[h], q_rots[h], ss_q[h], tab_ref[0], tab_ref[1]).astype(BF16)
        k_out[0, h] = norm_rope(ks[h], kr_rot, ss_k[h], tab_ref[2], tab_ref[3]).astype(BF16)
        v_out[0, h] = jnp.where(lane < V_D, v_all[:, h * HP:(h + 1) * HP], 1.0).astype(BF16)


def _mla_prep(pmla, tabs, qng, wuq, kvng, wuk, wuv, rot, ones):
    B = pmla.shape[0]
    full = lambda a: pl.BlockSpec(a.shape, lambda b, t, _n=a.ndim: (0,) * _n)
    hd = lambda w_: pl.BlockSpec((1, MLA_H, TT, w_), lambda b, t: (b, 0, t, 0))
    return pl.pallas_call(
        _mla_prep_kernel,
        grid=(B, NT),
        in_specs=[pl.BlockSpec((1, TT, C_U), lambda b, t: (b, t, 0)),
                  pl.BlockSpec((4, TT, HP), lambda b, t: (0, t, 0)),
                  full(qng), full(wuq), full(kvng), full(wuk), full(wuv), full(rot), full(ones)],
        out_specs=[hd(HP), hd(HP), hd(HP)],
        out_shape=[jax.ShapeDtypeStruct((B, MLA_H, S, HP), BF16),
                   jax.ShapeDtypeStruct((B, MLA_H, S, HP), BF16),
                   jax.ShapeDtypeStruct((B, MLA_H, S, HP), BF16)],
        compiler_params=_cparams(("parallel", "parallel")),
        name="mla_prep",
    )(pmla, tabs, qng, wuq, kvng, wuk, wuv, rot, ones)


def _mla_attn_kernel(q_ref, k_ref, v_ref, o_ref):
    t = pl.program_id(1)

    def attend(k_lo, k_n):
        def scores(h):
            return _dot_nt(q_ref[0, h], k_ref[0, h, k_lo:k_lo + k_n, :])

        outs = []
        s_next = scores(0)
        for h in range(MLA_H):
            s = s_next
            if h + 1 < MLA_H:
                s_next = scores(h + 1)
            m = jnp.max(s, axis=-1, keepdims=True)
            p = jnp.exp2(s - m)
            pv = _dot(p.astype(BF16), v_ref[0, h, k_lo:k_lo + k_n, :])
            outs.append(pv[:, :V_D] / pv[:, V_D:V_D + 1])
        o_ref[0] = jnp.concatenate(outs, axis=-1).astype(BF16)

    @pl.when(t < NT - 1)
    def _():
        attend(0, S)

    @pl.when(t == NT - 1)
    def _():
        attend(SEQ, CTX)


def _mla_attn(q, k, v):
    B = q.shape[0]
    return pl.pallas_call(
        _mla_attn_kernel,
        grid=(B, NT),
        in_specs=[pl.BlockSpec((1, MLA_H, TT, HP), lambda b, t: (b, 0, t, 0)),
                  pl.BlockSpec((1, MLA_H, S, HP), lambda b, t: (b, 0, 0, 0)),
                  pl.BlockSpec((1, MLA_H, S, HP), lambda b, t: (b, 0, 0, 0))],
        out_specs=pl.BlockSpec((1, TT, MLA_H * V_D), lambda b, t: (b, t, 0)),
        out_shape=jax.ShapeDtypeStruct((B, S, MLA_H * V_D), BF16),
        compiler_params=_cparams(("parallel", "arbitrary")),
        name="mla_attn",
    )(q, k, v)


def _log_sigmoid(x):
    return jnp.minimum(x, 0.0) - jnp.log(1.0 + jnp.exp(-jnp.abs(x)))


def _mlstm_kernel(pu_ref, pg_ref, cw_ref, cb_ref, wq_ref, wk_ref, wv_ref, gb_ref, ng_ref, sk_ref,
                  tril_ref, o_ref, uc_s, q_s, kt_s, v_s, h_s, c_s, m_s, pm_s, b_s, rt_s):
    CA = 2 * ML_CHUNK
    row = lax.broadcasted_iota(jnp.int32, (CA, ML_WP), 0)

    def conv_body(i, carry):
        r0 = pl.multiple_of(i * CA, CA)
        x = pu_ref[0, pl.ds(r0, CA), 0:ML_WP].astype(F32)
        pr = pl.multiple_of(jnp.maximum(r0 - 16, 0), 16)
        nx = pl.multiple_of(jnp.minimum(r0 + CA, S - 16), 16)
        prev = pu_ref[0, pl.ds(pr, 16), 0:ML_WP].astype(F32)[15:16, :]
        nxt = pu_ref[0, pl.ds(nx, 16), 0:ML_WP].astype(F32)[0:1, :]
        seq_start = jnp.logical_or(r0 == 0, r0 == SEQ)
        seq_end = jnp.logical_or(r0 + CA == SEQ, r0 + CA == S)
        prev = jnp.where(seq_start, 0.0, prev)
        nxt = jnp.where(seq_end, 0.0, nxt)
        up = jnp.where(row == 0, prev, pltpu.roll(x, 1, 0))
        dn = jnp.where(row == CA - 1, nxt, pltpu.roll(x, CA - 1, 0))
        uc = _silu(cw_ref[0:1, :] * up + cw_ref[1:2, :] * x + cw_ref[2:3, :] * dn + cb_ref[...])
        ucb = uc.astype(BF16)
        uc_s[pl.ds(r0, CA), :] = ucb
        xb = x.astype(BF16)
        for h in range(ML_H):
            sl = slice(h * HP, (h + 1) * HP)
            q_s[pl.ds(r0, CA), sl] = _dot(ucb[:, sl], wq_ref[h]).astype(BF16)
            kt = _dot_nt(wk_ref[h], ucb[:, sl])
            kt_s[2 * i, sl, :] = kt[:, :ML_CHUNK].astype(BF16)
            kt_s[2 * i + 1, sl, :] = kt[:, ML_CHUNK:].astype(BF16)
            v_s[pl.ds(r0, CA), 2 * h * HP:(2 * h + 1) * HP] = _dot(xb[:, sl], wv_ref[h]).astype(BF16)
            v_s[pl.ds(r0, CA), (2 * h + 1) * HP:(2 * h + 2) * HP] = jnp.ones((CA, HP), BF16)

        for half in range(2):
            rows = pl.ds(r0 + half * ML_CHUNK, ML_CHUNK)
            g = pg_ref[0, rows, :] + gb_ref[...]
            parts = _split_bf16(_log_sigmoid(g), 3)
            cum_f = sum(_dot(tril_ref[0], part) for part in parts)
            cum_b = sum(_dot(tril_ref[1], part) for part in parts)
            bsh = pltpu.roll(jnp.where(bwd_lane, cum_b, cum_f), 128 - n_chain, 1)
            r = g - bsh
            pf = r
            pb = r
            k = 1
            while k < ML_CHUNK:
                pf = jnp.maximum(pf, jnp.where(ti >= k, pltpu.roll(pf, k, 0), NEG))
                pb = jnp.maximum(pb, jnp.where(ti < ML_CHUNK - k, pltpu.roll(pb, ML_CHUNK - k, 0), NEG))
                k *= 2
            pm_s[2 * i + half] = jnp.where(bwd_lane, pb, pf)
            b_s[2 * i + half] = bsh
            rt_s[2 * i + half] = r.T[0:n_chain, :]
        return carry

    n_chain = 2 * ML_H
    ti = lax.broadcasted_iota(jnp.int32, (ML_CHUNK, ML_CHUNK), 0)
    si = lax.broadcasted_iota(jnp.int32, (ML_CHUNK, ML_CHUNK), 1)
    bwd_lane = (si % n_chain) >= ML_H
    lax.fori_loop(0, S // CA, conv_body, 0)

    c_s[...] = jnp.zeros_like(c_s)
    m_s[...] = jnp.zeros_like(m_s)
    masks = (si <= ti, si >= ti)

    def scan_body(j, carry):
        chunk = (jnp.where(j < N_CTX_CHUNK, j + N_CHUNK - N_CTX_CHUNK, j - N_CTX_CHUNK), N_CHUNK - 1 - j)
        chains = []
        for d in range(2):
            r0 = pl.multiple_of(chunk[d] * ML_CHUNK, ML_CHUNK)
            p_col = pm_s[chunk[d]]
            bsh = b_s[chunk[d]]
            r_t = rt_s[chunk[d]]
            end = ML_CHUNK - 1 if d == 0 else 0
            for h in range(ML_H):
                c = d * ML_H + h
                sl = slice(h * HP, (h + 1) * HP)
                qc = q_s[pl.ds(r0, ML_CHUNK), sl]
                kt = kt_s[chunk[d], sl, :]
                vx = v_s[pl.ds(r0, ML_CHUNK), 2 * h * HP:(2 * h + 2) * HP]
                r_row = r_t[c:c + 1, :]
                m = m_s[c]
                st = c_s[c]
                big_m = jnp.maximum(m, jnp.broadcast_to(p_col[:, c:c + 1], (ML_CHUNK, HP)))
                b_b = jnp.broadcast_to(bsh[:, c:c + 1], (ML_CHUNK, HP))
                m_end = big_m[end:end + 1, :]
                ktw = (kt.astype(F32) * jnp.exp(r_row - m_end)).astype(BF16)
                chains.append(dict(d=d, r0=r0, sl=sl, c=c, vx=vx, r_row=r_row, m=m, st=st, big_m=big_m,
                                   b_b=b_b, m_end=m_end, end=end,
                                   qk=_dot(qc, kt), inter=_dot(qc, st.astype(BF16)), upd=_dot(ktw, vx)))
        for ch in chains:
            dw = jnp.exp(jnp.where(masks[ch["d"]], ch["r_row"] - ch["big_m"], NEG))
            ch["intra"] = _dot((ch["qk"] * dw).astype(BF16), ch["vx"])
        for ch in chains:
            m, big_m, inter, intra, end = ch["m"], ch["big_m"], ch["inter"], ch["intra"], ch["end"]
            iw = jnp.exp(m - big_m)
            num = iw * inter[:, :HP] + intra[:, :HP]
            nq = iw * inter[:, HP:] + intra[:, HP:]
            hv = num / jnp.maximum(jnp.abs(nq), jnp.exp(-(ch["b_b"] + big_m)))
            a = jnp.exp(m - ch["m_end"])
            ch["out"] = (hv, jnp.concatenate([a, a], axis=1) * ch["st"] + ch["upd"],
                         ch["b_b"][end:end + 1, :] + ch["m_end"])
        for ch in chains:
            hv, st_new, m_new = ch["out"]
            h_s[ch["d"], pl.ds(ch["r0"], ML_CHUNK), ch["sl"]] = hv
            c_s[ch["c"]] = st_new
            m_s[ch["c"]] = m_new
        return carry

    lax.fori_loop(0, N_CHUNK, scan_body, 0)

    live = (lax.broadcasted_iota(jnp.int32, (CA, HP), 1) < ML_D).astype(F32)

    def out_body(i, carry):
        r0 = pl.multiple_of(i * CA, CA)
        for h in range(ML_H):
            sl = slice(h * HP, (h + 1) * HP)
            hh = h_s[0, pl.ds(r0, CA), sl] + h_s[1, pl.ds(r0, CA), sl]
            mu = jnp.sum(hh, axis=-1, keepdims=True) * (1.0 / ML_D)
            dv = (hh - mu) * live
            var = jnp.sum(dv * dv, axis=-1, keepdims=True) * (1.0 / ML_D)
            hn = dv * lax.rsqrt(var + EPS) * ng_ref[:, sl]
            uc = uc_s[pl.ds(r0, CA), sl].astype(F32)
            zz = pu_ref[0, pl.ds(r0, CA), ML_WP + h * HP:ML_WP + (h + 1) * HP].astype(F32)
            o_ref[0, pl.ds(r0, CA), sl] = ((hn + sk_ref[:, sl] * uc) * _silu(zz)).astype(BF16)
        return carry

    lax.fori_loop(0, S // CA, out_body, 0)


def _mlstm(pu, pg, cw, cb, wq, wk, wv, gb, ng, sk, tril):
    B = pu.shape[0]
    full = lambda a: pl.BlockSpec(a.shape, lambda b, _n=a.ndim: (0,) * _n)
    n_chain = 2 * ML_H
    return pl.pallas_call(
        _mlstm_kernel,
        grid=(B,),
        in_specs=[pl.BlockSpec((1, S, 2 * ML_WP), lambda b: (b, 0, 0)),
                  pl.BlockSpec((1, S, 128), lambda b: (b, 0, 0)),
                  full(cw), full(cb), full(wq), full(wk), full(wv), full(gb), full(ng), full(sk), full(tril)],
        out_specs=pl.BlockSpec((1, S, ML_WP), lambda b: (b, 0, 0)),
        out_shape=jax.ShapeDtypeStruct((B, S, ML_WP), BF16),
        scratch_shapes=[pltpu.VMEM((S, ML_WP), BF16), pltpu.VMEM((S, ML_WP), BF16),
                        pltpu.VMEM((N_CHUNK, ML_WP, ML_CHUNK), BF16), pltpu.VMEM((S, 2 * ML_WP), BF16),
                        pltpu.VMEM((2, S, ML_WP), F32),
                        pltpu.VMEM((n_chain, HP, 2 * HP), F32),
                        pltpu.VMEM((n_chain, 1, HP), F32),
                        pltpu.VMEM((N_CHUNK, ML_CHUNK, 128), F32),
                        pltpu.VMEM((N_CHUNK, ML_CHUNK, 128), F32),
                        pltpu.VMEM((N_CHUNK, n_chain, ML_CHUNK), F32)],
        compiler_params=_cparams(("parallel",)),
        name="mlstm",
    )(pu, pg, cw, cb, wq, wk, wv, gb, ng, sk, tril)


def _na_kernel(idx_ref, p_ref, pt_ref, qg_ref, kg_ref, seg_ref, o_ref, kn_s, bias_s):
    j = pl.program_id(1)
    seg = seg_ref[...]

    def headnorm(x, g):
        ss = _dot((x * x).astype(BF16), seg)
        return x * lax.rsqrt(ss * (1.0 / NA_D) + EPS) * g

    @pl.when(j == 0)
    def _():
        def body(i, carry):
            r0 = pl.multiple_of(i * TT, TT)
            kk = p_ref[0, pl.ds(r0, TT), NA_W:2 * NA_W].astype(F32)
            kn_s[pl.ds(r0, TT), :] = headnorm(kk, kg_ref[...]).astype(BF16)
            return carry
        lax.fori_loop(0, NT, body, 0)

    scale = float(NA_D ** -0.5 * np.log2(np.e))
    kctx = kn_s[SEQ:S, :]
    vctx = p_ref[0, SEQ:S, 2 * NA_W:3 * NA_W]

    @pl.when(j < NA_NBLK)
    def _():
        q0 = pl.multiple_of(j * NA_QB, NA_QB)
        k0 = pl.multiple_of(jnp.clip(j * NA_QROWS - WIN_R // 2, 0, ROWS - NA_KROWS) * GRID_W, 256)
        q = headnorm(p_ref[0, pl.ds(q0, NA_QB), 0:NA_W].astype(F32), qg_ref[...]) * scale
        kl = kn_s[pl.ds(k0, NA_KB), :]
        vl = p_ref[0, pl.ds(k0, NA_KB), 2 * NA_W:3 * NA_W]
        head = lax.broadcasted_iota(jnp.int32, (NA_QB, NA_W), 1) // NA_D
        acc = jnp.zeros((NA_QB, NA_W), F32)

        def scores(h):
            qm = jnp.where(head == h, q, 0.0).astype(BF16)
            for i in range(NA_QROWS):
                for p in range(NA_KROWS // 2):
                    code = idx_ref[(j * NA_QROWS + i) * (NA_KROWS // 2) + p]
                    bias_s[i * GRID_W:(i + 1) * GRID_W, p * 2 * GRID_W:(p + 1) * 2 * GRID_W] = pt_ref[h, code]
            return _dot_nt(qm, kl) + bias_s[...].astype(F32), _dot_nt(qm, kctx)

        s_next = scores(0)
        for h in range(NA_H):
            s1, s2 = s_next
            if h + 1 < NA_H:
                s_next = scores(h + 1)
            m = jnp.maximum(jnp.max(s1, axis=-1, keepdims=True), jnp.max(s2, axis=-1, keepdims=True))
            p1 = jnp.exp2(s1 - m)
            p2 = jnp.exp2(s2 - m)
            l = jnp.sum(p1, axis=-1, keepdims=True) + jnp.sum(p2, axis=-1, keepdims=True)
            o = (_dot(p1.astype(BF16), vl) + _dot(p2.astype(BF16), vctx)) / l
            acc = jnp.where(head == h, o, acc)
        o_ref[0, pl.ds(q0, NA_QB), :] = acc.astype(BF16)

    @pl.when(j == NA_NBLK)
    def _():
        q = headnorm(p_ref[0, SEQ:S, 0:NA_W].astype(F32), qg_ref[...]) * scale
        head = lax.broadcasted_iota(jnp.int32, (CTX, NA_W), 1) // NA_D
        acc = jnp.zeros((CTX, NA_W), F32)
        for h in range(NA_H):
            qm = jnp.where(head == h, q, 0.0).astype(BF16)
            s2 = _dot_nt(qm, kctx)
            m = jnp.max(s2, axis=-1, keepdims=True)
            p2 = jnp.exp2(s2 - m)
            l = jnp.sum(p2, axis=-1, keepdims=True)
            o = _dot(p2.astype(BF16), vctx) / l
            acc = jnp.where(head == h, o, acc)
        o_ref[0, SEQ:S, :] = acc.astype(BF16)


def _na_attn(pair_idx, pna, pair_tiles, qg, kg, seg):
    B = pna.shape[0]
    return pl.pallas_call(
        _na_kernel,
        grid_spec=pltpu.PrefetchScalarGridSpec(
            num_scalar_prefetch=1,
            grid=(B, NA_NBLK + 1),
            in_specs=[pl.BlockSpec((1, S, 3 * NA_W), lambda b, j, idx: (b, 0, 0)),
                      pl.BlockSpec(pair_tiles.shape, lambda b, j, idx: (0, 0, 0, 0)),
                      pl.BlockSpec((1, NA_W), lambda b, j, idx: (0, 0)),
                      pl.BlockSpec((1, NA_W), lambda b, j, idx: (0, 0)),
                      pl.BlockSpec((NA_W, NA_W), lambda b, j, idx: (0, 0))],
            out_specs=pl.BlockSpec((1, S, NA_W), lambda b, j, idx: (b, 0, 0)),
            scratch_shapes=[pltpu.VMEM((S, NA_W), BF16), pltpu.VMEM((NA_QB, NA_KB), BF16)]),
        out_shape=jax.ShapeDtypeStruct((B, S, NA_W), BF16),
        compiler_params=_cparams(("parallel", "arbitrary")),
        name="na_attn",
    )(pair_idx, pna, pair_tiles, qg, kg, seg)


def _out_proj_kernel(z_ref, modx_ref, mody_ref, a_ref, m_ref, n_ref, wa_ref, wm_ref, wn_ref, g2_ref, rw_ref, rb_ref,
                     before_ref, below_ref, z1_ref, hs_ref, gs_ref, tm_ref, sm_ref):
    lane = lax.broadcasted_iota(jnp.int32, (TT, 128), 1)
    live = lane < N_EXPERTS
    groups = [slice(i * TT, (i + 1) * TT) for i in range(OT // TT)]

    splits = []
    for rows in groups:
        is_ctx = lax.broadcasted_iota(jnp.int32, (TT, 1), 0) + (pl.program_id(1) * OT + rows.start) >= SEQ
        mod = lambda i: jnp.where(is_ctx, mody_ref[0][:, i * D:(i + 1) * D], modx_ref[0][:, i * D:(i + 1) * D])
        mix = (_dot(a_ref[0, rows, :], wa_ref[...]) + _dot(m_ref[0, rows, :], wm_ref[...])
               + _dot(n_ref[0, rows, :], wn_ref[...]))
        x = z_ref[0, rows, :] + mod(2) * mix
        z1_ref[0, rows, :] = x
        hn = x * lax.rsqrt(jnp.mean(x * x, axis=-1, keepdims=True) + EPS) * g2_ref[...]
        hn = hn * (1.0 + mod(4)) + mod(3)
        h_hi, h_lo = _split_bf16(hn, 2)
        splits.append((h_hi, h_lo))

    affs = [_sigmoid(_dot(h_hi, rw_ref[0]) + (_dot(h_hi, rw_ref[1]) + _dot(h_lo, rw_ref[0])))
            for h_hi, h_lo in splits]

    def cyc(x, k, width):
        fwd = pltpu.roll(x, 128 - k, 1)
        back = pltpu.roll(x, width - k, 1)
        return jnp.where((lane % width) + k < width, fwd, back)

    def rank(x, width, step):
        r = jnp.zeros((TT, 128), F32)
        for k in range(1, width // step):
            y = cyc(x, k * step, width)
            wrapped = (lane % width) + k * step >= width
            beats = jnp.logical_or(y > x, jnp.logical_and(y == x, wrapped))
            r = r + beats.astype(F32)
        return r

    routed = []
    for aff in affs:
        sel = aff + rb_ref[...]
        top2 = rank(sel, EPG, 1) < 2.0
        part = jnp.where(top2, sel, 0.0)
        gscore = part
        for k in range(1, EPG):
            gscore = gscore + cyc(part, k, EPG)
        best = rank(gscore, N_EXPERTS, EPG) < 1.0
        chosen = jnp.logical_and(jnp.logical_and(top2, best), live)
        w = jnp.where(chosen, aff, 0.0)
        group = jnp.sum(jnp.where(chosen, (lane // EPG).astype(F32), 0.0), axis=-1, keepdims=True) * (1.0 / TOP_K)
        routed.append((w / jnp.sum(w, axis=-1, keepdims=True), group))

    sub = lax.broadcasted_iota(jnp.int32, (128, TT), 0)
    row_id = lax.broadcasted_iota(jnp.int32, (SORT_ROWS, TT), 0).astype(F32)
    for gi, ((gate, group), (h_hi, _)) in enumerate(zip(routed, splits)):
        member_t = (lane.astype(F32) == group).astype(F32).T
        member_t = jnp.where(sub < N_GROUPS, member_t, 0.0)
        ahead = _dot(member_t.astype(BF16), before_ref[...])
        count = jnp.sum(member_t, axis=1, keepdims=True)
        padded = ((count.astype(jnp.int32) + (SORT_BLK - 1)) & -SORT_BLK).astype(F32)
        start = _dot(below_ref[...], jnp.broadcast_to(padded, (128, 128)).astype(BF16))
        where_t = jnp.sum(member_t * (start[:, 0:1] + ahead), axis=0, keepdims=True)
        perm = (row_id == where_t).astype(BF16)
        hs_ref[0, gi] = _dot(perm, h_hi)
        gs_ref[0, gi] = sum(_dot(perm, part) for part in _split_bf16(gate, 3))
        tm_ref[0, gi, 0:1, :] = where_t
        tm_ref[0, gi, 1:8, :] = jnp.zeros((7, TT), F32)
        lane_m = lax.broadcasted_iota(jnp.int32, (128, 128), 1)
        sm_ref[0, gi] = jnp.where(lane_m == 0, jnp.broadcast_to(padded, (128, 128)), start)


def _out_proj(z, mods, mla_o, ml_o, na_o, wa, wm, wn, g2, rw, rb):
    B = z.shape[0]
    tok = lambda w_: pl.BlockSpec((1, OT, w_), lambda b, t: (b, t, 0))
    full = lambda a: pl.BlockSpec(a.shape, lambda b, t, _n=a.ndim: (0,) * _n)
    per_group = lambda r, w_: pl.BlockSpec((1, OT // TT, r, w_), lambda b, t: (b, t, 0, 0))
    grouped = lambda r, w_: jax.ShapeDtypeStruct((B, NT, r, w_), F32)
    before = jnp.asarray(np.triu(np.ones((TT, TT)), 1), BF16)
    below = jnp.asarray(np.tril(np.ones((128, 128)), -1), BF16)
    return pl.pallas_call(
        _out_proj_kernel,
        grid=(B, S // OT),
        in_specs=[tok(D),
                  pl.BlockSpec((1, 1, 6 * D), lambda b, t: (2 * b, 0, 0)),
                  pl.BlockSpec((1, 1, 6 * D), lambda b, t: (2 * b + 1, 0, 0)),
                  tok(MLA_H * V_D), tok(ML_WP), tok(NA_W),
                  full(wa), full(wm), full(wn), full(g2), full(rw), full(rb), full(before), full(below)],
        out_specs=[tok(D), per_group(SORT_ROWS, D), per_group(SORT_ROWS, 128), per_group(8, TT), per_group(128, 128)],
        out_shape=[jax.ShapeDtypeStruct((B, S, D), F32),
                   grouped(SORT_ROWS, D), grouped(SORT_ROWS, 128), grouped(8, TT), grouped(128, 128)],
        compiler_params=_cparams(("parallel", "parallel")),
        name="out_proj",
    )(z, mods, mods, mla_o, ml_o, na_o, wa, wm, wn, g2, rw, rb, before, below)


ROW_WAIT = 64


def _row_copy_start(n, row_copy):
    def issue(i, carry):
        row_copy(i).start()
        return carry

    lax.fori_loop(0, n, issue, 0, unroll=8)


def _row_copy_wait(n, slab_copy):
    def drain(i, carry):
        slab_copy(ROW_WAIT).wait()
        return carry

    lax.fori_loop(0, n // ROW_WAIT, drain, 0)


def _experts_kernel(blk_ref, tg_ref, hs_ref, gs_ref, w1_ref, w3_ref, w2_ref, ys_ref, hbuf, gbuf, sems):
    i = pl.program_id(0)
    n_blk = GROUP_T // SORT_BLK

    def fetch(tile, slot):
        def src(q):
            return pl.ds(pl.multiple_of(blk_ref[tile * n_blk + q] * SORT_BLK, SORT_BLK), SORT_BLK)

        def dst(q):
            return pl.ds(pl.multiple_of(q * SORT_BLK, SORT_BLK), SORT_BLK)

        _row_copy_start(n_blk, lambda q: pltpu.make_async_copy(hs_ref.at[src(q)], hbuf.at[slot, dst(q)], sems.at[0, slot]))
        _row_copy_start(n_blk, lambda q: pltpu.make_async_copy(gs_ref.at[src(q)], gbuf.at[slot, dst(q)], sems.at[1, slot]))

    @pl.when(i == 0)
    def _():
        fetch(0, 0)

    slot = i % 2
    _row_copy_wait(GROUP_T, lambda k: pltpu.make_async_copy(
        hs_ref.at[pl.ds(0, k)], hbuf.at[slot, pl.ds(0, k)], sems.at[0, slot]))
    _row_copy_wait(GROUP_T, lambda k: pltpu.make_async_copy(
        gs_ref.at[pl.ds(0, k)], gbuf.at[slot, pl.ds(0, k)], sems.at[1, slot]))

    @pl.when(i + 1 < pl.num_programs(0))
    def _():
        fetch(i + 1, 1 - slot)

    g = tg_ref[i]
    x = hbuf[slot].astype(BF16)
    gates = gbuf[slot]
    lane = lax.broadcasted_iota(jnp.int32, (GROUP_T, 128), 1)
    ups = [(_dot(x, w1_ref[0, e].astype(BF16)), _dot(x, w3_ref[0, e].astype(BF16))) for e in range(EPG)]
    acc = jnp.zeros((GROUP_T, D), F32)
    for e in range(EPG):
        ge = jnp.sum(jnp.where(lane == g * EPG + e, gates, 0.0), axis=-1, keepdims=True)
        a, b = ups[e]
        acc = acc + _dot((_silu(a) * b * ge).astype(BF16), w2_ref[0, e].astype(BF16))
    ys_ref[...] = acc


def _experts(src_blk, tile_group, hs, gs, n_rows, w1, w3, w2, l):
    wspec = lambda k, n: pl.BlockSpec((1, EPG, k, n), lambda i, blk, tg: (l, tg[i], 0, 0))
    return pl.pallas_call(
        _experts_kernel,
        grid_spec=pltpu.PrefetchScalarGridSpec(
            num_scalar_prefetch=2,
            grid=(n_rows // GROUP_T,),
            in_specs=[pl.BlockSpec(memory_space=pl.ANY), pl.BlockSpec(memory_space=pl.ANY),
                      wspec(D, D_FF), wspec(D, D_FF), wspec(D_FF, D)],
            out_specs=pl.BlockSpec((GROUP_T, D), lambda i, blk, tg: (i, 0)),
            scratch_shapes=[pltpu.VMEM((2, GROUP_T, D), F32), pltpu.VMEM((2, GROUP_T, 128), F32),
                            pltpu.SemaphoreType.DMA((2, 2))]),
        out_shape=jax.ShapeDtypeStruct((n_rows, D), F32),
        compiler_params=_cparams(("arbitrary",)),
        name="moe_experts",
    )(src_blk, tile_group, hs, gs, w1, w3, w2)


def _combine_kernel(blk_ref, z1_ref, modx_ref, mody_ref, tm_ref, ys_ref, o_ref, buf, sems):
    b, t = pl.program_id(0), pl.program_id(1)
    n_t = pl.num_programs(1)
    step = b * n_t + t
    n_blk = SORT_ROWS // SORT_BLK

    def fetch(bb, tt, slot):
        base = (bb * NT + tt) * n_blk
        _row_copy_start(n_blk, lambda q: pltpu.make_async_copy(
            ys_ref.at[pl.ds(pl.multiple_of(blk_ref[base + q] * SORT_BLK, SORT_BLK), SORT_BLK)],
            buf.at[slot, pl.ds(pl.multiple_of(q * SORT_BLK, SORT_BLK), SORT_BLK)], sems.at[slot]))

    @pl.when(step == 0)
    def _():
        fetch(0, 0, 0)

    slot = step % 2
    _row_copy_wait(SORT_ROWS, lambda k: pltpu.make_async_copy(
        ys_ref.at[pl.ds(0, k)], buf.at[slot, pl.ds(0, k)], sems.at[slot]))

    @pl.when(step + 1 < pl.num_programs(0) * n_t)
    def _():
        wrap = t + 1 == n_t
        fetch(jnp.where(wrap, b + 1, b), jnp.where(wrap, 0, t + 1), 1 - slot)

    row_id = lax.broadcasted_iota(jnp.int32, (SORT_ROWS, TT), 0).astype(F32)
    perm = (row_id == tm_ref[0, 0, 0:1, :]).astype(BF16)
    y = sum(_dot_tn(perm, part) for part in _split_bf16(buf[slot], 2))
    g2 = jnp.where(t == NT - 1, mody_ref[0][:, 5 * D:], modx_ref[0][:, 5 * D:])
    o_ref[0] = z1_ref[0] + g2 * y


def _combine(back_blk, z1, mods, tok_meta, ys, n_t):
    B = z1.shape[0]
    tok = pl.BlockSpec((1, TT, D), lambda b, t, blk: (b, t, 0))
    return pl.pallas_call(
        _combine_kernel,
        grid_spec=pltpu.PrefetchScalarGridSpec(
            num_scalar_prefetch=1,
            grid=(B, n_t),
            in_specs=[tok,
                      pl.BlockSpec((1, 1, 6 * D), lambda b, t, blk: (2 * b, 0, 0)),
                      pl.BlockSpec((1, 1, 6 * D), lambda b, t, blk: (2 * b + 1, 0, 0)),
                      pl.BlockSpec((1, 1, 8, TT), lambda b, t, blk: (b, t, 0, 0)),
                      pl.BlockSpec(memory_space=pl.ANY)],
            out_specs=tok,
            scratch_shapes=[pltpu.VMEM((2, SORT_ROWS, D), F32), pltpu.SemaphoreType.DMA((2,))]),
        out_shape=jax.ShapeDtypeStruct((B, n_t * TT, D), F32),
        compiler_params=_cparams(("arbitrary", "arbitrary")),
        name="moe_combine",
    )(back_blk, z1, mods, mods, tok_meta, ys)


def _group_layout(seg_meta, n_rows):
    n_tg = seg_meta.shape[0]
    seg_len = seg_meta[:, :N_GROUPS, 0].astype(jnp.int32).T
    seg_local = seg_meta[:, :N_GROUPS, 1].astype(jnp.int32).T
    total = jnp.sum(seg_len, axis=1)
    padded = (total + GROUP_T - 1) // GROUP_T * GROUP_T
    group_end = jnp.cumsum(padded)
    seg_start = (group_end - padded)[:, None] + jnp.cumsum(seg_len, axis=1) - seg_len
    flat_start, flat_len, flat_local = seg_start.reshape(-1), seg_len.reshape(-1), seg_local.reshape(-1)

    row0 = (jnp.arange(n_rows // SORT_BLK, dtype=jnp.int32) * SORT_BLK)[:, None]
    covers = (row0 >= flat_start[None, :]) & (row0 < (flat_start + flat_len)[None, :])
    token_group = (jnp.arange(flat_start.shape[0], dtype=jnp.int32) % n_tg)[None, :]
    src_row = jnp.sum(jnp.where(covers, token_group * SORT_ROWS + flat_local[None, :] + row0 - flat_start[None, :], 0),
                      axis=1)
    filled = jnp.any(covers, axis=1)
    src_blk = jnp.where(filled, src_row // SORT_BLK, SORT_ROWS // SORT_BLK - 1)

    tile_group = jnp.sum(jnp.arange(n_rows // GROUP_T)[:, None] * GROUP_T >= group_end[None, :], axis=1)

    local0 = (jnp.arange(SORT_ROWS // SORT_BLK, dtype=jnp.int32) * SORT_BLK)[None, None, :]
    lo, ln, st = seg_local[:, :, None], seg_len[:, :, None], seg_start[:, :, None]
    back_row = jnp.sum(jnp.where((local0 >= lo) & (local0 < lo + ln), st + local0 - lo, 0), axis=0)
    return (src_blk.astype(jnp.int32), jnp.minimum(tile_group, N_GROUPS - 1).astype(jnp.int32),
            (back_row // SORT_BLK).reshape(-1).astype(jnp.int32))


def _moe(z1, hs, gs, tok_meta, seg_meta, mods, w1, w3, w2, l):
    B = z1.shape[0]
    n_tg = B * NT
    n_rows = (-(-(n_tg * (TT + N_GROUPS * (SORT_BLK - 1))) // GROUP_T) + N_GROUPS) * GROUP_T
    src_blk, tile_group, back_blk = _group_layout(seg_meta.reshape(n_tg, 128, 128), n_rows)
    ys = _experts(src_blk, tile_group, hs.reshape(n_tg * SORT_ROWS, D), gs.reshape(n_tg * SORT_ROWS, 128),
                  n_rows, w1, w3, w2, l)
    return _combine(back_blk, z1, mods, tok_meta, ys, NT - 1 if l == DEPTH - 1 else NT)


def _in_proj_layout(w):
    cuts = np.cumsum([Q_RANK, KV_RANK, ROPE_D, ML_W, ML_W, 4 * ML_H])
    qc, ckv, kr, u, zz, g, na = jnp.split(w, [int(v) for v in cuts], axis=-1)
    zeros = lambda n: jnp.zeros((w.shape[0], n), w.dtype)
    out = jnp.concatenate([qc, ckv, zeros(NOPE_D), kr, zeros(HP - QK_D),
                           _pad_heads(u, ML_H, ML_D, HP), _pad_heads(zz, ML_H, ML_D, HP),
                           _gate_order(g), zeros(128 - 4 * ML_H), na], axis=-1)
    assert out.shape[-1] == NP_IN
    return out


def _gate_order(g):
    i_f, f_f, i_b, f_b = jnp.split(g, 4, axis=-1)
    return jnp.concatenate([i_f, i_b, f_f, f_b], axis=-1)


def _pad_heads(v, nh, d, dp):
    lead = v.shape[:-1]
    v = v.reshape(lead + (nh, d))
    v = jnp.pad(v, [(0, 0)] * len(lead) + [(0, 0), (0, dp - d)])
    return v.reshape(lead + (nh * dp,))


def _rope_tables():
    t = np.arange(SEQ)
    row = (t // GRID_W).astype(np.float32)
    col = (t % GRID_W).astype(np.float32)
    quarter = ROPE_D // 4
    inv = jnp.asarray(ROPE_BASE, F32) ** (-jnp.arange(quarter, dtype=F32) / quarter)
    ar = jnp.asarray(row)[:, None] * inv
    ac = jnp.asarray(col)[:, None] * inv
    ang = jnp.concatenate([ar, ar, ac, ac], axis=-1)
    cos = jnp.ones((S, HP), F32).at[:SEQ, NOPE_D:QK_D].set(jnp.cos(ang))
    sin = jnp.zeros((S, HP), F32).at[:SEQ, NOPE_D:QK_D].set(jnp.sin(ang))
    return cos, sin


def _rotate_half_index():
    q = ROPE_D // 4
    src = np.arange(QK_D)
    sign = np.zeros((QK_D,), np.float32)
    for blk in range(2):
        lo = NOPE_D + 2 * q * blk
        src[lo:lo + q] = np.arange(lo + q, lo + 2 * q)
        sign[lo:lo + q] = -1.0
        src[lo + q:lo + 2 * q] = np.arange(lo, lo + q)
        sign[lo + q:lo + 2 * q] = 1.0
    return src, sign


def _rotate_half(w):
    src, sign = _rotate_half_index()
    return w[..., src] * sign


NA_NDR = 2 * WIN_R - 1
NA_NPAIR = 3 * NA_NDR


def _na_pair_index():
    idx = np.zeros((NA_NBLK, NA_QROWS, NA_KROWS // 2), np.int32)
    for blk in range(NA_NBLK):
        k0 = int(np.clip(blk * NA_QROWS - WIN_R // 2, 0, ROWS - NA_KROWS))
        for i in range(NA_QROWS):
            qr = blk * NA_QROWS + i
            rs = int(np.clip(qr - WIN_R // 2, 0, ROWS - WIN_R))
            assert k0 <= rs and rs + WIN_R <= k0 + NA_KROWS
            for p in range(NA_KROWS // 2):
                kr = k0 + 2 * p
                dr = kr - qr + WIN_R - 1
                left = rs <= kr < rs + WIN_R
                right = rs <= kr + 1 < rs + WIN_R
                if left and right:
                    idx[blk, i, p] = 1 + dr
                elif left:
                    idx[blk, i, p] = NA_NDR + dr
                elif right:
                    idx[blk, i, p] = 2 * NA_NDR + dr + 1
    return idx.reshape(-1)


def _na_pair_tiles(rpb):
    cq = np.arange(GRID_W)
    cs = np.clip(cq - WIN_C // 2, 0, GRID_W - WIN_C)
    col_ok = (cq[None, :] >= cs[:, None]) & (cq[None, :] < cs[:, None] + WIN_C)
    dc = np.clip(cq[None, :] - cq[:, None], -(WIN_C - 1), WIN_C - 1) + (WIN_C - 1)
    onehot = jnp.asarray(np.eye(2 * WIN_C - 1, dtype=np.float32)[dc])
    tiles = jnp.einsum('hrc,qkc->hrqk', rpb, onehot, precision=lax.Precision.HIGHEST)
    tiles = jnp.where(jnp.asarray(col_ok), tiles * float(np.log2(np.e)), NEG)
    masked = jnp.full_like(tiles, NEG)
    both = jnp.concatenate([tiles[:, :-1], tiles[:, 1:]], axis=-1)
    left = jnp.concatenate([tiles, masked], axis=-1)
    right = jnp.concatenate([masked, tiles], axis=-1)
    none = jnp.concatenate([masked[:, :1], masked[:, :1]], axis=-1)
    out = jnp.concatenate([none, both, left, right], axis=1)
    assert out.shape[1] == NA_NPAIR
    return out.astype(BF16)


def kernel(x, c, ctx, c_ctx, w_mod, b_mod, norm1_g, norm2_g, w_in, w_out, mla_qnorm_g, mla_w_uq, mla_kvnorm_g, mla_w_ukv, mla_q_g, mla_k_g, ml_conv_w, ml_conv_b, ml_w_q, ml_w_k, ml_w_v, ml_gate_b, ml_norm_g, ml_skip, na_q_g, na_k_g, na_rpb, router_w, router_b, moe_w1, moe_w3, moe_w2):
    B = x.shape[0]
    z = jnp.concatenate([x, ctx], axis=1)
    cc = jnp.zeros((16, D), F32).at[:B].set(c).at[B].set(c_ctx)
    mod_all = _modulation(cc, w_mod, b_mod)
    cos, sin = _rope_tables()
    rot_src, rot_sign = _rotate_half_index()
    rot_np = np.zeros((HP, HP), np.float32)
    rot_np[rot_src, np.arange(QK_D)] = rot_sign
    rot_mat = jnp.asarray(rot_np, BF16)
    ones_hp = jnp.ones((HP, HP), BF16)
    seg = jnp.asarray(np.kron(np.eye(NA_H), np.ones((NA_D, NA_D))), BF16)
    tril = jnp.asarray(np.stack([np.tril(np.ones((ML_CHUNK, ML_CHUNK))), np.triu(np.ones((ML_CHUNK, ML_CHUNK)))]), BF16)
    pair_idx = jnp.asarray(_na_pair_index())
    rw = jnp.stack(_split_bf16(jnp.pad(router_w, ((0, 0), (0, 128 - N_EXPERTS))), 2))
    rb = jnp.pad(router_b, (0, 128 - N_EXPERTS), constant_values=NEG).reshape(1, 128)

    def pad_lanes(v, n):
        return jnp.pad(v, [(0, 0)] * (v.ndim - 1) + [(0, n - v.shape[-1])])

    for l in range(DEPTH):
        mx = mod_all[l, :B]
        my = jnp.broadcast_to(mod_all[l, B], (B, 6 * D))
        mods = jnp.stack([mx, my], axis=1).reshape(2 * B, 1, 6 * D)

        w_in_p = _in_proj_layout(w_in[l]).astype(BF16)
        pmla, pu, pg, pna = _in_proj(z, mods, norm1_g[l].reshape(1, D), w_in_p)

        wq = jnp.transpose(mla_w_uq[l].reshape(Q_RANK, MLA_H, QK_D), (1, 0, 2))
        wuq = jnp.concatenate([pad_lanes(wq, HP), pad_lanes(_rotate_half(wq), HP)], axis=-1).astype(BF16)
        wukv = jnp.transpose(mla_w_ukv[l].reshape(KV_RANK, MLA_H, NOPE_D + V_D), (1, 0, 2))
        wuk = pad_lanes(wukv[..., :NOPE_D], HP).astype(BF16)
        wuv = pad_lanes(wukv[..., NOPE_D:], HP).astype(BF16)
        qg, kg = mla_q_g[l], mla_k_g[l]
        q_scale = float(QK_D ** -0.5 * np.log2(np.e))
        tabs = jnp.stack([cos * pad_lanes(qg, HP) * q_scale, sin * pad_lanes(jnp.abs(rot_sign) * qg[rot_src], HP) * q_scale,
                          cos * pad_lanes(kg, HP), sin * pad_lanes(jnp.abs(rot_sign) * kg[rot_src], HP)])
        heads_last = lambda w_: jnp.transpose(w_, (1, 0, 2)).reshape(w_.shape[1], -1)
        q, k, v = _mla_prep(pmla, tabs, mla_qnorm_g[l].reshape(1, Q_RANK), heads_last(wuq),
                            mla_kvnorm_g[l].reshape(1, KV_RANK), heads_last(wuk), heads_last(wuv), rot_mat, ones_hp)
        mla_o = _mla_attn(q, k, v)

        padh = lambda a: _pad_heads(a, ML_H, ML_D, HP)
        padw = lambda w_: jnp.pad(w_, ((0, 0), (0, HP - ML_D), (0, HP - ML_D))).astype(BF16)
        cw = jnp.pad(padh(ml_conv_w[l]), ((0, 8 - 3), (0, 0)))
        ml_o = _mlstm(pu, pg, cw, padh(ml_conv_b[l]).reshape(1, ML_WP),
                      padw(ml_w_q[l]), padw(jnp.swapaxes(ml_w_k[l], 1, 2) * (ML_D ** -0.5)), padw(ml_w_v[l]),
                      pad_lanes(_gate_order(ml_gate_b[l]).reshape(1, 4 * ML_H), 128),
                      padh(ml_norm_g[l]).reshape(1, ML_WP), padh(ml_skip[l]).reshape(1, ML_WP), tril)

        na_o = _na_attn(pair_idx, pna, _na_pair_tiles(na_rpb[l]), jnp.tile(na_q_g[l], NA_H).reshape(1, NA_W),
                        jnp.tile(na_k_g[l], NA_H).reshape(1, NA_W), seg)

        wo = w_out[l]
        wa = wo[:MLA_H * V_D].astype(BF16)
        wm = jnp.pad(wo[MLA_H * V_D:MLA_H * V_D + ML_W].reshape(ML_H, ML_D, D),
                     ((0, 0), (0, HP - ML_D), (0, 0))).reshape(ML_WP, D).astype(BF16)
        wn = wo[MLA_H * V_D + ML_W:].astype(BF16)
        z1, hs, gs, tok_meta, seg_meta = _out_proj(z, mods, mla_o, ml_o, na_o, wa, wm, wn,
                                                   norm2_g[l].reshape(1, D), rw, rb)
        z = _moe(z1, hs, gs, tok_meta, seg_meta, mods, moe_w1, moe_w3, moe_w2, l)

    return z
```

```python
import functools

import numpy as np
import jax
import jax.numpy as jnp
from jax import lax
from jax.experimental import pallas as pl
from jax.experimental.pallas import tpu as pltpu

F32 = jnp.float32
BF16 = jnp.bfloat16

D = 1024
SEQ = 2048
CTX = 256
S = SEQ + CTX
DEPTH = 4
GRID_W = 64
ROWS = SEQ // GRID_W
EPS = 1e-6

MLA_H = 6
Q_RANK = 256
KV_RANK = 128
NOPE_D = 64
ROPE_D = 32
V_D = 64
QK_D = NOPE_D + ROPE_D
ROPE_BASE = 10000.0

ML_H = 4
ML_D = 96
ML_W = ML_H * ML_D
HP = 128
ML_WP = ML_H * HP
ML_CHUNK = 128
N_CHUNK = S // ML_CHUNK
N_CTX_CHUNK = CTX // ML_CHUNK

NA_H = 4
NA_D = 64
NA_W = NA_H * NA_D
WIN_R = 8
WIN_C = 16
NA_QROWS = 4
NA_KROWS = 12
NA_QB = NA_QROWS * GRID_W
NA_KB = NA_KROWS * GRID_W
NA_NBLK = ROWS // NA_QROWS

N_EXPERTS = 16
N_GROUPS = 4
EPG = N_EXPERTS // N_GROUPS
D_FF = 256

TT = 256
NT = S // TT
OT = 768
TOP_K = 2
SORT_BLK = 8
SORT_ROWS = 320
GROUP_T = 512
NEG = -1e30

C_QC = 0
C_CKV = 256
C_KR = 384
C_U = 512
C_Z = C_U + ML_WP
C_G = C_Z + ML_WP
C_NA = C_G + 128
NP_IN = C_NA + 3 * NA_W

VMEM_LIMIT = 56 * 1024 * 1024


def _cparams(sem):
    return pltpu.CompilerParams(dimension_semantics=sem, vmem_limit_bytes=VMEM_LIMIT)


def _sigmoid(x):
    return 1.0 / (1.0 + jnp.exp(-x))


def _silu(x):
    return x * _sigmoid(x)


def _dot(a, b):
    return jnp.dot(a, b, preferred_element_type=F32)


def _dot_nt(a, b):
    return lax.dot_general(a, b, (((1,), (1,)), ((), ())), preferred_element_type=F32)


def _dot_tn(a, b):
    return lax.dot_general(a, b, (((0,), (0,)), ((), ())), preferred_element_type=F32)


def _dot_hi(a, b):
    return jnp.dot(a, b, preferred_element_type=F32, precision=lax.Precision.HIGHEST)


def _split_bf16(x, n):
    parts = []
    for _ in range(n):
        p = x.astype(BF16)
        parts.append(p)
        x = x - p.astype(F32)
    return parts


def _mod_rows(mod_ref, t):
    m = mod_ref[0]
    return [m[:, i * D:(i + 1) * D] for i in range(6)]


def _mod_kernel(c_ref, w_ref, b_ref, o_ref):
    sc = _silu(c_ref[...])
    o_ref[0] = _dot_hi(sc, w_ref[0]) + b_ref[0]


def _modulation(cc, w_mod, b_mod):
    nc = 6
    return pl.pallas_call(
        _mod_kernel,
        grid=(DEPTH, nc),
        in_specs=[pl.BlockSpec((16, D), lambda l, j: (0, 0)),
                  pl.BlockSpec((1, D, D), lambda l, j: (l, 0, j)),
                  pl.BlockSpec((1, 1, D), lambda l, j: (l, 0, j))],
        out_specs=pl.BlockSpec((1, 16, D), lambda l, j: (l, 0, j)),
        out_shape=jax.ShapeDtypeStruct((DEPTH, 16, 6 * D), F32),
        compiler_params=_cparams(("parallel", "parallel")),
        name="modulation",
    )(cc, w_mod, b_mod.reshape(DEPTH, 1, 6 * D))


def _in_proj_kernel(z_ref, mod_ref, g_ref, w_ref, pmla_ref, pu_ref, pg_ref, pna_ref):
    sh1, sc1 = _mod_rows(mod_ref, None)[:2]
    x = z_ref[0]
    xn = x * lax.rsqrt(jnp.mean(x * x, axis=-1, keepdims=True) + EPS) * g_ref[...]
    xn = xn * (1.0 + sc1) + sh1
    p = _dot(xn.astype(BF16), w_ref[...])
    pmla_ref[0] = p[:, :C_U].astype(BF16)
    pu_ref[0] = p[:, C_U:C_G].astype(BF16)
    pg_ref[0] = p[:, C_G:C_NA]
    pna_ref[0] = p[:, C_NA:].astype(BF16)


def _mod_spec():
    return pl.BlockSpec((1, 1, 6 * D), lambda b, t: (2 * b + t // (NT - 1), 0, 0))


def _in_proj(z, mods, g, w):
    B = z.shape[0]
    tok = lambda w_: pl.BlockSpec((1, TT, w_), lambda b, t: (b, t, 0))
    return pl.pallas_call(
        _in_proj_kernel,
        grid=(B, NT),
        in_specs=[tok(D), _mod_spec(),
                  pl.BlockSpec((1, D), lambda b, t: (0, 0)),
                  pl.BlockSpec((D, NP_IN), lambda b, t: (0, 0))],
        out_specs=[tok(C_U), tok(2 * ML_WP), tok(128), tok(3 * NA_W)],
        out_shape=[jax.ShapeDtypeStruct((B, S, C_U), BF16),
                   jax.ShapeDtypeStruct((B, S, 2 * ML_WP), BF16),
                   jax.ShapeDtypeStruct((B, S, 128), F32),
                   jax.ShapeDtypeStruct((B, S, 3 * NA_W), BF16)],
        compiler_params=_cparams(("parallel", "parallel")),
        name="in_proj",
    )(z, mods, g, w)


def _mla_prep_kernel(p_ref, tab_ref, qng_ref, wuq_ref, kvng_ref, wuk_ref, wuv_ref, rot_ref, ones_ref,
                     q_out, k_out, v_out):
    p = p_ref[0].astype(F32)
    qc = p[:, C_QC:C_CKV]
    ckv = p[:, C_CKV:C_KR]
    kr = p[:, C_KR:C_U]
    qcn = (qc * lax.rsqrt(jnp.mean(qc * qc, axis=-1, keepdims=True) + EPS) * qng_ref[...]).astype(BF16)
    ckvn = (ckv * lax.rsqrt(jnp.mean(ckv * ckv, axis=-1, keepdims=True) + EPS) * kvng_ref[...]).astype(BF16)
    lane = lax.broadcasted_iota(jnp.int32, (TT, HP), 1)
    ones = ones_ref[...]
    kr_rot = _dot(kr.astype(BF16), rot_ref[...])

    q_all = _dot(qcn, wuq_ref[...])
    k_all = _dot(ckvn, wuk_ref[...])
    v_all = _dot(ckvn, wuv_ref[...])
    qs = [q_all[:, 2 * h * HP:(2 * h + 1) * HP] for h in range(MLA_H)]
    q_rots = [q_all[:, (2 * h + 1) * HP:(2 * h + 2) * HP] for h in range(MLA_H)]
    ks = [k_all[:, h * HP:(h + 1) * HP] + kr for h in range(MLA_H)]
    ss_q = [_dot((x * x).astype(BF16), ones) for x in qs]
    ss_k = [_dot((x * x).astype(BF16), ones) for x in ks]

    def norm_rope(x, x_rot, ss, cos_g, sin_g):
        return lax.rsqrt(ss * (1.0 / QK_D) + EPS) * (x * cos_g + x_rot * sin_g)

    for h in range(MLA_H):
        q_out[0, h] = norm_rope(qs[h], q_rots[h], ss_q[h], tab_ref[0], tab_ref[1]).astype(BF16)
        k_out[0, h] = norm_rope(ks[h], kr_rot, ss_k[h], tab_ref[2], tab_ref[3]).astype(BF16)
        v_out[0, h] = jnp.where(lane < V_D, v_all[:, h * HP:(h + 1) * HP], 1.0).astype(BF16)


def _mla_prep(pmla, tabs, qng, wuq, kvng, wuk, wuv, rot, ones):
    B = pmla.shape[0]
    full = lambda a: pl.BlockSpec(a.shape, lambda b, t, _n=a.ndim: (0,) * _n)
    hd = lambda w_: pl.BlockSpec((1, MLA_H, TT, w_), lambda b, t: (b, 0, t, 0))
    return pl.pallas_call(
        _mla_prep_kernel,
        grid=(B, NT),
        in_specs=[pl.BlockSpec((1, TT, C_U), lambda b, t: (b, t, 0)),
                  pl.BlockSpec((4, TT, HP), lambda b, t: (0, t, 0)),
                  full(qng), full(wuq), full(kvng), full(wuk), full(wuv), full(rot), full(ones)],
        out_specs=[hd(HP), hd(HP), hd(HP)],
        out_shape=[jax.ShapeDtypeStruct((B, MLA_H, S, HP), BF16),
                   jax.ShapeDtypeStruct((B, MLA_H, S, HP), BF16),
                   jax.ShapeDtypeStruct((B, MLA_H, S, HP), BF16)],
        compiler_params=_cparams(("parallel", "parallel")),
        name="mla_prep",
    )(pmla, tabs, qng, wuq, kvng, wuk, wuv, rot, ones)


def _mla_attn_kernel(q_ref, k_ref, v_ref, o_ref):
    t = pl.program_id(1)

    def attend(k_lo, k_n):
        def scores(h):
            return _dot_nt(q_ref[0, h], k_ref[0, h, k_lo:k_lo + k_n, :])

        outs = []
        s_next = scores(0)
        for h in range(MLA_H):
            s = s_next
            if h + 1 < MLA_H:
                s_next = scores(h + 1)
            m = jnp.max(s, axis=-1, keepdims=True)
            p = jnp.exp2(s - m)
            pv = _dot(p.astype(BF16), v_ref[0, h, k_lo:k_lo + k_n, :])
            outs.append(pv[:, :V_D] / pv[:, V_D:V_D + 1])
        o_ref[0] = jnp.concatenate(outs, axis=-1).astype(BF16)

    @pl.when(t < NT - 1)
    def _():
        attend(0, S)

    @pl.when(t == NT - 1)
    def _():
        attend(SEQ, CTX)


def _mla_attn(q, k, v):
    B = q.shape[0]
    return pl.pallas_call(
        _mla_attn_kernel,
        grid=(B, NT),
        in_specs=[pl.BlockSpec((1, MLA_H, TT, HP), lambda b, t: (b, 0, t, 0)),
                  pl.BlockSpec((1, MLA_H, S, HP), lambda b, t: (b, 0, 0, 0)),
                  pl.BlockSpec((1, MLA_H, S, HP), lambda b, t: (b, 0, 0, 0))],
        out_specs=pl.BlockSpec((1, TT, MLA_H * V_D), lambda b, t: (b, t, 0)),
        out_shape=jax.ShapeDtypeStruct((B, S, MLA_H * V_D), BF16),
        compiler_params=_cparams(("parallel", "arbitrary")),
        name="mla_attn",
    )(q, k, v)


def _log_sigmoid(x):
    return jnp.minimum(x, 0.0) - jnp.log(1.0 + jnp.exp(-jnp.abs(x)))


def _mlstm_kernel(pu_ref, pg_ref, cw_ref, cb_ref, wq_ref, wk_ref, wv_ref, gb_ref, ng_ref, sk_ref,
                  tril_ref, o_ref, uc_s, q_s, kt_s, v_s, h_s, c_s, m_s, pm_s, b_s, rt_s):
    CA = 2 * ML_CHUNK
    row = lax.broadcasted_iota(jnp.int32, (CA, ML_WP), 0)

    def conv_body(i, carry):
        r0 = pl.multiple_of(i * CA, CA)
        x = pu_ref[0, pl.ds(r0, CA), 0:ML_WP].astype(F32)
        pr = pl.multiple_of(jnp.maximum(r0 - 16, 0), 16)
        nx = pl.multiple_of(jnp.minimum(r0 + CA, S - 16), 16)
        prev = pu_ref[0, pl.ds(pr, 16), 0:ML_WP].astype(F32)[15:16, :]
        nxt = pu_ref[0, pl.ds(nx, 16), 0:ML_WP].astype(F32)[0:1, :]
        seq_start = jnp.logical_or(r0 == 0, r0 == SEQ)
        seq_end = jnp.logical_or(r0 + CA == SEQ, r0 + CA == S)
        prev = jnp.where(seq_start, 0.0, prev)
        nxt = jnp.where(seq_end, 0.0, nxt)
        up = jnp.where(row == 0, prev, pltpu.roll(x, 1, 0))
        dn = jnp.where(row == CA - 1, nxt, pltpu.roll(x, CA - 1, 0))
        uc = _silu(cw_ref[0:1, :] * up + cw_ref[1:2, :] * x + cw_ref[2:3, :] * dn + cb_ref[...])
        ucb = uc.astype(BF16)
        uc_s[pl.ds(r0, CA), :] = ucb
        xb = x.astype(BF16)
        for h in range(ML_H):
            sl = slice(h * HP, (h + 1) * HP)
            q_s[pl.ds(r0, CA), sl] = _dot(ucb[:, sl], wq_ref[h]).astype(BF16)
            kt = _dot_nt(wk_ref[h], ucb[:, sl])
            kt_s[2 * i, sl, :] = kt[:, :ML_CHUNK].astype(BF16)
            kt_s[2 * i + 1, sl, :] = kt[:, ML_CHUNK:].astype(BF16)
            v_s[pl.ds(r0, CA), 2 * h * HP:(2 * h + 1) * HP] = _dot(xb[:, sl], wv_ref[h]).astype(BF16)
            v_s[pl.ds(r0, CA), (2 * h + 1) * HP:(2 * h + 2) * HP] = jnp.ones((CA, HP), BF16)

        for half in range(2):
            rows = pl.ds(r0 + half * ML_CHUNK, ML_CHUNK)
            g = pg_ref[0, rows, :] + gb_ref[...]
            parts = _split_bf16(_log_sigmoid(g), 3)
            cum_f = sum(_dot(tril_ref[0], part) for part in parts)
            cum_b = sum(_dot(tril_ref[1], part) for part in parts)
            bsh = pltpu.roll(jnp.where(bwd_lane, cum_b, cum_f), 128 - n_chain, 1)
            r = g - bsh
            pf = r
            pb = r
            k = 1
            while k < ML_CHUNK:
                pf = jnp.maximum(pf, jnp.where(ti >= k, pltpu.roll(pf, k, 0), NEG))
                pb = jnp.maximum(pb, jnp.where(ti < ML_CHUNK - k, pltpu.roll(pb, ML_CHUNK - k, 0), NEG))
                k *= 2
            pm_s[2 * i + half] = jnp.where(bwd_lane, pb, pf)
            b_s[2 * i + half] = bsh
            rt_s[2 * i + half] = r.T[0:n_chain, :]
        return carry

    n_chain = 2 * ML_H
    ti = lax.broadcasted_iota(jnp.int32, (ML_CHUNK, ML_CHUNK), 0)
    si = lax.broadcasted_iota(jnp.int32, (ML_CHUNK, ML_CHUNK), 1)
    bwd_lane = (si % n_chain) >= ML_H
    lax.fori_loop(0, S // CA, conv_body, 0)

    c_s[...] = jnp.zeros_like(c_s)
    m_s[...] = jnp.zeros_like(m_s)
    masks = (si <= ti, si >= ti)

    def scan_body(j, carry):
        chunk = (jnp.where(j < N_CTX_CHUNK, j + N_CHUNK - N_CTX_CHUNK, j - N_CTX_CHUNK), N_CHUNK - 1 - j)
        chains = []
        for d in range(2):
            r0 = pl.multiple_of(chunk[d] * ML_CHUNK, ML_CHUNK)
            p_col = pm_s[chunk[d]]
            bsh = b_s[chunk[d]]
            r_t = rt_s[chunk[d]]
            end = ML_CHUNK - 1 if d == 0 else 0
            for h in range(ML_H):
                c = d * ML_H + h
                sl = slice(h * HP, (h + 1) * HP)
                qc = q_s[pl.ds(r0, ML_CHUNK), sl]
                kt = kt_s[chunk[d], sl, :]
                vx = v_s[pl.ds(r0, ML_CHUNK), 2 * h * HP:(2 * h + 2) * HP]
                r_row = r_t[c:c + 1, :]
                m = m_s[c]
                st = c_s[c]
                big_m = jnp.maximum(m, jnp.broadcast_to(p_col[:, c:c + 1], (ML_CHUNK, HP)))
                b_b = jnp.broadcast_to(bsh[:, c:c + 1], (ML_CHUNK, HP))
                m_end = big_m[end:end + 1, :]
                ktw = (kt.astype(F32) * jnp.exp(r_row - m_end)).astype(BF16)
                chains.append(dict(d=d, r0=r0, sl=sl, c=c, vx=vx, r_row=r_row, m=m, st=st, big_m=big_m,
                                   b_b=b_b, m_end=m_end, end=end,
                                   qk=_dot(qc, kt), inter=_dot(qc, st.astype(BF16)), upd=_dot(ktw, vx)))
        for ch in chains:
            dw = jnp.exp(jnp.where(masks[ch["d"]], ch["r_row"] - ch["big_m"], NEG))
            ch["intra"] = _dot((ch["qk"] * dw).astype(BF16), ch["vx"])
        for ch in chains:
            m, big_m, inter, intra, end = ch["m"], ch["big_m"], ch["inter"], ch["intra"], ch["end"]
            iw = jnp.exp(m - big_m)
            num = iw * inter[:, :HP] + intra[:, :HP]
            nq = iw * inter[:, HP:] + intra[:, HP:]
            hv = num / jnp.maximum(jnp.abs(nq), jnp.exp(-(ch["b_b"] + big_m)))
            a = jnp.exp(m - ch["m_end"])
            ch["out"] = (hv, jnp.concatenate([a, a], axis=1) * ch["st"] + ch["upd"],
                         ch["b_b"][end:end + 1, :] + ch["m_end"])
        for ch in chains:
            hv, st_new, m_new = ch["out"]
            h_s[ch["d"], pl.ds(ch["r0"], ML_CHUNK), ch["sl"]] = hv
            c_s[ch["c"]] = st_new
            m_s[ch["c"]] = m_new
        return carry

    lax.fori_loop(0, N_CHUNK, scan_body, 0)

    live = (lax.broadcasted_iota(jnp.int32, (CA, HP), 1) < ML_D).astype(F32)

    def out_body(i, carry):
        r0 = pl.multiple_of(i * CA, CA)
        for h in range(ML_H):
            sl = slice(h * HP, (h + 1) * HP)
            hh = h_s[0, pl.ds(r0, CA), sl] + h_s[1, pl.ds(r0, CA), sl]
            mu = jnp.sum(hh, axis=-1, keepdims=True) * (1.0 / ML_D)
            dv = (hh - mu) * live
            var = jnp.sum(dv * dv, axis=-1, keepdims=True) * (1.0 / ML_D)
            hn = dv * lax.rsqrt(var + EPS) * ng_ref[:, sl]
            uc = uc_s[pl.ds(r0, CA), sl].astype(F32)
            zz = pu_ref[0, pl.ds(r0, CA), ML_WP + h * HP:ML_WP + (h + 1) * HP].astype(F32)
            o_ref[0, pl.ds(r0, CA), sl] = ((hn + sk_ref[:, sl] * uc) * _silu(zz)).astype(BF16)
        return carry

    lax.fori_loop(0, S // CA, out_body, 0)


def _mlstm(pu, pg, cw, cb, wq, wk, wv, gb, ng, sk, tril):
    B = pu.shape[0]
    full = lambda a: pl.BlockSpec(a.shape, lambda b, _n=a.ndim: (0,) * _n)
    n_chain = 2 * ML_H
    return pl.pallas_call(
        _mlstm_kernel,
        grid=(B,),
        in_specs=[pl.BlockSpec((1, S, 2 * ML_WP), lambda b: (b, 0, 0)),
                  pl.BlockSpec((1, S, 128), lambda b: (b, 0, 0)),
                  full(cw), full(cb), full(wq), full(wk), full(wv), full(gb), full(ng), full(sk), full(tril)],
        out_specs=pl.BlockSpec((1, S, ML_WP), lambda b: (b, 0, 0)),
        out_shape=jax.ShapeDtypeStruct((B, S, ML_WP), BF16),
        scratch_shapes=[pltpu.VMEM((S, ML_WP), BF16), pltpu.VMEM((S, ML_WP), BF16),
                        pltpu.VMEM((N_CHUNK, ML_WP, ML_CHUNK), BF16), pltpu.VMEM((S, 2 * ML_WP), BF16),
                        pltpu.VMEM((2, S, ML_WP), F32),
                        pltpu.VMEM((n_chain, HP, 2 * HP), F32),
                        pltpu.VMEM((n_chain, 1, HP), F32),
                        pltpu.VMEM((N_CHUNK, ML_CHUNK, 128), F32),
                        pltpu.VMEM((N_CHUNK, ML_CHUNK, 128), F32),
                        pltpu.VMEM((N_CHUNK, n_chain, ML_CHUNK), F32)],
        compiler_params=_cparams(("parallel",)),
        name="mlstm",
    )(pu, pg, cw, cb, wq, wk, wv, gb, ng, sk, tril)


def _na_kernel(idx_ref, p_ref, pt_ref, qg_ref, kg_ref, seg_ref, o_ref, kn_s, bias_s):
    j = pl.program_id(1)
    seg = seg_ref[...]

    def headnorm(x, g):
        ss = _dot((x * x).astype(BF16), seg)
        return x * lax.rsqrt(ss * (1.0 / NA_D) + EPS) * g

    @pl.when(j == 0)
    def _():
        def body(i, carry):
            r0 = pl.multiple_of(i * TT, TT)
            kk = p_ref[0, pl.ds(r0, TT), NA_W:2 * NA_W].astype(F32)
            kn_s[pl.ds(r0, TT), :] = headnorm(kk, kg_ref[...]).astype(BF16)
            return carry
        lax.fori_loop(0, NT, body, 0)

    scale = float(NA_D ** -0.5 * np.log2(np.e))
    kctx = kn_s[SEQ:S, :]
    vctx = p_ref[0, SEQ:S, 2 * NA_W:3 * NA_W]

    @pl.when(j < NA_NBLK)
    def _():
        q0 = pl.multiple_of(j * NA_QB, NA_QB)
        k0 = pl.multiple_of(jnp.clip(j * NA_QROWS - WIN_R // 2, 0, ROWS - NA_KROWS) * GRID_W, 256)
        q = headnorm(p_ref[0, pl.ds(q0, NA_QB), 0:NA_W].astype(F32), qg_ref[...]) * scale
        kl = kn_s[pl.ds(k0, NA_KB), :]
        vl = p_ref[0, pl.ds(k0, NA_KB), 2 * NA_W:3 * NA_W]
        head = lax.broadcasted_iota(jnp.int32, (NA_QB, NA_W), 1) // NA_D
        acc = jnp.zeros((NA_QB, NA_W), F32)

        def scores(h):
            qm = jnp.where(head == h, q, 0.0).astype(BF16)
            for i in range(NA_QROWS):
                for p in range(NA_KROWS // 2):
                    code = idx_ref[(j * NA_QROWS + i) * (NA_KROWS // 2) + p]
                    bias_s[i * GRID_W:(i + 1) * GRID_W, p * 2 * GRID_W:(p + 1) * 2 * GRID_W] = pt_ref[h, code]
            return _dot_nt(qm, kl) + bias_s[...].astype(F32), _dot_nt(qm, kctx)

        s_next = scores(0)
        for h in range(NA_H):
            s1, s2 = s_next
            if h + 1 < NA_H:
                s_next = scores(h + 1)
            m = jnp.maximum(jnp.max(s1, axis=-1, keepdims=True), jnp.max(s2, axis=-1, keepdims=True))
            p1 = jnp.exp2(s1 - m)
            p2 = jnp.exp2(s2 - m)
            l = jnp.sum(p1, axis=-1, keepdims=True) + jnp.sum(p2, axis=-1, keepdims=True)
            o = (_dot(p1.astype(BF16), vl) + _dot(p2.astype(BF16), vctx)) / l
            acc = jnp.where(head == h, o, acc)
        o_ref[0, pl.ds(q0, NA_QB), :] = acc.astype(BF16)

    @pl.when(j == NA_NBLK)
    def _():
        q = headnorm(p_ref[0, SEQ:S, 0:NA_W].astype(F32), qg_ref[...]) * scale
        head = lax.broadcasted_iota(jnp.int32, (CTX, NA_W), 1) // NA_D
        acc = jnp.zeros((CTX, NA_W), F32)
        for h in range(NA_H):
            qm = jnp.where(head == h, q, 0.0).astype(BF16)
            s2 = _dot_nt(qm, kctx)
            m = jnp.max(s2, axis=-1, keepdims=True)
            p2 = jnp.exp2(s2 - m)
            l = jnp.sum(p2, axis=-1, keepdims=True)
            o = _dot(p2.astype(BF16), vctx) / l
            acc = jnp.where(head == h, o, acc)
        o_ref[0, SEQ:S, :] = acc.astype(BF16)


def _na_attn(pair_idx, pna, pair_tiles, qg, kg, seg):
    B = pna.shape[0]
    return pl.pallas_call(
        _na_kernel,
        grid_spec=pltpu.PrefetchScalarGridSpec(
            num_scalar_prefetch=1,
            grid=(B, NA_NBLK + 1),
            in_specs=[pl.BlockSpec((1, S, 3 * NA_W), lambda b, j, idx: (b, 0, 0)),
                      pl.BlockSpec(pair_tiles.shape, lambda b, j, idx: (0, 0, 0, 0)),
                      pl.BlockSpec((1, NA_W), lambda b, j, idx: (0, 0)),
                      pl.BlockSpec((1, NA_W), lambda b, j, idx: (0, 0)),
                      pl.BlockSpec((NA_W, NA_W), lambda b, j, idx: (0, 0))],
            out_specs=pl.BlockSpec((1, S, NA_W), lambda b, j, idx: (b, 0, 0)),
            scratch_shapes=[pltpu.VMEM((S, NA_W), BF16), pltpu.VMEM((NA_QB, NA_KB), BF16)]),
        out_shape=jax.ShapeDtypeStruct((B, S, NA_W), BF16),
        compiler_params=_cparams(("parallel", "arbitrary")),
        name="na_attn",
    )(pair_idx, pna, pair_tiles, qg, kg, seg)


def _out_proj_kernel(z_ref, modx_ref, mody_ref, a_ref, m_ref, n_ref, wa_ref, wm_ref, wn_ref, g2_ref, rw_ref, rb_ref,
                     before_ref, below_ref, z1_ref, hs_ref, gs_ref, tm_ref, sm_ref):
    lane = lax.broadcasted_iota(jnp.int32, (TT, 128), 1)
    live = lane < N_EXPERTS
    groups = [slice(i * TT, (i + 1) * TT) for i in range(OT // TT)]

    splits = []
    for rows in groups:
        is_ctx = lax.broadcasted_iota(jnp.int32, (TT, 1), 0) + (pl.program_id(1) * OT + rows.start) >= SEQ
        mod = lambda i: jnp.where(is_ctx, mody_ref[0][:, i * D:(i + 1) * D], modx_ref[0][:, i * D:(i + 1) * D])
        mix = (_dot(a_ref[0, rows, :], wa_ref[...]) + _dot(m_ref[0, rows, :], wm_ref[...])
               + _dot(n_ref[0, rows, :], wn_ref[...]))
        x = z_ref[0, rows, :] + mod(2) * mix
        z1_ref[0, rows, :] = x
        hn = x * lax.rsqrt(jnp.mean(x * x, axis=-1, keepdims=True) + EPS) * g2_ref[...]
        hn = hn * (1.0 + mod(4)) + mod(3)
        h_hi, h_lo = _split_bf16(hn, 2)
        splits.append((h_hi, h_lo))

    affs = [_sigmoid(_dot(h_hi, rw_ref[0]) + (_dot(h_hi, rw_ref[1]) + _dot(h_lo, rw_ref[0])))
            for h_hi, h_lo in splits]

    def cyc(x, k, width):
        fwd = pltpu.roll(x, 128 - k, 1)
        back = pltpu.roll(x, width - k, 1)
        return jnp.where((lane % width) + k < width, fwd, back)

    def rank(x, width, step):
        r = jnp.zeros((TT, 128), F32)
        for k in range(1, width // step):
            y = cyc(x, k * step, width)
            wrapped = (lane % width) + k * step >= width
            beats = jnp.logical_or(y > x, jnp.logical_and(y == x, wrapped))
            r = r + beats.astype(F32)
        return r

    routed = []
    for aff in affs:
        sel = aff + rb_ref[...]
        top2 = rank(sel, EPG, 1) < 2.0
        part = jnp.where(top2, sel, 0.0)
        gscore = part
        for k in range(1, EPG):
            gscore = gscore + cyc(part, k, EPG)
        best = rank(gscore, N_EXPERTS, EPG) < 1.0
        chosen = jnp.logical_and(jnp.logical_and(top2, best), live)
        w = jnp.where(chosen, aff, 0.0)
        group = jnp.sum(jnp.where(chosen, (lane // EPG).astype(F32), 0.0), axis=-1, keepdims=True) * (1.0 / TOP_K)
        routed.append((w / jnp.sum(w, axis=-1, keepdims=True), group))

    sub = lax.broadcasted_iota(jnp.int32, (128, TT), 0)
    row_id = lax.broadcasted_iota(jnp.int32, (SORT_ROWS, TT), 0).astype(F32)
    for gi, ((gate, group), (h_hi, _)) in enumerate(zip(routed, splits)):
        member_t = (lane.astype(F32) == group).astype(F32).T
        member_t = jnp.where(sub < N_GROUPS, member_t, 0.0)
        ahead = _dot(member_t.astype(BF16), before_ref[...])
        count = jnp.sum(member_t, axis=1, keepdims=True)
        padded = ((count.astype(jnp.int32) + (SORT_BLK - 1)) & -SORT_BLK).astype(F32)
        start = _dot(below_ref[...], jnp.broadcast_to(padded, (128, 128)).astype(BF16))
        where_t = jnp.sum(member_t * (start[:, 0:1] + ahead), axis=0, keepdims=True)
        perm = (row_id == where_t).astype(BF16)
        hs_ref[0, gi] = _dot(perm, h_hi)
        gs_ref[0, gi] = sum(_dot(perm, part) for part in _split_bf16(gate, 3))
        tm_ref[0, gi, 0:1, :] = where_t
        tm_ref[0, gi, 1:8, :] = jnp.zeros((7, TT), F32)
        lane_m = lax.broadcasted_iota(jnp.int32, (128, 128), 1)
        sm_ref[0, gi] = jnp.where(lane_m == 0, jnp.broadcast_to(padded, (128, 128)), start)


def _out_proj(z, mods, mla_o, ml_o, na_o, wa, wm, wn, g2, rw, rb):
    B = z.shape[0]
    tok = lambda w_: pl.BlockSpec((1, OT, w_), lambda b, t: (b, t, 0))
    full = lambda a: pl.BlockSpec(a.shape, lambda b, t, _n=a.ndim: (0,) * _n)
    per_group = lambda r, w_: pl.BlockSpec((1, OT // TT, r, w_), lambda b, t: (b, t, 0, 0))
    grouped = lambda r, w_: jax.ShapeDtypeStruct((B, NT, r, w_), F32)
    before = jnp.asarray(np.triu(np.ones((TT, TT)), 1), BF16)
    below = jnp.asarray(np.tril(np.ones((128, 128)), -1), BF16)
    return pl.pallas_call(
        _out_proj_kernel,
        grid=(B, S // OT),
        in_specs=[tok(D),
                  pl.BlockSpec((1, 1, 6 * D), lambda b, t: (2 * b, 0, 0)),
                  pl.BlockSpec((1, 1, 6 * D), lambda b, t: (2 * b + 1, 0, 0)),
                  tok(MLA_H * V_D), tok(ML_WP), tok(NA_W),
                  full(wa), full(wm), full(wn), full(g2), full(rw), full(rb), full(before), full(below)],
        out_specs=[tok(D), per_group(SORT_ROWS, D), per_group(SORT_ROWS, 128), per_group(8, TT), per_group(128, 128)],
        out_shape=[jax.ShapeDtypeStruct((B, S, D), F32),
                   grouped(SORT_ROWS, D), grouped(SORT_ROWS, 128), grouped(8, TT), grouped(128, 128)],
        compiler_params=_cparams(("parallel", "parallel")),
        name="out_proj",
    )(z, mods, mods, mla_o, ml_o, na_o, wa, wm, wn, g2, rw, rb, before, below)


ROW_WAIT = 64


def _row_copy_start(n, row_copy):
    def issue(i, carry):
        row_copy(i).start()
        return carry

    lax.fori_loop(0, n, issue, 0, unroll=8)


def _row_copy_wait(n, slab_copy):
    def drain(i, carry):
        slab_copy(ROW_WAIT).wait()
        return carry

    lax.fori_loop(0, n // ROW_WAIT, drain, 0)


def _experts_kernel(blk_ref, tg_ref, hs_ref, gs_ref, w1_ref, w3_ref, w2_ref, ys_ref, hbuf, gbuf, sems):
    i = pl.program_id(0)
    n_blk = GROUP_T // SORT_BLK

    def fetch(tile, slot):
        def src(q):
            return pl.ds(pl.multiple_of(blk_ref[tile * n_blk + q] * SORT_BLK, SORT_BLK), SORT_BLK)

        def dst(q):
            return pl.ds(pl.multiple_of(q * SORT_BLK, SORT_BLK), SORT_BLK)

        _row_copy_start(n_blk, lambda q: pltpu.make_async_copy(hs_ref.at[src(q)], hbuf.at[slot, dst(q)], sems.at[0, slot]))
        _row_copy_start(n_blk, lambda q: pltpu.make_async_copy(gs_ref.at[src(q)], gbuf.at[slot, dst(q)], sems.at[1, slot]))

    @pl.when(i == 0)
    def _():
        fetch(0, 0)

    slot = i % 2
    _row_copy_wait(GROUP_T, lambda k: pltpu.make_async_copy(
        hs_ref.at[pl.ds(0, k)], hbuf.at[slot, pl.ds(0, k)], sems.at[0, slot]))
    _row_copy_wait(GROUP_T, lambda k: pltpu.make_async_copy(
        gs_ref.at[pl.ds(0, k)], gbuf.at[slot, pl.ds(0, k)], sems.at[1, slot]))

    @pl.when(i + 1 < pl.num_programs(0))
    def _():
        fetch(i + 1, 1 - slot)

    g = tg_ref[i]
    x = hbuf[slot].astype(BF16)
    gates = gbuf[slot]
    lane = lax.broadcasted_iota(jnp.int32, (GROUP_T, 128), 1)
    ups = [(_dot(x, w1_ref[0, e].astype(BF16)), _dot(x, w3_ref[0, e].astype(BF16))) for e in range(EPG)]
    acc = jnp.zeros((GROUP_T, D), F32)
    for e in range(EPG):
        ge = jnp.sum(jnp.where(lane == g * EPG + e, gates, 0.0), axis=-1, keepdims=True)
        a, b = ups[e]
        acc = acc + _dot((_silu(a) * b * ge).astype(BF16), w2_ref[0, e].astype(BF16))
    ys_ref[...] = acc


def _experts(src_blk, tile_group, hs, gs, n_rows, w1, w3, w2, l):
    wspec = lambda k, n: pl.BlockSpec((1, EPG, k, n), lambda i, blk, tg: (l, tg[i], 0, 0))
    return pl.pallas_call(
        _experts_kernel,
        grid_spec=pltpu.PrefetchScalarGridSpec(
            num_scalar_prefetch=2,
            grid=(n_rows // GROUP_T,),
            in_specs=[pl.BlockSpec(memory_space=pl.ANY), pl.BlockSpec(memory_space=pl.ANY),
                      wspec(D, D_FF), wspec(D, D_FF), wspec(D_FF, D)],
            out_specs=pl.BlockSpec((GROUP_T, D), lambda i, blk, tg: (i, 0)),
            scratch_shapes=[pltpu.VMEM((2, GROUP_T, D), F32), pltpu.VMEM((2, GROUP_T, 128), F32),
                            pltpu.SemaphoreType.DMA((2, 2))]),
        out_shape=jax.ShapeDtypeStruct((n_rows, D), F32),
        compiler_params=_cparams(("arbitrary",)),
        name="moe_experts",
    )(src_blk, tile_group, hs, gs, w1, w3, w2)


def _combine_kernel(blk_ref, z1_ref, modx_ref, mody_ref, tm_ref, ys_ref, o_ref, buf, sems):
    t = pl.program_id(1)
    step = pl.program_id(0) * NT + t
    n_blk = SORT_ROWS // SORT_BLK

    def fetch(token_group, slot):
        base = token_group * n_blk
        _row_copy_start(n_blk, lambda q: pltpu.make_async_copy(
            ys_ref.at[pl.ds(pl.multiple_of(blk_ref[base + q] * SORT_BLK, SORT_BLK), SORT_BLK)],
            buf.at[slot, pl.ds(pl.multiple_of(q * SORT_BLK, SORT_BLK), SORT_BLK)], sems.at[slot]))

    @pl.when(step == 0)
    def _():
        fetch(0, 0)

    slot = step % 2
    _row_copy_wait(SORT_ROWS, lambda k: pltpu.make_async_copy(
        ys_ref.at[pl.ds(0, k)], buf.at[slot, pl.ds(0, k)], sems.at[slot]))

    @pl.when(step + 1 < pl.num_programs(0) * NT)
    def _():
        fetch(step + 1, 1 - slot)

    row_id = lax.broadcasted_iota(jnp.int32, (SORT_ROWS, TT), 0).astype(F32)
    perm = (row_id == tm_ref[0, 0, 0:1, :]).astype(BF16)
    y = sum(_dot_tn(perm, part) for part in _split_bf16(buf[slot], 2))
    g2 = jnp.where(t == NT - 1, mody_ref[0][:, 5 * D:], modx_ref[0][:, 5 * D:])
    o_ref[0] = z1_ref[0] + g2 * y


def _combine(back_blk, z1, mods, tok_meta, ys):
    B = z1.shape[0]
    tok = pl.BlockSpec((1, TT, D), lambda b, t, blk: (b, t, 0))
    return pl.pallas_call(
        _combine_kernel,
        grid_spec=pltpu.PrefetchScalarGridSpec(
            num_scalar_prefetch=1,
            grid=(B, NT),
            in_specs=[tok,
                      pl.BlockSpec((1, 1, 6 * D), lambda b, t, blk: (2 * b, 0, 0)),
                      pl.BlockSpec((1, 1, 6 * D), lambda b, t, blk: (2 * b + 1, 0, 0)),
                      pl.BlockSpec((1, 1, 8, TT), lambda b, t, blk: (b, t, 0, 0)),
                      pl.BlockSpec(memory_space=pl.ANY)],
            out_specs=tok,
            scratch_shapes=[pltpu.VMEM((2, SORT_ROWS, D), F32), pltpu.SemaphoreType.DMA((2,))]),
        out_shape=jax.ShapeDtypeStruct((B, S, D), F32),
        compiler_params=_cparams(("arbitrary", "arbitrary")),
        name="moe_combine",
    )(back_blk, z1, mods, mods, tok_meta, ys)


def _group_layout(seg_meta, n_rows):
    n_tg = seg_meta.shape[0]
    seg_len = seg_meta[:, :N_GROUPS, 0].astype(jnp.int32).T
    seg_local = seg_meta[:, :N_GROUPS, 1].astype(jnp.int32).T
    total = jnp.sum(seg_len, axis=1)
    padded = (total + GROUP_T - 1) // GROUP_T * GROUP_T
    group_end = jnp.cumsum(padded)
    seg_start = (group_end - padded)[:, None] + jnp.cumsum(seg_len, axis=1) - seg_len
    flat_start, flat_len, flat_local = seg_start.reshape(-1), seg_len.reshape(-1), seg_local.reshape(-1)

    row0 = (jnp.arange(n_rows // SORT_BLK, dtype=jnp.int32) * SORT_BLK)[:, None]
    covers = (row0 >= flat_start[None, :]) & (row0 < (flat_start + flat_len)[None, :])
    token_group = (jnp.arange(flat_start.shape[0], dtype=jnp.int32) % n_tg)[None, :]
    src_row = jnp.sum(jnp.where(covers, token_group * SORT_ROWS + flat_local[None, :] + row0 - flat_start[None, :], 0),
                      axis=1)
    filled = jnp.any(covers, axis=1)
    src_blk = jnp.where(filled, src_row // SORT_BLK, SORT_ROWS // SORT_BLK - 1)

    tile_group = jnp.sum(jnp.arange(n_rows // GROUP_T)[:, None] * GROUP_T >= group_end[None, :], axis=1)

    local0 = (jnp.arange(SORT_ROWS // SORT_BLK, dtype=jnp.int32) * SORT_BLK)[None, None, :]
    lo, ln, st = seg_local[:, :, None], seg_len[:, :, None], seg_start[:, :, None]
    back_row = jnp.sum(jnp.where((local0 >= lo) & (local0 < lo + ln), st + local0 - lo, 0), axis=0)
    return (src_blk.astype(jnp.int32), jnp.minimum(tile_group, N_GROUPS - 1).astype(jnp.int32),
            (back_row // SORT_BLK).reshape(-1).astype(jnp.int32))


def _moe(z1, hs, gs, tok_meta, seg_meta, mods, w1, w3, w2, l):
    B = z1.shape[0]
    n_tg = B * NT
    n_rows = (-(-(n_tg * (TT + N_GROUPS * (SORT_BLK - 1))) // GROUP_T) + N_GROUPS) * GROUP_T
    src_blk, tile_group, back_blk = _group_layout(seg_meta.reshape(n_tg, 128, 128), n_rows)
    ys = _experts(src_blk, tile_group, hs.reshape(n_tg * SORT_ROWS, D), gs.reshape(n_tg * SORT_ROWS, 128),
                  n_rows, w1, w3, w2, l)
    return _combine(back_blk, z1, mods, tok_meta, ys)


def _in_proj_layout(w):
    cuts = np.cumsum([Q_RANK, KV_RANK, ROPE_D, ML_W, ML_W, 4 * ML_H])
    qc, ckv, kr, u, zz, g, na = jnp.split(w, [int(v) for v in cuts], axis=-1)
    zeros = lambda n: jnp.zeros((w.shape[0], n), w.dtype)
    out = jnp.concatenate([qc, ckv, zeros(NOPE_D), kr, zeros(HP - QK_D),
                           _pad_heads(u, ML_H, ML_D, HP), _pad_heads(zz, ML_H, ML_D, HP),
                           _gate_order(g), zeros(128 - 4 * ML_H), na], axis=-1)
    assert out.shape[-1] == NP_IN
    return out


def _gate_order(g):
    i_f, f_f, i_b, f_b = jnp.split(g, 4, axis=-1)
    return jnp.concatenate([i_f, i_b, f_f, f_b], axis=-1)


def _pad_heads(v, nh, d, dp):
    lead = v.shape[:-1]
    v = v.reshape(lead + (nh, d))
    v = jnp.pad(v, [(0, 0)] * len(lead) + [(0, 0), (0, dp - d)])
    return v.reshape(lead + (nh * dp,))


def _rope_tables():
    t = np.arange(SEQ)
    row = (t // GRID_W).astype(np.float32)
    col = (t % GRID_W).astype(np.float32)
    quarter = ROPE_D // 4
    inv = jnp.asarray(ROPE_BASE, F32) ** (-jnp.arange(quarter, dtype=F32) / quarter)
    ar = jnp.asarray(row)[:, None] * inv
    ac = jnp.asarray(col)[:, None] * inv
    ang = jnp.concatenate([ar, ar, ac, ac], axis=-1)
    cos = jnp.ones((S, HP), F32).at[:SEQ, NOPE_D:QK_D].set(jnp.cos(ang))
    sin = jnp.zeros((S, HP), F32).at[:SEQ, NOPE_D:QK_D].set(jnp.sin(ang))
    return cos, sin


def _rotate_half_index():
    q = ROPE_D // 4
    src = np.arange(QK_D)
    sign = np.zeros((QK_D,), np.float32)
    for blk in range(2):
        lo = NOPE_D + 2 * q * blk
        src[lo:lo + q] = np.arange(lo + q, lo + 2 * q)
        sign[lo:lo + q] = -1.0
        src[lo + q:lo + 2 * q] = np.arange(lo, lo + q)
        sign[lo + q:lo + 2 * q] = 1.0
    return src, sign


def _rotate_half(w):
    src, sign = _rotate_half_index()
    return w[..., src] * sign


NA_NDR = 2 * WIN_R - 1
NA_NPAIR = 3 * NA_NDR


def _na_pair_index():
    idx = np.zeros((NA_NBLK, NA_QROWS, NA_KROWS // 2), np.int32)
    for blk in range(NA_NBLK):
        k0 = int(np.clip(blk * NA_QROWS - WIN_R // 2, 0, ROWS - NA_KROWS))
        for i in range(NA_QROWS):
            qr = blk * NA_QROWS + i
            rs = int(np.clip(qr - WIN_R // 2, 0, ROWS - WIN_R))
            assert k0 <= rs and rs + WIN_R <= k0 + NA_KROWS
            for p in range(NA_KROWS // 2):
                kr = k0 + 2 * p
                dr = kr - qr + WIN_R - 1
                left = rs <= kr < rs + WIN_R
                right = rs <= kr + 1 < rs + WIN_R
                if left and right:
                    idx[blk, i, p] = 1 + dr
                elif left:
                    idx[blk, i, p] = NA_NDR + dr
                elif right:
                    idx[blk, i, p] = 2 * NA_NDR + dr + 1
    return idx.reshape(-1)


def _na_pair_tiles(rpb):
    cq = np.arange(GRID_W)
    cs = np.clip(cq - WIN_C // 2, 0, GRID_W - WIN_C)
    col_ok = (cq[None, :] >= cs[:, None]) & (cq[None, :] < cs[:, None] + WIN_C)
    dc = np.clip(cq[None, :] - cq[:, None], -(WIN_C - 1), WIN_C - 1) + (WIN_C - 1)
    onehot = jnp.asarray(np.eye(2 * WIN_C - 1, dtype=np.float32)[dc])
    tiles = jnp.einsum('hrc,qkc->hrqk', rpb, onehot, precision=lax.Precision.HIGHEST)
    tiles = jnp.where(jnp.asarray(col_ok), tiles * float(np.log2(np.e)), NEG)
    masked = jnp.full_like(tiles, NEG)
    both = jnp.concatenate([tiles[:, :-1], tiles[:, 1:]], axis=-1)
    left = jnp.concatenate([tiles, masked], axis=-1)
    right = jnp.concatenate([masked, tiles], axis=-1)
    none = jnp.concatenate([masked[:, :1], masked[:, :1]], axis=-1)
    out = jnp.concatenate([none, both, left, right], axis=1)
    assert out.shape[1] == NA_NPAIR
    return out.astype(BF16)


def kernel(x, c, ctx, c_ctx, w_mod, b_mod, norm1_g, norm2_g, w_in, w_out, mla_qnorm_g, mla_w_uq, mla_kvnorm_g, mla_w_ukv, mla_q_g, mla_k_g, ml_conv_w, ml_conv_b, ml_w_q, ml_w_k, ml_w_v, ml_gate_b, ml_norm_g, ml_skip, na_q_g, na_k_g, na_rpb, router_w, router_b, moe_w1, moe_w3, moe_w2):
    B = x.shape[0]
    z = jnp.concatenate([x, ctx], axis=1)
    cc = jnp.zeros((16, D), F32).at[:B].set(c).at[B].set(c_ctx)
    mod_all = _modulation(cc, w_mod, b_mod)
    cos, sin = _rope_tables()
    rot_src, rot_sign = _rotate_half_index()
    rot_np = np.zeros((HP, HP), np.float32)
    rot_np[rot_src, np.arange(QK_D)] = rot_sign
    rot_mat = jnp.asarray(rot_np, BF16)
    ones_hp = jnp.ones((HP, HP), BF16)
    seg = jnp.asarray(np.kron(np.eye(NA_H), np.ones((NA_D, NA_D))), BF16)
    tril = jnp.asarray(np.stack([np.tril(np.ones((ML_CHUNK, ML_CHUNK))), np.triu(np.ones((ML_CHUNK, ML_CHUNK)))]), BF16)
    pair_idx = jnp.asarray(_na_pair_index())
    rw = jnp.stack(_split_bf16(jnp.pad(router_w, ((0, 0), (0, 128 - N_EXPERTS))), 2))
    rb = jnp.pad(router_b, (0, 128 - N_EXPERTS), constant_values=NEG).reshape(1, 128)

    def pad_lanes(v, n):
        return jnp.pad(v, [(0, 0)] * (v.ndim - 1) + [(0, n - v.shape[-1])])

    for l in range(DEPTH):
        mx = mod_all[l, :B]
        my = jnp.broadcast_to(mod_all[l, B], (B, 6 * D))
        mods = jnp.stack([mx, my], axis=1).reshape(2 * B, 1, 6 * D)

        w_in_p = _in_proj_layout(w_in[l]).astype(BF16)
        pmla, pu, pg, pna = _in_proj(z, mods, norm1_g[l].reshape(1, D), w_in_p)

        wq = jnp.transpose(mla_w_uq[l].reshape(Q_RANK, MLA_H, QK_D), (1, 0, 2))
        wuq = jnp.concatenate([pad_lanes(wq, HP), pad_lanes(_rotate_half(wq), HP)], axis=-1).astype(BF16)
        wukv = jnp.transpose(mla_w_ukv[l].reshape(KV_RANK, MLA_H, NOPE_D + V_D), (1, 0, 2))
        wuk = pad_lanes(wukv[..., :NOPE_D], HP).astype(BF16)
        wuv = pad_lanes(wukv[..., NOPE_D:], HP).astype(BF16)
        qg, kg = mla_q_g[l], mla_k_g[l]
        q_scale = float(QK_D ** -0.5 * np.log2(np.e))
        tabs = jnp.stack([cos * pad_lanes(qg, HP) * q_scale, sin * pad_lanes(jnp.abs(rot_sign) * qg[rot_src], HP) * q_scale,
                          cos * pad_lanes(kg, HP), sin * pad_lanes(jnp.abs(rot_sign) * kg[rot_src], HP)])
        heads_last = lambda w_: jnp.transpose(w_, (1, 0, 2)).reshape(w_.shape[1], -1)
        q, k, v = _mla_prep(pmla, tabs, mla_qnorm_g[l].reshape(1, Q_RANK), heads_last(wuq),
                            mla_kvnorm_g[l].reshape(1, KV_RANK), heads_last(wuk), heads_last(wuv), rot_mat, ones_hp)
        mla_o = _mla_attn(q, k, v)

        padh = lambda a: _pad_heads(a, ML_H, ML_D, HP)
        padw = lambda w_: jnp.pad(w_, ((0, 0), (0, HP - ML_D), (0, HP - ML_D))).astype(BF16)
        cw = jnp.pad(padh(ml_conv_w[l]), ((0, 8 - 3), (0, 0)))
        ml_o = _mlstm(pu, pg, cw, padh(ml_conv_b[l]).reshape(1, ML_WP),
                      padw(ml_w_q[l]), padw(jnp.swapaxes(ml_w_k[l], 1, 2) * (ML_D ** -0.5)), padw(ml_w_v[l]),
                      pad_lanes(_gate_order(ml_gate_b[l]).reshape(1, 4 * ML_H), 128),
                      padh(ml_norm_g[l]).reshape(1, ML_WP), padh(ml_skip[l]).reshape(1, ML_WP), tril)

        na_o = _na_attn(pair_idx, pna, _na_pair_tiles(na_rpb[l]), jnp.tile(na_q_g[l], NA_H).reshape(1, NA_W),
                        jnp.tile(na_k_g[l], NA_H).reshape(1, NA_W), seg)

        wo = w_out[l]
        wa = wo[:MLA_H * V_D].astype(BF16)
        wm = jnp.pad(wo[MLA_H * V_D:MLA_H * V_D + ML_W].reshape(ML_H, ML_D, D),
                     ((0, 0), (0, HP - ML_D), (0, 0))).reshape(ML_WP, D).astype(BF16)
        wn = wo[MLA_H * V_D + ML_W:].astype(BF16)
        z1, hs, gs, tok_meta, seg_meta = _out_proj(z, mods, mla_o, ml_o, na_o, wa, wm, wn,
                                                   norm2_g[l].reshape(1, D), rw, rb)
        z = _moe(z1, hs, gs, tok_meta, seg_meta, mods, moe_w1, moe_w3, moe_w2, l)

    return z[:, :SEQ]
```

```python
import functools

import numpy as np
import jax
import jax.numpy as jnp
from jax import lax
from jax.experimental import pallas as pl
from jax.experimental.pallas import tpu as pltpu

F32 = jnp.float32
BF16 = jnp.bfloat16

D = 1024
SEQ = 2048
CTX = 256
S = SEQ + CTX
DEPTH = 4
GRID_W = 64
ROWS = SEQ // GRID_W
EPS = 1e-6

MLA_H = 6
Q_RANK = 256
KV_RANK = 128
NOPE_D = 64
ROPE_D = 32
V_D = 64
QK_D = NOPE_D + ROPE_D
ROPE_BASE = 10000.0

ML_H = 4
ML_D = 96
ML_W = ML_H * ML_D
HP = 128
ML_WP = ML_H * HP
ML_CHUNK = 128
N_CHUNK = S // ML_CHUNK
N_CTX_CHUNK = CTX // ML_CHUNK

NA_H = 4
NA_D = 64
NA_W = NA_H * NA_D
WIN_R = 8
WIN_C = 16
NA_QROWS = 4
NA_KROWS = 12
NA_QB = NA_QROWS * GRID_W
NA_KB = NA_KROWS * GRID_W
NA_NBLK = ROWS // NA_QROWS

N_EXPERTS = 16
N_GROUPS = 4
EPG = N_EXPERTS // N_GROUPS
D_FF = 256

TT = 256
NT = S // TT
OT = 768
TOP_K = 2
SORT_BLK = 8
SORT_ROWS = 320
GROUP_T = 512
NEG = -1e30

C_QC = 0
C_CKV = 256
C_KR = 384
C_U = 512
C_Z = C_U + ML_WP
C_G = C_Z + ML_WP
C_NA = C_G + 128
NP_IN = C_NA + 3 * NA_W

VMEM_LIMIT = 56 * 1024 * 1024


def _cparams(sem):
    return pltpu.CompilerParams(dimension_semantics=sem, vmem_limit_bytes=VMEM_LIMIT)


def _sigmoid(x):
    return 1.0 / (1.0 + jnp.exp(-x))


def _silu(x):
    return x * _sigmoid(x)


def _dot(a, b):
    return jnp.dot(a, b, preferred_element_type=F32)


def _dot_nt(a, b):
    return lax.dot_general(a, b, (((1,), (1,)), ((), ())), preferred_element_type=F32)


def _dot_tn(a, b):
    return lax.dot_general(a, b, (((0,), (0,)), ((), ())), preferred_element_type=F32)


def _dot_hi(a, b):
    return jnp.dot(a, b, preferred_element_type=F32, precision=lax.Precision.HIGHEST)


def _split_bf16(x, n):
    parts = []
    for _ in range(n):
        p = x.astype(BF16)
        parts.append(p)
        x = x - p.astype(F32)
    return parts


def _mod_rows(mod_ref, t):
    m = mod_ref[0]
    return [m[:, i * D:(i + 1) * D] for i in range(6)]


def _mod_kernel(c_ref, w_ref, b_ref, o_ref):
    sc = _silu(c_ref[...])
    o_ref[0] = _dot_hi(sc, w_ref[0]) + b_ref[0]


def _modulation(cc, w_mod, b_mod):
    nc = 6
    return pl.pallas_call(
        _mod_kernel,
        grid=(DEPTH, nc),
        in_specs=[pl.BlockSpec((16, D), lambda l, j: (0, 0)),
                  pl.BlockSpec((1, D, D), lambda l, j: (l, 0, j)),
                  pl.BlockSpec((1, 1, D), lambda l, j: (l, 0, j))],
        out_specs=pl.BlockSpec((1, 16, D), lambda l, j: (l, 0, j)),
        out_shape=jax.ShapeDtypeStruct((DEPTH, 16, 6 * D), F32),
        compiler_params=_cparams(("parallel", "parallel")),
        name="modulation",
    )(cc, w_mod, b_mod.reshape(DEPTH, 1, 6 * D))


def _in_proj_kernel(z_ref, mod_ref, g_ref, w_ref, pmla_ref, pu_ref, pg_ref, pna_ref):
    sh1, sc1 = _mod_rows(mod_ref, None)[:2]
    x = z_ref[0]
    xn = x * lax.rsqrt(jnp.mean(x * x, axis=-1, keepdims=True) + EPS) * g_ref[...]
    xn = xn * (1.0 + sc1) + sh1
    p = _dot(xn.astype(BF16), w_ref[...])
    pmla_ref[0] = p[:, :C_U].astype(BF16)
    pu_ref[0] = p[:, C_U:C_G].astype(BF16)
    pg_ref[0] = p[:, C_G:C_NA]
    pna_ref[0] = p[:, C_NA:].astype(BF16)


def _mod_spec():
    return pl.BlockSpec((1, 1, 6 * D), lambda b, t: (2 * b + t // (NT - 1), 0, 0))


def _in_proj(z, mods, g, w):
    B = z.shape[0]
    tok = lambda w_: pl.BlockSpec((1, TT, w_), lambda b, t: (b, t, 0))
    return pl.pallas_call(
        _in_proj_kernel,
        grid=(B, NT),
        in_specs=[tok(D), _mod_spec(),
                  pl.BlockSpec((1, D), lambda b, t: (0, 0)),
                  pl.BlockSpec((D, NP_IN), lambda b, t: (0, 0))],
        out_specs=[tok(C_U), tok(2 * ML_WP), tok(128), tok(3 * NA_W)],
        out_shape=[jax.ShapeDtypeStruct((B, S, C_U), BF16),
                   jax.ShapeDtypeStruct((B, S, 2 * ML_WP), BF16),
                   jax.ShapeDtypeStruct((B, S, 128), F32),
                   jax.ShapeDtypeStruct((B, S, 3 * NA_W), BF16)],
        compiler_params=_cparams(("parallel", "parallel")),
        name="in_proj",
    )(z, mods, g, w)


def _mla_prep_kernel(p_ref, tab_ref, qng_ref, wuq_ref, kvng_ref, wuk_ref, wuv_ref, rot_ref, ones_ref,
                     q_out, k_out, v_out):
    p = p_ref[0].astype(F32)
    qc = p[:, C_QC:C_CKV]
    ckv = p[:, C_CKV:C_KR]
    kr = p[:, C_KR:C_U]
    qcn = (qc * lax.rsqrt(jnp.mean(qc * qc, axis=-1, keepdims=True) + EPS) * qng_ref[...]).astype(BF16)
    ckvn = (ckv * lax.rsqrt(jnp.mean(ckv * ckv, axis=-1, keepdims=True) + EPS) * kvng_ref[...]).astype(BF16)
    lane = lax.broadcasted_iota(jnp.int32, (TT, HP), 1)
    ones = ones_ref[...]
    kr_rot = _dot(kr.astype(BF16), rot_ref[...])

    q_all = _dot(qcn, wuq_ref[...])
    k_all = _dot(ckvn, wuk_ref[...])
    v_all = _dot(ckvn, wuv_ref[...])
    qs = [q_all[:, 2 * h * HP:(2 * h + 1) * HP] for h in range(MLA_H)]
    q_rots = [q_all[:, (2 * h + 1) * HP:(2 * h + 2) * HP] for h in range(MLA_H)]
    ks = [k_all[:, h * HP:(h + 1) * HP] + kr for h in range(MLA_H)]
    ss_q = [_dot((x * x).astype(BF16), ones) for x in qs]
    ss_k = [_dot((x * x).astype(BF16), ones) for x in ks]

    def norm_rope(x, x_rot, ss, cos_g, sin_g):
        return lax.rsqrt(ss * (1.0 / QK_D) + EPS) * (x * cos_g + x_rot * sin_g)

    for h in range(MLA_H):
        q_out[0, h] = norm_rope(qs[h], q_rots[h], ss_q[h], tab_ref[0], tab_ref[1]).astype(BF16)
        k_out[0, h] = norm_rope(ks[h], kr_rot, ss_k[h], tab_ref[2], tab_ref[3]).astype(BF16)
        v_out[0, h] = jnp.where(lane < V_D, v_all[:, h * HP:(h + 1) * HP], 1.0).astype(BF16)


def _mla_prep(pmla, tabs, qng, wuq, kvng, wuk, wuv, rot, ones):
    B = pmla.shape[0]
    full = lambda a: pl.BlockSpec(a.shape, lambda b, t, _n=a.ndim: (0,) * _n)
    hd = lambda w_: pl.BlockSpec((1, MLA_H, TT, w_), lambda b, t: (b, 0, t, 0))
    return pl.pallas_call(
        _mla_prep_kernel,
        grid=(B, NT),
        in_specs=[pl.BlockSpec((1, TT, C_U), lambda b, t: (b, t, 0)),
                  pl.BlockSpec((4, TT, HP), lambda b, t: (0, t, 0)),
                  full(qng), full(wuq), full(kvng), full(wuk), full(wuv), full(rot), full(ones)],
        out_specs=[hd(HP), hd(HP), hd(HP)],
        out_shape=[jax.ShapeDtypeStruct((B, MLA_H, S, HP), BF16),
                   jax.ShapeDtypeStruct((B, MLA_H, S, HP), BF16),
                   jax.ShapeDtypeStruct((B, MLA_H, S, HP), BF16)],
        compiler_params=_cparams(("parallel", "parallel")),
        name="mla_prep",
    )(pmla, tabs, qng, wuq, kvng, wuk, wuv, rot, ones)


def _mla_attn_kernel(q_ref, k_ref, v_ref, o_ref):
    t = pl.program_id(1)

    def attend(k_lo, k_n):
        def scores(h):
            return _dot_nt(q_ref[0, h], k_ref[0, h, k_lo:k_lo + k_n, :])

        outs = []
        s_next = scores(0)
        for h in range(MLA_H):
            s = s_next
            if h + 1 < MLA_H:
                s_next = scores(h + 1)
            m = jnp.max(s, axis=-1, keepdims=True)
            p = jnp.exp2(s - m)
            pv = _dot(p.astype(BF16), v_ref[0, h, k_lo:k_lo + k_n, :])
            outs.append(pv[:, :V_D] / pv[:, V_D:V_D + 1])
        o_ref[0] = jnp.concatenate(outs, axis=-1).astype(BF16)

    @pl.when(t < NT - 1)
    def _():
        attend(0, S)

    @pl.when(t == NT - 1)
    def _():
        attend(SEQ, CTX)


def _mla_attn(q, k, v):
    B = q.shape[0]
    return pl.pallas_call(
        _mla_attn_kernel,
        grid=(B, NT),
        in_specs=[pl.BlockSpec((1, MLA_H, TT, HP), lambda b, t: (b, 0, t, 0)),
                  pl.BlockSpec((1, MLA_H, S, HP), lambda b, t: (b, 0, 0, 0)),
                  pl.BlockSpec((1, MLA_H, S, HP), lambda b, t: (b, 0, 0, 0))],
        out_specs=pl.BlockSpec((1, TT, MLA_H * V_D), lambda b, t: (b, t, 0)),
        out_shape=jax.ShapeDtypeStruct((B, S, MLA_H * V_D), BF16),
        compiler_params=_cparams(("parallel", "arbitrary")),
        name="mla_attn",
    )(q, k, v)


def _log_sigmoid(x):
    return jnp.minimum(x, 0.0) - jnp.log(1.0 + jnp.exp(-jnp.abs(x)))


def _mlstm_kernel(pu_ref, pg_ref, cw_ref, cb_ref, wq_ref, wk_ref, wv_ref, gb_ref, ng_ref, sk_ref,
                  tril_ref, o_ref, uc_s, q_s, kt_s, v_s, h_s, c_s, m_s, pm_s, b_s, rt_s):
    CA = 2 * ML_CHUNK
    row = lax.broadcasted_iota(jnp.int32, (CA, ML_WP), 0)

    def conv_body(i, carry):
        r0 = pl.multiple_of(i * CA, CA)
        x = pu_ref[0, pl.ds(r0, CA), 0:ML_WP].astype(F32)
        pr = pl.multiple_of(jnp.maximum(r0 - 16, 0), 16)
        nx = pl.multiple_of(jnp.minimum(r0 + CA, S - 16), 16)
        prev = pu_ref[0, pl.ds(pr, 16), 0:ML_WP].astype(F32)[15:16, :]
        nxt = pu_ref[0, pl.ds(nx, 16), 0:ML_WP].astype(F32)[0:1, :]
        seq_start = jnp.logical_or(r0 == 0, r0 == SEQ)
        seq_end = jnp.logical_or(r0 + CA == SEQ, r0 + CA == S)
        prev = jnp.where(seq_start, 0.0, prev)
        nxt = jnp.where(seq_end, 0.0, nxt)
        up = jnp.where(row == 0, prev, pltpu.roll(x, 1, 0))
        dn = jnp.where(row == CA - 1, nxt, pltpu.roll(x, CA - 1, 0))
        uc = _silu(cw_ref[0:1, :] * up + cw_ref[1:2, :] * x + cw_ref[2:3, :] * dn + cb_ref[...])
        ucb = uc.astype(BF16)
        uc_s[pl.ds(r0, CA), :] = ucb
        xb = x.astype(BF16)
        for h in range(ML_H):
            sl = slice(h * HP, (h + 1) * HP)
            q_s[pl.ds(r0, CA), sl] = _dot(ucb[:, sl], wq_ref[h]).astype(BF16)
            kt = _dot_nt(wk_ref[h], ucb[:, sl])
            kt_s[2 * i, sl, :] = kt[:, :ML_CHUNK].astype(BF16)
            kt_s[2 * i + 1, sl, :] = kt[:, ML_CHUNK:].astype(BF16)
            v_s[pl.ds(r0, CA), 2 * h * HP:(2 * h + 1) * HP] = _dot(xb[:, sl], wv_ref[h]).astype(BF16)
            v_s[pl.ds(r0, CA), (2 * h + 1) * HP:(2 * h + 2) * HP] = jnp.ones((CA, HP), BF16)

        for half in range(2):
            rows = pl.ds(r0 + half * ML_CHUNK, ML_CHUNK)
            g = pg_ref[0, rows, :] + gb_ref[...]
            parts = _split_bf16(_log_sigmoid(g), 3)
            cum_f = sum(_dot(tril_ref[0], part) for part in parts)
            cum_b = sum(_dot(tril_ref[1], part) for part in parts)
            bsh = pltpu.roll(jnp.where(bwd_lane, cum_b, cum_f), 128 - n_chain, 1)
            r = g - bsh
            pf = r
            pb = r
            k = 1
            while k < ML_CHUNK:
                pf = jnp.maximum(pf, jnp.where(ti >= k, pltpu.roll(pf, k, 0), NEG))
                pb = jnp.maximum(pb, jnp.where(ti < ML_CHUNK - k, pltpu.roll(pb, ML_CHUNK - k, 0), NEG))
                k *= 2
            pm_s[2 * i + half] = jnp.where(bwd_lane, pb, pf)
            b_s[2 * i + half] = bsh
            rt_s[2 * i + half] = r.T[0:n_chain, :]
        return carry

    n_chain = 2 * ML_H
    ti = lax.broadcasted_iota(jnp.int32, (ML_CHUNK, ML_CHUNK), 0)
    si = lax.broadcasted_iota(jnp.int32, (ML_CHUNK, ML_CHUNK), 1)
    bwd_lane = (si % n_chain) >= ML_H
    lax.fori_loop(0, S // CA, conv_body, 0)

    c_s[...] = jnp.zeros_like(c_s)
    m_s[...] = jnp.zeros_like(m_s)
    masks = (si <= ti, si >= ti)

    def scan_body(j, carry):
        chunk = (jnp.where(j < N_CTX_CHUNK, j + N_CHUNK - N_CTX_CHUNK, j - N_CTX_CHUNK), N_CHUNK - 1 - j)
        chains = []
        for d in range(2):
            r0 = pl.multiple_of(chunk[d] * ML_CHUNK, ML_CHUNK)
            p_col = pm_s[chunk[d]]
            bsh = b_s[chunk[d]]
            r_t = rt_s[chunk[d]]
            end = ML_CHUNK - 1 if d == 0 else 0
            for h in range(ML_H):
                c = d * ML_H + h
                sl = slice(h * HP, (h + 1) * HP)
                qc = q_s[pl.ds(r0, ML_CHUNK), sl]
                kt = kt_s[chunk[d], sl, :]
                vx = v_s[pl.ds(r0, ML_CHUNK), 2 * h * HP:(2 * h + 2) * HP]
                r_row = r_t[c:c + 1, :]
                m = m_s[c]
                st = c_s[c]
                big_m = jnp.maximum(m, jnp.broadcast_to(p_col[:, c:c + 1], (ML_CHUNK, HP)))
                b_b = jnp.broadcast_to(bsh[:, c:c + 1], (ML_CHUNK, HP))
                m_end = big_m[end:end + 1, :]
                ktw = (kt.astype(F32) * jnp.exp(r_row - m_end)).astype(BF16)
                chains.append(dict(d=d, r0=r0, sl=sl, c=c, vx=vx, r_row=r_row, m=m, st=st, big_m=big_m,
                                   b_b=b_b, m_end=m_end, end=end,
                                   qk=_dot(qc, kt), inter=_dot(qc, st.astype(BF16)), upd=_dot(ktw, vx)))
        for ch in chains:
            dw = jnp.exp(jnp.where(masks[ch["d"]], ch["r_row"] - ch["big_m"], NEG))
            ch["intra"] = _dot((ch["qk"] * dw).astype(BF16), ch["vx"])
        for ch in chains:
            m, big_m, inter, intra, end = ch["m"], ch["big_m"], ch["inter"], ch["intra"], ch["end"]
            iw = jnp.exp(m - big_m)
            num = iw * inter[:, :HP] + intra[:, :HP]
            nq = iw * inter[:, HP:] + intra[:, HP:]
            hv = num / jnp.maximum(jnp.abs(nq), jnp.exp(-(ch["b_b"] + big_m)))
            a = jnp.exp(m - ch["m_end"])
            ch["out"] = (hv, jnp.concatenate([a, a], axis=1) * ch["st"] + ch["upd"],
                         ch["b_b"][end:end + 1, :] + ch["m_end"])
        for ch in chains:
            hv, st_new, m_new = ch["out"]
            h_s[ch["d"], pl.ds(ch["r0"], ML_CHUNK), ch["sl"]] = hv
            c_s[ch["c"]] = st_new
            m_s[ch["c"]] = m_new
        return carry

    lax.fori_loop(0, N_CHUNK, scan_body, 0)

    live = (lax.broadcasted_iota(jnp.int32, (CA, HP), 1) < ML_D).astype(F32)

    def out_body(i, carry):
        r0 = pl.multiple_of(i * CA, CA)
        for h in range(ML_H):
            sl = slice(h * HP, (h + 1) * HP)
            hh = h_s[0, pl.ds(r0, CA), sl] + h_s[1, pl.ds(r0, CA), sl]
            mu = jnp.sum(hh, axis=-1, keepdims=True) * (1.0 / ML_D)
            dv = (hh - mu) * live
            var = jnp.sum(dv * dv, axis=-1, keepdims=True) * (1.0 / ML_D)
            hn = dv * lax.rsqrt(var + EPS) * ng_ref[:, sl]
            uc = uc_s[pl.ds(r0, CA), sl].astype(F32)
            zz = pu_ref[0, pl.ds(r0, CA), ML_WP + h * HP:ML_WP + (h + 1) * HP].astype(F32)
            o_ref[0, pl.ds(r0, CA), sl] = ((hn + sk_ref[:, sl] * uc) * _silu(zz)).astype(BF16)
        return carry

    lax.fori_loop(0, S // CA, out_body, 0)


def _mlstm(pu, pg, cw, cb, wq, wk, wv, gb, ng, sk, tril):
    B = pu.shape[0]
    full = lambda a: pl.BlockSpec(a.shape, lambda b, _n=a.ndim: (0,) * _n)
    n_chain = 2 * ML_H
    return pl.pallas_call(
        _mlstm_kernel,
        grid=(B,),
        in_specs=[pl.BlockSpec((1, S, 2 * ML_WP), lambda b: (b, 0, 0)),
                  pl.BlockSpec((1, S, 128), lambda b: (b, 0, 0)),
                  full(cw), full(cb), full(wq), full(wk), full(wv), full(gb), full(ng), full(sk), full(tril)],
        out_specs=pl.BlockSpec((1, S, ML_WP), lambda b: (b, 0, 0)),
        out_shape=jax.ShapeDtypeStruct((B, S, ML_WP), BF16),
        scratch_shapes=[pltpu.VMEM((S, ML_WP), BF16), pltpu.VMEM((S, ML_WP), BF16),
                        pltpu.VMEM((N_CHUNK, ML_WP, ML_CHUNK), BF16), pltpu.VMEM((S, 2 * ML_WP), BF16),
                        pltpu.VMEM((2, S, ML_WP), F32),
                        pltpu.VMEM((n_chain, HP, 2 * HP), F32),
                        pltpu.VMEM((n_chain, 1, HP), F32),
                        pltpu.VMEM((N_CHUNK, ML_CHUNK, 128), F32),
                        pltpu.VMEM((N_CHUNK, ML_CHUNK, 128), F32),
                        pltpu.VMEM((N_CHUNK, n_chain, ML_CHUNK), F32)],
        compiler_params=_cparams(("parallel",)),
        name="mlstm",
    )(pu, pg, cw, cb, wq, wk, wv, gb, ng, sk, tril)


def _na_kernel(idx_ref, p_ref, pt_ref, qg_ref, kg_ref, seg_ref, o_ref, kn_s, bias_s):
    j = pl.program_id(1)
    seg = seg_ref[...]

    def headnorm(x, g):
        ss = _dot((x * x).astype(BF16), seg)
        return x * lax.rsqrt(ss * (1.0 / NA_D) + EPS) * g

    @pl.when(j == 0)
    def _():
        def body(i, carry):
            r0 = pl.multiple_of(i * TT, TT)
            kk = p_ref[0, pl.ds(r0, TT), NA_W:2 * NA_W].astype(F32)
            kn_s[pl.ds(r0, TT), :] = headnorm(kk, kg_ref[...]).astype(BF16)
            return carry
        lax.fori_loop(0, NT, body, 0)

    scale = float(NA_D ** -0.5 * np.log2(np.e))
    kctx = kn_s[SEQ:S, :]
    vctx = p_ref[0, SEQ:S, 2 * NA_W:3 * NA_W]

    @pl.when(j < NA_NBLK)
    def _():
        q0 = pl.multiple_of(j * NA_QB, NA_QB)
        k0 = pl.multiple_of(jnp.clip(j * NA_QROWS - WIN_R // 2, 0, ROWS - NA_KROWS) * GRID_W, 256)
        q = headnorm(p_ref[0, pl.ds(q0, NA_QB), 0:NA_W].astype(F32), qg_ref[...]) * scale
        kl = kn_s[pl.ds(k0, NA_KB), :]
        vl = p_ref[0, pl.ds(k0, NA_KB), 2 * NA_W:3 * NA_W]
        head = lax.broadcasted_iota(jnp.int32, (NA_QB, NA_W), 1) // NA_D
        acc = jnp.zeros((NA_QB, NA_W), F32)

        def scores(h):
            qm = jnp.where(head == h, q, 0.0).astype(BF16)
            for i in range(NA_QROWS):
                for p in range(NA_KROWS // 2):
                    code = idx_ref[(j * NA_QROWS + i) * (NA_KROWS // 2) + p]
                    bias_s[i * GRID_W:(i + 1) * GRID_W, p * 2 * GRID_W:(p + 1) * 2 * GRID_W] = pt_ref[h, code]
            return _dot_nt(qm, kl) + bias_s[...].astype(F32), _dot_nt(qm, kctx)

        s_next = scores(0)
        for h in range(NA_H):
            s1, s2 = s_next
            if h + 1 < NA_H:
                s_next = scores(h + 1)
            m = jnp.maximum(jnp.max(s1, axis=-1, keepdims=True), jnp.max(s2, axis=-1, keepdims=True))
            p1 = jnp.exp2(s1 - m)
            p2 = jnp.exp2(s2 - m)
            l = jnp.sum(p1, axis=-1, keepdims=True) + jnp.sum(p2, axis=-1, keepdims=True)
            o = (_dot(p1.astype(BF16), vl) + _dot(p2.astype(BF16), vctx)) / l
            acc = jnp.where(head == h, o, acc)
        o_ref[0, pl.ds(q0, NA_QB), :] = acc.astype(BF16)

    @pl.when(j == NA_NBLK)
    def _():
        q = headnorm(p_ref[0, SEQ:S, 0:NA_W].astype(F32), qg_ref[...]) * scale
        head = lax.broadcasted_iota(jnp.int32, (CTX, NA_W), 1) // NA_D
        acc = jnp.zeros((CTX, NA_W), F32)
        for h in range(NA_H):
            qm = jnp.where(head == h, q, 0.0).astype(BF16)
            s2 = _dot_nt(qm, kctx)
            m = jnp.max(s2, axis=-1, keepdims=True)
            p2 = jnp.exp2(s2 - m)
            l = jnp.sum(p2, axis=-1, keepdims=True)
            o = _dot(p2.astype(BF16), vctx) / l
            acc = jnp.where(head == h, o, acc)
        o_ref[0, SEQ:S, :] = acc.astype(BF16)


def _na_attn(pair_idx, pna, pair_tiles, qg, kg, seg):
    B = pna.shape[0]
    return pl.pallas_call(
        _na_kernel,
        grid_spec=pltpu.PrefetchScalarGridSpec(
            num_scalar_prefetch=1,
            grid=(B, NA_NBLK + 1),
            in_specs=[pl.BlockSpec((1, S, 3 * NA_W), lambda b, j, idx: (b, 0, 0)),
                      pl.BlockSpec(pair_tiles.shape, lambda b, j, idx: (0, 0, 0, 0)),
                      pl.BlockSpec((1, NA_W), lambda b, j, idx: (0, 0)),
                      pl.BlockSpec((1, NA_W), lambda b, j, idx: (0, 0)),
                      pl.BlockSpec((NA_W, NA_W), lambda b, j, idx: (0, 0))],
            out_specs=pl.BlockSpec((1, S, NA_W), lambda b, j, idx: (b, 0, 0)),
            scratch_shapes=[pltpu.VMEM((S, NA_W), BF16), pltpu.VMEM((NA_QB, NA_KB), BF16)]),
        out_shape=jax.ShapeDtypeStruct((B, S, NA_W), BF16),
        compiler_params=_cparams(("parallel", "arbitrary")),
        name="na_attn",
    )(pair_idx, pna, pair_tiles, qg, kg, seg)


def _out_proj_kernel(z_ref, modx_ref, mody_ref, a_ref, m_ref, n_ref, wa_ref, wm_ref, wn_ref, g2_ref, rw_ref, rb_ref,
                     before_ref, below_ref, z1_ref, hs_ref, gs_ref, tm_ref, sm_ref):
    lane = lax.broadcasted_iota(jnp.int32, (TT, 128), 1)
    live = lane < N_EXPERTS
    groups = [slice(i * TT, (i + 1) * TT) for i in range(OT // TT)]

    splits = []
    for rows in groups:
        is_ctx = lax.broadcasted_iota(jnp.int32, (TT, 1), 0) + (pl.program_id(1) * OT + rows.start) >= SEQ
        mod = lambda i: jnp.where(is_ctx, mody_ref[0][:, i * D:(i + 1) * D], modx_ref[0][:, i * D:(i + 1) * D])
        mix = (_dot(a_ref[0, rows, :], wa_ref[...]) + _dot(m_ref[0, rows, :], wm_ref[...])
               + _dot(n_ref[0, rows, :], wn_ref[...]))
        x = z_ref[0, rows, :] + mod(2) * mix
        z1_ref[0, rows, :] = x
        hn = x * lax.rsqrt(jnp.mean(x * x, axis=-1, keepdims=True) + EPS) * g2_ref[...]
        hn = hn * (1.0 + mod(4)) + mod(3)
        h_hi, h_lo = _split_bf16(hn, 2)
        splits.append((h_hi, h_lo))

    affs = [_sigmoid(_dot(h_hi, rw_ref[0]) + (_dot(h_hi, rw_ref[1]) + _dot(h_lo, rw_ref[0])))
            for h_hi, h_lo in splits]

    def cyc(x, k, width):
        fwd = pltpu.roll(x, 128 - k, 1)
        back = pltpu.roll(x, width - k, 1)
        return jnp.where((lane % width) + k < width, fwd, back)

    def rank(x, width, step):
        r = jnp.zeros((TT, 128), F32)
        for k in range(1, width // step):
            y = cyc(x, k * step, width)
            wrapped = (lane % width) + k * step >= width
            beats = jnp.logical_or(y > x, jnp.logical_and(y == x, wrapped))
            r = r + beats.astype(F32)
        return r

    routed = []
    for aff in affs:
        sel = aff + rb_ref[...]
        top2 = rank(sel, EPG, 1) < 2.0
        part = jnp.where(top2, sel, 0.0)
        gscore = part
        for k in range(1, EPG):
            gscore = gscore + cyc(part, k, EPG)
        best = rank(gscore, N_EXPERTS, EPG) < 1.0
        chosen = jnp.logical_and(jnp.logical_and(top2, best), live)
        w = jnp.where(chosen, aff, 0.0)
        group = jnp.sum(jnp.where(chosen, (lane // EPG).astype(F32), 0.0), axis=-1, keepdims=True) * (1.0 / TOP_K)
        routed.append((w / jnp.sum(w, axis=-1, keepdims=True), group))

    sub = lax.broadcasted_iota(jnp.int32, (128, TT), 0)
    row_id = lax.broadcasted_iota(jnp.int32, (SORT_ROWS, TT), 0).astype(F32)
    for gi, ((gate, group), (h_hi, _)) in enumerate(zip(routed, splits)):
        member_t = (lane.astype(F32) == group).astype(F32).T
        member_t = jnp.where(sub < N_GROUPS, member_t, 0.0)
        ahead = _dot(member_t.astype(BF16), before_ref[...])
        count = jnp.sum(member_t, axis=1, keepdims=True)
        padded = ((count.astype(jnp.int32) + (SORT_BLK - 1)) & -SORT_BLK).astype(F32)
        start = _dot(below_ref[...], jnp.broadcast_to(padded, (128, 128)).astype(BF16))
        where_t = jnp.sum(member_t * (start[:, 0:1] + ahead), axis=0, keepdims=True)
        perm = (row_id == where_t).astype(BF16)
        hs_ref[0, gi] = _dot(perm, h_hi)
        gs_ref[0, gi] = sum(_dot(perm, part) for part in _split_bf16(gate, 3))
        tm_ref[0, gi, 0:1, :] = where_t
        tm_ref[0, gi, 1:8, :] = jnp.zeros((7, TT), F32)
        lane_m = lax.broadcasted_iota(jnp.int32, (128, 128), 1)
        sm_ref[0, gi] = jnp.where(lane_m == 0, jnp.broadcast_to(padded, (128, 128)), start)


def _out_proj(z, mods, mla_o, ml_o, na_o, wa, wm, wn, g2, rw, rb):
    B = z.shape[0]
    tok = lambda w_: pl.BlockSpec((1, OT, w_), lambda b, t: (b, t, 0))
    full = lambda a: pl.BlockSpec(a.shape, lambda b, t, _n=a.ndim: (0,) * _n)
    per_group = lambda r, w_: pl.BlockSpec((1, OT // TT, r, w_), lambda b, t: (b, t, 0, 0))
    grouped = lambda r, w_: jax.ShapeDtypeStruct((B, NT, r, w_), F32)
    before = jnp.asarray(np.triu(np.ones((TT, TT)), 1), BF16)
    below = jnp.asarray(np.tril(np.ones((128, 128)), -1), BF16)
    return pl.pallas_call(
        _out_proj_kernel,
        grid=(B, S // OT),
        in_specs=[tok(D),
                  pl.BlockSpec((1, 1, 6 * D), lambda b, t: (2 * b, 0, 0)),
                  pl.BlockSpec((1, 1, 6 * D), lambda b, t: (2 * b + 1, 0, 0)),
                  tok(MLA_H * V_D), tok(ML_WP), tok(NA_W),
                  full(wa), full(wm), full(wn), full(g2), full(rw), full(rb), full(before), full(below)],
        out_specs=[tok(D), per_group(SORT_ROWS, D), per_group(SORT_ROWS, 128), per_group(8, TT), per_group(128, 128)],
        out_shape=[jax.ShapeDtypeStruct((B, S, D), F32),
                   grouped(SORT_ROWS, D), grouped(SORT_ROWS, 128), grouped(8, TT), grouped(128, 128)],
        compiler_params=_cparams(("parallel", "parallel")),
        name="out_proj",
    )(z, mods, mods, mla_o, ml_o, na_o, wa, wm, wn, g2, rw, rb, before, below)


ROW_WAIT = 64


def _row_copy_start(n, row_copy):
    def issue(i, carry):
        row_copy(i).start()
        return carry

    lax.fori_loop(0, n, issue, 0, unroll=8)


def _row_copy_wait(n, slab_copy):
    def drain(i, carry):
        slab_copy(ROW_WAIT).wait()
        return carry

    lax.fori_loop(0, n // ROW_WAIT, drain, 0)


def _experts_kernel(blk_ref, tg_ref, hs_ref, gs_ref, w1_ref, w3_ref, w2_ref, ys_ref, hbuf, gbuf, sems):
    i = pl.program_id(0)
    n_blk = GROUP_T // SORT_BLK

    def fetch(tile, slot):
        def src(q):
            return pl.ds(pl.multiple_of(blk_ref[tile * n_blk + q] * SORT_BLK, SORT_BLK), SORT_BLK)

        def dst(q):
            return pl.ds(pl.multiple_of(q * SORT_BLK, SORT_BLK), SORT_BLK)

        _row_copy_start(n_blk, lambda q: pltpu.make_async_copy(hs_ref.at[src(q)], hbuf.at[slot, dst(q)], sems.at[0, slot]))
        _row_copy_start(n_blk, lambda q: pltpu.make_async_copy(gs_ref.at[src(q)], gbuf.at[slot, dst(q)], sems.at[1, slot]))

    n_used = tg_ref[pl.num_programs(0)]

    @pl.when(i == 0)
    def _():
        fetch(0, 0)

    @pl.when(i < n_used)
    def _():
        slot = i % 2
        _row_copy_wait(GROUP_T, lambda k: pltpu.make_async_copy(
            hs_ref.at[pl.ds(0, k)], hbuf.at[slot, pl.ds(0, k)], sems.at[0, slot]))
        _row_copy_wait(GROUP_T, lambda k: pltpu.make_async_copy(
            gs_ref.at[pl.ds(0, k)], gbuf.at[slot, pl.ds(0, k)], sems.at[1, slot]))

        @pl.when(i + 1 < n_used)
        def _():
            fetch(i + 1, 1 - slot)

        g = tg_ref[i]
        x = hbuf[slot].astype(BF16)
        gates = gbuf[slot]
        lane = lax.broadcasted_iota(jnp.int32, (GROUP_T, 128), 1)
        ups = [(_dot(x, w1_ref[0, e].astype(BF16)), _dot(x, w3_ref[0, e].astype(BF16))) for e in range(EPG)]
        acc = jnp.zeros((GROUP_T, D), F32)
        for e in range(EPG):
            ge = jnp.sum(jnp.where(lane == g * EPG + e, gates, 0.0), axis=-1, keepdims=True)
            a, b = ups[e]
            acc = acc + _dot((_silu(a) * b * ge).astype(BF16), w2_ref[0, e].astype(BF16))
        ys_ref[...] = acc

    @pl.when(i >= n_used)
    def _():
        ys_ref[...] = jnp.zeros((GROUP_T, D), F32)


def _experts(src_blk, tile_group, hs, gs, n_rows, w1, w3, w2, l):
    wspec = lambda k, n: pl.BlockSpec((1, EPG, k, n), lambda i, blk, tg: (l, tg[i], 0, 0))
    return pl.pallas_call(
        _experts_kernel,
        grid_spec=pltpu.PrefetchScalarGridSpec(
            num_scalar_prefetch=2,
            grid=(n_rows // GROUP_T,),
            in_specs=[pl.BlockSpec(memory_space=pl.ANY), pl.BlockSpec(memory_space=pl.ANY),
                      wspec(D, D_FF), wspec(D, D_FF), wspec(D_FF, D)],
            out_specs=pl.BlockSpec((GROUP_T, D), lambda i, blk, tg: (i, 0)),
            scratch_shapes=[pltpu.VMEM((2, GROUP_T, D), F32), pltpu.VMEM((2, GROUP_T, 128), F32),
                            pltpu.SemaphoreType.DMA((2, 2))]),
        out_shape=jax.ShapeDtypeStruct((n_rows, D), F32),
        compiler_params=_cparams(("arbitrary",)),
        name="moe_experts",
    )(src_blk, tile_group, hs, gs, w1, w3, w2)


def _combine_kernel(blk_ref, z1_ref, modx_ref, mody_ref, tm_ref, ys_ref, o_ref, buf, sems):
    t = pl.program_id(1)
    step = pl.program_id(0) * NT + t
    n_blk = SORT_ROWS // SORT_BLK

    def fetch(token_group, slot):
        base = token_group * n_blk
        _row_copy_start(n_blk, lambda q: pltpu.make_async_copy(
            ys_ref.at[pl.ds(pl.multiple_of(blk_ref[base + q] * SORT_BLK, SORT_BLK), SORT_BLK)],
            buf.at[slot, pl.ds(pl.multiple_of(q * SORT_BLK, SORT_BLK), SORT_BLK)], sems.at[slot]))

    @pl.when(step == 0)
    def _():
        fetch(0, 0)

    slot = step % 2
    _row_copy_wait(SORT_ROWS, lambda k: pltpu.make_async_copy(
        ys_ref.at[pl.ds(0, k)], buf.at[slot, pl.ds(0, k)], sems.at[slot]))

    @pl.when(step + 1 < pl.num_programs(0) * NT)
    def _():
        fetch(step + 1, 1 - slot)

    row_id = lax.broadcasted_iota(jnp.int32, (SORT_ROWS, TT), 0).astype(F32)
    perm = (row_id == tm_ref[0, 0, 0:1, :]).astype(BF16)
    y = sum(_dot_tn(perm, part) for part in _split_bf16(buf[slot], 2))
    g2 = jnp.where(t == NT - 1, mody_ref[0][:, 5 * D:], modx_ref[0][:, 5 * D:])
    o_ref[0] = z1_ref[0] + g2 * y


def _combine(back_blk, z1, mods, tok_meta, ys):
    B = z1.shape[0]
    tok = pl.BlockSpec((1, TT, D), lambda b, t, blk: (b, t, 0))
    return pl.pallas_call(
        _combine_kernel,
        grid_spec=pltpu.PrefetchScalarGridSpec(
            num_scalar_prefetch=1,
            grid=(B, NT),
            in_specs=[tok,
                      pl.BlockSpec((1, 1, 6 * D), lambda b, t, blk: (2 * b, 0, 0)),
                      pl.BlockSpec((1, 1, 6 * D), lambda b, t, blk: (2 * b + 1, 0, 0)),
                      pl.BlockSpec((1, 1, 8, TT), lambda b, t, blk: (b, t, 0, 0)),
                      pl.BlockSpec(memory_space=pl.ANY)],
            out_specs=tok,
            scratch_shapes=[pltpu.VMEM((2, SORT_ROWS, D), F32), pltpu.SemaphoreType.DMA((2,))]),
        out_shape=jax.ShapeDtypeStruct((B, S, D), F32),
        compiler_params=_cparams(("arbitrary", "arbitrary")),
        name="moe_combine",
    )(back_blk, z1, mods, mods, tok_meta, ys)


def _group_layout(seg_meta, n_rows):
    n_tg = seg_meta.shape[0]
    seg_len = seg_meta[:, :N_GROUPS, 0].astype(jnp.int32).T
    seg_local = seg_meta[:, :N_GROUPS, 1].astype(jnp.int32).T
    total = jnp.sum(seg_len, axis=1)
    padded = (total + GROUP_T - 1) // GROUP_T * GROUP_T
    group_end = jnp.cumsum(padded)
    seg_start = (group_end - padded)[:, None] + jnp.cumsum(seg_len, axis=1) - seg_len
    flat_start, flat_len, flat_local = seg_start.reshape(-1), seg_len.reshape(-1), seg_local.reshape(-1)

    row0 = (jnp.arange(n_rows // SORT_BLK, dtype=jnp.int32) * SORT_BLK)[:, None]
    covers = (row0 >= flat_start[None, :]) & (row0 < (flat_start + flat_len)[None, :])
    token_group = (jnp.arange(flat_start.shape[0], dtype=jnp.int32) % n_tg)[None, :]
    src_row = jnp.sum(jnp.where(covers, token_group * SORT_ROWS + flat_local[None, :] + row0 - flat_start[None, :], 0),
                      axis=1)
    filled = jnp.any(covers, axis=1)
    src_blk = jnp.where(filled, src_row // SORT_BLK, SORT_ROWS // SORT_BLK - 1)

    tile_group = jnp.sum(jnp.arange(n_rows // GROUP_T)[:, None] * GROUP_T >= group_end[None, :], axis=1)

    local0 = (jnp.arange(SORT_ROWS // SORT_BLK, dtype=jnp.int32) * SORT_BLK)[None, None, :]
    lo, ln, st = seg_local[:, :, None], seg_len[:, :, None], seg_start[:, :, None]
    back_row = jnp.sum(jnp.where((local0 >= lo) & (local0 < lo + ln), st + local0 - lo, 0), axis=0)
    tile_group = jnp.concatenate([jnp.minimum(tile_group, N_GROUPS - 1), group_end[-1:] // GROUP_T])
    return (src_blk.astype(jnp.int32), tile_group.astype(jnp.int32),
            (back_row // SORT_BLK).reshape(-1).astype(jnp.int32))


def _moe(z1, hs, gs, tok_meta, seg_meta, mods, w1, w3, w2, l):
    B = z1.shape[0]
    n_tg = B * NT
    n_rows = (-(-(n_tg * (TT + N_GROUPS * (SORT_BLK - 1))) // GROUP_T) + N_GROUPS) * GROUP_T
    src_blk, tile_group, back_blk = _group_layout(seg_meta.reshape(n_tg, 128, 128), n_rows)
    ys = _experts(src_blk, tile_group, hs.reshape(n_tg * SORT_ROWS, D), gs.reshape(n_tg * SORT_ROWS, 128),
                  n_rows, w1, w3, w2, l)
    return _combine(back_blk, z1, mods, tok_meta, ys)


def _in_proj_layout(w):
    cuts = np.cumsum([Q_RANK, KV_RANK, ROPE_D, ML_W, ML_W, 4 * ML_H])
    qc, ckv, kr, u, zz, g, na = jnp.split(w, [int(v) for v in cuts], axis=-1)
    zeros = lambda n: jnp.zeros((w.shape[0], n), w.dtype)
    out = jnp.concatenate([qc, ckv, zeros(NOPE_D), kr, zeros(HP - QK_D),
                           _pad_heads(u, ML_H, ML_D, HP), _pad_heads(zz, ML_H, ML_D, HP),
                           _gate_order(g), zeros(128 - 4 * ML_H), na], axis=-1)
    assert out.shape[-1] == NP_IN
    return out


def _gate_order(g):
    i_f, f_f, i_b, f_b = jnp.split(g, 4, axis=-1)
    return jnp.concatenate([i_f, i_b, f_f, f_b], axis=-1)


def _pad_heads(v, nh, d, dp):
    lead = v.shape[:-1]
    v = v.reshape(lead + (nh, d))
    v = jnp.pad(v, [(0, 0)] * len(lead) + [(0, 0), (0, dp - d)])
    return v.reshape(lead + (nh * dp,))


def _rope_tables():
    t = np.arange(SEQ)
    row = (t // GRID_W).astype(np.float32)
    col = (t % GRID_W).astype(np.float32)
    quarter = ROPE_D // 4
    inv = jnp.asarray(ROPE_BASE, F32) ** (-jnp.arange(quarter, dtype=F32) / quarter)
    ar = jnp.asarray(row)[:, None] * inv
    ac = jnp.asarray(col)[:, None] * inv
    ang = jnp.concatenate([ar, ar, ac, ac], axis=-1)
    cos = jnp.ones((S, HP), F32).at[:SEQ, NOPE_D:QK_D].set(jnp.cos(ang))
    sin = jnp.zeros((S, HP), F32).at[:SEQ, NOPE_D:QK_D].set(jnp.sin(ang))
    return cos, sin


def _rotate_half_index():
    q = ROPE_D // 4
    src = np.arange(QK_D)
    sign = np.zeros((QK_D,), np.float32)
    for blk in range(2):
        lo = NOPE_D + 2 * q * blk
        src[lo:lo + q] = np.arange(lo + q, lo + 2 * q)
        sign[lo:lo + q] = -1.0
        src[lo + q:lo + 2 * q] = np.arange(lo, lo + q)
        sign[lo + q:lo + 2 * q] = 1.0
    return src, sign


def _rotate_half(w):
    src, sign = _rotate_half_index()
    return w[..., src] * sign


NA_NDR = 2 * WIN_R - 1
NA_NPAIR = 3 * NA_NDR


def _na_pair_index():
    idx = np.zeros((NA_NBLK, NA_QROWS, NA_KROWS // 2), np.int32)
    for blk in range(NA_NBLK):
        k0 = int(np.clip(blk * NA_QROWS - WIN_R // 2, 0, ROWS - NA_KROWS))
        for i in range(NA_QROWS):
            qr = blk * NA_QROWS + i
            rs = int(np.clip(qr - WIN_R // 2, 0, ROWS - WIN_R))
            assert k0 <= rs and rs + WIN_R <= k0 + NA_KROWS
            for p in range(NA_KROWS // 2):
                kr = k0 + 2 * p
                dr = kr - qr + WIN_R - 1
                left = rs <= kr < rs + WIN_R
                right = rs <= kr + 1 < rs + WIN_R
                if left and right:
                    idx[blk, i, p] = 1 + dr
                elif left:
                    idx[blk, i, p] = NA_NDR + dr
                elif right:
                    idx[blk, i, p] = 2 * NA_NDR + dr + 1
    return idx.reshape(-1)


def _na_pair_tiles(rpb):
    cq = np.arange(GRID_W)
    cs = np.clip(cq - WIN_C // 2, 0, GRID_W - WIN_C)
    col_ok = (cq[None, :] >= cs[:, None]) & (cq[None, :] < cs[:, None] + WIN_C)
    dc = np.clip(cq[None, :] - cq[:, None], -(WIN_C - 1), WIN_C - 1) + (WIN_C - 1)
    onehot = jnp.asarray(np.eye(2 * WIN_C - 1, dtype=np.float32)[dc])
    tiles = jnp.einsum('hrc,qkc->hrqk', rpb, onehot, precision=lax.Precision.HIGHEST)
    tiles = jnp.where(jnp.asarray(col_ok), tiles * float(np.log2(np.e)), NEG)
    masked = jnp.full_like(tiles, NEG)
    both = jnp.concatenate([tiles[:, :-1], tiles[:, 1:]], axis=-1)
    left = jnp.concatenate([tiles, masked], axis=-1)
    right = jnp.concatenate([masked, tiles], axis=-1)
    none = jnp.concatenate([masked[:, :1], masked[:, :1]], axis=-1)
    out = jnp.concatenate([none, both, left, right], axis=1)
    assert out.shape[1] == NA_NPAIR
    return out.astype(BF16)


def kernel(x, c, ctx, c_ctx, w_mod, b_mod, norm1_g, norm2_g, w_in, w_out, mla_qnorm_g, mla_w_uq, mla_kvnorm_g, mla_w_ukv, mla_q_g, mla_k_g, ml_conv_w, ml_conv_b, ml_w_q, ml_w_k, ml_w_v, ml_gate_b, ml_norm_g, ml_skip, na_q_g, na_k_g, na_rpb, router_w, router_b, moe_w1, moe_w3, moe_w2):
    B = x.shape[0]
    z = jnp.concatenate([x, ctx], axis=1)
    cc = jnp.zeros((16, D), F32).at[:B].set(c).at[B].set(c_ctx)
    mod_all = _modulation(cc, w_mod, b_mod)
    cos, sin = _rope_tables()
    rot_src, rot_sign = _rotate_half_index()
    rot_np = np.zeros((HP, HP), np.float32)
    rot_np[rot_src, np.arange(QK_D)] = rot_sign
    rot_mat = jnp.asarray(rot_np, BF16)
    ones_hp = jnp.ones((HP, HP), BF16)
    seg = jnp.asarray(np.kron(np.eye(NA_H), np.ones((NA_D, NA_D))), BF16)
    tril = jnp.asarray(np.stack([np.tril(np.ones((ML_CHUNK, ML_CHUNK))), np.triu(np.ones((ML_CHUNK, ML_CHUNK)))]), BF16)
    pair_idx = jnp.asarray(_na_pair_index())
    rw = jnp.stack(_split_bf16(jnp.pad(router_w, ((0, 0), (0, 128 - N_EXPERTS))), 2))
    rb = jnp.pad(router_b, (0, 128 - N_EXPERTS), constant_values=NEG).reshape(1, 128)

    def pad_lanes(v, n):
        return jnp.pad(v, [(0, 0)] * (v.ndim - 1) + [(0, n - v.shape[-1])])

    for l in range(DEPTH):
        mx = mod_all[l, :B]
        my = jnp.broadcast_to(mod_all[l, B], (B, 6 * D))
        mods = jnp.stack([mx, my], axis=1).reshape(2 * B, 1, 6 * D)

        w_in_p = _in_proj_layout(w_in[l]).astype(BF16)
        pmla, pu, pg, pna = _in_proj(z, mods, norm1_g[l].reshape(1, D), w_in_p)

        wq = jnp.transpose(mla_w_uq[l].reshape(Q_RANK, MLA_H, QK_D), (1, 0, 2))
        wuq = jnp.concatenate([pad_lanes(wq, HP), pad_lanes(_rotate_half(wq), HP)], axis=-1).astype(BF16)
        wukv = jnp.transpose(mla_w_ukv[l].reshape(KV_RANK, MLA_H, NOPE_D + V_D), (1, 0, 2))
        wuk = pad_lanes(wukv[..., :NOPE_D], HP).astype(BF16)
        wuv = pad_lanes(wukv[..., NOPE_D:], HP).astype(BF16)
        qg, kg = mla_q_g[l], mla_k_g[l]
        q_scale = float(QK_D ** -0.5 * np.log2(np.e))
        tabs = jnp.stack([cos * pad_lanes(qg, HP) * q_scale, sin * pad_lanes(jnp.abs(rot_sign) * qg[rot_src], HP) * q_scale,
                          cos * pad_lanes(kg, HP), sin * pad_lanes(jnp.abs(rot_sign) * kg[rot_src], HP)])
        heads_last = lambda w_: jnp.transpose(w_, (1, 0, 2)).reshape(w_.shape[1], -1)
        q, k, v = _mla_prep(pmla, tabs, mla_qnorm_g[l].reshape(1, Q_RANK), heads_last(wuq),
                            mla_kvnorm_g[l].reshape(1, KV_RANK), heads_last(wuk), heads_last(wuv), rot_mat, ones_hp)
        mla_o = _mla_attn(q, k, v)

        padh = lambda a: _pad_heads(a, ML_H, ML_D, HP)
        padw = lambda w_: jnp.pad(w_, ((0, 0), (0, HP - ML_D), (0, HP - ML_D))).astype(BF16)
        cw = jnp.pad(padh(ml_conv_w[l]), ((0, 8 - 3), (0, 0)))
        ml_o = _mlstm(pu, pg, cw, padh(ml_conv_b[l]).reshape(1, ML_WP),
                      padw(ml_w_q[l]), padw(jnp.swapaxes(ml_w_k[l], 1, 2) * (ML_D ** -0.5)), padw(ml_w_v[l]),
                      pad_lanes(_gate_order(ml_gate_b[l]).reshape(1, 4 * ML_H), 128),
                      padh(ml_norm_g[l]).reshape(1, ML_WP), padh(ml_skip[l]).reshape(1, ML_WP), tril)

        na_o = _na_attn(pair_idx, pna, _na_pair_tiles(na_rpb[l]), jnp.tile(na_q_g[l], NA_H).reshape(1, NA_W),
                        jnp.tile(na_k_g[l], NA_H).reshape(1, NA_W), seg)

        wo = w_out[l]
        wa = wo[:MLA_H * V_D].astype(BF16)
        wm = jnp.pad(wo[MLA_H * V_D:MLA_H * V_D + ML_W].reshape(ML_H, ML_D, D),
                     ((0, 0), (0, HP - ML_D), (0, 0))).reshape(ML_WP, D).astype(BF16)
        wn = wo[MLA_H * V_D + ML_W:].astype(BF16)
        z1, hs, gs, tok_meta, seg_meta = _out_proj(z, mods, mla_o, ml_o, na_o, wa, wm, wn,
                                                   norm2_g[l].reshape(1, D), rw, rb)
        z = _moe(z1, hs, gs, tok_meta, seg_meta, mods, moe_w1, moe_w3, moe_w2, l)

    return z[:, :SEQ]
```

```python
import functools

import numpy as np
import jax
import jax.numpy as jnp
from jax import lax
from jax.experimental import pallas as pl
from jax.experimental.pallas import tpu as pltpu

F32 = jnp.float32
BF16 = jnp.bfloat16

D = 1024
SEQ = 2048
CTX = 256
S = SEQ + CTX
DEPTH = 4
GRID_W = 64
ROWS = SEQ // GRID_W
EPS = 1e-6

MLA_H = 6
Q_RANK = 256
KV_RANK = 128
NOPE_D = 64
ROPE_D = 32
V_D = 64
QK_D = NOPE_D + ROPE_D
ROPE_BASE = 10000.0

ML_H = 4
ML_D = 96
ML_W = ML_H * ML_D
HP = 128
ML_WP = ML_H * HP
ML_CHUNK = 128
N_CHUNK = S // ML_CHUNK
N_CTX_CHUNK = CTX // ML_CHUNK

NA_H = 4
NA_D = 64
NA_W = NA_H * NA_D
WIN_R = 8
WIN_C = 16
NA_QROWS = 8
NA_KROWS = 16
NA_QB = NA_QROWS * GRID_W
NA_KB = NA_KROWS * GRID_W
NA_NBLK = ROWS // NA_QROWS

N_EXPERTS = 16
N_GROUPS = 4
EPG = N_EXPERTS // N_GROUPS
D_FF = 256

TT = 256
NT = S // TT
OT = 768
TOP_K = 2
SORT_BLK = 8
SORT_ROWS = 320
GROUP_T = 512
NEG = -1e30

C_QC = 0
C_CKV = 256
C_KR = 384
C_U = 512
C_Z = C_U + ML_WP
C_G = C_Z + ML_WP
C_NA = C_G + 128
NP_IN = C_NA + 3 * NA_W

VMEM_LIMIT = 56 * 1024 * 1024


def _cparams(sem):
    return pltpu.CompilerParams(dimension_semantics=sem, vmem_limit_bytes=VMEM_LIMIT)


def _sigmoid(x):
    return 1.0 / (1.0 + jnp.exp(-x))


def _silu(x):
    return x * _sigmoid(x)


def _dot(a, b):
    return jnp.dot(a, b, preferred_element_type=F32)


def _dot_nt(a, b):
    return lax.dot_general(a, b, (((1,), (1,)), ((), ())), preferred_element_type=F32)


def _dot_tn(a, b):
    return lax.dot_general(a, b, (((0,), (0,)), ((), ())), preferred_element_type=F32)


def _dot_hi(a, b):
    return jnp.dot(a, b, preferred_element_type=F32, precision=lax.Precision.HIGHEST)


def _split_bf16(x, n):
    parts = []
    for _ in range(n):
        p = x.astype(BF16)
        parts.append(p)
        x = x - p.astype(F32)
    return parts


def _mod_rows(mod_ref, t):
    m = mod_ref[0]
    return [m[:, i * D:(i + 1) * D] for i in range(6)]


def _mod_kernel(c_ref, w_ref, b_ref, o_ref):
    sc = _silu(c_ref[...])
    o_ref[0] = _dot_hi(sc, w_ref[0]) + b_ref[0]


def _modulation(cc, w_mod, b_mod):
    nc = 6
    return pl.pallas_call(
        _mod_kernel,
        grid=(DEPTH, nc),
        in_specs=[pl.BlockSpec((16, D), lambda l, j: (0, 0)),
                  pl.BlockSpec((1, D, D), lambda l, j: (l, 0, j)),
                  pl.BlockSpec((1, 1, D), lambda l, j: (l, 0, j))],
        out_specs=pl.BlockSpec((1, 16, D), lambda l, j: (l, 0, j)),
        out_shape=jax.ShapeDtypeStruct((DEPTH, 16, 6 * D), F32),
        compiler_params=_cparams(("parallel", "parallel")),
        name="modulation",
    )(cc, w_mod, b_mod.reshape(DEPTH, 1, 6 * D))


def _in_proj_kernel(z_ref, mod_ref, g_ref, w_ref, pmla_ref, pu_ref, pg_ref, pna_ref):
    sh1, sc1 = _mod_rows(mod_ref, None)[:2]
    x = z_ref[0]
    xn = x * lax.rsqrt(jnp.mean(x * x, axis=-1, keepdims=True) + EPS) * g_ref[...]
    xn = xn * (1.0 + sc1) + sh1
    p = _dot(xn.astype(BF16), w_ref[...])
    pmla_ref[0] = p[:, :C_U].astype(BF16)
    pu_ref[0] = p[:, C_U:C_G].astype(BF16)
    pg_ref[0] = p[:, C_G:C_NA]
    pna_ref[0] = p[:, C_NA:].astype(BF16)


def _mod_spec():
    return pl.BlockSpec((1, 1, 6 * D), lambda b, t: (2 * b + t // (NT - 1), 0, 0))


def _in_proj(z, mods, g, w):
    B = z.shape[0]
    tok = lambda w_: pl.BlockSpec((1, TT, w_), lambda b, t: (b, t, 0))
    return pl.pallas_call(
        _in_proj_kernel,
        grid=(B, NT),
        in_specs=[tok(D), _mod_spec(),
                  pl.BlockSpec((1, D), lambda b, t: (0, 0)),
                  pl.BlockSpec((D, NP_IN), lambda b, t: (0, 0))],
        out_specs=[tok(C_U), tok(2 * ML_WP), tok(128), tok(3 * NA_W)],
        out_shape=[jax.ShapeDtypeStruct((B, S, C_U), BF16),
                   jax.ShapeDtypeStruct((B, S, 2 * ML_WP), BF16),
                   jax.ShapeDtypeStruct((B, S, 128), F32),
                   jax.ShapeDtypeStruct((B, S, 3 * NA_W), BF16)],
        compiler_params=_cparams(("parallel", "parallel")),
        name="in_proj",
    )(z, mods, g, w)


def _mla_prep_kernel(p_ref, tab_ref, qng_ref, wuq_ref, kvng_ref, wuk_ref, wuv_ref, rot_ref, ones_ref,
                     q_out, k_out, v_out):
    p = p_ref[0].astype(F32)
    qc = p[:, C_QC:C_CKV]
    ckv = p[:, C_CKV:C_KR]
    kr = p[:, C_KR:C_U]
    qcn = (qc * lax.rsqrt(jnp.mean(qc * qc, axis=-1, keepdims=True) + EPS) * qng_ref[...]).astype(BF16)
    ckvn = (ckv * lax.rsqrt(jnp.mean(ckv * ckv, axis=-1, keepdims=True) + EPS) * kvng_ref[...]).astype(BF16)
    lane = lax.broadcasted_iota(jnp.int32, (TT, HP), 1)
    ones = ones_ref[...]
    kr_rot = _dot(kr.astype(BF16), rot_ref[...])

    q_all = _dot(qcn, wuq_ref[...])
    k_all = _dot(ckvn, wuk_ref[...])
    v_all = _dot(ckvn, wuv_ref[...])
    qs = [q_all[:, 2 * h * HP:(2 * h + 1) * HP] for h in range(MLA_H)]
    q_rots = [q_all[:, (2 * h + 1) * HP:(2 * h + 2) * HP] for h in range(MLA_H)]
    ks = [k_all[:, h * HP:(h + 1) * HP] + kr for h in range(MLA_H)]
    ss_q = [_dot((x * x).astype(BF16), ones) for x in qs]
    ss_k = [_dot((x * x).astype(BF16), ones) for x in ks]

    def norm_rope(x, x_rot, ss, cos_g, sin_g):
        return lax.rsqrt(ss * (1.0 / QK_D) + EPS) * (x * cos_g + x_rot * sin_g)

    for h in range(MLA_H):
        q_out[0, h] = norm_rope(qs[h], q_rots[h], ss_q[h], tab_ref[0], tab_ref[1]).astype(BF16)
        k_out[0, h] = norm_rope(ks[h], kr_rot, ss_k[h], tab_ref[2], tab_ref[3]).astype(BF16)
        v_out[0, h] = jnp.where(lane < V_D, v_all[:, h * HP:(h + 1) * HP], 1.0).astype(BF16)


def _mla_prep(pmla, tabs, qng, wuq, kvng, wuk, wuv, rot, ones):
    B = pmla.shape[0]
    full = lambda a: pl.BlockSpec(a.shape, lambda b, t, _n=a.ndim: (0,) * _n)
    hd = lambda w_: pl.BlockSpec((1, MLA_H, TT, w_), lambda b, t: (b, 0, t, 0))
    return pl.pallas_call(
        _mla_prep_kernel,
        grid=(B, NT),
        in_specs=[pl.BlockSpec((1, TT, C_U), lambda b, t: (b, t, 0)),
                  pl.BlockSpec((4, TT, HP), lambda b, t: (0, t, 0)),
                  full(qng), full(wuq), full(kvng), full(wuk), full(wuv), full(rot), full(ones)],
        out_specs=[hd(HP), hd(HP), hd(HP)],
        out_shape=[jax.ShapeDtypeStruct((B, MLA_H, S, HP), BF16),
                   jax.ShapeDtypeStruct((B, MLA_H, S, HP), BF16),
                   jax.ShapeDtypeStruct((B, MLA_H, S, HP), BF16)],
        compiler_params=_cparams(("parallel", "parallel")),
        name="mla_prep",
    )(pmla, tabs, qng, wuq, kvng, wuk, wuv, rot, ones)


def _mla_attn_kernel(q_ref, k_ref, v_ref, o_ref):
    t = pl.program_id(1)

    def attend(k_lo, k_n):
        def scores(h):
            return _dot_nt(q_ref[0, h], k_ref[0, h, k_lo:k_lo + k_n, :])

        outs = []
        s_next = scores(0)
        for h in range(MLA_H):
            s = s_next
            if h + 1 < MLA_H:
                s_next = scores(h + 1)
            m = jnp.max(s, axis=-1, keepdims=True)
            p = jnp.exp2(s - m)
            pv = _dot(p.astype(BF16), v_ref[0, h, k_lo:k_lo + k_n, :])
            outs.append(pv[:, :V_D] / pv[:, V_D:V_D + 1])
        o_ref[0] = jnp.concatenate(outs, axis=-1).astype(BF16)

    @pl.when(t < NT - 1)
    def _():
        attend(0, S)

    @pl.when(t == NT - 1)
    def _():
        attend(SEQ, CTX)


def _mla_attn(q, k, v):
    B = q.shape[0]
    return pl.pallas_call(
        _mla_attn_kernel,
        grid=(B, NT),
        in_specs=[pl.BlockSpec((1, MLA_H, TT, HP), lambda b, t: (b, 0, t, 0)),
                  pl.BlockSpec((1, MLA_H, S, HP), lambda b, t: (b, 0, 0, 0)),
                  pl.BlockSpec((1, MLA_H, S, HP), lambda b, t: (b, 0, 0, 0))],
        out_specs=pl.BlockSpec((1, TT, MLA_H * V_D), lambda b, t: (b, t, 0)),
        out_shape=jax.ShapeDtypeStruct((B, S, MLA_H * V_D), BF16),
        compiler_params=_cparams(("parallel", "arbitrary")),
        name="mla_attn",
    )(q, k, v)


def _log_sigmoid(x):
    return jnp.minimum(x, 0.0) - jnp.log(1.0 + jnp.exp(-jnp.abs(x)))


def _mlstm_kernel(pu_ref, pg_ref, cw_ref, cb_ref, wq_ref, wk_ref, wv_ref, gb_ref, ng_ref, sk_ref,
                  tril_ref, o_ref, uc_s, q_s, kt_s, v_s, h_s, c_s, m_s, pm_s, b_s, rt_s):
    CA = 2 * ML_CHUNK
    row = lax.broadcasted_iota(jnp.int32, (CA, ML_WP), 0)

    def conv_body(i, carry):
        r0 = pl.multiple_of(i * CA, CA)
        x = pu_ref[0, pl.ds(r0, CA), 0:ML_WP].astype(F32)
        pr = pl.multiple_of(jnp.maximum(r0 - 16, 0), 16)
        nx = pl.multiple_of(jnp.minimum(r0 + CA, S - 16), 16)
        prev = pu_ref[0, pl.ds(pr, 16), 0:ML_WP].astype(F32)[15:16, :]
        nxt = pu_ref[0, pl.ds(nx, 16), 0:ML_WP].astype(F32)[0:1, :]
        seq_start = jnp.logical_or(r0 == 0, r0 == SEQ)
        seq_end = jnp.logical_or(r0 + CA == SEQ, r0 + CA == S)
        prev = jnp.where(seq_start, 0.0, prev)
        nxt = jnp.where(seq_end, 0.0, nxt)
        up = jnp.where(row == 0, prev, pltpu.roll(x, 1, 0))
        dn = jnp.where(row == CA - 1, nxt, pltpu.roll(x, CA - 1, 0))
        uc = _silu(cw_ref[0:1, :] * up + cw_ref[1:2, :] * x + cw_ref[2:3, :] * dn + cb_ref[...])
        ucb = uc.astype(BF16)
        uc_s[pl.ds(r0, CA), :] = ucb
        xb = x.astype(BF16)
        for h in range(ML_H):
            sl = slice(h * HP, (h + 1) * HP)
            q_s[pl.ds(r0, CA), sl] = _dot(ucb[:, sl], wq_ref[h]).astype(BF16)
            kt = _dot_nt(wk_ref[h], ucb[:, sl])
            kt_s[2 * i, sl, :] = kt[:, :ML_CHUNK].astype(BF16)
            kt_s[2 * i + 1, sl, :] = kt[:, ML_CHUNK:].astype(BF16)
            v_s[pl.ds(r0, CA), 2 * h * HP:(2 * h + 1) * HP] = _dot(xb[:, sl], wv_ref[h]).astype(BF16)
            v_s[pl.ds(r0, CA), (2 * h + 1) * HP:(2 * h + 2) * HP] = jnp.ones((CA, HP), BF16)

        for half in range(2):
            rows = pl.ds(r0 + half * ML_CHUNK, ML_CHUNK)
            g = pg_ref[0, rows, :] + gb_ref[...]
            parts = _split_bf16(_log_sigmoid(g), 3)
            cum_f = sum(_dot(tril_ref[0], part) for part in parts)
            cum_b = sum(_dot(tril_ref[1], part) for part in parts)
            bsh = pltpu.roll(jnp.where(bwd_lane, cum_b, cum_f), 128 - n_chain, 1)
            r = g - bsh
            pf = r
            pb = r
            k = 1
            while k < ML_CHUNK:
                pf = jnp.maximum(pf, jnp.where(ti >= k, pltpu.roll(pf, k, 0), NEG))
                pb = jnp.maximum(pb, jnp.where(ti < ML_CHUNK - k, pltpu.roll(pb, ML_CHUNK - k, 0), NEG))
                k *= 2
            pm_s[2 * i + half] = jnp.where(bwd_lane, pb, pf)
            b_s[2 * i + half] = bsh
            rt_s[2 * i + half] = r.T[0:n_chain, :]
        return carry

    n_chain = 2 * ML_H
    ti = lax.broadcasted_iota(jnp.int32, (ML_CHUNK, ML_CHUNK), 0)
    si = lax.broadcasted_iota(jnp.int32, (ML_CHUNK, ML_CHUNK), 1)
    bwd_lane = (si % n_chain) >= ML_H
    lax.fori_loop(0, S // CA, conv_body, 0)

    c_s[...] = jnp.zeros_like(c_s)
    m_s[...] = jnp.zeros_like(m_s)
    masks = (si <= ti, si >= ti)

    def scan_body(j, carry):
        chunk = (jnp.where(j < N_CTX_CHUNK, j + N_CHUNK - N_CTX_CHUNK, j - N_CTX_CHUNK), N_CHUNK - 1 - j)
        chains = []
        for d in range(2):
            r0 = pl.multiple_of(chunk[d] * ML_CHUNK, ML_CHUNK)
            p_col = pm_s[chunk[d]]
            bsh = b_s[chunk[d]]
            r_t = rt_s[chunk[d]]
            end = ML_CHUNK - 1 if d == 0 else 0
            for h in range(ML_H):
                c = d * ML_H + h
                sl = slice(h * HP, (h + 1) * HP)
                qc = q_s[pl.ds(r0, ML_CHUNK), sl]
                kt = kt_s[chunk[d], sl, :]
                vx = v_s[pl.ds(r0, ML_CHUNK), 2 * h * HP:(2 * h + 2) * HP]
                r_row = r_t[c:c + 1, :]
                m = m_s[c]
                st = c_s[c]
                big_m = jnp.maximum(m, jnp.broadcast_to(p_col[:, c:c + 1], (ML_CHUNK, HP)))
                b_b = jnp.broadcast_to(bsh[:, c:c + 1], (ML_CHUNK, HP))
                m_end = big_m[end:end + 1, :]
                ktw = (kt.astype(F32) * jnp.exp(r_row - m_end)).astype(BF16)
                chains.append(dict(d=d, r0=r0, sl=sl, c=c, vx=vx, r_row=r_row, m=m, st=st, big_m=big_m,
                                   b_b=b_b, m_end=m_end, end=end,
                                   qk=_dot(qc, kt), inter=_dot(qc, st.astype(BF16)), upd=_dot(ktw, vx)))
        for ch in chains:
            dw = jnp.exp(jnp.where(masks[ch["d"]], ch["r_row"] - ch["big_m"], NEG))
            ch["intra"] = _dot((ch["qk"] * dw).astype(BF16), ch["vx"])
        for ch in chains:
            m, big_m, inter, intra, end = ch["m"], ch["big_m"], ch["inter"], ch["intra"], ch["end"]
            iw = jnp.exp(m - big_m)
            num = iw * inter[:, :HP] + intra[:, :HP]
            nq = iw * inter[:, HP:] + intra[:, HP:]
            hv = num / jnp.maximum(jnp.abs(nq), jnp.exp(-(ch["b_b"] + big_m)))
            a = jnp.exp(m - ch["m_end"])
            ch["out"] = (hv, jnp.concatenate([a, a], axis=1) * ch["st"] + ch["upd"],
                         ch["b_b"][end:end + 1, :] + ch["m_end"])
        for ch in chains:
            hv, st_new, m_new = ch["out"]
            h_s[ch["d"], pl.ds(ch["r0"], ML_CHUNK), ch["sl"]] = hv
            c_s[ch["c"]] = st_new
            m_s[ch["c"]] = m_new
        return carry

    lax.fori_loop(0, N_CHUNK, scan_body, 0)

    live = (lax.broadcasted_iota(jnp.int32, (CA, HP), 1) < ML_D).astype(F32)

    def out_body(i, carry):
        r0 = pl.multiple_of(i * CA, CA)
        for h in range(ML_H):
            sl = slice(h * HP, (h + 1) * HP)
            hh = h_s[0, pl.ds(r0, CA), sl] + h_s[1, pl.ds(r0, CA), sl]
            mu = jnp.sum(hh, axis=-1, keepdims=True) * (1.0 / ML_D)
            dv = (hh - mu) * live
            var = jnp.sum(dv * dv, axis=-1, keepdims=True) * (1.0 / ML_D)
            hn = dv * lax.rsqrt(var + EPS) * ng_ref[:, sl]
            uc = uc_s[pl.ds(r0, CA), sl].astype(F32)
            zz = pu_ref[0, pl.ds(r0, CA), ML_WP + h * HP:ML_WP + (h + 1) * HP].astype(F32)
            o_ref[0, pl.ds(r0, CA), sl] = ((hn + sk_ref[:, sl] * uc) * _silu(zz)).astype(BF16)
        return carry

    lax.fori_loop(0, S // CA, out_body, 0)


def _mlstm(pu, pg, cw, cb, wq, wk, wv, gb, ng, sk, tril):
    B = pu.shape[0]
    full = lambda a: pl.BlockSpec(a.shape, lambda b, _n=a.ndim: (0,) * _n)
    n_chain = 2 * ML_H
    return pl.pallas_call(
        _mlstm_kernel,
        grid=(B,),
        in_specs=[pl.BlockSpec((1, S, 2 * ML_WP), lambda b: (b, 0, 0)),
                  pl.BlockSpec((1, S, 128), lambda b: (b, 0, 0)),
                  full(cw), full(cb), full(wq), full(wk), full(wv), full(gb), full(ng), full(sk), full(tril)],
        out_specs=pl.BlockSpec((1, S, ML_WP), lambda b: (b, 0, 0)),
        out_shape=jax.ShapeDtypeStruct((B, S, ML_WP), BF16),
        scratch_shapes=[pltpu.VMEM((S, ML_WP), BF16), pltpu.VMEM((S, ML_WP), BF16),
                        pltpu.VMEM((N_CHUNK, ML_WP, ML_CHUNK), BF16), pltpu.VMEM((S, 2 * ML_WP), BF16),
                        pltpu.VMEM((2, S, ML_WP), F32),
                        pltpu.VMEM((n_chain, HP, 2 * HP), F32),
                        pltpu.VMEM((n_chain, 1, HP), F32),
                        pltpu.VMEM((N_CHUNK, ML_CHUNK, 128), F32),
                        pltpu.VMEM((N_CHUNK, ML_CHUNK, 128), F32),
                        pltpu.VMEM((N_CHUNK, n_chain, ML_CHUNK), F32)],
        compiler_params=_cparams(("parallel",)),
        name="mlstm",
    )(pu, pg, cw, cb, wq, wk, wv, gb, ng, sk, tril)


def _na_kernel(idx_ref, p_ref, pt_ref, qg_ref, kg_ref, seg_ref, o_ref, kn_s, bias_s):
    j = pl.program_id(1)
    seg = seg_ref[...]

    def headnorm(x, g):
        ss = _dot((x * x).astype(BF16), seg)
        return x * lax.rsqrt(ss * (1.0 / NA_D) + EPS) * g

    @pl.when(j == 0)
    def _():
        def body(i, carry):
            r0 = pl.multiple_of(i * TT, TT)
            kk = p_ref[0, pl.ds(r0, TT), NA_W:2 * NA_W].astype(F32)
            kn_s[pl.ds(r0, TT), :] = headnorm(kk, kg_ref[...]).astype(BF16)
            return carry
        lax.fori_loop(0, NT, body, 0)

    scale = float(NA_D ** -0.5 * np.log2(np.e))
    kctx = kn_s[SEQ:S, :]
    vctx = p_ref[0, SEQ:S, 2 * NA_W:3 * NA_W]

    @pl.when(j < NA_NBLK)
    def _():
        q0 = pl.multiple_of(j * NA_QB, NA_QB)
        k0 = pl.multiple_of(jnp.clip(j * NA_QROWS - WIN_R // 2, 0, ROWS - NA_KROWS) * GRID_W, 256)
        q = headnorm(p_ref[0, pl.ds(q0, NA_QB), 0:NA_W].astype(F32), qg_ref[...]) * scale
        kl = kn_s[pl.ds(k0, NA_KB), :]
        vl = p_ref[0, pl.ds(k0, NA_KB), 2 * NA_W:3 * NA_W]
        head = lax.broadcasted_iota(jnp.int32, (NA_QB, NA_W), 1) // NA_D
        acc = jnp.zeros((NA_QB, NA_W), F32)

        def scores(h):
            qm = jnp.where(head == h, q, 0.0).astype(BF16)
            for i in range(NA_QROWS):
                for p in range(NA_KROWS // 2):
                    code = idx_ref[(j * NA_QROWS + i) * (NA_KROWS // 2) + p]
                    bias_s[i * GRID_W:(i + 1) * GRID_W, p * 2 * GRID_W:(p + 1) * 2 * GRID_W] = pt_ref[h, code]
            return _dot_nt(qm, kl) + bias_s[...].astype(F32), _dot_nt(qm, kctx)

        s_next = scores(0)
        for h in range(NA_H):
            s1, s2 = s_next
            if h + 1 < NA_H:
                s_next = scores(h + 1)
            m = jnp.maximum(jnp.max(s1, axis=-1, keepdims=True), jnp.max(s2, axis=-1, keepdims=True))
            p1 = jnp.exp2(s1 - m)
            p2 = jnp.exp2(s2 - m)
            l = jnp.sum(p1, axis=-1, keepdims=True) + jnp.sum(p2, axis=-1, keepdims=True)
            o = (_dot(p1.astype(BF16), vl) + _dot(p2.astype(BF16), vctx)) / l
            acc = jnp.where(head == h, o, acc)
        o_ref[0, pl.ds(q0, NA_QB), :] = acc.astype(BF16)

    @pl.when(j == NA_NBLK)
    def _():
        q = headnorm(p_ref[0, SEQ:S, 0:NA_W].astype(F32), qg_ref[...]) * scale
        head = lax.broadcasted_iota(jnp.int32, (CTX, NA_W), 1) // NA_D
        acc = jnp.zeros((CTX, NA_W), F32)
        for h in range(NA_H):
            qm = jnp.where(head == h, q, 0.0).astype(BF16)
            s2 = _dot_nt(qm, kctx)
            m = jnp.max(s2, axis=-1, keepdims=True)
            p2 = jnp.exp2(s2 - m)
            l = jnp.sum(p2, axis=-1, keepdims=True)
            o = _dot(p2.astype(BF16), vctx) / l
            acc = jnp.where(head == h, o, acc)
        o_ref[0, SEQ:S, :] = acc.astype(BF16)


def _na_attn(pair_idx, pna, pair_tiles, qg, kg, seg):
    B = pna.shape[0]
    return pl.pallas_call(
        _na_kernel,
        grid_spec=pltpu.PrefetchScalarGridSpec(
            num_scalar_prefetch=1,
            grid=(B, NA_NBLK + 1),
            in_specs=[pl.BlockSpec((1, S, 3 * NA_W), lambda b, j, idx: (b, 0, 0)),
                      pl.BlockSpec(pair_tiles.shape, lambda b, j, idx: (0, 0, 0, 0)),
                      pl.BlockSpec((1, NA_W), lambda b, j, idx: (0, 0)),
                      pl.BlockSpec((1, NA_W), lambda b, j, idx: (0, 0)),
                      pl.BlockSpec((NA_W, NA_W), lambda b, j, idx: (0, 0))],
            out_specs=pl.BlockSpec((1, S, NA_W), lambda b, j, idx: (b, 0, 0)),
            scratch_shapes=[pltpu.VMEM((S, NA_W), BF16), pltpu.VMEM((NA_QB, NA_KB), BF16)]),
        out_shape=jax.ShapeDtypeStruct((B, S, NA_W), BF16),
        compiler_params=_cparams(("parallel", "arbitrary")),
        name="na_attn",
    )(pair_idx, pna, pair_tiles, qg, kg, seg)


def _out_proj_kernel(z_ref, modx_ref, mody_ref, a_ref, m_ref, n_ref, wa_ref, wm_ref, wn_ref, g2_ref, rw_ref, rb_ref,
                     before_ref, below_ref, z1_ref, hs_ref, gs_ref, tm_ref, sm_ref):
    lane = lax.broadcasted_iota(jnp.int32, (TT, 128), 1)
    live = lane < N_EXPERTS
    groups = [slice(i * TT, (i + 1) * TT) for i in range(OT // TT)]

    splits = []
    for rows in groups:
        is_ctx = lax.broadcasted_iota(jnp.int32, (TT, 1), 0) + (pl.program_id(1) * OT + rows.start) >= SEQ
        mod = lambda i: jnp.where(is_ctx, mody_ref[0][:, i * D:(i + 1) * D], modx_ref[0][:, i * D:(i + 1) * D])
        mix = (_dot(a_ref[0, rows, :], wa_ref[...]) + _dot(m_ref[0, rows, :], wm_ref[...])
               + _dot(n_ref[0, rows, :], wn_ref[...]))
        x = z_ref[0, rows, :] + mod(2) * mix
        z1_ref[0, rows, :] = x
        hn = x * lax.rsqrt(jnp.mean(x * x, axis=-1, keepdims=True) + EPS) * g2_ref[...]
        hn = hn * (1.0 + mod(4)) + mod(3)
        h_hi, h_lo = _split_bf16(hn, 2)
        splits.append((h_hi, h_lo))

    affs = [_sigmoid(_dot(h_hi, rw_ref[0]) + (_dot(h_hi, rw_ref[1]) + _dot(h_lo, rw_ref[0])))
            for h_hi, h_lo in splits]

    def cyc(x, k, width):
        fwd = pltpu.roll(x, 128 - k, 1)
        back = pltpu.roll(x, width - k, 1)
        return jnp.where((lane % width) + k < width, fwd, back)

    def rank(x, width, step):
        r = jnp.zeros((TT, 128), F32)
        for k in range(1, width // step):
            y = cyc(x, k * step, width)
            wrapped = (lane % width) + k * step >= width
            beats = jnp.logical_or(y > x, jnp.logical_and(y == x, wrapped))
            r = r + beats.astype(F32)
        return r

    routed = []
    for aff in affs:
        sel = aff + rb_ref[...]
        top2 = rank(sel, EPG, 1) < 2.0
        part = jnp.where(top2, sel, 0.0)
        gscore = part
        for k in range(1, EPG):
            gscore = gscore + cyc(part, k, EPG)
        best = rank(gscore, N_EXPERTS, EPG) < 1.0
        chosen = jnp.logical_and(jnp.logical_and(top2, best), live)
        w = jnp.where(chosen, aff, 0.0)
        group = jnp.sum(jnp.where(chosen, (lane // EPG).astype(F32), 0.0), axis=-1, keepdims=True) * (1.0 / TOP_K)
        routed.append((w / jnp.sum(w, axis=-1, keepdims=True), group))

    sub = lax.broadcasted_iota(jnp.int32, (128, TT), 0)
    row_id = lax.broadcasted_iota(jnp.int32, (SORT_ROWS, TT), 0).astype(F32)
    for gi, ((gate, group), (h_hi, _)) in enumerate(zip(routed, splits)):
        member_t = (lane.astype(F32) == group).astype(F32).T
        member_t = jnp.where(sub < N_GROUPS, member_t, 0.0)
        ahead = _dot(member_t.astype(BF16), before_ref[...])
        count = jnp.sum(member_t, axis=1, keepdims=True)
        padded = ((count.astype(jnp.int32) + (SORT_BLK - 1)) & -SORT_BLK).astype(F32)
        start = _dot(below_ref[...], jnp.broadcast_to(padded, (128, 128)).astype(BF16))
        where_t = jnp.sum(member_t * (start[:, 0:1] + ahead), axis=0, keepdims=True)
        perm = (row_id == where_t).astype(BF16)
        hs_ref[0, gi] = _dot(perm, h_hi)
        gs_ref[0, gi] = sum(_dot(perm, part) for part in _split_bf16(gate, 3))
        tm_ref[0, gi, 0:1, :] = where_t
        tm_ref[0, gi, 1:8, :] = jnp.zeros((7, TT), F32)
        lane_m = lax.broadcasted_iota(jnp.int32, (128, 128), 1)
        sm_ref[0, gi] = jnp.where(lane_m == 0, jnp.broadcast_to(padded, (128, 128)), start)


def _out_proj(z, mods, mla_o, ml_o, na_o, wa, wm, wn, g2, rw, rb):
    B = z.shape[0]
    tok = lambda w_: pl.BlockSpec((1, OT, w_), lambda b, t: (b, t, 0))
    full = lambda a: pl.BlockSpec(a.shape, lambda b, t, _n=a.ndim: (0,) * _n)
    per_group = lambda r, w_: pl.BlockSpec((1, OT // TT, r, w_), lambda b, t: (b, t, 0, 0))
    grouped = lambda r, w_: jax.ShapeDtypeStruct((B, NT, r, w_), F32)
    before = jnp.asarray(np.triu(np.ones((TT, TT)), 1), BF16)
    below = jnp.asarray(np.tril(np.ones((128, 128)), -1), BF16)
    return pl.pallas_call(
        _out_proj_kernel,
        grid=(B, S // OT),
        in_specs=[tok(D),
                  pl.BlockSpec((1, 1, 6 * D), lambda b, t: (2 * b, 0, 0)),
                  pl.BlockSpec((1, 1, 6 * D), lambda b, t: (2 * b + 1, 0, 0)),
                  tok(MLA_H * V_D), tok(ML_WP), tok(NA_W),
                  full(wa), full(wm), full(wn), full(g2), full(rw), full(rb), full(before), full(below)],
        out_specs=[tok(D), per_group(SORT_ROWS, D), per_group(SORT_ROWS, 128), per_group(8, TT), per_group(128, 128)],
        out_shape=[jax.ShapeDtypeStruct((B, S, D), F32),
                   grouped(SORT_ROWS, D), grouped(SORT_ROWS, 128), grouped(8, TT), grouped(128, 128)],
        compiler_params=_cparams(("parallel", "parallel")),
        name="out_proj",
    )(z, mods, mods, mla_o, ml_o, na_o, wa, wm, wn, g2, rw, rb, before, below)


ROW_WAIT = 64


def _row_copy_start(n, row_copy):
    def issue(i, carry):
        row_copy(i).start()
        return carry

    lax.fori_loop(0, n, issue, 0, unroll=8)


def _row_copy_wait(n, slab_copy):
    def drain(i, carry):
        slab_copy(ROW_WAIT).wait()
        return carry

    lax.fori_loop(0, n // ROW_WAIT, drain, 0)


def _experts_kernel(blk_ref, tg_ref, hs_ref, gs_ref, w1_ref, w3_ref, w2_ref, ys_ref, hbuf, gbuf, sems):
    i = pl.program_id(0)
    n_blk = GROUP_T // SORT_BLK

    def fetch(tile, slot):
        def src(q):
            return pl.ds(pl.multiple_of(blk_ref[tile * n_blk + q] * SORT_BLK, SORT_BLK), SORT_BLK)

        def dst(q):
            return pl.ds(pl.multiple_of(q * SORT_BLK, SORT_BLK), SORT_BLK)

        _row_copy_start(n_blk, lambda q: pltpu.make_async_copy(hs_ref.at[src(q)], hbuf.at[slot, dst(q)], sems.at[0, slot]))
        _row_copy_start(n_blk, lambda q: pltpu.make_async_copy(gs_ref.at[src(q)], gbuf.at[slot, dst(q)], sems.at[1, slot]))

    n_used = tg_ref[pl.num_programs(0)]

    @pl.when(i == 0)
    def _():
        fetch(0, 0)

    @pl.when(i < n_used)
    def _():
        slot = i % 2
        _row_copy_wait(GROUP_T, lambda k: pltpu.make_async_copy(
            hs_ref.at[pl.ds(0, k)], hbuf.at[slot, pl.ds(0, k)], sems.at[0, slot]))
        _row_copy_wait(GROUP_T, lambda k: pltpu.make_async_copy(
            gs_ref.at[pl.ds(0, k)], gbuf.at[slot, pl.ds(0, k)], sems.at[1, slot]))

        @pl.when(i + 1 < n_used)
        def _():
            fetch(i + 1, 1 - slot)

        g = tg_ref[i]
        x = hbuf[slot].astype(BF16)
        gates = gbuf[slot]
        lane = lax.broadcasted_iota(jnp.int32, (GROUP_T, 128), 1)
        ups = [(_dot(x, w1_ref[0, e].astype(BF16)), _dot(x, w3_ref[0, e].astype(BF16))) for e in range(EPG)]
        acc = jnp.zeros((GROUP_T, D), F32)
        for e in range(EPG):
            ge = jnp.sum(jnp.where(lane == g * EPG + e, gates, 0.0), axis=-1, keepdims=True)
            a, b = ups[e]
            acc = acc + _dot((_silu(a) * b * ge).astype(BF16), w2_ref[0, e].astype(BF16))
        ys_ref[...] = acc

    @pl.when(i >= n_used)
    def _():
        ys_ref[...] = jnp.zeros((GROUP_T, D), F32)


def _experts(src_blk, tile_group, hs, gs, n_rows, w1, w3, w2, l):
    wspec = lambda k, n: pl.BlockSpec((1, EPG, k, n), lambda i, blk, tg: (l, tg[i], 0, 0))
    return pl.pallas_call(
        _experts_kernel,
        grid_spec=pltpu.PrefetchScalarGridSpec(
            num_scalar_prefetch=2,
            grid=(n_rows // GROUP_T,),
            in_specs=[pl.BlockSpec(memory_space=pl.ANY), pl.BlockSpec(memory_space=pl.ANY),
                      wspec(D, D_FF), wspec(D, D_FF), wspec(D_FF, D)],
            out_specs=pl.BlockSpec((GROUP_T, D), lambda i, blk, tg: (i, 0)),
            scratch_shapes=[pltpu.VMEM((2, GROUP_T, D), F32), pltpu.VMEM((2, GROUP_T, 128), F32),
                            pltpu.SemaphoreType.DMA((2, 2))]),
        out_shape=jax.ShapeDtypeStruct((n_rows, D), F32),
        compiler_params=_cparams(("arbitrary",)),
        name="moe_experts",
    )(src_blk, tile_group, hs, gs, w1, w3, w2)


def _combine_kernel(blk_ref, z1_ref, modx_ref, mody_ref, tm_ref, ys_ref, o_ref, buf, sems):
    t = pl.program_id(1)
    step = pl.program_id(0) * NT + t
    n_blk = SORT_ROWS // SORT_BLK

    def fetch(token_group, slot):
        base = token_group * n_blk
        _row_copy_start(n_blk, lambda q: pltpu.make_async_copy(
            ys_ref.at[pl.ds(pl.multiple_of(blk_ref[base + q] * SORT_BLK, SORT_BLK), SORT_BLK)],
            buf.at[slot, pl.ds(pl.multiple_of(q * SORT_BLK, SORT_BLK), SORT_BLK)], sems.at[slot]))

    @pl.when(step == 0)
    def _():
        fetch(0, 0)

    slot = step % 2
    _row_copy_wait(SORT_ROWS, lambda k: pltpu.make_async_copy(
        ys_ref.at[pl.ds(0, k)], buf.at[slot, pl.ds(0, k)], sems.at[slot]))

    @pl.when(step + 1 < pl.num_programs(0) * NT)
    def _():
        fetch(step + 1, 1 - slot)

    row_id = lax.broadcasted_iota(jnp.int32, (SORT_ROWS, TT), 0).astype(F32)
    perm = (row_id == tm_ref[0, 0, 0:1, :]).astype(BF16)
    y = sum(_dot_tn(perm, part) for part in _split_bf16(buf[slot], 2))
    g2 = jnp.where(t == NT - 1, mody_ref[0][:, 5 * D:], modx_ref[0][:, 5 * D:])
    o_ref[0] = z1_ref[0] + g2 * y


def _combine(back_blk, z1, mods, tok_meta, ys):
    B = z1.shape[0]
    tok = pl.BlockSpec((1, TT, D), lambda b, t, blk: (b, t, 0))
    return pl.pallas_call(
        _combine_kernel,
        grid_spec=pltpu.PrefetchScalarGridSpec(
            num_scalar_prefetch=1,
            grid=(B, NT),
            in_specs=[tok,
                      pl.BlockSpec((1, 1, 6 * D), lambda b, t, blk: (2 * b, 0, 0)),
                      pl.BlockSpec((1, 1, 6 * D), lambda b, t, blk: (2 * b + 1, 0, 0)),
                      pl.BlockSpec((1, 1, 8, TT), lambda b, t, blk: (b, t, 0, 0)),
                      pl.BlockSpec(memory_space=pl.ANY)],
            out_specs=tok,
            scratch_shapes=[pltpu.VMEM((2, SORT_ROWS, D), F32), pltpu.SemaphoreType.DMA((2,))]),
        out_shape=jax.ShapeDtypeStruct((B, S, D), F32),
        compiler_params=_cparams(("arbitrary", "arbitrary")),
        name="moe_combine",
    )(back_blk, z1, mods, mods, tok_meta, ys)


def _group_layout(seg_meta, n_rows):
    n_tg = seg_meta.shape[0]
    seg_len = seg_meta[:, :N_GROUPS, 0].astype(jnp.int32).T
    seg_local = seg_meta[:, :N_GROUPS, 1].astype(jnp.int32).T
    total = jnp.sum(seg_len, axis=1)
    padded = (total + GROUP_T - 1) // GROUP_T * GROUP_T
    group_end = jnp.cumsum(padded)
    seg_start = (group_end - padded)[:, None] + jnp.cumsum(seg_len, axis=1) - seg_len
    flat_start, flat_len, flat_local = seg_start.reshape(-1), seg_len.reshape(-1), seg_local.reshape(-1)

    row0 = (jnp.arange(n_rows // SORT_BLK, dtype=jnp.int32) * SORT_BLK)[:, None]
    covers = (row0 >= flat_start[None, :]) & (row0 < (flat_start + flat_len)[None, :])
    token_group = (jnp.arange(flat_start.shape[0], dtype=jnp.int32) % n_tg)[None, :]
    src_row = jnp.sum(jnp.where(covers, token_group * SORT_ROWS + flat_local[None, :] + row0 - flat_start[None, :], 0),
                      axis=1)
    filled = jnp.any(covers, axis=1)
    src_blk = jnp.where(filled, src_row // SORT_BLK, SORT_ROWS // SORT_BLK - 1)

    tile_group = jnp.sum(jnp.arange(n_rows // GROUP_T)[:, None] * GROUP_T >= group_end[None, :], axis=1)

    local0 = (jnp.arange(SORT_ROWS // SORT_BLK, dtype=jnp.int32) * SORT_BLK)[None, None, :]
    lo, ln, st = seg_local[:, :, None], seg_len[:, :, None], seg_start[:, :, None]
    back_row = jnp.sum(jnp.where((local0 >= lo) & (local0 < lo + ln), st + local0 - lo, 0), axis=0)
    tile_group = jnp.concatenate([jnp.minimum(tile_group, N_GROUPS - 1), group_end[-1:] // GROUP_T])
    return (src_blk.astype(jnp.int32), tile_group.astype(jnp.int32),
            (back_row // SORT_BLK).reshape(-1).astype(jnp.int32))


def _moe(z1, hs, gs, tok_meta, seg_meta, mods, w1, w3, w2, l):
    B = z1.shape[0]
    n_tg = B * NT
    n_rows = (-(-(n_tg * (TT + N_GROUPS * (SORT_BLK - 1))) // GROUP_T) + N_GROUPS) * GROUP_T
    src_blk, tile_group, back_blk = _group_layout(seg_meta.reshape(n_tg, 128, 128), n_rows)
    ys = _experts(src_blk, tile_group, hs.reshape(n_tg * SORT_ROWS, D), gs.reshape(n_tg * SORT_ROWS, 128),
                  n_rows, w1, w3, w2, l)
    return _combine(back_blk, z1, mods, tok_meta, ys)


def _in_proj_layout(w):
    cuts = np.cumsum([Q_RANK, KV_RANK, ROPE_D, ML_W, ML_W, 4 * ML_H])
    qc, ckv, kr, u, zz, g, na = jnp.split(w, [int(v) for v in cuts], axis=-1)
    zeros = lambda n: jnp.zeros((w.shape[0], n), w.dtype)
    out = jnp.concatenate([qc, ckv, zeros(NOPE_D), kr, zeros(HP - QK_D),
                           _pad_heads(u, ML_H, ML_D, HP), _pad_heads(zz, ML_H, ML_D, HP),
                           _gate_order(g), zeros(128 - 4 * ML_H), na], axis=-1)
    assert out.shape[-1] == NP_IN
    return out


def _gate_order(g):
    i_f, f_f, i_b, f_b = jnp.split(g, 4, axis=-1)
    return jnp.concatenate([i_f, i_b, f_f, f_b], axis=-1)


def _pad_heads(v, nh, d, dp):
    lead = v.shape[:-1]
    v = v.reshape(lead + (nh, d))
    v = jnp.pad(v, [(0, 0)] * len(lead) + [(0, 0), (0, dp - d)])
    return v.reshape(lead + (nh * dp,))


def _rope_tables():
    t = np.arange(SEQ)
    row = (t // GRID_W).astype(np.float32)
    col = (t % GRID_W).astype(np.float32)
    quarter = ROPE_D // 4
    inv = jnp.asarray(ROPE_BASE, F32) ** (-jnp.arange(quarter, dtype=F32) / quarter)
    ar = jnp.asarray(row)[:, None] * inv
    ac = jnp.asarray(col)[:, None] * inv
    ang = jnp.concatenate([ar, ar, ac, ac], axis=-1)
    cos = jnp.ones((S, HP), F32).at[:SEQ, NOPE_D:QK_D].set(jnp.cos(ang))
    sin = jnp.zeros((S, HP), F32).at[:SEQ, NOPE_D:QK_D].set(jnp.sin(ang))
    return cos, sin


def _rotate_half_index():
    q = ROPE_D // 4
    src = np.arange(QK_D)
    sign = np.zeros((QK_D,), np.float32)
    for blk in range(2):
        lo = NOPE_D + 2 * q * blk
        src[lo:lo + q] = np.arange(lo + q, lo + 2 * q)
        sign[lo:lo + q] = -1.0
        src[lo + q:lo + 2 * q] = np.arange(lo, lo + q)
        sign[lo + q:lo + 2 * q] = 1.0
    return src, sign


def _rotate_half(w):
    src, sign = _rotate_half_index()
    return w[..., src] * sign


NA_NDR = 2 * WIN_R - 1
NA_NPAIR = 3 * NA_NDR


def _na_pair_index():
    idx = np.zeros((NA_NBLK, NA_QROWS, NA_KROWS // 2), np.int32)
    for blk in range(NA_NBLK):
        k0 = int(np.clip(blk * NA_QROWS - WIN_R // 2, 0, ROWS - NA_KROWS))
        for i in range(NA_QROWS):
            qr = blk * NA_QROWS + i
            rs = int(np.clip(qr - WIN_R // 2, 0, ROWS - WIN_R))
            assert k0 <= rs and rs + WIN_R <= k0 + NA_KROWS
            for p in range(NA_KROWS // 2):
                kr = k0 + 2 * p
                dr = kr - qr + WIN_R - 1
                left = rs <= kr < rs + WIN_R
                right = rs <= kr + 1 < rs + WIN_R
                if left and right:
                    idx[blk, i, p] = 1 + dr
                elif left:
                    idx[blk, i, p] = NA_NDR + dr
                elif right:
                    idx[blk, i, p] = 2 * NA_NDR + dr + 1
    return idx.reshape(-1)


def _na_pair_tiles(rpb):
    cq = np.arange(GRID_W)
    cs = np.clip(cq - WIN_C // 2, 0, GRID_W - WIN_C)
    col_ok = (cq[None, :] >= cs[:, None]) & (cq[None, :] < cs[:, None] + WIN_C)
    dc = np.clip(cq[None, :] - cq[:, None], -(WIN_C - 1), WIN_C - 1) + (WIN_C - 1)
    onehot = jnp.asarray(np.eye(2 * WIN_C - 1, dtype=np.float32)[dc])
    tiles = jnp.einsum('hrc,qkc->hrqk', rpb, onehot, precision=lax.Precision.HIGHEST)
    tiles = jnp.where(jnp.asarray(col_ok), tiles * float(np.log2(np.e)), NEG)
    masked = jnp.full_like(tiles, NEG)
    both = jnp.concatenate([tiles[:, :-1], tiles[:, 1:]], axis=-1)
    left = jnp.concatenate([tiles, masked], axis=-1)
    right = jnp.concatenate([masked, tiles], axis=-1)
    none = jnp.concatenate([masked[:, :1], masked[:, :1]], axis=-1)
    out = jnp.concatenate([none, both, left, right], axis=1)
    assert out.shape[1] == NA_NPAIR
    return out.astype(BF16)


def kernel(x, c, ctx, c_ctx, w_mod, b_mod, norm1_g, norm2_g, w_in, w_out, mla_qnorm_g, mla_w_uq, mla_kvnorm_g, mla_w_ukv, mla_q_g, mla_k_g, ml_conv_w, ml_conv_b, ml_w_q, ml_w_k, ml_w_v, ml_gate_b, ml_norm_g, ml_skip, na_q_g, na_k_g, na_rpb, router_w, router_b, moe_w1, moe_w3, moe_w2):
    B = x.shape[0]
    z = jnp.concatenate([x, ctx], axis=1)
    cc = jnp.zeros((16, D), F32).at[:B].set(c).at[B].set(c_ctx)
    mod_all = _modulation(cc, w_mod, b_mod)
    cos, sin = _rope_tables()
    rot_src, rot_sign = _rotate_half_index()
    rot_np = np.zeros((HP, HP), np.float32)
    rot_np[rot_src, np.arange(QK_D)] = rot_sign
    rot_mat = jnp.asarray(rot_np, BF16)
    ones_hp = jnp.ones((HP, HP), BF16)
    seg = jnp.asarray(np.kron(np.eye(NA_H), np.ones((NA_D, NA_D))), BF16)
    tril = jnp.asarray(np.stack([np.tril(np.ones((ML_CHUNK, ML_CHUNK))), np.triu(np.ones((ML_CHUNK, ML_CHUNK)))]), BF16)
    pair_idx = jnp.asarray(_na_pair_index())
    rw = jnp.stack(_split_bf16(jnp.pad(router_w, ((0, 0), (0, 128 - N_EXPERTS))), 2))
    rb = jnp.pad(router_b, (0, 128 - N_EXPERTS), constant_values=NEG).reshape(1, 128)

    def pad_lanes(v, n):
        return jnp.pad(v, [(0, 0)] * (v.ndim - 1) + [(0, n - v.shape[-1])])

    for l in range(DEPTH):
        mx = mod_all[l, :B]
        my = jnp.broadcast_to(mod_all[l, B], (B, 6 * D))
        mods = jnp.stack([mx, my], axis=1).reshape(2 * B, 1, 6 * D)

        w_in_p = _in_proj_layout(w_in[l]).astype(BF16)
        pmla, pu, pg, pna = _in_proj(z, mods, norm1_g[l].reshape(1, D), w_in_p)

        wq = jnp.transpose(mla_w_uq[l].reshape(Q_RANK, MLA_H, QK_D), (1, 0, 2))
        wuq = jnp.concatenate([pad_lanes(wq, HP), pad_lanes(_rotate_half(wq), HP)], axis=-1).astype(BF16)
        wukv = jnp.transpose(mla_w_ukv[l].reshape(KV_RANK, MLA_H, NOPE_D + V_D), (1, 0, 2))
        wuk = pad_lanes(wukv[..., :NOPE_D], HP).astype(BF16)
        wuv = pad_lanes(wukv[..., NOPE_D:], HP).astype(BF16)
        qg, kg = mla_q_g[l], mla_k_g[l]
        q_scale = float(QK_D ** -0.5 * np.log2(np.e))
        tabs = jnp.stack([cos * pad_lanes(qg, HP) * q_scale, sin * pad_lanes(jnp.abs(rot_sign) * qg[rot_src], HP) * q_scale,
                          cos * pad_lanes(kg, HP), sin * pad_lanes(jnp.abs(rot_sign) * kg[rot_src], HP)])
        heads_last = lambda w_: jnp.transpose(w_, (1, 0, 2)).reshape(w_.shape[1], -1)
        q, k, v = _mla_prep(pmla, tabs, mla_qnorm_g[l].reshape(1, Q_RANK), heads_last(wuq),
                            mla_kvnorm_g[l].reshape(1, KV_RANK), heads_last(wuk), heads_last(wuv), rot_mat, ones_hp)
        mla_o = _mla_attn(q, k, v)

        padh = lambda a: _pad_heads(a, ML_H, ML_D, HP)
        padw = lambda w_: jnp.pad(w_, ((0, 0), (0, HP - ML_D), (0, HP - ML_D))).astype(BF16)
        cw = jnp.pad(padh(ml_conv_w[l]), ((0, 8 - 3), (0, 0)))
        ml_o = _mlstm(pu, pg, cw, padh(ml_conv_b[l]).reshape(1, ML_WP),
                      padw(ml_w_q[l]), padw(jnp.swapaxes(ml_w_k[l], 1, 2) * (ML_D ** -0.5)), padw(ml_w_v[l]),
                      pad_lanes(_gate_order(ml_gate_b[l]).reshape(1, 4 * ML_H), 128),
                      padh(ml_norm_g[l]).reshape(1, ML_WP), padh(ml_skip[l]).reshape(1, ML_WP), tril)

        na_o = _na_attn(pair_idx, pna, _na_pair_tiles(na_rpb[l]), jnp.tile(na_q_g[l], NA_H).reshape(1, NA_W),
                        jnp.tile(na_k_g[l], NA_H).reshape(1, NA_W), seg)

        wo = w_out[l]
        wa = wo[:MLA_H * V_D].astype(BF16)
        wm = jnp.pad(wo[MLA_H * V_D:MLA_H * V_D + ML_W].reshape(ML_H, ML_D, D),
                     ((0, 0), (0, HP - ML_D), (0, 0))).reshape(ML_WP, D).astype(BF16)
        wn = wo[MLA_H * V_D + ML_W:].astype(BF16)
        z1, hs, gs, tok_meta, seg_meta = _out_proj(z, mods, mla_o, ml_o, na_o, wa, wm, wn,
                                                   norm2_g[l].reshape(1, D), rw, rb)
        z = _moe(z1, hs, gs, tok_meta, seg_meta, mods, moe_w1, moe_w3, moe_w2, l)

    return z[:, :SEQ]
```
